```python
import math
import jax, jax.numpy as jnp
from jax import lax
import numpy as np

D_MODEL = 1024
BATCH = 8
SEQ = 2048
DEPTH = 1
DEC_BATCH = 128
DEC_SEQ = 1
PAST_LEN = 16384
PAGE_SIZE = 128

D_RNN = D_MODEL
RG_BLOCKS = 8
RG_BW = D_RNN // RG_BLOCKS
CONV_W = 4
RG_C = 8.0
M_HEADS = 4
M_DK = D_MODEL // M_HEADS
M_DV = D_MODEL // M_HEADS
D_M = M_HEADS * M_DV
MLSTM_CHUNK = 128
D_FF = ((8 * D_MODEL // 3) + 255) // 256 * 256
N_MOD = 9
N_IN = D_RNN + 2 * M_HEADS * M_DK + 2 * D_M + 2 * M_HEADS + 2 * D_MODEL
EPS = 1e-6

kernel_name = 'griffin_mlstm_macaron_adaln_step'


def _rms(x, gain):
    x32 = x.astype(jnp.float32)
    return x32 * lax.rsqrt(jnp.mean(x32 * x32, axis=-1, keepdims=True) + EPS) * gain.astype(jnp.float32)


def _rms_mod(x, gain, shift, scale):
    return (_rms(x, gain) * (1.0 + scale) + shift).astype(x.dtype)


def _swiglu(x, w_in, w_out):
    g, u = jnp.split(x @ w_in, 2, axis=-1)
    return (jax.nn.silu(g) * u) @ w_out


def _lin_comb(left, right):
    a1, b1 = left
    a2, b2 = right
    return a1 * a2, a2 * b1 + b2


def _rglru(x_rg, conv_buf, h0, conv_w, conv_b, w_a, b_a, w_i, b_i, lam):
    f32 = jnp.float32
    bsz, L, _ = x_rg.shape
    xc = jnp.concatenate([conv_buf.astype(x_rg.dtype), x_rg], axis=1)
    new_buf = xc[:, L:]
    u = conv_b.astype(f32)
    for j in range(CONV_W):
        u = u + xc[:, j:j + L].astype(f32) * conv_w[j].astype(f32)
    ub = u.reshape(bsz, L, RG_BLOCKS, RG_BW)
    r = jax.nn.sigmoid(jnp.einsum('blnc,ncd->blnd', ub, w_a.astype(f32)).reshape(bsz, L, D_RNN) + b_a.astype(f32))
    i_g = jax.nn.sigmoid(jnp.einsum('blnc,ncd->blnd', ub, w_i.astype(f32)).reshape(bsz, L, D_RNN) + b_i.astype(f32))
    log_a = -RG_C * r * jax.nn.softplus(-lam.astype(f32))
    a = jnp.exp(log_a)
    b = jnp.sqrt(-jnp.expm1(2.0 * log_a)) * (i_g * u)
    b = b.at[:, 0].add(a[:, 0] * h0.astype(f32))
    _, h = lax.associative_scan(_lin_comb, (a, b), axis=1)
    return h, new_buf, h[:, -1]


def _mlstm(q, k, v, ig, lf, C0, n0, m0):
    f32 = jnp.float32
    bsz, L = q.shape[:2]
    cs = MLSTM_CHUNK if L % MLSTM_CHUNK == 0 else L
    nc = L // cs

    def to_chunks(t):
        t = t.reshape((bsz, nc, cs) + t.shape[2:])
        return t.transpose((1, 0, 3, 2) + tuple(range(4, t.ndim)))

    mask = jnp.tril(jnp.ones((cs, cs), dtype=bool))
    scale = M_DK ** -0.5

    def step(carry, inp):
        C, n, m = carry
        qc, kc, vc, ic, lfc = inp
        bcum = jnp.cumsum(lfc, axis=-1)
        log_d = jnp.where(mask, bcum[..., :, None] - bcum[..., None, :] + ic[..., None, :], -jnp.inf)
        log_past = bcum + m[..., None]
        m_t = jnp.maximum(log_past, jnp.max(log_d, axis=-1))
        d_mat = jnp.exp(log_d - m_t[..., None])
        past_w = jnp.exp(log_past - m_t)
        qs = qc * scale
        s = jnp.einsum('bhtd,bhsd->bhts', qs, kc) * d_mat
        num = past_w[..., None] * jnp.einsum('bhtd,bhde->bhte', qs, C) + jnp.einsum('bhts,bhse->bhte', s, vc)
        den = past_w * jnp.einsum('bhtd,bhd->bht', qs, n) + jnp.sum(s, axis=-1)
        h = num / jnp.maximum(jnp.abs(den), jnp.exp(-m_t))[..., None]
        m_new = m_t[..., -1]
        w_s = jnp.exp(bcum[..., -1:] - bcum + ic - m_new[..., None])
        decay = jnp.exp(bcum[..., -1] + m - m_new)
        C_new = decay[..., None, None] * C + jnp.einsum('bhs,bhsd,bhse->bhde', w_s, kc, vc)
        n_new = decay[..., None] * n + jnp.einsum('bhs,bhsd->bhd', w_s, kc)
        return (C_new, n_new, m_new), h

    carry0 = (C0.astype(f32), n0.astype(f32), m0.astype(f32))
    (C1, n1, m1), h = lax.scan(step, carry0, (to_chunks(q), to_chunks(k), to_chunks(v), to_chunks(ig), to_chunks(lf)))
    h = h.transpose(1, 0, 3, 2, 4).reshape(bsz, L, M_HEADS * M_DV)
    return h, C1, n1, m1


def _mixer(xn, conv_buf, h0, C0, n0, m0, w_in, conv_w, conv_b, w_rg_a, b_rg_a, w_rg_i, b_rg_i,
           rg_lambda, b_ig, b_fg, w_br_a, w_br_b, w_out):
    f32 = jnp.float32
    dt = xn.dtype
    bsz, L, _ = xn.shape
    sizes = [D_RNN, M_HEADS * M_DK, M_HEADS * M_DK, D_M, D_M, M_HEADS, M_HEADS, D_MODEL, D_MODEL]
    x_rg, q, k, v, o_pre, i_pre, f_pre, g_a, g_b = jnp.split(xn @ w_in, np.cumsum(sizes)[:-1].tolist(), axis=-1)
    h_a, new_buf, h_last = _rglru(x_rg, conv_buf, h0, conv_w, conv_b, w_rg_a, b_rg_a, w_rg_i, b_rg_i, rg_lambda)
    q = q.astype(f32).reshape(bsz, L, M_HEADS, M_DK)
    k = k.astype(f32).reshape(bsz, L, M_HEADS, M_DK)
    v = v.astype(f32).reshape(bsz, L, M_HEADS, M_DV)
    ig = i_pre.astype(f32) + b_ig.astype(f32)
    lf = jax.nn.log_sigmoid(f_pre.astype(f32) + b_fg.astype(f32))
    h_m, C1, n1, m1 = _mlstm(q, k, v, ig, lf, C0, n0, m0)
    h_b = jax.nn.sigmoid(o_pre.astype(f32)) * h_m
    merged = (jax.nn.sigmoid(g_a.astype(f32)) * (h_a.astype(dt) @ w_br_a)
              + jax.nn.sigmoid(g_b.astype(f32)) * (h_b.astype(dt) @ w_br_b))
    return merged.astype(dt) @ w_out, (new_buf, h_last, C1, n1, m1)


def _layer(x, c, lw, st):
    (w_ada, b_ada, g1, wf1i, wf1o, g2, w_in, conv_w, conv_b, w_rg_a, b_rg_a, w_rg_i, b_rg_i,
     rg_lambda, b_ig, b_fg, w_br_a, w_br_b, w_out, g3, wf2i, wf2o) = lw
    conv_buf, h0, C0, n0, m0 = st
    bsz = x.shape[0]
    mod = (jax.nn.silu(c.astype(jnp.float32)) @ w_ada.astype(jnp.float32) + b_ada.astype(jnp.float32))
    mod = mod.reshape(bsz, N_MOD, 1, D_MODEL)
    sh1, sc1, ga1, sh2, sc2, ga2, sh3, sc3, ga3 = [mod[:, j] for j in range(N_MOD)]
    x = x + (0.5 * ga1 * _swiglu(_rms_mod(x, g1, sh1, sc1), wf1i, wf1o)).astype(x.dtype)
    h, new_st = _mixer(_rms_mod(x, g2, sh2, sc2), conv_buf, h0, C0, n0, m0, w_in, conv_w, conv_b,
                       w_rg_a, b_rg_a, w_rg_i, b_rg_i, rg_lambda, b_ig, b_fg, w_br_a, w_br_b, w_out)
    x = x + (ga2 * h).astype(x.dtype)
    x = x + (0.5 * ga3 * _swiglu(_rms_mod(x, g3, sh3, sc3), wf2i, wf2o)).astype(x.dtype)
    return x, new_st


def setup_inputs(seed: int = 0) -> dict:
    key = jax.random.key(seed)
    ks = jax.random.split(key, 40)
    f32 = jnp.float32

    def nrm(k, shape, s):
        return jax.random.normal(k, shape, f32) * s

    a_init = jax.random.uniform(ks[20], (DEPTH, D_RNN), f32, minval=0.9, maxval=0.999)
    a_root = a_init ** (1.0 / RG_C)
    rg_lambda = jnp.log(a_root) - jnp.log1p(-a_root)
    return {
        'x_prompt': nrm(ks[0], (BATCH, SEQ, D_MODEL), 1.0),
        'x_sample': nrm(ks[1], (DEC_BATCH, DEC_SEQ, D_MODEL), 1.0),
        'state_conv': nrm(ks[2], (DEPTH, DEC_BATCH, CONV_W - 1, D_RNN), 1.0),
        'state_rg_h': nrm(ks[3], (DEPTH, DEC_BATCH, D_RNN), 0.5),
        'state_C': nrm(ks[4], (DEPTH, DEC_BATCH, M_HEADS, M_DK, M_DV), 0.1),
        'state_n': nrm(ks[5], (DEPTH, DEC_BATCH, M_HEADS, M_DK), 0.1),
        'state_m': nrm(ks[6], (DEPTH, DEC_BATCH, M_HEADS), 1.0),
        'c_prompt': nrm(ks[7], (BATCH, D_MODEL), 1.0),
        'c_sample': nrm(ks[8], (DEC_BATCH, D_MODEL), 1.0),
        'w_ada': nrm(ks[9], (DEPTH, D_MODEL, N_MOD * D_MODEL), 0.5 * D_MODEL ** -0.5),
        'b_ada': nrm(ks[10], (DEPTH, N_MOD * D_MODEL), 0.02),
        'g_norm1': 1.0 + nrm(ks[11], (DEPTH, D_MODEL), 0.05),
        'w_ff1_in': nrm(ks[12], (DEPTH, D_MODEL, 2 * D_FF), D_MODEL ** -0.5),
        'w_ff1_out': nrm(ks[13], (DEPTH, D_FF, D_MODEL), D_FF ** -0.5),
        'g_norm2': 1.0 + nrm(ks[14], (DEPTH, D_MODEL), 0.05),
        'w_in': nrm(ks[15], (DEPTH, D_MODEL, N_IN), D_MODEL ** -0.5),
        'conv_w': nrm(ks[16], (DEPTH, CONV_W, D_RNN), CONV_W ** -0.5),
        'conv_b': nrm(ks[17], (DEPTH, D_RNN), 0.02),
        'w_rg_a': nrm(ks[18], (DEPTH, RG_BLOCKS, RG_BW, RG_BW), RG_BW ** -0.5),
        'b_rg_a': nrm(ks[19], (DEPTH, D_RNN), 0.02),
        'w_rg_i': nrm(ks[21], (DEPTH, RG_BLOCKS, RG_BW, RG_BW), RG_BW ** -0.5),
        'b_rg_i': nrm(ks[22], (DEPTH, D_RNN), 0.02),
        'rg_lambda': rg_lambda,
        'b_ig': nrm(ks[23], (DEPTH, M_HEADS), 0.1),
        'b_fg': 3.0 + nrm(ks[24], (DEPTH, M_HEADS), 0.5),
        'w_br_a': nrm(ks[25], (DEPTH, D_RNN, D_MODEL), D_RNN ** -0.5),
        'w_br_b': nrm(ks[26], (DEPTH, D_M, D_MODEL), D_M ** -0.5),
        'w_out': nrm(ks[27], (DEPTH, D_MODEL, D_MODEL), D_MODEL ** -0.5),
        'g_norm3': 1.0 + nrm(ks[28], (DEPTH, D_MODEL), 0.05),
        'w_ff2_in': nrm(ks[29], (DEPTH, D_MODEL, 2 * D_FF), D_MODEL ** -0.5),
        'w_ff2_out': nrm(ks[30], (DEPTH, D_FF, D_MODEL), D_FF ** -0.5),
        'g_final': 1.0 + nrm(ks[31], (D_MODEL,), 0.05),
    }


def reference(x_prompt, x_sample, state_conv, state_rg_h, state_C, state_n, state_m, c_prompt, c_sample,
              w_ada, b_ada, g_norm1, w_ff1_in, w_ff1_out, g_norm2, w_in, conv_w, conv_b, w_rg_a, b_rg_a,
              w_rg_i, b_rg_i, rg_lambda, b_ig, b_fg, w_br_a, w_br_b, w_out, g_norm3, w_ff2_in, w_ff2_out,
              g_final):
    f32 = jnp.float32
    bp = x_prompt.shape[0]
    zero_st = (jnp.zeros((bp, CONV_W - 1, D_RNN), x_prompt.dtype), jnp.zeros((bp, D_RNN), f32),
               jnp.zeros((bp, M_HEADS, M_DK, M_DV), f32), jnp.zeros((bp, M_HEADS, M_DK), f32),
               jnp.zeros((bp, M_HEADS), f32))
    yp, ys = x_prompt, x_sample
    sts_p, sts_s = [], []
    for l in range(DEPTH):
        lw = (w_ada[l], b_ada[l], g_norm1[l], w_ff1_in[l], w_ff1_out[l], g_norm2[l], w_in[l], conv_w[l],
              conv_b[l], w_rg_a[l], b_rg_a[l], w_rg_i[l], b_rg_i[l], rg_lambda[l], b_ig[l], b_fg[l],
              w_br_a[l], w_br_b[l], w_out[l], g_norm3[l], w_ff2_in[l], w_ff2_out[l])
        yp, sp = _layer(yp, c_prompt, lw, zero_st)
        ys, ss = _layer(ys, c_sample, lw, (state_conv[l], state_rg_h[l], state_C[l], state_n[l], state_m[l]))
        sts_p.append(sp)
        sts_s.append(ss)
    conv_p, h_p, C_p, n_p, m_p = [jnp.stack([s[j] for s in sts_p]) for j in range(5)]
    conv_s, h_s, C_s, n_s, m_s = [jnp.stack([s[j] for s in sts_s]) for j in range(5)]
    yp = _rms(yp, g_final).astype(x_prompt.dtype)
    ys = _rms(ys, g_final).astype(x_sample.dtype)
    return (yp, ys, conv_p, h_p, C_p, n_p, m_p, conv_s, h_s, C_s, n_s, m_s)
```

```python
import functools

import jax
import jax.numpy as jnp
from jax import lax
from jax.experimental import pallas as pl
from jax.experimental.pallas import tpu as pltpu

F32 = jnp.float32
BF16 = jnp.bfloat16

EPS = 1e-6
RG_C = 8.0
CONV_W = 4
N_MOD = 9
M_HEADS = 4
RG_BLOCKS = 8

SUBLANES = 8
LANES = 128
MXU_DIM = 256
VMEM_LIMIT_BYTES = 60 * 1024 * 1024

FFN_ROWS = 512
FFN_COLS = 256
RG_STEPS = 64
ML_ROWS = 256
ML_CHUNK = 128
DEC_BLOCK = 2


def _cparams(sem):
    return pltpu.CompilerParams(dimension_semantics=sem, vmem_limit_bytes=VMEM_LIMIT_BYTES)


def _resident(shape):
    nd = len(shape)
    return pl.BlockSpec(shape, lambda *_: (0,) * nd, pipeline_mode=pl.Buffered(1))


def _rms_mod(x, gain, shift, scale):
    ms = jnp.mean(x * x, axis=-1, keepdims=True)
    return x * lax.rsqrt(ms + EPS) * gain * (1.0 + scale) + shift


def _dot(a, b):
    return jnp.dot(a, b, preferred_element_type=F32)


def _dot_nt(a, b):
    return lax.dot_general(a, b, (((1,), (1,)), ((), ())), preferred_element_type=F32)


def _dot_tn(a, b):
    return lax.dot_general(a, b, (((0,), (0,)), ((), ())), preferred_element_type=F32)


def _split3(x):
    hi = x.astype(BF16)
    r1 = x - hi.astype(F32)
    mid = r1.astype(BF16)
    lo = (r1 - mid.astype(F32)).astype(BF16)
    return hi, mid, lo


def _ada_kernel(c_ref, w_ref, b_ref, o_ref):
    c = c_ref[...]
    sc = (c * jax.nn.sigmoid(c)).astype(BF16)
    o_ref[...] = _dot(sc, w_ref[...].astype(BF16)) + b_ref[...]


def _ada(c, w_ada, b_ada):
    rows, d = c.shape
    n = w_ada.shape[1]
    tn = d
    return pl.pallas_call(
        _ada_kernel,
        grid=(n // tn,),
        in_specs=[pl.BlockSpec((rows, d), lambda j: (0, 0)),
                  pl.BlockSpec((d, tn), lambda j: (0, j)),
                  pl.BlockSpec((1, tn), lambda j: (0, j))],
        out_specs=pl.BlockSpec((rows, tn), lambda j: (0, j)),
        out_shape=jax.ShapeDtypeStruct((rows, n), F32),
        compiler_params=_cparams(("arbitrary",)),
        name="adaln_mod",
    )(c, w_ada, b_ada.reshape(1, n))


def _ffn_kernel(*refs, d_ff, final):
    if final:
        x_ref, sh_ref, sc_ref, ga_ref, g_ref, wi_ref, wo_ref, gf_ref, o_ref, act_ref = refs
    else:
        x_ref, sh_ref, sc_ref, ga_ref, g_ref, wi_ref, wo_ref, o_ref, act_ref = refs
    x = x_ref[...]
    xn = _rms_mod(x, g_ref[...], sh_ref[...], sc_ref[...]).astype(BF16)
    for j in range(d_ff // FFN_COLS):
        lo = j * FFN_COLS
        hg = _dot(xn, wi_ref[:, lo:lo + FFN_COLS])
        hu = _dot(xn, wi_ref[:, d_ff + lo:d_ff + lo + FFN_COLS])
        act_ref[:, lo:lo + FFN_COLS] = (hg * jax.nn.sigmoid(hg) * hu).astype(BF16)
    y = _dot(act_ref[...], wo_ref[...])
    out = x + (0.5 * ga_ref[...]) * y
    if final:
        ms = jnp.mean(out * out, axis=-1, keepdims=True)
        out = out * lax.rsqrt(ms + EPS) * gf_ref[...]
    o_ref[...] = out


def _mod_spec_prompt(j, tiles_per_seq, d):
    return pl.BlockSpec((None, 1, d), lambda i: (i // tiles_per_seq, 0, j))


def _ffn(x2, mod, j0, gain, wi, wo, g_final, *, rows_per_seq, per_row_mod):
    rows, d = x2.shape
    d_ff = wo.shape[0]
    if per_row_mod:
        tm = rows
        mspec = lambda j: pl.BlockSpec((tm, d), lambda i: (0, j))
    else:
        tm = min(FFN_ROWS, rows_per_seq)
        mspec = lambda j: _mod_spec_prompt(j, rows_per_seq // tm, d)
    final = g_final is not None
    in_specs = [pl.BlockSpec((tm, d), lambda i: (i, 0)),
                mspec(j0), mspec(j0 + 1), mspec(j0 + 2),
                _resident((1, d)), _resident(wi.shape), _resident(wo.shape)]
    args = [x2, mod, mod, mod, gain.reshape(1, d), wi, wo]
    if final:
        in_specs.append(_resident((1, d)))
        args.append(g_final.reshape(1, d))
    return pl.pallas_call(
        functools.partial(_ffn_kernel, d_ff=d_ff, final=final),
        grid=(rows // tm,),
        in_specs=in_specs,
        out_specs=pl.BlockSpec((tm, d), lambda i: (i, 0)),
        out_shape=jax.ShapeDtypeStruct((rows, d), F32),
        scratch_shapes=[pltpu.VMEM((tm, d_ff), BF16)],
        compiler_params=_cparams(("arbitrary",)),
        name="ffn_final" if final else "ffn",
    )(*args)


def _rg_gates(u, wa_ref, wi_ref, ba, bi, lam):
    ub = u.astype(BF16)
    ra, ri = [], []
    for p in range(wa_ref.shape[0]):
        blk = ub[:, p * MXU_DIM:(p + 1) * MXU_DIM]
        ra.append(_dot(blk, wa_ref[p]))
        ri.append(_dot(blk, wi_ref[p]))
    r = jax.nn.sigmoid(jnp.concatenate(ra, axis=-1) + ba)
    i_g = jax.nn.sigmoid(jnp.concatenate(ri, axis=-1) + bi)
    log_a = -RG_C * r * jax.nn.softplus(-lam)
    a = jnp.exp(log_a)
    b = jnp.sqrt(-jnp.tanh(log_a) * (a * a + 1.0)) * (i_g * u)
    return a, b


def _rglru_kernel(x_ref, sh_ref, sc_ref, g_ref, w_ref, cw_ref, cb_ref, wa_ref, wi_ref, ba_ref, bi_ref,
                  lam_ref, ha_ref, conv_ref, hl_ref,
                  xn_ref, bm_ref, tb_ref, a_ref, b_ref, hs_ref, h_ref):
    nb, tt, d = x_ref.shape
    tail = (CONV_W - 1) * nb
    rows = nb * tt
    t_idx = pl.program_id(0)

    @pl.when(t_idx == 0)
    def _():
        tb_ref[0:tail, :] = jnp.zeros((tail, d), F32)
        h_ref[...] = jnp.zeros_like(h_ref)

    for bq in range(nb):
        xn = _rms_mod(x_ref[bq], g_ref[...], sh_ref[bq:bq + 1, :], sc_ref[bq:bq + 1, :])
        xn_ref[bq * tt:(bq + 1) * tt, :] = xn.astype(BF16)
    x_rg = _dot(xn_ref[...], w_ref[...])
    for cb in range(d // LANES):
        bm_ref[cb] = x_rg[:, cb * LANES:(cb + 1) * LANES]

    def to_time_major(t, carry):
        dst = pl.multiple_of(tail + t * nb, nb)
        for cb in range(d // LANES):
            tb_ref[pl.ds(dst, nb), cb * LANES:(cb + 1) * LANES] = bm_ref[cb, pl.ds(t, nb, stride=tt), :]
        return carry
    lax.fori_loop(0, tt, to_time_major, 0, unroll=8)

    u = cb_ref[...]
    for j in range(CONV_W):
        u = u + tb_ref[j * nb:j * nb + rows, :] * cw_ref[j:j + 1, :]
    new_tail = tb_ref[rows:rows + tail, :]
    for j in range(CONV_W - 1):
        conv_ref[j] = new_tail[j * nb:(j + 1) * nb, :]
    tb_ref[0:tail, :] = new_tail

    a, b = _rg_gates(u, wa_ref, wi_ref, ba_ref[...], bi_ref[...], lam_ref[...])
    a_ref[...] = a
    b_ref[...] = b

    def step(t, h):
        src = pl.multiple_of(t * nb, nb)
        h = a_ref[pl.ds(src, nb), :] * h + b_ref[pl.ds(src, nb), :]
        for cb in range(d // LANES):
            hs_ref[cb, pl.ds(src, nb), :] = h[:, cb * LANES:(cb + 1) * LANES]
        return h
    h_last = lax.fori_loop(0, tt, step, h_ref[...], unroll=8)
    h_ref[...] = h_last
    hl_ref[...] = h_last

    for bq in range(nb):
        for cb in range(d // LANES):
            ha_ref[bq, :, cb * LANES:(cb + 1) * LANES] = hs_ref[cb, pl.ds(bq, tt, stride=nb), :].astype(BF16)


def _rglru_prompt(x3, mod2, gain, w_rgx, conv_w, conv_b, wa, wi, ba, bi, lam):
    nb, seq, d = x3.shape
    tt = min(RG_STEPS, seq)
    rows = nb * tt
    tail = (CONV_W - 1) * nb
    row = lambda v: v.reshape(1, d)
    return pl.pallas_call(
        _rglru_kernel,
        grid=(seq // tt,),
        in_specs=[pl.BlockSpec((nb, tt, d), lambda t: (0, t, 0)),
                  pl.BlockSpec((nb, d), lambda t: (0, 3)),
                  pl.BlockSpec((nb, d), lambda t: (0, 4)),
                  _resident((1, d)), _resident(w_rgx.shape), _resident(conv_w.shape), _resident((1, d)),
                  _resident(wa.shape), _resident(wi.shape), _resident((1, d)), _resident((1, d)),
                  _resident((1, d))],
        out_specs=[pl.BlockSpec((nb, tt, d), lambda t: (0, t, 0)),
                   pl.BlockSpec((CONV_W - 1, nb, d), lambda t: (0, 0, 0)),
                   pl.BlockSpec((nb, d), lambda t: (0, 0))],
        out_shape=[jax.ShapeDtypeStruct((nb, seq, d), BF16),
                   jax.ShapeDtypeStruct((CONV_W - 1, nb, d), F32),
                   jax.ShapeDtypeStruct((nb, d), F32)],
        scratch_shapes=[pltpu.VMEM((rows, d), BF16),
                        pltpu.VMEM((d // LANES, rows, LANES), F32),
                        pltpu.VMEM((tail + rows, d), F32),
                        pltpu.VMEM((rows, d), F32),
                        pltpu.VMEM((rows, d), F32),
                        pltpu.VMEM((d // LANES, rows, LANES), F32),
                        pltpu.VMEM((nb, d), F32)],
        compiler_params=_cparams(("arbitrary",)),
        name="rglru_prompt",
    )(x3, mod2, mod2, row(gain), w_rgx, conv_w, row(conv_b), wa, wi, row(ba), row(bi), row(lam))


def _mlstm_kernel(x_ref, ha_ref, sh_ref, sc_ref, ga_ref, g_ref, wm_ref, wif_ref, wift_ref, bif_ref, bift_ref,
                  wbra_ref, wbrb_ref, wout_ref, o_ref, c_ref, n_ref, m_ref, hm_ref):
    tl, d = x_ref.shape
    dk = d // M_HEADS
    cs = min(ML_CHUNK, tl)
    scale = dk ** -0.5

    @pl.when(pl.program_id(1) == 0)
    def _():
        c_ref[...] = jnp.zeros_like(c_ref)
        n_ref[...] = jnp.zeros_like(n_ref)
        m_ref[...] = jnp.zeros_like(m_ref)

    x = x_ref[...]
    xn = _rms_mod(x, g_ref[...], sh_ref[...], sc_ref[...]).astype(BF16)

    pre_c = _dot(xn, wif_ref[...]) + bif_ref[...]
    pre_r = _dot_nt(wift_ref[...], xn) + bift_ref[...]
    col_is_f = lax.broadcasted_iota(jnp.int32, pre_c.shape, 1) >= M_HEADS
    row_is_f = lax.broadcasted_iota(jnp.int32, pre_r.shape, 0) >= M_HEADS
    gate_c = jnp.where(col_is_f, jax.nn.log_sigmoid(pre_c), pre_c)
    gate_r = jnp.where(row_is_f, jax.nn.log_sigmoid(pre_r), pre_r)

    ti = lax.broadcasted_iota(jnp.int32, (cs, cs), 0)
    si = lax.broadcasted_iota(jnp.int32, (cs, cs), 1)
    causal = si <= ti
    lower = causal.astype(BF16)
    upper = (ti <= si).astype(BF16)

    chunks = []
    for c in range(tl // cs):
        gc = gate_c[c * cs:(c + 1) * cs, :]
        gr = gate_r[:, c * cs:(c + 1) * cs]
        cum_c = sum(_dot(lower, part) for part in _split3(gc))
        cum_r = sum(_dot(part, upper) for part in _split3(gr))
        chunks.append((gc, gr, cum_c, cum_r))

    for h in range(M_HEADS):
        c0 = h * dk
        q_all = _dot(xn, wm_ref[:, c0:c0 + dk])
        k_all = _dot(xn, wm_ref[:, d + c0:d + c0 + dk])
        v_all = _dot(xn, wm_ref[:, 2 * d + c0:2 * d + c0 + dk])
        for c, (gc, gr, cum_c, cum_r) in enumerate(chunks):
            r0 = c * cs
            q = q_all[r0:r0 + cs, :]
            k = k_all[r0:r0 + cs, :]
            v = v_all[r0:r0 + cs, :]
            bc = cum_c[:, M_HEADS + h:M_HEADS + h + 1]
            br = cum_r[M_HEADS + h:M_HEADS + h + 1, :]
            ig_c = gc[:, h:h + 1]
            ig_r = gr[h:h + 1, :]
            m_prev = m_ref[h:h + 1, :]
            c_prev = c_ref[h]
            n_prev = n_ref[h:h + 1, :]

            log_d = jnp.where(causal, bc - br + ig_r, -jnp.inf)
            log_past = bc + m_prev
            m_t = jnp.maximum(log_past, jnp.max(log_d, axis=-1, keepdims=True))
            d_mat = jnp.exp(log_d - m_t)
            past_w = jnp.exp(log_past - m_t)
            qs = q * scale
            qsb = qs.astype(BF16)
            kb = k.astype(BF16)
            vb = v.astype(BF16)
            s = _dot_nt(qsb, kb) * d_mat
            num = past_w * _dot(qsb, c_prev.astype(BF16)) + _dot(s.astype(BF16), vb)
            den = past_w * jnp.sum(qs * n_prev, axis=-1, keepdims=True) + jnp.sum(s, axis=-1, keepdims=True)
            hm_ref[r0:r0 + cs, c0:c0 + dk] = num / jnp.maximum(jnp.abs(den), jnp.exp(-m_t))

            m_new = m_t[cs - 1:cs, :]
            b_last = bc[cs - 1:cs, :]
            w_s = jnp.exp(b_last - bc + ig_c - m_new)
            decay = jnp.exp(b_last + m_prev - m_new)
            kw = w_s * k
            c_ref[h] = decay * c_prev + _dot_tn(kw.astype(BF16), vb)
            n_ref[h:h + 1, :] = decay * n_prev + jnp.sum(kw, axis=0, keepdims=True)
            m_ref[h:h + 1, :] = m_new

    o_pre = _dot(xn, wm_ref[:, 3 * d:4 * d])
    hb = (jax.nn.sigmoid(o_pre) * hm_ref[...]).astype(BF16)
    g_a = _dot(xn, wm_ref[:, 4 * d:5 * d])
    g_b = _dot(xn, wm_ref[:, 5 * d:6 * d])
    merged = (jax.nn.sigmoid(g_a) * _dot(ha_ref[...], wbra_ref[...])
              + jax.nn.sigmoid(g_b) * _dot(hb, wbrb_ref[...]))
    o_ref[...] = x + ga_ref[...] * _dot(merged.astype(BF16), wout_ref[...])


def _mlstm_prompt(x2, ha2, mod3, gain, wm, wif, wift, bif, bift, wbra, wbrb, wout, *, nb, seq):
    rows, d = x2.shape
    dk = d // M_HEADS
    tl = min(ML_ROWS, seq)
    nt = seq // tl
    mspec = lambda j: pl.BlockSpec((None, 1, d), lambda b, t: (b, 0, j))
    tile = lambda: pl.BlockSpec((tl, d), lambda b, t: (b * nt + t, 0))
    return pl.pallas_call(
        _mlstm_kernel,
        grid=(nb, nt),
        in_specs=[tile(), tile(), mspec(3), mspec(4), mspec(5), _resident((1, d)),
                  _resident(wm.shape), _resident(wif.shape), _resident(wift.shape),
                  _resident(bif.shape), _resident(bift.shape),
                  _resident(wbra.shape), _resident(wbrb.shape), _resident(wout.shape)],
        out_specs=[tile(),
                   pl.BlockSpec((None, M_HEADS, dk, dk), lambda b, t: (b, 0, 0, 0)),
                   pl.BlockSpec((None, M_HEADS, dk), lambda b, t: (b, 0, 0)),
                   pl.BlockSpec((None, M_HEADS, 1), lambda b, t: (b, 0, 0))],
        out_shape=[jax.ShapeDtypeStruct((rows, d), F32),
                   jax.ShapeDtypeStruct((nb, M_HEADS, dk, dk), F32),
                   jax.ShapeDtypeStruct((nb, M_HEADS, dk), F32),
                   jax.ShapeDtypeStruct((nb, M_HEADS, 1), F32)],
        scratch_shapes=[pltpu.VMEM((tl, d), F32)],
        compiler_params=_cparams(("arbitrary", "arbitrary")),
        name="mlstm_prompt",
    )(x2, ha2, mod3, mod3, mod3, gain.reshape(1, d), wm, wif, wift, bif, bift, wbra, wbrb, wout)


def _head_sum(x, dk):
    parts = []
    for h in range(x.shape[1] // dk):
        sl = x[:, h * dk:(h + 1) * dk]
        parts.append(jnp.broadcast_to(jnp.sum(sl, axis=-1, keepdims=True), sl.shape))
    return jnp.concatenate(parts, axis=-1)


def _head_spread(cols, lane0, dk, rows):
    return jnp.concatenate(
        [jnp.broadcast_to(cols[:, lane0 + h:lane0 + h + 1], (rows, dk)) for h in range(M_HEADS)], axis=-1)


def _dec_pre_kernel(x_ref, sh_ref, sc_ref, g_ref, wrg_ref, wm_ref, wqt_ref, wkt_ref, wif_ref, bif_ref,
                    cw_ref, cb_ref, wa_ref, wi_ref, ba_ref, bi_ref, lam_ref,
                    conv0_ref, h0_ref, n0_ref, m0_ref,
                    ha_ref, conv_ref, hs_ref, n_ref, m_ref, qt_ref, kt_ref, dec_ref, wv_ref, pw_ref, sv_ref,
                    den_ref, em_ref):
    rows, d = x_ref.shape
    dk = d // M_HEADS
    scale = dk ** -0.5
    xn = _rms_mod(x_ref[...], g_ref[...], sh_ref[...], sc_ref[...]).astype(BF16)

    x_rg = _dot(xn, wrg_ref[...])
    u = cb_ref[...]
    for j in range(CONV_W - 1):
        u = u + conv0_ref[j] * cw_ref[j:j + 1, :]
        if j > 0:
            conv_ref[j - 1] = conv0_ref[j]
    u = u + x_rg * cw_ref[CONV_W - 1:CONV_W, :]
    conv_ref[CONV_W - 2] = x_rg
    a, b = _rg_gates(u, wa_ref, wi_ref, ba_ref[...], bi_ref[...], lam_ref[...])
    h = a * h0_ref[...] + b
    hs_ref[...] = h
    ha_ref[...] = h

    q = _dot(xn, wm_ref[:, 0:d])
    k = _dot(xn, wm_ref[:, d:2 * d])
    v = _dot(xn, wm_ref[:, 2 * d:3 * d])
    qt_ref[...] = _dot_nt(wqt_ref[...], xn).astype(BF16)
    kt_ref[...] = _dot_nt(wkt_ref[...], xn).astype(BF16)
    pre = _dot(xn, wif_ref[...]) + bif_ref[...]
    ig = _head_spread(pre, 0, dk, rows)
    lf = jax.nn.log_sigmoid(_head_spread(pre, M_HEADS, dk, rows))
    m0 = _head_spread(m0_ref[...], 0, dk, rows)
    n0 = n0_ref[...]
    log_past = lf + m0
    m_t = jnp.maximum(log_past, ig)
    d_w = jnp.exp(ig - m_t)
    past_w = jnp.exp(log_past - m_t)
    qs = q * scale
    s = _head_sum(qs * k, dk) * d_w
    den_ref[...] = past_w * _head_sum(qs * n0, dk) + s
    em_ref[...] = jnp.exp(-m_t)
    pw_ref[...] = past_w
    sv_ref[...] = s * v
    w_s = jnp.exp(ig - m_t)
    decay = jnp.exp(lf + m0 - m_t)
    dec_ref[...] = decay
    wv_ref[...] = w_s * v
    n_ref[...] = decay * n0 + w_s * k
    m_ref[...] = m_t


def _dec_pre(x2, mod, gain, w_rgx, wm, wqt, wkt, wif, bif, conv_w, conv_b, wa, wi, ba, bi, lam,
             conv0, h0, n0, m0):
    rows, d = x2.shape
    row = lambda v: v.reshape(1, d)
    m0p = jnp.pad(m0, ((0, 0), (0, LANES - M_HEADS)))
    full = lambda shape: pl.BlockSpec(shape, lambda i, _n=len(shape): (0,) * _n)
    mspec = lambda j: pl.BlockSpec((rows, d), lambda i: (0, j))
    vec = jax.ShapeDtypeStruct((rows, d), F32)
    outs = [vec,
            jax.ShapeDtypeStruct((CONV_W - 1, rows, d), F32),
            vec,
            vec,
            vec,
            jax.ShapeDtypeStruct((d, rows), BF16),
            jax.ShapeDtypeStruct((d, rows), BF16),
            vec, vec, vec, vec, vec, vec]
    args = [x2, mod, mod, row(gain), w_rgx, wm, wqt, wkt, wif, bif, conv_w, row(conv_b), wa, wi,
            row(ba), row(bi), row(lam), conv0, h0, n0, m0p]
    in_specs = [full(x2.shape), mspec(3), mspec(4)] + [full(a.shape) for a in args[3:]]
    return pl.pallas_call(
        _dec_pre_kernel,
        grid=(1,),
        in_specs=in_specs,
        out_specs=[full(o.shape) for o in outs],
        out_shape=outs,
        compiler_params=_cparams(("arbitrary",)),
        name="decode_pre",
    )(*args)


def _dec_mem_kernel(qt_ref, kt_ref, dec_ref, wv_ref, c0_ref, c_ref, qc_ref):
    bb = c0_ref.shape[0]
    dk = c0_ref.shape[2]
    nseq = qt_ref.shape[1]
    scale = dk ** -0.5
    base = pl.program_id(0) * bb
    seq_id = lax.broadcasted_iota(jnp.int32, (nseq, dk), 0)
    for j in range(bb):
        pick = (seq_id == base + j).astype(BF16)
        qcol = _dot(qt_ref[...], pick) * scale
        kcol = _dot(kt_ref[...], pick)
        for h in range(M_HEADS):
            c0 = c0_ref[j, h]
            dec = dec_ref[j, :, h * dk:(h + 1) * dk]
            wv = wv_ref[j, :, h * dk:(h + 1) * dk]
            c_ref[j, h] = dec * c0 + kcol[h * dk:(h + 1) * dk, :] * wv
            qc_ref[j, :, h * dk:(h + 1) * dk] = jnp.sum(qcol[h * dk:(h + 1) * dk, :] * c0, axis=0, keepdims=True)


def _dec_mem(qt, kt, decay, wv, c0):
    nseq, heads, dk, _ = c0.shape
    d = heads * dk
    bb = DEC_BLOCK if nseq % DEC_BLOCK == 0 else 1
    rowblk = pl.BlockSpec((bb, 1, d), lambda i: (i, 0, 0))
    return pl.pallas_call(
        _dec_mem_kernel,
        grid=(nseq // bb,),
        in_specs=[pl.BlockSpec((d, nseq), lambda i: (0, 0)), pl.BlockSpec((d, nseq), lambda i: (0, 0)),
                  rowblk, rowblk,
                  pl.BlockSpec((bb, heads, dk, dk), lambda i: (i, 0, 0, 0))],
        out_specs=[pl.BlockSpec((bb, heads, dk, dk), lambda i: (i, 0, 0, 0)), rowblk],
        out_shape=[jax.ShapeDtypeStruct(c0.shape, F32), jax.ShapeDtypeStruct((nseq, 1, d), F32)],
        compiler_params=_cparams(("arbitrary",)),
        name="decode_mem",
    )(qt, kt, decay.reshape(nseq, 1, d), wv.reshape(nseq, 1, d), c0)


def _dec_post_kernel(x_ref, sh_ref, sc_ref, ga_ref, g_ref, wm_ref, wbra_ref, wbrb_ref, wout_ref,
                     ha_ref, qc_ref, pw_ref, sv_ref, den_ref, em_ref, o_ref):
    d = x_ref.shape[1]
    x = x_ref[...]
    xn = _rms_mod(x, g_ref[...], sh_ref[...], sc_ref[...]).astype(BF16)
    num = pw_ref[...] * qc_ref[...] + sv_ref[...]
    hm = num / jnp.maximum(jnp.abs(den_ref[...]), em_ref[...])
    hb = (jax.nn.sigmoid(_dot(xn, wm_ref[:, 3 * d:4 * d])) * hm).astype(BF16)
    g_a = _dot(xn, wm_ref[:, 4 * d:5 * d])
    g_b = _dot(xn, wm_ref[:, 5 * d:6 * d])
    merged = (jax.nn.sigmoid(g_a) * _dot(ha_ref[...].astype(BF16), wbra_ref[...])
              + jax.nn.sigmoid(g_b) * _dot(hb, wbrb_ref[...]))
    o_ref[...] = x + ga_ref[...] * _dot(merged.astype(BF16), wout_ref[...])


def _dec_post(x2, mod, gain, wm, wbra, wbrb, wout, ha, qc, pw, sv, den, em):
    rows, d = x2.shape
    full = lambda shape: pl.BlockSpec(shape, lambda i, _n=len(shape): (0,) * _n)
    mspec = lambda j: pl.BlockSpec((rows, d), lambda i: (0, j))
    args = [x2, mod, mod, mod, gain.reshape(1, d), wm, wbra, wbrb, wout, ha, qc, pw, sv, den, em]
    in_specs = [full(x2.shape), mspec(3), mspec(4), mspec(5)] + [full(a.shape) for a in args[4:]]
    return pl.pallas_call(
        _dec_post_kernel,
        grid=(1,),
        in_specs=in_specs,
        out_specs=full((rows, d)),
        out_shape=jax.ShapeDtypeStruct((rows, d), F32),
        compiler_params=_cparams(("arbitrary",)),
        name="decode_post",
    )(*args)


def _pair_blocks(w):
    nblk, c, _ = w.shape
    z = jnp.zeros((nblk // 2, c, c), w.dtype)
    top = jnp.concatenate([w[0::2], z], axis=2)
    bot = jnp.concatenate([z, w[1::2]], axis=2)
    return jnp.concatenate([top, bot], axis=1).astype(BF16)


def kernel(x_prompt, x_sample, state_conv, state_rg_h, state_C, state_n, state_m, c_prompt, c_sample, w_ada, b_ada, g_norm1, w_ff1_in, w_ff1_out, g_norm2, w_in, conv_w, conv_b, w_rg_a, b_rg_a, w_rg_i, b_rg_i, rg_lambda, b_ig, b_fg, w_br_a, w_br_b, w_out, g_norm3, w_ff2_in, w_ff2_out, g_final):
    nb, seq, d = x_prompt.shape
    ns = x_sample.shape[0]
    depth = w_ada.shape[0]
    assert depth == 1 and x_sample.shape[1] == 1 and nb == SUBLANES
    assert seq % RG_STEPS == 0 and seq % ML_ROWS == 0 and seq % FFN_ROWS == 0
    heads, dk = M_HEADS, d // M_HEADS

    w_in0 = w_in[0]
    seg = lambda i: w_in0[:, i * d:(i + 1) * d]
    gate0 = 5 * d
    w_rgx = seg(0).astype(BF16)
    w_gates = w_in0[:, gate0:gate0 + 2 * heads]
    rest = w_in0[:, gate0 + 2 * heads:]
    wm = jnp.concatenate([seg(1), seg(2), seg(3), seg(4), rest], axis=1).astype(BF16)
    wif = jnp.pad(w_gates, ((0, 0), (0, LANES - 2 * heads))).astype(BF16)
    wift = w_gates.T.astype(BF16)
    b_gates = jnp.concatenate([b_ig[0], b_fg[0]])
    bif = jnp.pad(b_gates, (0, LANES - 2 * heads)).reshape(1, LANES)
    bift = b_gates.reshape(2 * heads, 1)
    wqt = seg(1).T.astype(BF16)
    wkt = seg(2).T.astype(BF16)
    wa = _pair_blocks(w_rg_a[0])
    wi = _pair_blocks(w_rg_i[0])
    wf1i, wf1o = w_ff1_in[0].astype(BF16), w_ff1_out[0].astype(BF16)
    wf2i, wf2o = w_ff2_in[0].astype(BF16), w_ff2_out[0].astype(BF16)
    wbra, wbrb, wout = w_br_a[0].astype(BF16), w_br_b[0].astype(BF16), w_out[0].astype(BF16)

    mod = _ada(jnp.concatenate([c_prompt, c_sample], axis=0), w_ada[0], b_ada[0])
    mod_p2 = mod[:nb]
    mod_p3 = mod_p2.reshape(nb, 1, N_MOD * d)
    mod_s = mod[nb:]

    xp = x_prompt.reshape(nb * seq, d)
    xp = _ffn(xp, mod_p3, 0, g_norm1[0], wf1i, wf1o, None, rows_per_seq=seq, per_row_mod=False)
    ha, conv_p, h_p = _rglru_prompt(xp.reshape(nb, seq, d), mod_p2, g_norm2[0], w_rgx, conv_w[0], conv_b[0],
                                    wa, wi, b_rg_a[0], b_rg_i[0], rg_lambda[0])
    xp, c_p, n_p, m_p = _mlstm_prompt(xp, ha.reshape(nb * seq, d), mod_p3, g_norm2[0], wm, wif, wift, bif, bift,
                                      wbra, wbrb, wout, nb=nb, seq=seq)
    yp = _ffn(xp, mod_p3, 6, g_norm3[0], wf2i, wf2o, g_final, rows_per_seq=seq, per_row_mod=False)

    xs = x_sample.reshape(ns, d)
    xs = _ffn(xs, mod_s, 0, g_norm1[0], wf1i, wf1o, None, rows_per_seq=1, per_row_mod=True)
    conv0 = jnp.swapaxes(state_conv[0], 0, 1)
    (ha_s, conv_s, h_s, n_s, m_s, qt, kt, decay, wv, pw, sv, den, em) = _dec_pre(
        xs, mod_s, g_norm2[0], w_rgx, wm, wqt, wkt, wif, bif, conv_w[0], conv_b[0], wa, wi,
        b_rg_a[0], b_rg_i[0], rg_lambda[0], conv0, state_rg_h[0], state_n[0].reshape(ns, d), state_m[0])
    c_s, qc = _dec_mem(qt, kt, decay, wv, state_C[0])
    xs = _dec_post(xs, mod_s, g_norm2[0], wm, wbra, wbrb, wout, ha_s, qc.reshape(ns, d), pw, sv, den, em)
    ys = _ffn(xs, mod_s, 6, g_norm3[0], wf2i, wf2o, g_final, rows_per_seq=1, per_row_mod=True)

    return (yp.reshape(nb, seq, d), ys.reshape(ns, 1, d),
            jnp.swapaxes(conv_p, 0, 1)[None], h_p[None], c_p[None], n_p[None], m_p.reshape(1, nb, heads),
            jnp.swapaxes(conv_s, 0, 1)[None], h_s[None], c_s[None], n_s.reshape(1, ns, heads, dk), m_s[:, ::dk][None])
```

```python
import functools

import jax
import jax.numpy as jnp
from jax import lax
from jax.experimental import pallas as pl
from jax.experimental.pallas import tpu as pltpu

F32 = jnp.float32
BF16 = jnp.bfloat16

EPS = 1e-6
RG_C = 8.0
CONV_W = 4
N_MOD = 9
M_HEADS = 4
RG_BLOCKS = 8

SUBLANES = 8
LANES = 128
MXU_DIM = 256
VMEM_LIMIT_BYTES = 60 * 1024 * 1024

FFN_ROWS = 1024
FFN_COLS = 256
RG_STEPS = 64
ML_ROWS = 512
ML_CHUNK = 256
DEC_BLOCK = 2


def _cparams(sem):
    return pltpu.CompilerParams(dimension_semantics=sem, vmem_limit_bytes=VMEM_LIMIT_BYTES)


def _resident(shape):
    nd = len(shape)
    return pl.BlockSpec(shape, lambda *_: (0,) * nd, pipeline_mode=pl.Buffered(1))


def _rms_mod(x, gain, shift, scale):
    ms = jnp.mean(x * x, axis=-1, keepdims=True)
    return x * lax.rsqrt(ms + EPS) * gain * (1.0 + scale) + shift


def _dot(a, b):
    return jnp.dot(a, b, preferred_element_type=F32)


def _dot_nt(a, b):
    return lax.dot_general(a, b, (((1,), (1,)), ((), ())), preferred_element_type=F32)


def _dot_tn(a, b):
    return lax.dot_general(a, b, (((0,), (0,)), ((), ())), preferred_element_type=F32)


def _split3(x):
    hi = x.astype(BF16)
    r1 = x - hi.astype(F32)
    mid = r1.astype(BF16)
    lo = (r1 - mid.astype(F32)).astype(BF16)
    return hi, mid, lo


def _ada_kernel(c_ref, w_ref, b_ref, o_ref):
    c = c_ref[...]
    sc = (c * jax.nn.sigmoid(c)).astype(BF16)
    o_ref[...] = _dot(sc, w_ref[...].astype(BF16)) + b_ref[...]


def _ada(c, w_ada, b_ada):
    rows, d = c.shape
    n = w_ada.shape[1]
    tn = d
    return pl.pallas_call(
        _ada_kernel,
        grid=(n // tn,),
        in_specs=[pl.BlockSpec((rows, d), lambda j: (0, 0)),
                  pl.BlockSpec((d, tn), lambda j: (0, j)),
                  pl.BlockSpec((1, tn), lambda j: (0, j))],
        out_specs=pl.BlockSpec((rows, tn), lambda j: (0, j)),
        out_shape=jax.ShapeDtypeStruct((rows, n), F32),
        compiler_params=_cparams(("arbitrary",)),
        name="adaln_mod",
    )(c, w_ada, b_ada.reshape(1, n))


def _ffn_kernel(*refs, d_ff, final):
    if final:
        x_ref, sh_ref, sc_ref, ga_ref, g_ref, wi_ref, wo_ref, gf_ref, o_ref, act_ref = refs
    else:
        x_ref, sh_ref, sc_ref, ga_ref, g_ref, wi_ref, wo_ref, o_ref, act_ref = refs
    x = x_ref[...]
    xn = _rms_mod(x, g_ref[...], sh_ref[...], sc_ref[...]).astype(BF16)
    for j in range(d_ff // FFN_COLS):
        lo = j * FFN_COLS
        hg = _dot(xn, wi_ref[:, lo:lo + FFN_COLS])
        hu = _dot(xn, wi_ref[:, d_ff + lo:d_ff + lo + FFN_COLS])
        act_ref[:, lo:lo + FFN_COLS] = (hg * jax.nn.sigmoid(hg) * hu).astype(BF16)
    y = _dot(act_ref[...], wo_ref[...])
    out = x + (0.5 * ga_ref[...]) * y
    if final:
        ms = jnp.mean(out * out, axis=-1, keepdims=True)
        out = out * lax.rsqrt(ms + EPS) * gf_ref[...]
    o_ref[...] = out


def _mod_spec_prompt(j, tiles_per_seq, d):
    return pl.BlockSpec((None, 1, d), lambda i: (i // tiles_per_seq, 0, j))


def _ffn(x2, mod, j0, gain, wi, wo, g_final, *, rows_per_seq, per_row_mod):
    rows, d = x2.shape
    d_ff = wo.shape[0]
    if per_row_mod:
        tm = rows
        mspec = lambda j: pl.BlockSpec((tm, d), lambda i: (0, j))
    else:
        tm = min(FFN_ROWS, rows_per_seq)
        mspec = lambda j: _mod_spec_prompt(j, rows_per_seq // tm, d)
    final = g_final is not None
    in_specs = [pl.BlockSpec((tm, d), lambda i: (i, 0)),
                mspec(j0), mspec(j0 + 1), mspec(j0 + 2),
                _resident((1, d)), _resident(wi.shape), _resident(wo.shape)]
    args = [x2, mod, mod, mod, gain.reshape(1, d), wi, wo]
    if final:
        in_specs.append(_resident((1, d)))
        args.append(g_final.reshape(1, d))
    return pl.pallas_call(
        functools.partial(_ffn_kernel, d_ff=d_ff, final=final),
        grid=(rows // tm,),
        in_specs=in_specs,
        out_specs=pl.BlockSpec((tm, d), lambda i: (i, 0)),
        out_shape=jax.ShapeDtypeStruct((rows, d), F32),
        scratch_shapes=[pltpu.VMEM((tm, d_ff), BF16)],
        compiler_params=_cparams(("arbitrary",)),
        name="ffn_final" if final else "ffn",
    )(*args)


def _rg_gates(u, wa_ref, wi_ref, ba, bi, lam):
    ub = u.astype(BF16)
    ra, ri = [], []
    for p in range(wa_ref.shape[0]):
        blk = ub[:, p * MXU_DIM:(p + 1) * MXU_DIM]
        ra.append(_dot(blk, wa_ref[p]))
        ri.append(_dot(blk, wi_ref[p]))
    r = jax.nn.sigmoid(jnp.concatenate(ra, axis=-1) + ba)
    i_g = jax.nn.sigmoid(jnp.concatenate(ri, axis=-1) + bi)
    log_a = -RG_C * r * jax.nn.softplus(-lam)
    a = jnp.exp(log_a)
    b = jnp.sqrt(-jnp.tanh(log_a) * (a * a + 1.0)) * (i_g * u)
    return a, b


def _rglru_kernel(x_ref, sh_ref, sc_ref, g_ref, w_ref, cw_ref, cb_ref, wa_ref, wi_ref, ba_ref, bi_ref,
                  lam_ref, ha_ref, conv_ref, hl_ref,
                  xn_ref, bm_ref, tb_ref, a_ref, b_ref, hs_ref, h_ref):
    nb, tt, d = x_ref.shape
    tail = (CONV_W - 1) * nb
    rows = nb * tt
    t_idx = pl.program_id(0)

    @pl.when(t_idx == 0)
    def _():
        tb_ref[0:tail, :] = jnp.zeros((tail, d), F32)
        h_ref[...] = jnp.zeros_like(h_ref)

    for bq in range(nb):
        xn = _rms_mod(x_ref[bq], g_ref[...], sh_ref[bq:bq + 1, :], sc_ref[bq:bq + 1, :])
        xn_ref[bq * tt:(bq + 1) * tt, :] = xn.astype(BF16)
    x_rg = _dot(xn_ref[...], w_ref[...])
    for cb in range(d // LANES):
        bm_ref[cb] = x_rg[:, cb * LANES:(cb + 1) * LANES]

    def to_time_major(t, carry):
        dst = pl.multiple_of(tail + t * nb, nb)
        for cb in range(d // LANES):
            tb_ref[pl.ds(dst, nb), cb * LANES:(cb + 1) * LANES] = bm_ref[cb, pl.ds(t, nb, stride=tt), :]
        return carry
    lax.fori_loop(0, tt, to_time_major, 0, unroll=8)

    u = cb_ref[...]
    for j in range(CONV_W):
        u = u + tb_ref[j * nb:j * nb + rows, :] * cw_ref[j:j + 1, :]
    new_tail = tb_ref[rows:rows + tail, :]
    for j in range(CONV_W - 1):
        conv_ref[j] = new_tail[j * nb:(j + 1) * nb, :]
    tb_ref[0:tail, :] = new_tail

    a, b = _rg_gates(u, wa_ref, wi_ref, ba_ref[...], bi_ref[...], lam_ref[...])
    a_ref[...] = a
    b_ref[...] = b

    def step(t, h):
        src = pl.multiple_of(t * nb, nb)
        h = a_ref[pl.ds(src, nb), :] * h + b_ref[pl.ds(src, nb), :]
        for cb in range(d // LANES):
            hs_ref[cb, pl.ds(src, nb), :] = h[:, cb * LANES:(cb + 1) * LANES]
        return h
    h_last = lax.fori_loop(0, tt, step, h_ref[...], unroll=8)
    h_ref[...] = h_last
    hl_ref[...] = h_last

    for bq in range(nb):
        for cb in range(d // LANES):
            ha_ref[bq, :, cb * LANES:(cb + 1) * LANES] = hs_ref[cb, pl.ds(bq, tt, stride=nb), :].astype(BF16)


def _rglru_prompt(x3, mod2, gain, w_rgx, conv_w, conv_b, wa, wi, ba, bi, lam):
    nb, seq, d = x3.shape
    tt = min(RG_STEPS, seq)
    rows = nb * tt
    tail = (CONV_W - 1) * nb
    row = lambda v: v.reshape(1, d)
    return pl.pallas_call(
        _rglru_kernel,
        grid=(seq // tt,),
        in_specs=[pl.BlockSpec((nb, tt, d), lambda t: (0, t, 0)),
                  pl.BlockSpec((nb, d), lambda t: (0, 3)),
                  pl.BlockSpec((nb, d), lambda t: (0, 4)),
                  _resident((1, d)), _resident(w_rgx.shape), _resident(conv_w.shape), _resident((1, d)),
                  _resident(wa.shape), _resident(wi.shape), _resident((1, d)), _resident((1, d)),
                  _resident((1, d))],
        out_specs=[pl.BlockSpec((nb, tt, d), lambda t: (0, t, 0)),
                   pl.BlockSpec((CONV_W - 1, nb, d), lambda t: (0, 0, 0)),
                   pl.BlockSpec((nb, d), lambda t: (0, 0))],
        out_shape=[jax.ShapeDtypeStruct((nb, seq, d), BF16),
                   jax.ShapeDtypeStruct((CONV_W - 1, nb, d), F32),
                   jax.ShapeDtypeStruct((nb, d), F32)],
        scratch_shapes=[pltpu.VMEM((rows, d), BF16),
                        pltpu.VMEM((d // LANES, rows, LANES), F32),
                        pltpu.VMEM((tail + rows, d), F32),
                        pltpu.VMEM((rows, d), F32),
                        pltpu.VMEM((rows, d), F32),
                        pltpu.VMEM((d // LANES, rows, LANES), F32),
                        pltpu.VMEM((nb, d), F32)],
        compiler_params=_cparams(("arbitrary",)),
        name="rglru_prompt",
    )(x3, mod2, mod2, row(gain), w_rgx, conv_w, row(conv_b), wa, wi, row(ba), row(bi), row(lam))


def _mlstm_kernel(x_ref, ha_ref, sh_ref, sc_ref, ga_ref, g_ref, wm_ref, wif_ref, wift_ref, bif_ref, bift_ref,
                  wbra_ref, wbrb_ref, wout_ref, o_ref, c_ref, n_ref, m_ref, hm_ref):
    tl, d = x_ref.shape
    dk = d // M_HEADS
    cs = min(ML_CHUNK, tl)
    scale = dk ** -0.5

    @pl.when(pl.program_id(1) == 0)
    def _():
        c_ref[...] = jnp.zeros_like(c_ref)
        n_ref[...] = jnp.zeros_like(n_ref)
        m_ref[...] = jnp.zeros_like(m_ref)

    x = x_ref[...]
    xn = _rms_mod(x, g_ref[...], sh_ref[...], sc_ref[...]).astype(BF16)

    pre_c = _dot(xn, wif_ref[...]) + bif_ref[...]
    pre_r = _dot_nt(wift_ref[...], xn) + bift_ref[...]
    col_is_f = lax.broadcasted_iota(jnp.int32, pre_c.shape, 1) >= M_HEADS
    row_is_f = lax.broadcasted_iota(jnp.int32, pre_r.shape, 0) >= M_HEADS
    gate_c = jnp.where(col_is_f, jax.nn.log_sigmoid(pre_c), pre_c)
    gate_r = jnp.where(row_is_f, jax.nn.log_sigmoid(pre_r), pre_r)

    ti = lax.broadcasted_iota(jnp.int32, (cs, cs), 0)
    si = lax.broadcasted_iota(jnp.int32, (cs, cs), 1)
    causal = si <= ti
    lower = causal.astype(BF16)
    upper = (ti <= si).astype(BF16)

    chunks = []
    for c in range(tl // cs):
        gc = gate_c[c * cs:(c + 1) * cs, :]
        gr = gate_r[:, c * cs:(c + 1) * cs]
        cum_c = sum(_dot(lower, part) for part in _split3(gc))
        cum_r = sum(_dot(part, upper) for part in _split3(gr))
        chunks.append((gc, gr, cum_c, cum_r))

    for h in range(M_HEADS):
        c0 = h * dk
        q_all = _dot(xn, wm_ref[:, c0:c0 + dk])
        k_all = _dot(xn, wm_ref[:, d + c0:d + c0 + dk])
        v_all = _dot(xn, wm_ref[:, 2 * d + c0:2 * d + c0 + dk])
        for c, (gc, gr, cum_c, cum_r) in enumerate(chunks):
            r0 = c * cs
            q = q_all[r0:r0 + cs, :]
            k = k_all[r0:r0 + cs, :]
            v = v_all[r0:r0 + cs, :]
            bc = cum_c[:, M_HEADS + h:M_HEADS + h + 1]
            br = cum_r[M_HEADS + h:M_HEADS + h + 1, :]
            ig_c = gc[:, h:h + 1]
            ig_r = gr[h:h + 1, :]
            m_prev = m_ref[h:h + 1, :]
            c_prev = c_ref[h]
            n_prev = n_ref[h:h + 1, :]

            log_d = jnp.where(causal, bc - br + ig_r, -jnp.inf)
            log_past = bc + m_prev
            m_t = jnp.maximum(log_past, jnp.max(log_d, axis=-1, keepdims=True))
            d_mat = jnp.exp(log_d - m_t)
            past_w = jnp.exp(log_past - m_t)
            qs = q * scale
            qsb = qs.astype(BF16)
            kb = k.astype(BF16)
            vb = v.astype(BF16)
            s = _dot_nt(qsb, kb) * d_mat
            num = past_w * _dot(qsb, c_prev.astype(BF16)) + _dot(s.astype(BF16), vb)
            den = past_w * jnp.sum(qs * n_prev, axis=-1, keepdims=True) + jnp.sum(s, axis=-1, keepdims=True)
            hm_ref[r0:r0 + cs, c0:c0 + dk] = num / jnp.maximum(jnp.abs(den), jnp.exp(-m_t))

            m_new = m_t[cs - 1:cs, :]
            b_last = bc[cs - 1:cs, :]
            w_s = jnp.exp(b_last - bc + ig_c - m_new)
            decay = jnp.exp(b_last + m_prev - m_new)
            kw = w_s * k
            c_ref[h] = decay * c_prev + _dot_tn(kw.astype(BF16), vb)
            n_ref[h:h + 1, :] = decay * n_prev + jnp.sum(kw, axis=0, keepdims=True)
            m_ref[h:h + 1, :] = m_new

    o_pre = _dot(xn, wm_ref[:, 3 * d:4 * d])
    hb = (jax.nn.sigmoid(o_pre) * hm_ref[...]).astype(BF16)
    g_a = _dot(xn, wm_ref[:, 4 * d:5 * d])
    g_b = _dot(xn, wm_ref[:, 5 * d:6 * d])
    merged = (jax.nn.sigmoid(g_a) * _dot(ha_ref[...], wbra_ref[...])
              + jax.nn.sigmoid(g_b) * _dot(hb, wbrb_ref[...]))
    o_ref[...] = x + ga_ref[...] * _dot(merged.astype(BF16), wout_ref[...])


def _mlstm_prompt(x2, ha2, mod3, gain, wm, wif, wift, bif, bift, wbra, wbrb, wout, *, nb, seq):
    rows, d = x2.shape
    dk = d // M_HEADS
    tl = min(ML_ROWS, seq)
    nt = seq // tl
    mspec = lambda j: pl.BlockSpec((None, 1, d), lambda b, t: (b, 0, j))
    tile = lambda: pl.BlockSpec((tl, d), lambda b, t: (b * nt + t, 0))
    return pl.pallas_call(
        _mlstm_kernel,
        grid=(nb, nt),
        in_specs=[tile(), tile(), mspec(3), mspec(4), mspec(5), _resident((1, d)),
                  _resident(wm.shape), _resident(wif.shape), _resident(wift.shape),
                  _resident(bif.shape), _resident(bift.shape),
                  _resident(wbra.shape), _resident(wbrb.shape), _resident(wout.shape)],
        out_specs=[tile(),
                   pl.BlockSpec((None, M_HEADS, dk, dk), lambda b, t: (b, 0, 0, 0)),
                   pl.BlockSpec((None, M_HEADS, dk), lambda b, t: (b, 0, 0)),
                   pl.BlockSpec((None, M_HEADS, 1), lambda b, t: (b, 0, 0))],
        out_shape=[jax.ShapeDtypeStruct((rows, d), F32),
                   jax.ShapeDtypeStruct((nb, M_HEADS, dk, dk), F32),
                   jax.ShapeDtypeStruct((nb, M_HEADS, dk), F32),
                   jax.ShapeDtypeStruct((nb, M_HEADS, 1), F32)],
        scratch_shapes=[pltpu.VMEM((tl, d), F32)],
        compiler_params=_cparams(("arbitrary", "arbitrary")),
        name="mlstm_prompt",
    )(x2, ha2, mod3, mod3, mod3, gain.reshape(1, d), wm, wif, wift, bif, bift, wbra, wbrb, wout)


def _head_sum(x, dk):
    parts = []
    for h in range(x.shape[1] // dk):
        sl = x[:, h * dk:(h + 1) * dk]
        parts.append(jnp.broadcast_to(jnp.sum(sl, axis=-1, keepdims=True), sl.shape))
    return jnp.concatenate(parts, axis=-1)


def _head_spread(cols, lane0, dk, rows):
    return jnp.concatenate(
        [jnp.broadcast_to(cols[:, lane0 + h:lane0 + h + 1], (rows, dk)) for h in range(M_HEADS)], axis=-1)


def _dec_pre_kernel(x_ref, sh_ref, sc_ref, g_ref, wrg_ref, wm_ref, wqt_ref, wkt_ref, wif_ref, bif_ref,
                    cw_ref, cb_ref, wa_ref, wi_ref, ba_ref, bi_ref, lam_ref,
                    conv0_ref, h0_ref, n0_ref, m0_ref,
                    ha_ref, conv_ref, hs_ref, n_ref, m_ref, qt_ref, kt_ref, dec_ref, wv_ref, pw_ref, sv_ref,
                    den_ref, em_ref):
    rows, d = x_ref.shape
    dk = d // M_HEADS
    scale = dk ** -0.5
    xn = _rms_mod(x_ref[...], g_ref[...], sh_ref[...], sc_ref[...]).astype(BF16)

    x_rg = _dot(xn, wrg_ref[...])
    u = cb_ref[...]
    for j in range(CONV_W - 1):
        u = u + conv0_ref[j] * cw_ref[j:j + 1, :]
        if j > 0:
            conv_ref[j - 1] = conv0_ref[j]
    u = u + x_rg * cw_ref[CONV_W - 1:CONV_W, :]
    conv_ref[CONV_W - 2] = x_rg
    a, b = _rg_gates(u, wa_ref, wi_ref, ba_ref[...], bi_ref[...], lam_ref[...])
    h = a * h0_ref[...] + b
    hs_ref[...] = h
    ha_ref[...] = h

    q = _dot(xn, wm_ref[:, 0:d])
    k = _dot(xn, wm_ref[:, d:2 * d])
    v = _dot(xn, wm_ref[:, 2 * d:3 * d])
    qt_ref[...] = _dot_nt(wqt_ref[...], xn).astype(BF16)
    kt_ref[...] = _dot_nt(wkt_ref[...], xn).astype(BF16)
    pre = _dot(xn, wif_ref[...]) + bif_ref[...]
    ig = _head_spread(pre, 0, dk, rows)
    lf = jax.nn.log_sigmoid(_head_spread(pre, M_HEADS, dk, rows))
    m0 = _head_spread(m0_ref[...], 0, dk, rows)
    n0 = n0_ref[...]
    log_past = lf + m0
    m_t = jnp.maximum(log_past, ig)
    d_w = jnp.exp(ig - m_t)
    past_w = jnp.exp(log_past - m_t)
    qs = q * scale
    s = _head_sum(qs * k, dk) * d_w
    den_ref[...] = past_w * _head_sum(qs * n0, dk) + s
    em_ref[...] = jnp.exp(-m_t)
    pw_ref[...] = past_w
    sv_ref[...] = s * v
    w_s = jnp.exp(ig - m_t)
    decay = jnp.exp(lf + m0 - m_t)
    dec_ref[...] = decay
    wv_ref[...] = w_s * v
    n_ref[...] = decay * n0 + w_s * k
    m_ref[...] = m_t


def _dec_pre(x2, mod, gain, w_rgx, wm, wqt, wkt, wif, bif, conv_w, conv_b, wa, wi, ba, bi, lam,
             conv0, h0, n0, m0):
    rows, d = x2.shape
    row = lambda v: v.reshape(1, d)
    m0p = jnp.pad(m0, ((0, 0), (0, LANES - M_HEADS)))
    full = lambda shape: pl.BlockSpec(shape, lambda i, _n=len(shape): (0,) * _n)
    mspec = lambda j: pl.BlockSpec((rows, d), lambda i: (0, j))
    vec = jax.ShapeDtypeStruct((rows, d), F32)
    outs = [vec,
            jax.ShapeDtypeStruct((CONV_W - 1, rows, d), F32),
            vec,
            vec,
            vec,
            jax.ShapeDtypeStruct((d, rows), BF16),
            jax.ShapeDtypeStruct((d, rows), BF16),
            vec, vec, vec, vec, vec, vec]
    args = [x2, mod, mod, row(gain), w_rgx, wm, wqt, wkt, wif, bif, conv_w, row(conv_b), wa, wi,
            row(ba), row(bi), row(lam), conv0, h0, n0, m0p]
    in_specs = [full(x2.shape), mspec(3), mspec(4)] + [full(a.shape) for a in args[3:]]
    return pl.pallas_call(
        _dec_pre_kernel,
        grid=(1,),
        in_specs=in_specs,
        out_specs=[full(o.shape) for o in outs],
        out_shape=outs,
        compiler_params=_cparams(("arbitrary",)),
        name="decode_pre",
    )(*args)


def _dec_mem_kernel(qt_ref, kt_ref, dec_ref, wv_ref, c0_ref, c_ref, qc_ref):
    bb = c0_ref.shape[0]
    dk = c0_ref.shape[2]
    nseq = qt_ref.shape[1]
    scale = dk ** -0.5
    base = pl.program_id(0) * bb
    seq_id = lax.broadcasted_iota(jnp.int32, (nseq, dk), 0)
    for j in range(bb):
        pick = (seq_id == base + j).astype(BF16)
        qcol = _dot(qt_ref[...], pick) * scale
        kcol = _dot(kt_ref[...], pick)
        for h in range(M_HEADS):
            c0 = c0_ref[j, h]
            dec = dec_ref[j, :, h * dk:(h + 1) * dk]
            wv = wv_ref[j, :, h * dk:(h + 1) * dk]
            c_ref[j, h] = dec * c0 + kcol[h * dk:(h + 1) * dk, :] * wv
            qc_ref[j, :, h * dk:(h + 1) * dk] = jnp.sum(qcol[h * dk:(h + 1) * dk, :] * c0, axis=0, keepdims=True)


def _dec_mem(qt, kt, decay, wv, c0):
    nseq, heads, dk, _ = c0.shape
    d = heads * dk
    bb = DEC_BLOCK if nseq % DEC_BLOCK == 0 else 1
    rowblk = pl.BlockSpec((bb, 1, d), lambda i: (i, 0, 0))
    return pl.pallas_call(
        _dec_mem_kernel,
        grid=(nseq // bb,),
        in_specs=[pl.BlockSpec((d, nseq), lambda i: (0, 0)), pl.BlockSpec((d, nseq), lambda i: (0, 0)),
                  rowblk, rowblk,
                  pl.BlockSpec((bb, heads, dk, dk), lambda i: (i, 0, 0, 0))],
        out_specs=[pl.BlockSpec((bb, heads, dk, dk), lambda i: (i, 0, 0, 0)), rowblk],
        out_shape=[jax.ShapeDtypeStruct(c0.shape, F32), jax.ShapeDtypeStruct((nseq, 1, d), F32)],
        compiler_params=_cparams(("arbitrary",)),
        name="decode_mem",
    )(qt, kt, decay.reshape(nseq, 1, d), wv.reshape(nseq, 1, d), c0)


def _dec_post_kernel(x_ref, sh_ref, sc_ref, ga_ref, g_ref, wm_ref, wbra_ref, wbrb_ref, wout_ref,
                     ha_ref, qc_ref, pw_ref, sv_ref, den_ref, em_ref, o_ref):
    d = x_ref.shape[1]
    x = x_ref[...]
    xn = _rms_mod(x, g_ref[...], sh_ref[...], sc_ref[...]).astype(BF16)
    num = pw_ref[...] * qc_ref[...] + sv_ref[...]
    hm = num / jnp.maximum(jnp.abs(den_ref[...]), em_ref[...])
    hb = (jax.nn.sigmoid(_dot(xn, wm_ref[:, 3 * d:4 * d])) * hm).astype(BF16)
    g_a = _dot(xn, wm_ref[:, 4 * d:5 * d])
    g_b = _dot(xn, wm_ref[:, 5 * d:6 * d])
    merged = (jax.nn.sigmoid(g_a) * _dot(ha_ref[...].astype(BF16), wbra_ref[...])
              + jax.nn.sigmoid(g_b) * _dot(hb, wbrb_ref[...]))
    o_ref[...] = x + ga_ref[...] * _dot(merged.astype(BF16), wout_ref[...])


def _dec_post(x2, mod, gain, wm, wbra, wbrb, wout, ha, qc, pw, sv, den, em):
    rows, d = x2.shape
    full = lambda shape: pl.BlockSpec(shape, lambda i, _n=len(shape): (0,) * _n)
    mspec = lambda j: pl.BlockSpec((rows, d), lambda i: (0, j))
    args = [x2, mod, mod, mod, gain.reshape(1, d), wm, wbra, wbrb, wout, ha, qc, pw, sv, den, em]
    in_specs = [full(x2.shape), mspec(3), mspec(4), mspec(5)] + [full(a.shape) for a in args[4:]]
    return pl.pallas_call(
        _dec_post_kernel,
        grid=(1,),
        in_specs=in_specs,
        out_specs=full((rows, d)),
        out_shape=jax.ShapeDtypeStruct((rows, d), F32),
        compiler_params=_cparams(("arbitrary",)),
        name="decode_post",
    )(*args)


def _pair_blocks(w):
    nblk, c, _ = w.shape
    z = jnp.zeros((nblk // 2, c, c), w.dtype)
    top = jnp.concatenate([w[0::2], z], axis=2)
    bot = jnp.concatenate([z, w[1::2]], axis=2)
    return jnp.concatenate([top, bot], axis=1).astype(BF16)


def kernel(x_prompt, x_sample, state_conv, state_rg_h, state_C, state_n, state_m, c_prompt, c_sample, w_ada, b_ada, g_norm1, w_ff1_in, w_ff1_out, g_norm2, w_in, conv_w, conv_b, w_rg_a, b_rg_a, w_rg_i, b_rg_i, rg_lambda, b_ig, b_fg, w_br_a, w_br_b, w_out, g_norm3, w_ff2_in, w_ff2_out, g_final):
    nb, seq, d = x_prompt.shape
    ns = x_sample.shape[0]
    depth = w_ada.shape[0]
    assert depth == 1 and x_sample.shape[1] == 1 and nb == SUBLANES
    assert seq % RG_STEPS == 0 and seq % ML_ROWS == 0 and seq % FFN_ROWS == 0
    heads, dk = M_HEADS, d // M_HEADS

    w_in0 = w_in[0]
    seg = lambda i: w_in0[:, i * d:(i + 1) * d]
    gate0 = 5 * d
    w_rgx = seg(0).astype(BF16)
    w_gates = w_in0[:, gate0:gate0 + 2 * heads]
    rest = w_in0[:, gate0 + 2 * heads:]
    wm = jnp.concatenate([seg(1), seg(2), seg(3), seg(4), rest], axis=1).astype(BF16)
    wif = jnp.pad(w_gates, ((0, 0), (0, LANES - 2 * heads))).astype(BF16)
    wift = w_gates.T.astype(BF16)
    b_gates = jnp.concatenate([b_ig[0], b_fg[0]])
    bif = jnp.pad(b_gates, (0, LANES - 2 * heads)).reshape(1, LANES)
    bift = b_gates.reshape(2 * heads, 1)
    wqt = seg(1).T.astype(BF16)
    wkt = seg(2).T.astype(BF16)
    wa = _pair_blocks(w_rg_a[0])
    wi = _pair_blocks(w_rg_i[0])
    wf1i, wf1o = w_ff1_in[0].astype(BF16), w_ff1_out[0].astype(BF16)
    wf2i, wf2o = w_ff2_in[0].astype(BF16), w_ff2_out[0].astype(BF16)
    wbra, wbrb, wout = w_br_a[0].astype(BF16), w_br_b[0].astype(BF16), w_out[0].astype(BF16)

    mod = _ada(jnp.concatenate([c_prompt, c_sample], axis=0), w_ada[0], b_ada[0])
    mod_p2 = mod[:nb]
    mod_p3 = mod_p2.reshape(nb, 1, N_MOD * d)
    mod_s = mod[nb:]

    xp = x_prompt.reshape(nb * seq, d)
    xp = _ffn(xp, mod_p3, 0, g_norm1[0], wf1i, wf1o, None, rows_per_seq=seq, per_row_mod=False)
    ha, conv_p, h_p = _rglru_prompt(xp.reshape(nb, seq, d), mod_p2, g_norm2[0], w_rgx, conv_w[0], conv_b[0],
                                    wa, wi, b_rg_a[0], b_rg_i[0], rg_lambda[0])
    xp, c_p, n_p, m_p = _mlstm_prompt(xp, ha.reshape(nb * seq, d), mod_p3, g_norm2[0], wm, wif, wift, bif, bift,
                                      wbra, wbrb, wout, nb=nb, seq=seq)
    yp = _ffn(xp, mod_p3, 6, g_norm3[0], wf2i, wf2o, g_final, rows_per_seq=seq, per_row_mod=False)

    xs = x_sample.reshape(ns, d)
    xs = _ffn(xs, mod_s, 0, g_norm1[0], wf1i, wf1o, None, rows_per_seq=1, per_row_mod=True)
    conv0 = jnp.swapaxes(state_conv[0], 0, 1)
    (ha_s, conv_s, h_s, n_s, m_s, qt, kt, decay, wv, pw, sv, den, em) = _dec_pre(
        xs, mod_s, g_norm2[0], w_rgx, wm, wqt, wkt, wif, bif, conv_w[0], conv_b[0], wa, wi,
        b_rg_a[0], b_rg_i[0], rg_lambda[0], conv0, state_rg_h[0], state_n[0].reshape(ns, d), state_m[0])
    c_s, qc = _dec_mem(qt, kt, decay, wv, state_C[0])
    xs = _dec_post(xs, mod_s, g_norm2[0], wm, wbra, wbrb, wout, ha_s, qc.reshape(ns, d), pw, sv, den, em)
    ys = _ffn(xs, mod_s, 6, g_norm3[0], wf2i, wf2o, g_final, rows_per_seq=1, per_row_mod=True)

    return (yp.reshape(nb, seq, d), ys.reshape(ns, 1, d),
            jnp.swapaxes(conv_p, 0, 1)[None], h_p[None], c_p[None], n_p[None], m_p.reshape(1, nb, heads),
            jnp.swapaxes(conv_s, 0, 1)[None], h_s[None], c_s[None], n_s.reshape(1, ns, heads, dk), m_s[:, ::dk][None])
```

```python
import functools

import jax
import jax.numpy as jnp
from jax import lax
from jax.experimental import pallas as pl
from jax.experimental.pallas import tpu as pltpu

F32 = jnp.float32
BF16 = jnp.bfloat16

EPS = 1e-6
RG_C = 8.0
CONV_W = 4
N_MOD = 9
M_HEADS = 4
RG_BLOCKS = 8

SUBLANES = 8
LANES = 128
MXU_DIM = 256
VMEM_LIMIT_BYTES = 60 * 1024 * 1024

FFN_ROWS = 1024
FFN_COLS = 256
RG_STEPS = 64
ML_ROWS = 512
ML_CHUNK = 256
DEC_BLOCK = 2
STAGE_ROWS = 256
STAGE_ROWS_WIDE = 64


def _cparams(sem):
    return pltpu.CompilerParams(dimension_semantics=sem, vmem_limit_bytes=VMEM_LIMIT_BYTES)


def _resident(shape):
    nd = len(shape)
    return pl.BlockSpec(shape, lambda *_: (0,) * nd, pipeline_mode=pl.Buffered(1))


def _rms_mod(x, gain, shift, scale):
    ms = jnp.mean(x * x, axis=-1, keepdims=True)
    return x * lax.rsqrt(ms + EPS) * gain * (1.0 + scale) + shift


def _dot(a, b):
    return jnp.dot(a, b, preferred_element_type=F32)


def _dot_nt(a, b):
    return lax.dot_general(a, b, (((1,), (1,)), ((), ())), preferred_element_type=F32)


def _dot_tn(a, b):
    return lax.dot_general(a, b, (((0,), (0,)), ((), ())), preferred_element_type=F32)


def _split3(x):
    hi = x.astype(BF16)
    r1 = x - hi.astype(F32)
    mid = r1.astype(BF16)
    lo = (r1 - mid.astype(F32)).astype(BF16)
    return hi, mid, lo


def _stage_weights(src, dst, stage, sem, *, src_col0=0, dst_col0=0):
    _, rows_per_copy, cols = stage.shape
    n = src.shape[0] // rows_per_copy

    def copy(c):
        return pltpu.make_async_copy(
            src.at[pl.ds(c * rows_per_copy, rows_per_copy), pl.ds(src_col0, cols)],
            stage.at[c % 2], sem.at[c % 2])

    copy(0).start()
    for c in range(n):
        if c + 1 < n:
            copy(c + 1).start()
        copy(c).wait()
        dst[c * rows_per_copy:(c + 1) * rows_per_copy, dst_col0:dst_col0 + cols] = stage[c % 2].astype(BF16)


def _stage_scratch(rows_per_copy, cols):
    return [pltpu.VMEM((2, rows_per_copy, cols), F32), pltpu.SemaphoreType.DMA((2,))]


_HBM = pl.BlockSpec(memory_space=pl.ANY)


def _ada_kernel(c_ref, w_ref, b_ref, o_ref):
    c = c_ref[...]
    sc = (c * jax.nn.sigmoid(c)).astype(BF16)
    o_ref[...] = _dot(sc, w_ref[...].astype(BF16)) + b_ref[...]


def _ada(c, w_ada, b_ada):
    rows, d = c.shape
    n = w_ada.shape[1]
    tn = d
    return pl.pallas_call(
        _ada_kernel,
        grid=(n // tn,),
        in_specs=[pl.BlockSpec((rows, d), lambda j: (0, 0)),
                  pl.BlockSpec((d, tn), lambda j: (0, j)),
                  pl.BlockSpec((1, tn), lambda j: (0, j))],
        out_specs=pl.BlockSpec((rows, tn), lambda j: (0, j)),
        out_shape=jax.ShapeDtypeStruct((rows, n), F32),
        compiler_params=_cparams(("arbitrary",)),
        name="adaln_mod",
    )(c, w_ada, b_ada.reshape(1, n))


def _ffn_kernel(*refs, d_ff, final):
    if final:
        x_ref, sh_ref, sc_ref, ga_ref, g_ref, wi_hbm, wo_hbm, gf_ref, o_ref = refs[:9]
    else:
        x_ref, sh_ref, sc_ref, ga_ref, g_ref, wi_hbm, wo_hbm, o_ref = refs[:8]
    act_ref, wi_ref, wo_ref, stage_i, sem_i, stage_o, sem_o = refs[-7:]

    @pl.when(pl.program_id(0) == 0)
    def _():
        _stage_weights(wi_hbm, wi_ref, stage_i, sem_i)
        _stage_weights(wo_hbm, wo_ref, stage_o, sem_o)

    x = x_ref[...]
    xn = _rms_mod(x, g_ref[...], sh_ref[...], sc_ref[...]).astype(BF16)
    for j in range(d_ff // FFN_COLS):
        lo = j * FFN_COLS
        hg = _dot(xn, wi_ref[:, lo:lo + FFN_COLS])
        hu = _dot(xn, wi_ref[:, d_ff + lo:d_ff + lo + FFN_COLS])
        act_ref[:, lo:lo + FFN_COLS] = (hg * jax.nn.sigmoid(hg) * hu).astype(BF16)
    y = _dot(act_ref[...], wo_ref[...])
    out = x + (0.5 * ga_ref[...]) * y
    if final:
        ms = jnp.mean(out * out, axis=-1, keepdims=True)
        out = out * lax.rsqrt(ms + EPS) * gf_ref[...]
    o_ref[...] = out


def _mod_spec_prompt(j, tiles_per_seq, d):
    return pl.BlockSpec((None, 1, d), lambda i: (i // tiles_per_seq, 0, j))


def _ffn(x2, mod, j0, gain, wi, wo, g_final, *, rows_per_seq, per_row_mod):
    rows, d = x2.shape
    d_ff = wo.shape[0]
    if per_row_mod:
        tm = rows
        mspec = lambda j: pl.BlockSpec((tm, d), lambda i: (0, j))
    else:
        tm = min(FFN_ROWS, rows_per_seq)
        mspec = lambda j: _mod_spec_prompt(j, rows_per_seq // tm, d)
    final = g_final is not None
    in_specs = [pl.BlockSpec((tm, d), lambda i: (i, 0)),
                mspec(j0), mspec(j0 + 1), mspec(j0 + 2),
                _resident((1, d)), _HBM, _HBM]
    args = [x2, mod, mod, mod, gain.reshape(1, d), wi, wo]
    if final:
        in_specs.append(_resident((1, d)))
        args.append(g_final.reshape(1, d))
    return pl.pallas_call(
        functools.partial(_ffn_kernel, d_ff=d_ff, final=final),
        grid=(rows // tm,),
        in_specs=in_specs,
        out_specs=pl.BlockSpec((tm, d), lambda i: (i, 0)),
        out_shape=jax.ShapeDtypeStruct((rows, d), F32),
        scratch_shapes=[pltpu.VMEM((tm, d_ff), BF16), pltpu.VMEM(wi.shape, BF16), pltpu.VMEM(wo.shape, BF16)]
        + _stage_scratch(STAGE_ROWS_WIDE, wi.shape[1]) + _stage_scratch(STAGE_ROWS, wo.shape[1]),
        compiler_params=_cparams(("arbitrary",)),
        name="ffn_final" if final else "ffn",
    )(*args)


def _rg_gates(u, wa_ref, wi_ref, ba, bi, lam):
    ub = u.astype(BF16)
    ra, ri = [], []
    for p in range(wa_ref.shape[0]):
        blk = ub[:, p * MXU_DIM:(p + 1) * MXU_DIM]
        ra.append(_dot(blk, wa_ref[p]))
        ri.append(_dot(blk, wi_ref[p]))
    r = jax.nn.sigmoid(jnp.concatenate(ra, axis=-1) + ba)
    i_g = jax.nn.sigmoid(jnp.concatenate(ri, axis=-1) + bi)
    log_a = -RG_C * r * jax.nn.softplus(-lam)
    a = jnp.exp(log_a)
    b = jnp.sqrt(-jnp.tanh(log_a) * (a * a + 1.0)) * (i_g * u)
    return a, b


def _rglru_kernel(x_ref, sh_ref, sc_ref, g_ref, win_hbm, cw_ref, cb_ref, wa_ref, wi_ref, ba_ref, bi_ref,
                  lam_ref, ha_ref, conv_ref, hl_ref,
                  xn_ref, bm_ref, tb_ref, a_ref, b_ref, hs_ref, h_ref, w_ref, stage, sem):
    nb, tt, d = x_ref.shape
    tail = (CONV_W - 1) * nb
    rows = nb * tt
    t_idx = pl.program_id(0)

    @pl.when(t_idx == 0)
    def _():
        _stage_weights(win_hbm, w_ref, stage, sem)
        tb_ref[0:tail, :] = jnp.zeros((tail, d), F32)
        h_ref[...] = jnp.zeros_like(h_ref)

    for bq in range(nb):
        xn = _rms_mod(x_ref[bq], g_ref[...], sh_ref[bq:bq + 1, :], sc_ref[bq:bq + 1, :])
        xn_ref[bq * tt:(bq + 1) * tt, :] = xn.astype(BF16)
    x_rg = _dot(xn_ref[...], w_ref[...])
    for cb in range(d // LANES):
        bm_ref[cb] = x_rg[:, cb * LANES:(cb + 1) * LANES]

    def to_time_major(t, carry):
        dst = pl.multiple_of(tail + t * nb, nb)
        for cb in range(d // LANES):
            tb_ref[pl.ds(dst, nb), cb * LANES:(cb + 1) * LANES] = bm_ref[cb, pl.ds(t, nb, stride=tt), :]
        return carry
    lax.fori_loop(0, tt, to_time_major, 0, unroll=8)

    u = cb_ref[...]
    for j in range(CONV_W):
        u = u + tb_ref[j * nb:j * nb + rows, :] * cw_ref[j:j + 1, :]
    new_tail = tb_ref[rows:rows + tail, :]
    for j in range(CONV_W - 1):
        conv_ref[j] = new_tail[j * nb:(j + 1) * nb, :]
    tb_ref[0:tail, :] = new_tail

    a, b = _rg_gates(u, wa_ref, wi_ref, ba_ref[...], bi_ref[...], lam_ref[...])
    a_ref[...] = a
    b_ref[...] = b

    def step(t, h):
        src = pl.multiple_of(t * nb, nb)
        h = a_ref[pl.ds(src, nb), :] * h + b_ref[pl.ds(src, nb), :]
        for cb in range(d // LANES):
            hs_ref[cb, pl.ds(src, nb), :] = h[:, cb * LANES:(cb + 1) * LANES]
        return h
    h_last = lax.fori_loop(0, tt, step, h_ref[...], unroll=8)
    h_ref[...] = h_last
    hl_ref[...] = h_last

    for bq in range(nb):
        for cb in range(d // LANES):
            ha_ref[bq, :, cb * LANES:(cb + 1) * LANES] = hs_ref[cb, pl.ds(bq, tt, stride=nb), :].astype(BF16)


def _rglru_prompt(x3, mod2, gain, w_in, conv_w, conv_b, wa, wi, ba, bi, lam):
    nb, seq, d = x3.shape
    tt = min(RG_STEPS, seq)
    rows = nb * tt
    tail = (CONV_W - 1) * nb
    row = lambda v: v.reshape(1, d)
    return pl.pallas_call(
        _rglru_kernel,
        grid=(seq // tt,),
        in_specs=[pl.BlockSpec((nb, tt, d), lambda t: (0, t, 0)),
                  pl.BlockSpec((nb, d), lambda t: (0, 3)),
                  pl.BlockSpec((nb, d), lambda t: (0, 4)),
                  _resident((1, d)), _HBM, _resident(conv_w.shape), _resident((1, d)),
                  _resident(wa.shape), _resident(wi.shape), _resident((1, d)), _resident((1, d)),
                  _resident((1, d))],
        out_specs=[pl.BlockSpec((nb, tt, d), lambda t: (0, t, 0)),
                   pl.BlockSpec((CONV_W - 1, nb, d), lambda t: (0, 0, 0)),
                   pl.BlockSpec((nb, d), lambda t: (0, 0))],
        out_shape=[jax.ShapeDtypeStruct((nb, seq, d), BF16),
                   jax.ShapeDtypeStruct((CONV_W - 1, nb, d), F32),
                   jax.ShapeDtypeStruct((nb, d), F32)],
        scratch_shapes=[pltpu.VMEM((rows, d), BF16),
                        pltpu.VMEM((d // LANES, rows, LANES), F32),
                        pltpu.VMEM((tail + rows, d), F32),
                        pltpu.VMEM((rows, d), F32),
                        pltpu.VMEM((rows, d), F32),
                        pltpu.VMEM((d // LANES, rows, LANES), F32),
                        pltpu.VMEM((nb, d), F32),
                        pltpu.VMEM((d, d), BF16)]
        + _stage_scratch(STAGE_ROWS, d),
        compiler_params=_cparams(("arbitrary",)),
        name="rglru_prompt",
    )(x3, mod2, mod2, row(gain), w_in, conv_w, row(conv_b), wa, wi, row(ba), row(bi), row(lam))


def _mlstm_kernel(x_ref, ha_ref, sh_ref, sc_ref, ga_ref, g_ref, win_hbm, wg_hbm, wif_ref, wift_ref, bif_ref, bift_ref,
                  wbra_hbm, wbrb_hbm, wout_hbm, o_ref, c_ref, n_ref, m_ref,
                  hm_ref, wm_ref, wbra_ref, wbrb_ref, wout_ref, stage_w, sem_w, stage_g, sem_g, stage_s, sem_s):
    tl, d = x_ref.shape
    dk = d // M_HEADS
    cs = min(ML_CHUNK, tl)
    scale = dk ** -0.5

    @pl.when((pl.program_id(0) == 0) & (pl.program_id(1) == 0))
    def _():
        _stage_weights(win_hbm, wm_ref, stage_w, sem_w, src_col0=d)
        _stage_weights(wg_hbm, wm_ref, stage_g, sem_g, dst_col0=4 * d)
        _stage_weights(wbra_hbm, wbra_ref, stage_s, sem_s)
        _stage_weights(wbrb_hbm, wbrb_ref, stage_s, sem_s)
        _stage_weights(wout_hbm, wout_ref, stage_s, sem_s)

    @pl.when(pl.program_id(1) == 0)
    def _():
        c_ref[...] = jnp.zeros_like(c_ref)
        n_ref[...] = jnp.zeros_like(n_ref)
        m_ref[...] = jnp.zeros_like(m_ref)

    x = x_ref[...]
    xn = _rms_mod(x, g_ref[...], sh_ref[...], sc_ref[...]).astype(BF16)

    pre_c = _dot(xn, wif_ref[...]) + bif_ref[...]
    pre_r = _dot_nt(wift_ref[...], xn) + bift_ref[...]
    col_is_f = lax.broadcasted_iota(jnp.int32, pre_c.shape, 1) >= M_HEADS
    row_is_f = lax.broadcasted_iota(jnp.int32, pre_r.shape, 0) >= M_HEADS
    gate_c = jnp.where(col_is_f, jax.nn.log_sigmoid(pre_c), pre_c)
    gate_r = jnp.where(row_is_f, jax.nn.log_sigmoid(pre_r), pre_r)

    ti = lax.broadcasted_iota(jnp.int32, (cs, cs), 0)
    si = lax.broadcasted_iota(jnp.int32, (cs, cs), 1)
    causal = si <= ti
    lower = causal.astype(BF16)
    upper = (ti <= si).astype(BF16)

    chunks = []
    for c in range(tl // cs):
        gc = gate_c[c * cs:(c + 1) * cs, :]
        gr = gate_r[:, c * cs:(c + 1) * cs]
        cum_c = sum(_dot(lower, part) for part in _split3(gc))
        cum_r = sum(_dot(part, upper) for part in _split3(gr))
        chunks.append((gc, gr, cum_c, cum_r))

    for h in range(M_HEADS):
        c0 = h * dk
        q_all = _dot(xn, wm_ref[:, c0:c0 + dk])
        k_all = _dot(xn, wm_ref[:, d + c0:d + c0 + dk])
        v_all = _dot(xn, wm_ref[:, 2 * d + c0:2 * d + c0 + dk])
        for c, (gc, gr, cum_c, cum_r) in enumerate(chunks):
            r0 = c * cs
            q = q_all[r0:r0 + cs, :]
            k = k_all[r0:r0 + cs, :]
            v = v_all[r0:r0 + cs, :]
            bc = cum_c[:, M_HEADS + h:M_HEADS + h + 1]
            br = cum_r[M_HEADS + h:M_HEADS + h + 1, :]
            ig_c = gc[:, h:h + 1]
            ig_r = gr[h:h + 1, :]
            m_prev = m_ref[h:h + 1, :]
            c_prev = c_ref[h]
            n_prev = n_ref[h:h + 1, :]

            log_d = jnp.where(causal, bc - br + ig_r, -jnp.inf)
            log_past = bc + m_prev
            m_t = jnp.maximum(log_past, jnp.max(log_d, axis=-1, keepdims=True))
            d_mat = jnp.exp(log_d - m_t)
            past_w = jnp.exp(log_past - m_t)
            qs = q * scale
            qsb = qs.astype(BF16)
            kb = k.astype(BF16)
            vb = v.astype(BF16)
            s = _dot_nt(qsb, kb) * d_mat
            num = past_w * _dot(qsb, c_prev.astype(BF16)) + _dot(s.astype(BF16), vb)
            den = past_w * jnp.sum(qs * n_prev, axis=-1, keepdims=True) + jnp.sum(s, axis=-1, keepdims=True)
            hm_ref[r0:r0 + cs, c0:c0 + dk] = num / jnp.maximum(jnp.abs(den), jnp.exp(-m_t))

            m_new = m_t[cs - 1:cs, :]
            b_last = bc[cs - 1:cs, :]
            w_s = jnp.exp(b_last - bc + ig_c - m_new)
            decay = jnp.exp(b_last + m_prev - m_new)
            kw = w_s * k
            c_ref[h] = decay * c_prev + _dot_tn(kw.astype(BF16), vb)
            n_ref[h:h + 1, :] = decay * n_prev + jnp.sum(kw, axis=0, keepdims=True)
            m_ref[h:h + 1, :] = m_new

    o_pre = _dot(xn, wm_ref[:, 3 * d:4 * d])
    hb = (jax.nn.sigmoid(o_pre) * hm_ref[...]).astype(BF16)
    g_a = _dot(xn, wm_ref[:, 4 * d:5 * d])
    g_b = _dot(xn, wm_ref[:, 5 * d:6 * d])
    merged = (jax.nn.sigmoid(g_a) * _dot(ha_ref[...], wbra_ref[...])
              + jax.nn.sigmoid(g_b) * _dot(hb, wbrb_ref[...]))
    o_ref[...] = x + ga_ref[...] * _dot(merged.astype(BF16), wout_ref[...])


def _mlstm_prompt(x2, ha2, mod3, gain, w_in, w_g, wif, wift, bif, bift, wbra, wbrb, wout, *, nb, seq):
    rows, d = x2.shape
    dk = d // M_HEADS
    tl = min(ML_ROWS, seq)
    nt = seq // tl
    mspec = lambda j: pl.BlockSpec((None, 1, d), lambda b, t: (b, 0, j))
    tile = lambda: pl.BlockSpec((tl, d), lambda b, t: (b * nt + t, 0))
    return pl.pallas_call(
        _mlstm_kernel,
        grid=(nb, nt),
        in_specs=[tile(), tile(), mspec(3), mspec(4), mspec(5), _resident((1, d)),
                  _HBM, _HBM, _resident(wif.shape), _resident(wift.shape),
                  _resident(bif.shape), _resident(bift.shape),
                  _HBM, _HBM, _HBM],
        out_specs=[tile(),
                   pl.BlockSpec((None, M_HEADS, dk, dk), lambda b, t: (b, 0, 0, 0)),
                   pl.BlockSpec((None, M_HEADS, dk), lambda b, t: (b, 0, 0)),
                   pl.BlockSpec((None, M_HEADS, 1), lambda b, t: (b, 0, 0))],
        out_shape=[jax.ShapeDtypeStruct((rows, d), F32),
                   jax.ShapeDtypeStruct((nb, M_HEADS, dk, dk), F32),
                   jax.ShapeDtypeStruct((nb, M_HEADS, dk), F32),
                   jax.ShapeDtypeStruct((nb, M_HEADS, 1), F32)],
        scratch_shapes=[pltpu.VMEM((tl, d), F32),
                        pltpu.VMEM((d, 6 * d), BF16),
                        pltpu.VMEM((d, d), BF16), pltpu.VMEM((d, d), BF16), pltpu.VMEM((d, d), BF16)]
        + _stage_scratch(STAGE_ROWS_WIDE, 4 * d) + _stage_scratch(STAGE_ROWS // 2, 2 * d)
        + _stage_scratch(STAGE_ROWS, d),
        compiler_params=_cparams(("arbitrary", "arbitrary")),
        name="mlstm_prompt",
    )(x2, ha2, mod3, mod3, mod3, gain.reshape(1, d), w_in, w_g, wif, wift, bif, bift, wbra, wbrb, wout)


def _head_sum(x, dk):
    parts = []
    for h in range(x.shape[1] // dk):
        sl = x[:, h * dk:(h + 1) * dk]
        parts.append(jnp.broadcast_to(jnp.sum(sl, axis=-1, keepdims=True), sl.shape))
    return jnp.concatenate(parts, axis=-1)


def _head_spread(cols, lane0, dk, rows):
    return jnp.concatenate(
        [jnp.broadcast_to(cols[:, lane0 + h:lane0 + h + 1], (rows, dk)) for h in range(M_HEADS)], axis=-1)


def _dec_pre_kernel(x_ref, sh_ref, sc_ref, g_ref, win_hbm, wqt_hbm, wkt_hbm, wif_ref, bif_ref,
                    cw_ref, cb_ref, wa_ref, wi_ref, ba_ref, bi_ref, lam_ref,
                    conv0_ref, h0_ref, n0_ref, m0_ref,
                    ha_ref, conv_ref, hs_ref, n_ref, m_ref, qt_ref, kt_ref, dec_ref, wv_ref, pw_ref, sv_ref,
                    den_ref, em_ref,
                    wm_ref, wqt_ref, wkt_ref, stage_w, sem_w, stage_s, sem_s):
    rows, d = x_ref.shape
    dk = d // M_HEADS
    scale = dk ** -0.5
    _stage_weights(win_hbm, wm_ref, stage_w, sem_w)
    _stage_weights(wqt_hbm, wqt_ref, stage_s, sem_s)
    _stage_weights(wkt_hbm, wkt_ref, stage_s, sem_s)
    xn = _rms_mod(x_ref[...], g_ref[...], sh_ref[...], sc_ref[...]).astype(BF16)

    x_rg = _dot(xn, wm_ref[:, 0:d])
    u = cb_ref[...]
    for j in range(CONV_W - 1):
        u = u + conv0_ref[j] * cw_ref[j:j + 1, :]
        if j > 0:
            conv_ref[j - 1] = conv0_ref[j]
    u = u + x_rg * cw_ref[CONV_W - 1:CONV_W, :]
    conv_ref[CONV_W - 2] = x_rg
    a, b = _rg_gates(u, wa_ref, wi_ref, ba_ref[...], bi_ref[...], lam_ref[...])
    h = a * h0_ref[...] + b
    hs_ref[...] = h
    ha_ref[...] = h

    q = _dot(xn, wm_ref[:, d:2 * d])
    k = _dot(xn, wm_ref[:, 2 * d:3 * d])
    v = _dot(xn, wm_ref[:, 3 * d:4 * d])
    qt_ref[...] = _dot_nt(wqt_ref[...], xn).astype(BF16)
    kt_ref[...] = _dot_nt(wkt_ref[...], xn).astype(BF16)
    pre = _dot(xn, wif_ref[...]) + bif_ref[...]
    ig = _head_spread(pre, 0, dk, rows)
    lf = jax.nn.log_sigmoid(_head_spread(pre, M_HEADS, dk, rows))
    m0 = _head_spread(m0_ref[...], 0, dk, rows)
    n0 = n0_ref[...]
    log_past = lf + m0
    m_t = jnp.maximum(log_past, ig)
    d_w = jnp.exp(ig - m_t)
    past_w = jnp.exp(log_past - m_t)
    qs = q * scale
    s = _head_sum(qs * k, dk) * d_w
    den_ref[...] = past_w * _head_sum(qs * n0, dk) + s
    em_ref[...] = jnp.exp(-m_t)
    pw_ref[...] = past_w
    sv_ref[...] = s * v
    w_s = jnp.exp(ig - m_t)
    decay = jnp.exp(lf + m0 - m_t)
    dec_ref[...] = decay
    wv_ref[...] = w_s * v
    n_ref[...] = decay * n0 + w_s * k
    m_ref[...] = m_t


def _dec_pre(x2, mod, gain, w_in, wqt, wkt, wif, bif, conv_w, conv_b, wa, wi, ba, bi, lam,
             conv0, h0, n0, m0):
    rows, d = x2.shape
    row = lambda v: v.reshape(1, d)
    m0p = jnp.pad(m0, ((0, 0), (0, LANES - M_HEADS)))
    full = lambda shape: pl.BlockSpec(shape, lambda i, _n=len(shape): (0,) * _n)
    mspec = lambda j: pl.BlockSpec((rows, d), lambda i: (0, j))
    vec = jax.ShapeDtypeStruct((rows, d), F32)
    outs = [vec,
            jax.ShapeDtypeStruct((CONV_W - 1, rows, d), F32),
            vec,
            vec,
            vec,
            jax.ShapeDtypeStruct((d, rows), BF16),
            jax.ShapeDtypeStruct((d, rows), BF16),
            vec, vec, vec, vec, vec, vec]
    args = [x2, mod, mod, row(gain), w_in, wqt, wkt, wif, bif, conv_w, row(conv_b), wa, wi,
            row(ba), row(bi), row(lam), conv0, h0, n0, m0p]
    in_specs = ([full(x2.shape), mspec(3), mspec(4), full((1, d)), _HBM, _HBM, _HBM]
                + [full(a.shape) for a in args[7:]])
    return pl.pallas_call(
        _dec_pre_kernel,
        grid=(1,),
        in_specs=in_specs,
        out_specs=[full(o.shape) for o in outs],
        out_shape=outs,
        scratch_shapes=[pltpu.VMEM((d, 4 * d), BF16), pltpu.VMEM((d, d), BF16), pltpu.VMEM((d, d), BF16)]
        + _stage_scratch(STAGE_ROWS_WIDE, 4 * d) + _stage_scratch(STAGE_ROWS, d),
        compiler_params=_cparams(("arbitrary",)),
        name="decode_pre",
    )(*args)


def _dec_mem_kernel(qt_ref, kt_ref, dec_ref, wv_ref, c0_ref, c_ref, qc_ref):
    bb = c0_ref.shape[0]
    dk = c0_ref.shape[2]
    nseq = qt_ref.shape[1]
    scale = dk ** -0.5
    base = pl.program_id(0) * bb
    seq_id = lax.broadcasted_iota(jnp.int32, (nseq, dk), 0)
    for j in range(bb):
        pick = (seq_id == base + j).astype(BF16)
        qcol = _dot(qt_ref[...], pick) * scale
        kcol = _dot(kt_ref[...], pick)
        for h in range(M_HEADS):
            c0 = c0_ref[j, h]
            dec = dec_ref[j, :, h * dk:(h + 1) * dk]
            wv = wv_ref[j, :, h * dk:(h + 1) * dk]
            c_ref[j, h] = dec * c0 + kcol[h * dk:(h + 1) * dk, :] * wv
            qc_ref[j, :, h * dk:(h + 1) * dk] = jnp.sum(qcol[h * dk:(h + 1) * dk, :] * c0, axis=0, keepdims=True)


def _dec_mem(qt, kt, decay, wv, c0):
    nseq, heads, dk, _ = c0.shape
    d = heads * dk
    bb = DEC_BLOCK if nseq % DEC_BLOCK == 0 else 1
    rowblk = pl.BlockSpec((bb, 1, d), lambda i: (i, 0, 0))
    return pl.pallas_call(
        _dec_mem_kernel,
        grid=(nseq // bb,),
        in_specs=[pl.BlockSpec((d, nseq), lambda i: (0, 0)), pl.BlockSpec((d, nseq), lambda i: (0, 0)),
                  rowblk, rowblk,
                  pl.BlockSpec((bb, heads, dk, dk), lambda i: (i, 0, 0, 0))],
        out_specs=[pl.BlockSpec((bb, heads, dk, dk), lambda i: (i, 0, 0, 0)), rowblk],
        out_shape=[jax.ShapeDtypeStruct(c0.shape, F32), jax.ShapeDtypeStruct((nseq, 1, d), F32)],
        compiler_params=_cparams(("arbitrary",)),
        name="decode_mem",
    )(qt, kt, decay.reshape(nseq, 1, d), wv.reshape(nseq, 1, d), c0)


def _dec_post_kernel(x_ref, sh_ref, sc_ref, ga_ref, g_ref, win_hbm, wg_hbm, wbra_hbm, wbrb_hbm, wout_hbm,
                     ha_ref, qc_ref, pw_ref, sv_ref, den_ref, em_ref, o_ref,
                     wo_ref, wg_ref, wbra_ref, wbrb_ref, wout_ref, stage_g, sem_g, stage_s, sem_s):
    d = x_ref.shape[1]
    _stage_weights(win_hbm, wo_ref, stage_s, sem_s, src_col0=4 * d)
    _stage_weights(wg_hbm, wg_ref, stage_g, sem_g)
    _stage_weights(wbra_hbm, wbra_ref, stage_s, sem_s)
    _stage_weights(wbrb_hbm, wbrb_ref, stage_s, sem_s)
    _stage_weights(wout_hbm, wout_ref, stage_s, sem_s)
    x = x_ref[...]
    xn = _rms_mod(x, g_ref[...], sh_ref[...], sc_ref[...]).astype(BF16)
    num = pw_ref[...] * qc_ref[...] + sv_ref[...]
    hm = num / jnp.maximum(jnp.abs(den_ref[...]), em_ref[...])
    hb = (jax.nn.sigmoid(_dot(xn, wo_ref[...])) * hm).astype(BF16)
    g_a = _dot(xn, wg_ref[:, 0:d])
    g_b = _dot(xn, wg_ref[:, d:2 * d])
    merged = (jax.nn.sigmoid(g_a) * _dot(ha_ref[...].astype(BF16), wbra_ref[...])
              + jax.nn.sigmoid(g_b) * _dot(hb, wbrb_ref[...]))
    o_ref[...] = x + ga_ref[...] * _dot(merged.astype(BF16), wout_ref[...])


def _dec_post(x2, mod, gain, w_in, w_g, wbra, wbrb, wout, ha, qc, pw, sv, den, em):
    rows, d = x2.shape
    full = lambda shape: pl.BlockSpec(shape, lambda i, _n=len(shape): (0,) * _n)
    mspec = lambda j: pl.BlockSpec((rows, d), lambda i: (0, j))
    args = [x2, mod, mod, mod, gain.reshape(1, d), w_in, w_g, wbra, wbrb, wout, ha, qc, pw, sv, den, em]
    in_specs = ([full(x2.shape), mspec(3), mspec(4), mspec(5), full((1, d))] + [_HBM] * 5
                + [full(a.shape) for a in args[10:]])
    return pl.pallas_call(
        _dec_post_kernel,
        grid=(1,),
        in_specs=in_specs,
        out_specs=full((rows, d)),
        out_shape=jax.ShapeDtypeStruct((rows, d), F32),
        scratch_shapes=[pltpu.VMEM((d, d), BF16), pltpu.VMEM((d, 2 * d), BF16),
                        pltpu.VMEM((d, d), BF16), pltpu.VMEM((d, d), BF16), pltpu.VMEM((d, d), BF16)]
        + _stage_scratch(STAGE_ROWS // 2, 2 * d) + _stage_scratch(STAGE_ROWS, d),
        compiler_params=_cparams(("arbitrary",)),
        name="decode_post",
    )(*args)


def _pair_blocks(w):
    nblk, c, _ = w.shape
    z = jnp.zeros((nblk // 2, c, c), w.dtype)
    top = jnp.concatenate([w[0::2], z], axis=2)
    bot = jnp.concatenate([z, w[1::2]], axis=2)
    return jnp.concatenate([top, bot], axis=1).astype(BF16)


def kernel(x_prompt, x_sample, state_conv, state_rg_h, state_C, state_n, state_m, c_prompt, c_sample, w_ada, b_ada, g_norm1, w_ff1_in, w_ff1_out, g_norm2, w_in, conv_w, conv_b, w_rg_a, b_rg_a, w_rg_i, b_rg_i, rg_lambda, b_ig, b_fg, w_br_a, w_br_b, w_out, g_norm3, w_ff2_in, w_ff2_out, g_final):
    nb, seq, d = x_prompt.shape
    ns = x_sample.shape[0]
    depth = w_ada.shape[0]
    assert depth == 1 and x_sample.shape[1] == 1 and nb == SUBLANES
    assert seq % RG_STEPS == 0 and seq % ML_ROWS == 0 and seq % FFN_ROWS == 0
    heads, dk = M_HEADS, d // M_HEADS

    w_in0 = w_in[0]
    gate0 = 5 * d
    w_gates = w_in0[:, gate0:gate0 + 2 * heads]
    w_g = w_in0[:, gate0 + 2 * heads:]
    wif = jnp.pad(w_gates, ((0, 0), (0, LANES - 2 * heads))).astype(BF16)
    wift = w_gates.T.astype(BF16)
    b_gates = jnp.concatenate([b_ig[0], b_fg[0]])
    bif = jnp.pad(b_gates, (0, LANES - 2 * heads)).reshape(1, LANES)
    bift = b_gates.reshape(2 * heads, 1)
    wqt = w_in0[:, d:2 * d].T
    wkt = w_in0[:, 2 * d:3 * d].T
    wa = _pair_blocks(w_rg_a[0])
    wi = _pair_blocks(w_rg_i[0])
    wf1i, wf1o, wf2i, wf2o = w_ff1_in[0], w_ff1_out[0], w_ff2_in[0], w_ff2_out[0]
    wbra, wbrb, wout = w_br_a[0], w_br_b[0], w_out[0]

    mod = _ada(jnp.concatenate([c_prompt, c_sample], axis=0), w_ada[0], b_ada[0])
    mod_p2 = mod[:nb]
    mod_p3 = mod_p2.reshape(nb, 1, N_MOD * d)
    mod_s = mod[nb:]

    xp = x_prompt.reshape(nb * seq, d)
    xp = _ffn(xp, mod_p3, 0, g_norm1[0], wf1i, wf1o, None, rows_per_seq=seq, per_row_mod=False)
    ha, conv_p, h_p = _rglru_prompt(xp.reshape(nb, seq, d), mod_p2, g_norm2[0], w_in0, conv_w[0], conv_b[0],
                                    wa, wi, b_rg_a[0], b_rg_i[0], rg_lambda[0])
    xp, c_p, n_p, m_p = _mlstm_prompt(xp, ha.reshape(nb * seq, d), mod_p3, g_norm2[0], w_in0, w_g, wif, wift,
                                      bif, bift, wbra, wbrb, wout, nb=nb, seq=seq)
    yp = _ffn(xp, mod_p3, 6, g_norm3[0], wf2i, wf2o, g_final, rows_per_seq=seq, per_row_mod=False)

    xs = x_sample.reshape(ns, d)
    xs = _ffn(xs, mod_s, 0, g_norm1[0], wf1i, wf1o, None, rows_per_seq=1, per_row_mod=True)
    conv0 = jnp.swapaxes(state_conv[0], 0, 1)
    (ha_s, conv_s, h_s, n_s, m_s, qt, kt, decay, wv, pw, sv, den, em) = _dec_pre(
        xs, mod_s, g_norm2[0], w_in0, wqt, wkt, wif, bif, conv_w[0], conv_b[0], wa, wi,
        b_rg_a[0], b_rg_i[0], rg_lambda[0], conv0, state_rg_h[0], state_n[0].reshape(ns, d), state_m[0])
    c_s, qc = _dec_mem(qt, kt, decay, wv, state_C[0])
    xs = _dec_post(xs, mod_s, g_norm2[0], w_in0, w_g, wbra, wbrb, wout, ha_s, qc.reshape(ns, d), pw, sv, den, em)
    ys = _ffn(xs, mod_s, 6, g_norm3[0], wf2i, wf2o, g_final, rows_per_seq=1, per_row_mod=True)

    return (yp.reshape(nb, seq, d), ys.reshape(ns, 1, d),
            jnp.swapaxes(conv_p, 0, 1)[None], h_p[None], c_p[None], n_p[None], m_p.reshape(1, nb, heads),
            jnp.swapaxes(conv_s, 0, 1)[None], h_s[None], c_s[None], n_s.reshape(1, ns, heads, dk), m_s[:, ::dk][None])
```

```python
import functools

import jax
import jax.numpy as jnp
from jax import lax
from jax.experimental import pallas as pl
from jax.experimental.pallas import tpu as pltpu

F32 = jnp.float32
BF16 = jnp.bfloat16

EPS = 1e-6
RG_C = 8.0
CONV_W = 4
N_MOD = 9
M_HEADS = 4
RG_BLOCKS = 8

SUBLANES = 8
LANES = 128
MXU_DIM = 256
VMEM_LIMIT_BYTES = 60 * 1024 * 1024

FFN_ROWS = 1024
FFN_COLS = 256
RG_STEPS = 64
ML_ROWS = 512
ML_CHUNK = 256
DEC_BLOCK = 2
STAGE_ROWS = 128
STAGE_SLOTS = 8
STAGE_ROWS_WIDE = 64
STAGE_SLOTS_WIDE = 4
GATE_ROWS = 16


def _cparams(sem):
    return pltpu.CompilerParams(dimension_semantics=sem, vmem_limit_bytes=VMEM_LIMIT_BYTES)


def _resident(shape):
    nd = len(shape)
    return pl.BlockSpec(shape, lambda *_: (0,) * nd, pipeline_mode=pl.Buffered(1))


_HBM = pl.BlockSpec(memory_space=pl.ANY)


def _rms_mod(x, gain, shift, scale):
    ms = jnp.mean(x * x, axis=-1, keepdims=True)
    return x * lax.rsqrt(ms + EPS) * gain * (1.0 + scale) + shift


def _dot(a, b):
    return jnp.dot(a, b, preferred_element_type=F32)


def _dot_nt(a, b):
    return lax.dot_general(a, b, (((1,), (1,)), ((), ())), preferred_element_type=F32)


def _dot_tn(a, b):
    return lax.dot_general(a, b, (((0,), (0,)), ((), ())), preferred_element_type=F32)


def _split3(x):
    hi = x.astype(BF16)
    r1 = x - hi.astype(F32)
    mid = r1.astype(BF16)
    lo = (r1 - mid.astype(F32)).astype(BF16)
    return hi, mid, lo


def _stage_rows(src, src_row0, nrows, dst, dst_row0, stage, sem):
    slots, rows_per_copy, _ = stage.shape
    n = nrows // rows_per_copy
    assert n * rows_per_copy == nrows

    def copy(c):
        return pltpu.make_async_copy(src.at[pl.ds(src_row0 + c * rows_per_copy, rows_per_copy)],
                                     stage.at[c % slots], sem.at[c % slots])

    for c in range(min(slots, n)):
        copy(c).start()
    for c in range(n):
        copy(c).wait()
        r0 = dst_row0 + c * rows_per_copy
        dst[r0:r0 + rows_per_copy, :] = stage[c % slots].astype(BF16)
        if c + slots < n:
            copy(c + slots).start()


def _stage_scratch(cols, wide=False):
    slots, rows = (STAGE_SLOTS_WIDE, STAGE_ROWS_WIDE) if wide else (STAGE_SLOTS, STAGE_ROWS)
    return [pltpu.VMEM((slots, rows, cols), F32), pltpu.SemaphoreType.DMA((slots,))]


def _ada_kernel(c_ref, w_ref, b_ref, o_ref):
    c = c_ref[...]
    sc = (c * jax.nn.sigmoid(c)).astype(BF16)
    o_ref[...] = _dot(sc, w_ref[...].astype(BF16)) + b_ref[...]


def _ada(c, w_ada, b_ada):
    rows, d = c.shape
    n = w_ada.shape[1]
    tn = d
    return pl.pallas_call(
        _ada_kernel,
        grid=(n // tn,),
        in_specs=[pl.BlockSpec((rows, d), lambda j: (0, 0)),
                  pl.BlockSpec((d, tn), lambda j: (0, j)),
                  pl.BlockSpec((1, tn), lambda j: (0, j))],
        out_specs=pl.BlockSpec((rows, tn), lambda j: (0, j)),
        out_shape=jax.ShapeDtypeStruct((rows, n), F32),
        compiler_params=_cparams(("arbitrary",)),
        name="adaln_mod",
    )(c, w_ada, b_ada.reshape(1, n))


def _ffn_kernel(*refs, d_ff, final):
    if final:
        x_ref, sh_ref, sc_ref, ga_ref, g_ref, wi_hbm, wo_hbm, gf_ref, o_ref = refs[:9]
    else:
        x_ref, sh_ref, sc_ref, ga_ref, g_ref, wi_hbm, wo_hbm, o_ref = refs[:8]
    act_ref, wi_ref, wo_ref, stage_i, sem_i, stage_o, sem_o = refs[-7:]

    @pl.when(pl.program_id(0) == 0)
    def _():
        _stage_rows(wi_hbm, 0, wi_ref.shape[0], wi_ref, 0, stage_i, sem_i)
        _stage_rows(wo_hbm, 0, wo_ref.shape[0], wo_ref, 0, stage_o, sem_o)

    x = x_ref[...]
    xn = _rms_mod(x, g_ref[...], sh_ref[...], sc_ref[...]).astype(BF16)
    for j in range(d_ff // FFN_COLS):
        lo = j * FFN_COLS
        hg = _dot(xn, wi_ref[:, lo:lo + FFN_COLS])
        hu = _dot(xn, wi_ref[:, d_ff + lo:d_ff + lo + FFN_COLS])
        act_ref[:, lo:lo + FFN_COLS] = (hg * jax.nn.sigmoid(hg) * hu).astype(BF16)
    y = _dot(act_ref[...], wo_ref[...])
    out = x + (0.5 * ga_ref[...]) * y
    if final:
        ms = jnp.mean(out * out, axis=-1, keepdims=True)
        out = out * lax.rsqrt(ms + EPS) * gf_ref[...]
    o_ref[...] = out


def _mod_spec_prompt(j, tiles_per_seq, d):
    return pl.BlockSpec((None, 1, d), lambda i: (i // tiles_per_seq, 0, j))


def _ffn(x2, mod, j0, gain, wi, wo, g_final, *, rows_per_seq, per_row_mod):
    rows, d = x2.shape
    d_ff = wo.shape[0]
    if per_row_mod:
        tm = rows
        mspec = lambda j: pl.BlockSpec((tm, d), lambda i: (0, j))
    else:
        tm = min(FFN_ROWS, rows_per_seq)
        mspec = lambda j: _mod_spec_prompt(j, rows_per_seq // tm, d)
    final = g_final is not None
    in_specs = [pl.BlockSpec((tm, d), lambda i: (i, 0)),
                mspec(j0), mspec(j0 + 1), mspec(j0 + 2),
                _resident((1, d)), _HBM, _HBM]
    args = [x2, mod, mod, mod, gain.reshape(1, d), wi, wo]
    if final:
        in_specs.append(_resident((1, d)))
        args.append(g_final.reshape(1, d))
    return pl.pallas_call(
        functools.partial(_ffn_kernel, d_ff=d_ff, final=final),
        grid=(rows // tm,),
        in_specs=in_specs,
        out_specs=pl.BlockSpec((tm, d), lambda i: (i, 0)),
        out_shape=jax.ShapeDtypeStruct((rows, d), F32),
        scratch_shapes=[pltpu.VMEM((tm, d_ff), BF16), pltpu.VMEM(wi.shape, BF16), pltpu.VMEM(wo.shape, BF16)]
        + _stage_scratch(wi.shape[1], wide=True) + _stage_scratch(wo.shape[1]),
        compiler_params=_cparams(("arbitrary",)),
        name="ffn_final" if final else "ffn",
    )(*args)


def _rg_gates(u, wa_ref, wi_ref, ba, bi, lam):
    ub = u.astype(BF16)
    ra, ri = [], []
    for p in range(wa_ref.shape[0]):
        blk = ub[:, p * MXU_DIM:(p + 1) * MXU_DIM]
        ra.append(_dot(blk, wa_ref[p]))
        ri.append(_dot(blk, wi_ref[p]))
    r = jax.nn.sigmoid(jnp.concatenate(ra, axis=-1) + ba)
    i_g = jax.nn.sigmoid(jnp.concatenate(ri, axis=-1) + bi)
    log_a = -RG_C * r * jax.nn.softplus(-lam)
    a = jnp.exp(log_a)
    b = jnp.sqrt(-jnp.tanh(log_a) * (a * a + 1.0)) * (i_g * u)
    return a, b


def _rglru_kernel(x_ref, sh_ref, sc_ref, g_ref, wint_hbm, cw_ref, cb_ref, wa_ref, wi_ref, ba_ref, bi_ref,
                  lam_ref, ha_ref, conv_ref, hl_ref,
                  xn_ref, bm_ref, tb_ref, a_ref, b_ref, hs_ref, h_ref, w_ref, stage, sem):
    nb, tt, d = x_ref.shape
    tail = (CONV_W - 1) * nb
    rows = nb * tt
    t_idx = pl.program_id(0)

    @pl.when(t_idx == 0)
    def _():
        _stage_rows(wint_hbm, 0, d, w_ref, 0, stage, sem)
        tb_ref[0:tail, :] = jnp.zeros((tail, d), F32)
        h_ref[...] = jnp.zeros_like(h_ref)

    for bq in range(nb):
        xn = _rms_mod(x_ref[bq], g_ref[...], sh_ref[bq:bq + 1, :], sc_ref[bq:bq + 1, :])
        xn_ref[bq * tt:(bq + 1) * tt, :] = xn.astype(BF16)
    x_rg = _dot_nt(xn_ref[...], w_ref[...])
    for cb in range(d // LANES):
        bm_ref[cb] = x_rg[:, cb * LANES:(cb + 1) * LANES]

    def to_time_major(t, carry):
        dst = pl.multiple_of(tail + t * nb, nb)
        for cb in range(d // LANES):
            tb_ref[pl.ds(dst, nb), cb * LANES:(cb + 1) * LANES] = bm_ref[cb, pl.ds(t, nb, stride=tt), :]
        return carry
    lax.fori_loop(0, tt, to_time_major, 0, unroll=8)

    u = cb_ref[...]
    for j in range(CONV_W):
        u = u + tb_ref[j * nb:j * nb + rows, :] * cw_ref[j:j + 1, :]
    new_tail = tb_ref[rows:rows + tail, :]
    for j in range(CONV_W - 1):
        conv_ref[j] = new_tail[j * nb:(j + 1) * nb, :]
    tb_ref[0:tail, :] = new_tail

    a, b = _rg_gates(u, wa_ref, wi_ref, ba_ref[...], bi_ref[...], lam_ref[...])
    a_ref[...] = a
    b_ref[...] = b

    def step(t, h):
        src = pl.multiple_of(t * nb, nb)
        h = a_ref[pl.ds(src, nb), :] * h + b_ref[pl.ds(src, nb), :]
        for cb in range(d // LANES):
            hs_ref[cb, pl.ds(src, nb), :] = h[:, cb * LANES:(cb + 1) * LANES]
        return h
    h_last = lax.fori_loop(0, tt, step, h_ref[...], unroll=8)
    h_ref[...] = h_last
    hl_ref[...] = h_last

    for bq in range(nb):
        for cb in range(d // LANES):
            ha_ref[bq, :, cb * LANES:(cb + 1) * LANES] = hs_ref[cb, pl.ds(bq, tt, stride=nb), :].astype(BF16)


def _rglru_prompt(x3, mod2, gain, w_int, conv_w, conv_b, wa, wi, ba, bi, lam):
    nb, seq, d = x3.shape
    tt = min(RG_STEPS, seq)
    rows = nb * tt
    tail = (CONV_W - 1) * nb
    row = lambda v: v.reshape(1, d)
    return pl.pallas_call(
        _rglru_kernel,
        grid=(seq // tt,),
        in_specs=[pl.BlockSpec((nb, tt, d), lambda t: (0, t, 0)),
                  pl.BlockSpec((nb, d), lambda t: (0, 3)),
                  pl.BlockSpec((nb, d), lambda t: (0, 4)),
                  _resident((1, d)), _HBM, _resident(conv_w.shape), _resident((1, d)),
                  _resident(wa.shape), _resident(wi.shape), _resident((1, d)), _resident((1, d)),
                  _resident((1, d))],
        out_specs=[pl.BlockSpec((nb, tt, d), lambda t: (0, t, 0)),
                   pl.BlockSpec((CONV_W - 1, nb, d), lambda t: (0, 0, 0)),
                   pl.BlockSpec((nb, d), lambda t: (0, 0))],
        out_shape=[jax.ShapeDtypeStruct((nb, seq, d), BF16),
                   jax.ShapeDtypeStruct((CONV_W - 1, nb, d), F32),
                   jax.ShapeDtypeStruct((nb, d), F32)],
        scratch_shapes=[pltpu.VMEM((rows, d), BF16),
                        pltpu.VMEM((d // LANES, rows, LANES), F32),
                        pltpu.VMEM((tail + rows, d), F32),
                        pltpu.VMEM((rows, d), F32),
                        pltpu.VMEM((rows, d), F32),
                        pltpu.VMEM((d // LANES, rows, LANES), F32),
                        pltpu.VMEM((nb, d), F32),
                        pltpu.VMEM((d, d), BF16)]
        + _stage_scratch(d),
        compiler_params=_cparams(("arbitrary",)),
        name="rglru_prompt",
    )(x3, mod2, mod2, row(gain), w_int, conv_w, row(conv_b), wa, wi, row(ba), row(bi), row(lam))


def _mlstm_kernel(x_ref, ha_ref, sh_ref, sc_ref, ga_ref, g_ref, wint_hbm, bif_ref, bift_ref,
                  wbra_hbm, wbrb_hbm, wout_hbm, o_ref, c_ref, n_ref, m_ref,
                  hm_ref, wm_ref, wif_ref, wbra_ref, wbrb_ref, wout_ref, stage, sem):
    tl, d = x_ref.shape
    dk = d // M_HEADS
    cs = min(ML_CHUNK, tl)
    scale = dk ** -0.5
    gate_row0 = 5 * d

    @pl.when((pl.program_id(0) == 0) & (pl.program_id(1) == 0))
    def _():
        _stage_rows(wint_hbm, d, 4 * d, wm_ref, 0, stage, sem)
        _stage_rows(wint_hbm, gate_row0 + 2 * M_HEADS, 2 * d, wm_ref, 4 * d, stage, sem)
        _stage_rows(wint_hbm, gate_row0, LANES, wif_ref, 0, stage, sem)
        _stage_rows(wbra_hbm, 0, d, wbra_ref, 0, stage, sem)
        _stage_rows(wbrb_hbm, 0, d, wbrb_ref, 0, stage, sem)
        _stage_rows(wout_hbm, 0, d, wout_ref, 0, stage, sem)

    @pl.when(pl.program_id(1) == 0)
    def _():
        c_ref[...] = jnp.zeros_like(c_ref)
        n_ref[...] = jnp.zeros_like(n_ref)
        m_ref[...] = jnp.zeros_like(m_ref)

    x = x_ref[...]
    xn = _rms_mod(x, g_ref[...], sh_ref[...], sc_ref[...]).astype(BF16)

    pre_c = _dot_nt(xn, wif_ref[...]) + bif_ref[...]
    pre_r = _dot_nt(wif_ref[0:GATE_ROWS, :], xn) + bift_ref[...]
    col_is_f = lax.broadcasted_iota(jnp.int32, pre_c.shape, 1) >= M_HEADS
    row_is_f = lax.broadcasted_iota(jnp.int32, pre_r.shape, 0) >= M_HEADS
    gate_c = jnp.where(col_is_f, jax.nn.log_sigmoid(pre_c), pre_c)
    gate_r = jnp.where(row_is_f, jax.nn.log_sigmoid(pre_r), pre_r)

    ti = lax.broadcasted_iota(jnp.int32, (cs, cs), 0)
    si = lax.broadcasted_iota(jnp.int32, (cs, cs), 1)
    causal = si <= ti
    lower = causal.astype(BF16)
    upper = (ti <= si).astype(BF16)

    chunks = []
    for c in range(tl // cs):
        gc = gate_c[c * cs:(c + 1) * cs, :]
        gr = gate_r[:, c * cs:(c + 1) * cs]
        cum_c = sum(_dot(lower, part) for part in _split3(gc))
        cum_r = sum(_dot(part, upper) for part in _split3(gr))
        chunks.append((gc, gr, cum_c, cum_r))

    for h in range(M_HEADS):
        c0 = h * dk
        q_all = _dot_nt(xn, wm_ref[c0:c0 + dk, :])
        k_all = _dot_nt(xn, wm_ref[d + c0:d + c0 + dk, :])
        v_all = _dot_nt(xn, wm_ref[2 * d + c0:2 * d + c0 + dk, :])
        for c, (gc, gr, cum_c, cum_r) in enumerate(chunks):
            r0 = c * cs
            q = q_all[r0:r0 + cs, :]
            k = k_all[r0:r0 + cs, :]
            v = v_all[r0:r0 + cs, :]
            bc = cum_c[:, M_HEADS + h:M_HEADS + h + 1]
            br = cum_r[M_HEADS + h:M_HEADS + h + 1, :]
            ig_c = gc[:, h:h + 1]
            ig_r = gr[h:h + 1, :]
            m_prev = m_ref[h:h + 1, :]
            c_prev = c_ref[h]
            n_prev = n_ref[h:h + 1, :]

            log_d = jnp.where(causal, bc - br + ig_r, -jnp.inf)
            log_past = bc + m_prev
            m_t = jnp.maximum(log_past, jnp.max(log_d, axis=-1, keepdims=True))
            d_mat = jnp.exp(log_d - m_t)
            past_w = jnp.exp(log_past - m_t)
            qs = q * scale
            qsb = qs.astype(BF16)
            kb = k.astype(BF16)
            vb = v.astype(BF16)
            s = _dot_nt(qsb, kb) * d_mat
            num = past_w * _dot(qsb, c_prev.astype(BF16)) + _dot(s.astype(BF16), vb)
            den = past_w * jnp.sum(qs * n_prev, axis=-1, keepdims=True) + jnp.sum(s, axis=-1, keepdims=True)
            hm_ref[r0:r0 + cs, c0:c0 + dk] = num / jnp.maximum(jnp.abs(den), jnp.exp(-m_t))

            m_new = m_t[cs - 1:cs, :]
            b_last = bc[cs - 1:cs, :]
            w_s = jnp.exp(b_last - bc + ig_c - m_new)
            decay = jnp.exp(b_last + m_prev - m_new)
            kw = w_s * k
            c_ref[h] = decay * c_prev + _dot_tn(kw.astype(BF16), vb)
            n_ref[h:h + 1, :] = decay * n_prev + jnp.sum(kw, axis=0, keepdims=True)
            m_ref[h:h + 1, :] = m_new

    o_pre = _dot_nt(xn, wm_ref[3 * d:4 * d, :])
    hb = (jax.nn.sigmoid(o_pre) * hm_ref[...]).astype(BF16)
    g_a = _dot_nt(xn, wm_ref[4 * d:5 * d, :])
    g_b = _dot_nt(xn, wm_ref[5 * d:6 * d, :])
    merged = (jax.nn.sigmoid(g_a) * _dot(ha_ref[...], wbra_ref[...])
              + jax.nn.sigmoid(g_b) * _dot(hb, wbrb_ref[...]))
    o_ref[...] = x + ga_ref[...] * _dot(merged.astype(BF16), wout_ref[...])


def _mlstm_prompt(x2, ha2, mod3, gain, w_int, bif, bift, wbra, wbrb, wout, *, nb, seq):
    rows, d = x2.shape
    dk = d // M_HEADS
    tl = min(ML_ROWS, seq)
    nt = seq // tl
    mspec = lambda j: pl.BlockSpec((None, 1, d), lambda b, t: (b, 0, j))
    tile = lambda: pl.BlockSpec((tl, d), lambda b, t: (b * nt + t, 0))
    return pl.pallas_call(
        _mlstm_kernel,
        grid=(nb, nt),
        in_specs=[tile(), tile(), mspec(3), mspec(4), mspec(5), _resident((1, d)),
                  _HBM, _resident(bif.shape), _resident(bift.shape), _HBM, _HBM, _HBM],
        out_specs=[tile(),
                   pl.BlockSpec((None, M_HEADS, dk, dk), lambda b, t: (b, 0, 0, 0)),
                   pl.BlockSpec((None, M_HEADS, dk), lambda b, t: (b, 0, 0)),
                   pl.BlockSpec((None, M_HEADS, 1), lambda b, t: (b, 0, 0))],
        out_shape=[jax.ShapeDtypeStruct((rows, d), F32),
                   jax.ShapeDtypeStruct((nb, M_HEADS, dk, dk), F32),
                   jax.ShapeDtypeStruct((nb, M_HEADS, dk), F32),
                   jax.ShapeDtypeStruct((nb, M_HEADS, 1), F32)],
        scratch_shapes=[pltpu.VMEM((tl, d), F32),
                        pltpu.VMEM((6 * d, d), BF16),
                        pltpu.VMEM((LANES, d), BF16),
                        pltpu.VMEM((d, d), BF16), pltpu.VMEM((d, d), BF16), pltpu.VMEM((d, d), BF16)]
        + _stage_scratch(d),
        compiler_params=_cparams(("arbitrary", "arbitrary")),
        name="mlstm_prompt",
    )(x2, ha2, mod3, mod3, mod3, gain.reshape(1, d), w_int, bif, bift, wbra, wbrb, wout)


def _head_sum(x, dk):
    parts = []
    for h in range(x.shape[1] // dk):
        sl = x[:, h * dk:(h + 1) * dk]
        parts.append(jnp.broadcast_to(jnp.sum(sl, axis=-1, keepdims=True), sl.shape))
    return jnp.concatenate(parts, axis=-1)


def _head_spread(cols, lane0, dk, rows):
    return jnp.concatenate(
        [jnp.broadcast_to(cols[:, lane0 + h:lane0 + h + 1], (rows, dk)) for h in range(M_HEADS)], axis=-1)


def _dec_pre_kernel(x_ref, sh_ref, sc_ref, g_ref, wint_hbm, bif_ref,
                    cw_ref, cb_ref, wa_ref, wi_ref, ba_ref, bi_ref, lam_ref,
                    conv0_ref, h0_ref, n0_ref, m0_ref,
                    ha_ref, conv_ref, hs_ref, n_ref, m_ref, qt_ref, kt_ref, dec_ref, wv_ref, pw_ref, sv_ref,
                    den_ref, em_ref,
                    wm_ref, wif_ref, stage, sem):
    rows, d = x_ref.shape
    dk = d // M_HEADS
    scale = dk ** -0.5
    _stage_rows(wint_hbm, 0, 4 * d, wm_ref, 0, stage, sem)
    _stage_rows(wint_hbm, 5 * d, LANES, wif_ref, 0, stage, sem)
    xn = _rms_mod(x_ref[...], g_ref[...], sh_ref[...], sc_ref[...]).astype(BF16)

    x_rg = _dot_nt(xn, wm_ref[0:d, :])
    u = cb_ref[...]
    for j in range(CONV_W - 1):
        u = u + conv0_ref[j] * cw_ref[j:j + 1, :]
        if j > 0:
            conv_ref[j - 1] = conv0_ref[j]
    u = u + x_rg * cw_ref[CONV_W - 1:CONV_W, :]
    conv_ref[CONV_W - 2] = x_rg
    a, b = _rg_gates(u, wa_ref, wi_ref, ba_ref[...], bi_ref[...], lam_ref[...])
    h = a * h0_ref[...] + b
    hs_ref[...] = h
    ha_ref[...] = h

    q = _dot_nt(xn, wm_ref[d:2 * d, :])
    k = _dot_nt(xn, wm_ref[2 * d:3 * d, :])
    v = _dot_nt(xn, wm_ref[3 * d:4 * d, :])
    qt_ref[...] = _dot_nt(wm_ref[d:2 * d, :], xn).astype(BF16)
    kt_ref[...] = _dot_nt(wm_ref[2 * d:3 * d, :], xn).astype(BF16)
    pre = _dot_nt(xn, wif_ref[...]) + bif_ref[...]
    ig = _head_spread(pre, 0, dk, rows)
    lf = jax.nn.log_sigmoid(_head_spread(pre, M_HEADS, dk, rows))
    m0 = _head_spread(m0_ref[...], 0, dk, rows)
    n0 = n0_ref[...]
    log_past = lf + m0
    m_t = jnp.maximum(log_past, ig)
    d_w = jnp.exp(ig - m_t)
    past_w = jnp.exp(log_past - m_t)
    qs = q * scale
    s = _head_sum(qs * k, dk) * d_w
    den_ref[...] = past_w * _head_sum(qs * n0, dk) + s
    em_ref[...] = jnp.exp(-m_t)
    pw_ref[...] = past_w
    sv_ref[...] = s * v
    w_s = jnp.exp(ig - m_t)
    decay = jnp.exp(lf + m0 - m_t)
    dec_ref[...] = decay
    wv_ref[...] = w_s * v
    n_ref[...] = decay * n0 + w_s * k
    m_ref[...] = m_t


def _dec_pre(x2, mod, gain, w_int, bif, conv_w, conv_b, wa, wi, ba, bi, lam, conv0, h0, n0, m0):
    rows, d = x2.shape
    row = lambda v: v.reshape(1, d)
    m0p = jnp.pad(m0, ((0, 0), (0, LANES - M_HEADS)))
    full = lambda shape: pl.BlockSpec(shape, lambda i, _n=len(shape): (0,) * _n)
    mspec = lambda j: pl.BlockSpec((rows, d), lambda i: (0, j))
    vec = jax.ShapeDtypeStruct((rows, d), F32)
    outs = [vec,
            jax.ShapeDtypeStruct((CONV_W - 1, rows, d), F32),
            vec,
            vec,
            vec,
            jax.ShapeDtypeStruct((d, rows), BF16),
            jax.ShapeDtypeStruct((d, rows), BF16),
            vec, vec, vec, vec, vec, vec]
    args = [x2, mod, mod, row(gain), w_int, bif, conv_w, row(conv_b), wa, wi,
            row(ba), row(bi), row(lam), conv0, h0, n0, m0p]
    in_specs = ([full(x2.shape), mspec(3), mspec(4), full((1, d)), _HBM]
                + [full(a.shape) for a in args[5:]])
    return pl.pallas_call(
        _dec_pre_kernel,
        grid=(1,),
        in_specs=in_specs,
        out_specs=[full(o.shape) for o in outs],
        out_shape=outs,
        scratch_shapes=[pltpu.VMEM((4 * d, d), BF16), pltpu.VMEM((LANES, d), BF16)] + _stage_scratch(d),
        compiler_params=_cparams(("arbitrary",)),
        name="decode_pre",
    )(*args)


def _dec_mem_kernel(qt_ref, kt_ref, dec_ref, wv_ref, c0_ref, c_ref, qc_ref):
    bb = c0_ref.shape[0]
    dk = c0_ref.shape[2]
    nseq = qt_ref.shape[1]
    scale = dk ** -0.5
    base = pl.program_id(0) * bb
    seq_id = lax.broadcasted_iota(jnp.int32, (nseq, dk), 0)
    for j in range(bb):
        pick = (seq_id == base + j).astype(BF16)
        qcol = _dot(qt_ref[...], pick) * scale
        kcol = _dot(kt_ref[...], pick)
        for h in range(M_HEADS):
            c0 = c0_ref[j, h]
            dec = dec_ref[j, :, h * dk:(h + 1) * dk]
            wv = wv_ref[j, :, h * dk:(h + 1) * dk]
            c_ref[j, h] = dec * c0 + kcol[h * dk:(h + 1) * dk, :] * wv
            qc_ref[j, :, h * dk:(h + 1) * dk] = jnp.sum(qcol[h * dk:(h + 1) * dk, :] * c0, axis=0, keepdims=True)


def _dec_mem(qt, kt, decay, wv, c0):
    nseq, heads, dk, _ = c0.shape
    d = heads * dk
    bb = DEC_BLOCK if nseq % DEC_BLOCK == 0 else 1
    rowblk = pl.BlockSpec((bb, 1, d), lambda i: (i, 0, 0))
    return pl.pallas_call(
        _dec_mem_kernel,
        grid=(nseq // bb,),
        in_specs=[pl.BlockSpec((d, nseq), lambda i: (0, 0)), pl.BlockSpec((d, nseq), lambda i: (0, 0)),
                  rowblk, rowblk,
                  pl.BlockSpec((bb, heads, dk, dk), lambda i: (i, 0, 0, 0))],
        out_specs=[pl.BlockSpec((bb, heads, dk, dk), lambda i: (i, 0, 0, 0)), rowblk],
        out_shape=[jax.ShapeDtypeStruct(c0.shape, F32), jax.ShapeDtypeStruct((nseq, 1, d), F32)],
        compiler_params=_cparams(("arbitrary",)),
        name="decode_mem",
    )(qt, kt, decay.reshape(nseq, 1, d), wv.reshape(nseq, 1, d), c0)


def _dec_post_kernel(x_ref, sh_ref, sc_ref, ga_ref, g_ref, wint_hbm, wbra_hbm, wbrb_hbm, wout_hbm,
                     ha_ref, qc_ref, pw_ref, sv_ref, den_ref, em_ref, o_ref,
                     wm_ref, wbra_ref, wbrb_ref, wout_ref, stage, sem):
    d = x_ref.shape[1]
    _stage_rows(wint_hbm, 4 * d, d, wm_ref, 0, stage, sem)
    _stage_rows(wint_hbm, 5 * d + 2 * M_HEADS, 2 * d, wm_ref, d, stage, sem)
    _stage_rows(wbra_hbm, 0, d, wbra_ref, 0, stage, sem)
    _stage_rows(wbrb_hbm, 0, d, wbrb_ref, 0, stage, sem)
    _stage_rows(wout_hbm, 0, d, wout_ref, 0, stage, sem)
    x = x_ref[...]
    xn = _rms_mod(x, g_ref[...], sh_ref[...], sc_ref[...]).astype(BF16)
    num = pw_ref[...] * qc_ref[...] + sv_ref[...]
    hm = num / jnp.maximum(jnp.abs(den_ref[...]), em_ref[...])
    hb = (jax.nn.sigmoid(_dot_nt(xn, wm_ref[0:d, :])) * hm).astype(BF16)
    g_a = _dot_nt(xn, wm_ref[d:2 * d, :])
    g_b = _dot_nt(xn, wm_ref[2 * d:3 * d, :])
    merged = (jax.nn.sigmoid(g_a) * _dot(ha_ref[...].astype(BF16), wbra_ref[...])
              + jax.nn.sigmoid(g_b) * _dot(hb, wbrb_ref[...]))
    o_ref[...] = x + ga_ref[...] * _dot(merged.astype(BF16), wout_ref[...])


def _dec_post(x2, mod, gain, w_int, wbra, wbrb, wout, ha, qc, pw, sv, den, em):
    rows, d = x2.shape
    full = lambda shape: pl.BlockSpec(shape, lambda i, _n=len(shape): (0,) * _n)
    mspec = lambda j: pl.BlockSpec((rows, d), lambda i: (0, j))
    args = [x2, mod, mod, mod, gain.reshape(1, d), w_int, wbra, wbrb, wout, ha, qc, pw, sv, den, em]
    in_specs = ([full(x2.shape), mspec(3), mspec(4), mspec(5), full((1, d))] + [_HBM] * 4
                + [full(a.shape) for a in args[9:]])
    return pl.pallas_call(
        _dec_post_kernel,
        grid=(1,),
        in_specs=in_specs,
        out_specs=full((rows, d)),
        out_shape=jax.ShapeDtypeStruct((rows, d), F32),
        scratch_shapes=[pltpu.VMEM((3 * d, d), BF16),
                        pltpu.VMEM((d, d), BF16), pltpu.VMEM((d, d), BF16), pltpu.VMEM((d, d), BF16)]
        + _stage_scratch(d),
        compiler_params=_cparams(("arbitrary",)),
        name="decode_post",
    )(*args)


def _pair_blocks(w):
    nblk, c, _ = w.shape
    z = jnp.zeros((nblk // 2, c, c), w.dtype)
    top = jnp.concatenate([w[0::2], z], axis=2)
    bot = jnp.concatenate([z, w[1::2]], axis=2)
    return jnp.concatenate([top, bot], axis=1).astype(BF16)


def kernel(x_prompt, x_sample, state_conv, state_rg_h, state_C, state_n, state_m, c_prompt, c_sample, w_ada, b_ada, g_norm1, w_ff1_in, w_ff1_out, g_norm2, w_in, conv_w, conv_b, w_rg_a, b_rg_a, w_rg_i, b_rg_i, rg_lambda, b_ig, b_fg, w_br_a, w_br_b, w_out, g_norm3, w_ff2_in, w_ff2_out, g_final):
    nb, seq, d = x_prompt.shape
    ns = x_sample.shape[0]
    depth = w_ada.shape[0]
    assert depth == 1 and x_sample.shape[1] == 1 and nb == SUBLANES
    assert seq % RG_STEPS == 0 and seq % ML_ROWS == 0 and seq % FFN_ROWS == 0
    heads, dk = M_HEADS, d // M_HEADS
    assert w_in.shape[2] == 7 * d + 2 * heads and GATE_ROWS >= 2 * heads

    w_int = w_in[0].T
    b_gates = jnp.concatenate([b_ig[0], b_fg[0]])
    bif = jnp.pad(b_gates, (0, LANES - 2 * heads)).reshape(1, LANES)
    bift = jnp.pad(b_gates, (0, GATE_ROWS - 2 * heads)).reshape(GATE_ROWS, 1)
    wa = _pair_blocks(w_rg_a[0])
    wi = _pair_blocks(w_rg_i[0])
    wf1i, wf1o, wf2i, wf2o = w_ff1_in[0], w_ff1_out[0], w_ff2_in[0], w_ff2_out[0]
    wbra, wbrb, wout = w_br_a[0], w_br_b[0], w_out[0]

    mod = _ada(jnp.concatenate([c_prompt, c_sample], axis=0), w_ada[0], b_ada[0])
    mod_p2 = mod[:nb]
    mod_p3 = mod_p2.reshape(nb, 1, N_MOD * d)
    mod_s = mod[nb:]

    xp = x_prompt.reshape(nb * seq, d)
    xp = _ffn(xp, mod_p3, 0, g_norm1[0], wf1i, wf1o, None, rows_per_seq=seq, per_row_mod=False)
    ha, conv_p, h_p = _rglru_prompt(xp.reshape(nb, seq, d), mod_p2, g_norm2[0], w_int, conv_w[0], conv_b[0],
                                    wa, wi, b_rg_a[0], b_rg_i[0], rg_lambda[0])
    xp, c_p, n_p, m_p = _mlstm_prompt(xp, ha.reshape(nb * seq, d), mod_p3, g_norm2[0], w_int, bif, bift,
                                      wbra, wbrb, wout, nb=nb, seq=seq)
    yp = _ffn(xp, mod_p3, 6, g_norm3[0], wf2i, wf2o, g_final, rows_per_seq=seq, per_row_mod=False)

    xs = x_sample.reshape(ns, d)
    xs = _ffn(xs, mod_s, 0, g_norm1[0], wf1i, wf1o, None, rows_per_seq=1, per_row_mod=True)
    conv0 = jnp.swapaxes(state_conv[0], 0, 1)
    (ha_s, conv_s, h_s, n_s, m_s, qt, kt, decay, wv, pw, sv, den, em) = _dec_pre(
        xs, mod_s, g_norm2[0], w_int, bif, conv_w[0], conv_b[0], wa, wi,
        b_rg_a[0], b_rg_i[0], rg_lambda[0], conv0, state_rg_h[0], state_n[0].reshape(ns, d), state_m[0])
    c_s, qc = _dec_mem(qt, kt, decay, wv, state_C[0])
    xs = _dec_post(xs, mod_s, g_norm2[0], w_int, wbra, wbrb, wout, ha_s, qc.reshape(ns, d), pw, sv, den, em)
    ys = _ffn(xs, mod_s, 6, g_norm3[0], wf2i, wf2o, g_final, rows_per_seq=1, per_row_mod=True)

    return (yp.reshape(nb, seq, d), ys.reshape(ns, 1, d),
            jnp.swapaxes(conv_p, 0, 1)[None], h_p[None], c_p[None], n_p[None], m_p.reshape(1, nb, heads),
            jnp.swapaxes(conv_s, 0, 1)[None], h_s[None], c_s[None], n_s.reshape(1, ns, heads, dk), m_s[:, ::dk][None])
```

```python
import functools

import jax
import jax.numpy as jnp
from jax import lax
from jax.experimental import pallas as pl
from jax.experimental.pallas import tpu as pltpu

F32 = jnp.float32
BF16 = jnp.bfloat16

EPS = 1e-6
RG_C = 8.0
CONV_W = 4
N_MOD = 9
M_HEADS = 4
RG_BLOCKS = 8

SUBLANES = 8
LANES = 128
MXU_DIM = 256
VMEM_LIMIT_BYTES = 60 * 1024 * 1024

FFN_ROWS = 1024
FFN_COLS = 256
RG_STEPS = 64
ML_ROWS = 512
ML_CHUNK = 256
DEC_BLOCK = 2
STAGE_ROWS = 256
STAGE_SLOTS = 4
STAGE_ROWS_WIDE = 64
STAGE_SLOTS_WIDE = 4
GATE_ROWS = 16


def _cparams(sem):
    return pltpu.CompilerParams(dimension_semantics=sem, vmem_limit_bytes=VMEM_LIMIT_BYTES)


def _resident(shape):
    nd = len(shape)
    return pl.BlockSpec(shape, lambda *_: (0,) * nd, pipeline_mode=pl.Buffered(1))


_HBM = pl.BlockSpec(memory_space=pl.ANY)


def _rms_mod(x, gain, shift, scale):
    ms = jnp.mean(x * x, axis=-1, keepdims=True)
    return x * lax.rsqrt(ms + EPS) * gain * (1.0 + scale) + shift


def _dot(a, b):
    return jnp.dot(a, b, preferred_element_type=F32)


def _dot_nt(a, b):
    return lax.dot_general(a, b, (((1,), (1,)), ((), ())), preferred_element_type=F32)


def _dot_tn(a, b):
    return lax.dot_general(a, b, (((0,), (0,)), ((), ())), preferred_element_type=F32)


def _split3(x):
    hi = x.astype(BF16)
    r1 = x - hi.astype(F32)
    mid = r1.astype(BF16)
    lo = (r1 - mid.astype(F32)).astype(BF16)
    return hi, mid, lo


def _stage_rows(src, src_row0, nrows, dst, dst_row0, stage, sem):
    slots, rows_per_copy, _ = stage.shape
    n = pl.cdiv(nrows, rows_per_copy)

    def rows_of(c):
        return min(rows_per_copy, nrows - c * rows_per_copy)

    def copy(c):
        return pltpu.make_async_copy(src.at[pl.ds(src_row0 + c * rows_per_copy, rows_of(c))],
                                     stage.at[c % slots, pl.ds(0, rows_of(c))], sem.at[c % slots])

    for c in range(min(slots, n)):
        copy(c).start(priority=c % 2)
    for c in range(n):
        copy(c).wait()
        r0 = dst_row0 + c * rows_per_copy
        dst[r0:r0 + rows_of(c), :] = stage[c % slots, 0:rows_of(c), :].astype(BF16)
        if c + slots < n:
            copy(c + slots).start(priority=c % 2)


def _stage_scratch(cols, wide=False):
    slots, rows = (STAGE_SLOTS_WIDE, STAGE_ROWS_WIDE) if wide else (STAGE_SLOTS, STAGE_ROWS)
    return [pltpu.VMEM((slots, rows, cols), F32), pltpu.SemaphoreType.DMA((slots,))]


def _ada_kernel(c_ref, w_ref, b_ref, o_ref):
    c = c_ref[...]
    sc = (c * jax.nn.sigmoid(c)).astype(BF16)
    o_ref[...] = _dot(sc, w_ref[...].astype(BF16)) + b_ref[...]


def _ada(c, w_ada, b_ada):
    rows, d = c.shape
    n = w_ada.shape[1]
    tn = d
    return pl.pallas_call(
        _ada_kernel,
        grid=(n // tn,),
        in_specs=[pl.BlockSpec((rows, d), lambda j: (0, 0)),
                  pl.BlockSpec((d, tn), lambda j: (0, j)),
                  pl.BlockSpec((1, tn), lambda j: (0, j))],
        out_specs=pl.BlockSpec((rows, tn), lambda j: (0, j)),
        out_shape=jax.ShapeDtypeStruct((rows, n), F32),
        compiler_params=_cparams(("arbitrary",)),
        name="adaln_mod",
    )(c, w_ada, b_ada.reshape(1, n))


def _ffn_kernel(*refs, d_ff, final):
    if final:
        x_ref, sh_ref, sc_ref, ga_ref, g_ref, wi_hbm, wo_hbm, gf_ref, o_ref = refs[:9]
    else:
        x_ref, sh_ref, sc_ref, ga_ref, g_ref, wi_hbm, wo_hbm, o_ref = refs[:8]
    act_ref, wi_ref, wo_ref, stage_i, sem_i, stage_o, sem_o = refs[-7:]

    @pl.when(pl.program_id(0) == 0)
    def _():
        _stage_rows(wi_hbm, 0, wi_ref.shape[0], wi_ref, 0, stage_i, sem_i)
        _stage_rows(wo_hbm, 0, wo_ref.shape[0], wo_ref, 0, stage_o, sem_o)

    x = x_ref[...]
    xn = _rms_mod(x, g_ref[...], sh_ref[...], sc_ref[...]).astype(BF16)
    for j in range(d_ff // FFN_COLS):
        lo = j * FFN_COLS
        hg = _dot(xn, wi_ref[:, lo:lo + FFN_COLS])
        hu = _dot(xn, wi_ref[:, d_ff + lo:d_ff + lo + FFN_COLS])
        act_ref[:, lo:lo + FFN_COLS] = (hg * jax.nn.sigmoid(hg) * hu).astype(BF16)
    y = _dot(act_ref[...], wo_ref[...])
    out = x + (0.5 * ga_ref[...]) * y
    if final:
        ms = jnp.mean(out * out, axis=-1, keepdims=True)
        out = out * lax.rsqrt(ms + EPS) * gf_ref[...]
    o_ref[...] = out


def _mod_spec_prompt(j, tiles_per_seq, d):
    return pl.BlockSpec((None, 1, d), lambda i: (i // tiles_per_seq, 0, j))


def _ffn(x2, mod, j0, gain, wi, wo, g_final, *, rows_per_seq, per_row_mod):
    rows, d = x2.shape
    d_ff = wo.shape[0]
    if per_row_mod:
        tm = rows
        mspec = lambda j: pl.BlockSpec((tm, d), lambda i: (0, j))
    else:
        tm = min(FFN_ROWS, rows_per_seq)
        mspec = lambda j: _mod_spec_prompt(j, rows_per_seq // tm, d)
    final = g_final is not None
    in_specs = [pl.BlockSpec((tm, d), lambda i: (i, 0)),
                mspec(j0), mspec(j0 + 1), mspec(j0 + 2),
                _resident((1, d)), _HBM, _HBM]
    args = [x2, mod, mod, mod, gain.reshape(1, d), wi, wo]
    if final:
        in_specs.append(_resident((1, d)))
        args.append(g_final.reshape(1, d))
    return pl.pallas_call(
        functools.partial(_ffn_kernel, d_ff=d_ff, final=final),
        grid=(rows // tm,),
        in_specs=in_specs,
        out_specs=pl.BlockSpec((tm, d), lambda i: (i, 0)),
        out_shape=jax.ShapeDtypeStruct((rows, d), F32),
        scratch_shapes=[pltpu.VMEM((tm, d_ff), BF16), pltpu.VMEM(wi.shape, BF16), pltpu.VMEM(wo.shape, BF16)]
        + _stage_scratch(wi.shape[1], wide=True) + _stage_scratch(wo.shape[1]),
        compiler_params=_cparams(("arbitrary",)),
        name="ffn_final" if final else "ffn",
    )(*args)


def _rg_gates(u, wa_ref, wi_ref, ba, bi, lam):
    ub = u.astype(BF16)
    ra, ri = [], []
    for p in range(wa_ref.shape[0]):
        blk = ub[:, p * MXU_DIM:(p + 1) * MXU_DIM]
        ra.append(_dot(blk, wa_ref[p]))
        ri.append(_dot(blk, wi_ref[p]))
    r = jax.nn.sigmoid(jnp.concatenate(ra, axis=-1) + ba)
    i_g = jax.nn.sigmoid(jnp.concatenate(ri, axis=-1) + bi)
    log_a = -RG_C * r * jax.nn.softplus(-lam)
    a = jnp.exp(log_a)
    b = jnp.sqrt(-jnp.tanh(log_a) * (a * a + 1.0)) * (i_g * u)
    return a, b


def _rglru_kernel(x_ref, sh_ref, sc_ref, g_ref, wint_hbm, cw_ref, cb_ref, wa_ref, wi_ref, ba_ref, bi_ref,
                  lam_ref, ha_ref, conv_ref, hl_ref,
                  xn_ref, bm_ref, tb_ref, a_ref, b_ref, hs_ref, h_ref, w_ref, stage, sem):
    nb, tt, d = x_ref.shape
    tail = (CONV_W - 1) * nb
    rows = nb * tt
    t_idx = pl.program_id(0)

    @pl.when(t_idx == 0)
    def _():
        _stage_rows(wint_hbm, 0, d, w_ref, 0, stage, sem)
        tb_ref[0:tail, :] = jnp.zeros((tail, d), F32)
        h_ref[...] = jnp.zeros_like(h_ref)

    for bq in range(nb):
        xn = _rms_mod(x_ref[bq], g_ref[...], sh_ref[bq:bq + 1, :], sc_ref[bq:bq + 1, :])
        xn_ref[bq * tt:(bq + 1) * tt, :] = xn.astype(BF16)
    x_rg = _dot_nt(xn_ref[...], w_ref[...])
    for cb in range(d // LANES):
        bm_ref[cb] = x_rg[:, cb * LANES:(cb + 1) * LANES]

    def to_time_major(t, carry):
        dst = pl.multiple_of(tail + t * nb, nb)
        for cb in range(d // LANES):
            tb_ref[pl.ds(dst, nb), cb * LANES:(cb + 1) * LANES] = bm_ref[cb, pl.ds(t, nb, stride=tt), :]
        return carry
    lax.fori_loop(0, tt, to_time_major, 0, unroll=8)

    u = cb_ref[...]
    for j in range(CONV_W):
        u = u + tb_ref[j * nb:j * nb + rows, :] * cw_ref[j:j + 1, :]
    new_tail = tb_ref[rows:rows + tail, :]
    for j in range(CONV_W - 1):
        conv_ref[j] = new_tail[j * nb:(j + 1) * nb, :]
    tb_ref[0:tail, :] = new_tail

    a, b = _rg_gates(u, wa_ref, wi_ref, ba_ref[...], bi_ref[...], lam_ref[...])
    a_ref[...] = a
    b_ref[...] = b

    def step(t, h):
        src = pl.multiple_of(t * nb, nb)
        h = a_ref[pl.ds(src, nb), :] * h + b_ref[pl.ds(src, nb), :]
        for cb in range(d // LANES):
            hs_ref[cb, pl.ds(src, nb), :] = h[:, cb * LANES:(cb + 1) * LANES]
        return h
    h_last = lax.fori_loop(0, tt, step, h_ref[...], unroll=8)
    h_ref[...] = h_last
    hl_ref[...] = h_last

    for bq in range(nb):
        for cb in range(d // LANES):
            ha_ref[bq, :, cb * LANES:(cb + 1) * LANES] = hs_ref[cb, pl.ds(bq, tt, stride=nb), :].astype(BF16)


def _rglru_prompt(x3, mod2, gain, w_int, conv_w, conv_b, wa, wi, ba, bi, lam):
    nb, seq, d = x3.shape
    tt = min(RG_STEPS, seq)
    rows = nb * tt
    tail = (CONV_W - 1) * nb
    row = lambda v: v.reshape(1, d)
    return pl.pallas_call(
        _rglru_kernel,
        grid=(seq // tt,),
        in_specs=[pl.BlockSpec((nb, tt, d), lambda t: (0, t, 0)),
                  pl.BlockSpec((nb, d), lambda t: (0, 3)),
                  pl.BlockSpec((nb, d), lambda t: (0, 4)),
                  _resident((1, d)), _HBM, _resident(conv_w.shape), _resident((1, d)),
                  _resident(wa.shape), _resident(wi.shape), _resident((1, d)), _resident((1, d)),
                  _resident((1, d))],
        out_specs=[pl.BlockSpec((nb, tt, d), lambda t: (0, t, 0)),
                   pl.BlockSpec((CONV_W - 1, nb, d), lambda t: (0, 0, 0)),
                   pl.BlockSpec((nb, d), lambda t: (0, 0))],
        out_shape=[jax.ShapeDtypeStruct((nb, seq, d), BF16),
                   jax.ShapeDtypeStruct((CONV_W - 1, nb, d), F32),
                   jax.ShapeDtypeStruct((nb, d), F32)],
        scratch_shapes=[pltpu.VMEM((rows, d), BF16),
                        pltpu.VMEM((d // LANES, rows, LANES), F32),
                        pltpu.VMEM((tail + rows, d), F32),
                        pltpu.VMEM((rows, d), F32),
                        pltpu.VMEM((rows, d), F32),
                        pltpu.VMEM((d // LANES, rows, LANES), F32),
                        pltpu.VMEM((nb, d), F32),
                        pltpu.VMEM((d, d), BF16)]
        + _stage_scratch(d),
        compiler_params=_cparams(("arbitrary",)),
        name="rglru_prompt",
    )(x3, mod2, mod2, row(gain), w_int, conv_w, row(conv_b), wa, wi, row(ba), row(bi), row(lam))


def _mlstm_kernel(x_ref, ha_ref, sh_ref, sc_ref, ga_ref, g_ref, wint_hbm, bif_ref, bift_ref,
                  wbra_hbm, wbrb_hbm, wout_hbm, o_ref, c_ref, n_ref, m_ref,
                  hm_ref, wm_ref, wif_ref, wbra_ref, wbrb_ref, wout_ref, stage, sem):
    tl, d = x_ref.shape
    dk = d // M_HEADS
    cs = min(ML_CHUNK, tl)
    scale = dk ** -0.5
    gate_row0 = 5 * d

    @pl.when((pl.program_id(0) == 0) & (pl.program_id(1) == 0))
    def _():
        _stage_rows(wint_hbm, d, 4 * d, wm_ref, 0, stage, sem)
        _stage_rows(wint_hbm, gate_row0 + 2 * M_HEADS, 2 * d, wm_ref, 4 * d, stage, sem)
        _stage_rows(wint_hbm, gate_row0, LANES, wif_ref, 0, stage, sem)
        _stage_rows(wbra_hbm, 0, d, wbra_ref, 0, stage, sem)
        _stage_rows(wbrb_hbm, 0, d, wbrb_ref, 0, stage, sem)
        _stage_rows(wout_hbm, 0, d, wout_ref, 0, stage, sem)

    @pl.when(pl.program_id(1) == 0)
    def _():
        c_ref[...] = jnp.zeros_like(c_ref)
        n_ref[...] = jnp.zeros_like(n_ref)
        m_ref[...] = jnp.zeros_like(m_ref)

    x = x_ref[...]
    xn = _rms_mod(x, g_ref[...], sh_ref[...], sc_ref[...]).astype(BF16)

    pre_c = _dot_nt(xn, wif_ref[...]) + bif_ref[...]
    pre_r = _dot_nt(wif_ref[0:GATE_ROWS, :], xn) + bift_ref[...]
    col_is_f = lax.broadcasted_iota(jnp.int32, pre_c.shape, 1) >= M_HEADS
    row_is_f = lax.broadcasted_iota(jnp.int32, pre_r.shape, 0) >= M_HEADS
    gate_c = jnp.where(col_is_f, jax.nn.log_sigmoid(pre_c), pre_c)
    gate_r = jnp.where(row_is_f, jax.nn.log_sigmoid(pre_r), pre_r)

    ti = lax.broadcasted_iota(jnp.int32, (cs, cs), 0)
    si = lax.broadcasted_iota(jnp.int32, (cs, cs), 1)
    causal = si <= ti
    lower = causal.astype(BF16)
    upper = (ti <= si).astype(BF16)

    chunks = []
    for c in range(tl // cs):
        gc = gate_c[c * cs:(c + 1) * cs, :]
        gr = gate_r[:, c * cs:(c + 1) * cs]
        cum_c = sum(_dot(lower, part) for part in _split3(gc))
        cum_r = sum(_dot(part, upper) for part in _split3(gr))
        chunks.append((gc, gr, cum_c, cum_r))

    for h in range(M_HEADS):
        c0 = h * dk
        q_all = _dot_nt(xn, wm_ref[c0:c0 + dk, :])
        k_all = _dot_nt(xn, wm_ref[d + c0:d + c0 + dk, :])
        v_all = _dot_nt(xn, wm_ref[2 * d + c0:2 * d + c0 + dk, :])
        for c, (gc, gr, cum_c, cum_r) in enumerate(chunks):
            r0 = c * cs
            q = q_all[r0:r0 + cs, :]
            k = k_all[r0:r0 + cs, :]
            v = v_all[r0:r0 + cs, :]
            bc = cum_c[:, M_HEADS + h:M_HEADS + h + 1]
            br = cum_r[M_HEADS + h:M_HEADS + h + 1, :]
            ig_c = gc[:, h:h + 1]
            ig_r = gr[h:h + 1, :]
            m_prev = m_ref[h:h + 1, :]
            c_prev = c_ref[h]
            n_prev = n_ref[h:h + 1, :]

            log_d = jnp.where(causal, bc - br + ig_r, -jnp.inf)
            log_past = bc + m_prev
            m_t = jnp.maximum(log_past, jnp.max(log_d, axis=-1, keepdims=True))
            d_mat = jnp.exp(log_d - m_t)
            past_w = jnp.exp(log_past - m_t)
            qs = q * scale
            qsb = qs.astype(BF16)
            kb = k.astype(BF16)
            vb = v.astype(BF16)
            s = _dot_nt(qsb, kb) * d_mat
            num = past_w * _dot(qsb, c_prev.astype(BF16)) + _dot(s.astype(BF16), vb)
            den = past_w * jnp.sum(qs * n_prev, axis=-1, keepdims=True) + jnp.sum(s, axis=-1, keepdims=True)
            hm_ref[r0:r0 + cs, c0:c0 + dk] = num / jnp.maximum(jnp.abs(den), jnp.exp(-m_t))

            m_new = m_t[cs - 1:cs, :]
            b_last = bc[cs - 1:cs, :]
            w_s = jnp.exp(b_last - bc + ig_c - m_new)
            decay = jnp.exp(b_last + m_prev - m_new)
            kw = w_s * k
            c_ref[h] = decay * c_prev + _dot_tn(kw.astype(BF16), vb)
            n_ref[h:h + 1, :] = decay * n_prev + jnp.sum(kw, axis=0, keepdims=True)
            m_ref[h:h + 1, :] = m_new

    o_pre = _dot_nt(xn, wm_ref[3 * d:4 * d, :])
    hb = (jax.nn.sigmoid(o_pre) * hm_ref[...]).astype(BF16)
    g_a = _dot_nt(xn, wm_ref[4 * d:5 * d, :])
    g_b = _dot_nt(xn, wm_ref[5 * d:6 * d, :])
    merged = (jax.nn.sigmoid(g_a) * _dot(ha_ref[...], wbra_ref[...])
              + jax.nn.sigmoid(g_b) * _dot(hb, wbrb_ref[...]))
    o_ref[...] = x + ga_ref[...] * _dot(merged.astype(BF16), wout_ref[...])


def _mlstm_prompt(x2, ha2, mod3, gain, w_int, bif, bift, wbra, wbrb, wout, *, nb, seq):
    rows, d = x2.shape
    dk = d // M_HEADS
    tl = min(ML_ROWS, seq)
    nt = seq // tl
    mspec = lambda j: pl.BlockSpec((None, 1, d), lambda b, t: (b, 0, j))
    tile = lambda: pl.BlockSpec((tl, d), lambda b, t: (b * nt + t, 0))
    return pl.pallas_call(
        _mlstm_kernel,
        grid=(nb, nt),
        in_specs=[tile(), tile(), mspec(3), mspec(4), mspec(5), _resident((1, d)),
                  _HBM, _resident(bif.shape), _resident(bift.shape), _HBM, _HBM, _HBM],
        out_specs=[tile(),
                   pl.BlockSpec((None, M_HEADS, dk, dk), lambda b, t: (b, 0, 0, 0)),
                   pl.BlockSpec((None, M_HEADS, dk), lambda b, t: (b, 0, 0)),
                   pl.BlockSpec((None, M_HEADS, 1), lambda b, t: (b, 0, 0))],
        out_shape=[jax.ShapeDtypeStruct((rows, d), F32),
                   jax.ShapeDtypeStruct((nb, M_HEADS, dk, dk), F32),
                   jax.ShapeDtypeStruct((nb, M_HEADS, dk), F32),
                   jax.ShapeDtypeStruct((nb, M_HEADS, 1), F32)],
        scratch_shapes=[pltpu.VMEM((tl, d), F32),
                        pltpu.VMEM((6 * d, d), BF16),
                        pltpu.VMEM((LANES, d), BF16),
                        pltpu.VMEM((d, d), BF16), pltpu.VMEM((d, d), BF16), pltpu.VMEM((d, d), BF16)]
        + _stage_scratch(d),
        compiler_params=_cparams(("arbitrary", "arbitrary")),
        name="mlstm_prompt",
    )(x2, ha2, mod3, mod3, mod3, gain.reshape(1, d), w_int, bif, bift, wbra, wbrb, wout)


def _head_sum(x, dk):
    parts = []
    for h in range(x.shape[1] // dk):
        sl = x[:, h * dk:(h + 1) * dk]
        parts.append(jnp.broadcast_to(jnp.sum(sl, axis=-1, keepdims=True), sl.shape))
    return jnp.concatenate(parts, axis=-1)


def _head_spread(cols, lane0, dk, rows):
    return jnp.concatenate(
        [jnp.broadcast_to(cols[:, lane0 + h:lane0 + h + 1], (rows, dk)) for h in range(M_HEADS)], axis=-1)


def _dec_pre_kernel(x_ref, sh_ref, sc_ref, g_ref, wint_hbm, bif_ref,
                    cw_ref, cb_ref, wa_ref, wi_ref, ba_ref, bi_ref, lam_ref,
                    conv0_ref, h0_ref, n0_ref, m0_ref,
                    ha_ref, conv_ref, hs_ref, n_ref, m_ref, qt_ref, kt_ref, dec_ref, wv_ref, pw_ref, sv_ref,
                    den_ref, em_ref,
                    wm_ref, wif_ref, stage, sem):
    rows, d = x_ref.shape
    dk = d // M_HEADS
    scale = dk ** -0.5
    _stage_rows(wint_hbm, 0, 4 * d, wm_ref, 0, stage, sem)
    _stage_rows(wint_hbm, 5 * d, LANES, wif_ref, 0, stage, sem)
    xn = _rms_mod(x_ref[...], g_ref[...], sh_ref[...], sc_ref[...]).astype(BF16)

    x_rg = _dot_nt(xn, wm_ref[0:d, :])
    u = cb_ref[...]
    for j in range(CONV_W - 1):
        u = u + conv0_ref[j] * cw_ref[j:j + 1, :]
        if j > 0:
            conv_ref[j - 1] = conv0_ref[j]
    u = u + x_rg * cw_ref[CONV_W - 1:CONV_W, :]
    conv_ref[CONV_W - 2] = x_rg
    a, b = _rg_gates(u, wa_ref, wi_ref, ba_ref[...], bi_ref[...], lam_ref[...])
    h = a * h0_ref[...] + b
    hs_ref[...] = h
    ha_ref[...] = h

    q = _dot_nt(xn, wm_ref[d:2 * d, :])
    k = _dot_nt(xn, wm_ref[2 * d:3 * d, :])
    v = _dot_nt(xn, wm_ref[3 * d:4 * d, :])
    qt_ref[...] = _dot_nt(wm_ref[d:2 * d, :], xn).astype(BF16)
    kt_ref[...] = _dot_nt(wm_ref[2 * d:3 * d, :], xn).astype(BF16)
    pre = _dot_nt(xn, wif_ref[...]) + bif_ref[...]
    ig = _head_spread(pre, 0, dk, rows)
    lf = jax.nn.log_sigmoid(_head_spread(pre, M_HEADS, dk, rows))
    m0 = _head_spread(m0_ref[...], 0, dk, rows)
    n0 = n0_ref[...]
    log_past = lf + m0
    m_t = jnp.maximum(log_past, ig)
    d_w = jnp.exp(ig - m_t)
    past_w = jnp.exp(log_past - m_t)
    qs = q * scale
    s = _head_sum(qs * k, dk) * d_w
    den_ref[...] = past_w * _head_sum(qs * n0, dk) + s
    em_ref[...] = jnp.exp(-m_t)
    pw_ref[...] = past_w
    sv_ref[...] = s * v
    w_s = jnp.exp(ig - m_t)
    decay = jnp.exp(lf + m0 - m_t)
    dec_ref[...] = decay
    wv_ref[...] = w_s * v
    n_ref[...] = decay * n0 + w_s * k
    m_ref[...] = m_t


def _dec_pre(x2, mod, gain, w_int, bif, conv_w, conv_b, wa, wi, ba, bi, lam, conv0, h0, n0, m0):
    rows, d = x2.shape
    row = lambda v: v.reshape(1, d)
    m0p = jnp.pad(m0, ((0, 0), (0, LANES - M_HEADS)))
    full = lambda shape: pl.BlockSpec(shape, lambda i, _n=len(shape): (0,) * _n)
    mspec = lambda j: pl.BlockSpec((rows, d), lambda i: (0, j))
    vec = jax.ShapeDtypeStruct((rows, d), F32)
    outs = [vec,
            jax.ShapeDtypeStruct((CONV_W - 1, rows, d), F32),
            vec,
            vec,
            vec,
            jax.ShapeDtypeStruct((d, rows), BF16),
            jax.ShapeDtypeStruct((d, rows), BF16),
            vec, vec, vec, vec, vec, vec]
    args = [x2, mod, mod, row(gain), w_int, bif, conv_w, row(conv_b), wa, wi,
            row(ba), row(bi), row(lam), conv0, h0, n0, m0p]
    in_specs = ([full(x2.shape), mspec(3), mspec(4), full((1, d)), _HBM]
                + [full(a.shape) for a in args[5:]])
    return pl.pallas_call(
        _dec_pre_kernel,
        grid=(1,),
        in_specs=in_specs,
        out_specs=[full(o.shape) for o in outs],
        out_shape=outs,
        scratch_shapes=[pltpu.VMEM((4 * d, d), BF16), pltpu.VMEM((LANES, d), BF16)] + _stage_scratch(d),
        compiler_params=_cparams(("arbitrary",)),
        name="decode_pre",
    )(*args)


def _dec_mem_kernel(qt_ref, kt_ref, dec_ref, wv_ref, c0_ref, c_ref, qc_ref):
    bb = c0_ref.shape[0]
    dk = c0_ref.shape[2]
    nseq = qt_ref.shape[1]
    scale = dk ** -0.5
    base = pl.program_id(0) * bb
    seq_id = lax.broadcasted_iota(jnp.int32, (nseq, dk), 0)
    for j in range(bb):
        pick = (seq_id == base + j).astype(BF16)
        qcol = _dot(qt_ref[...], pick) * scale
        kcol = _dot(kt_ref[...], pick)
        for h in range(M_HEADS):
            c0 = c0_ref[j, h]
            dec = dec_ref[j, :, h * dk:(h + 1) * dk]
            wv = wv_ref[j, :, h * dk:(h + 1) * dk]
            c_ref[j, h] = dec * c0 + kcol[h * dk:(h + 1) * dk, :] * wv
            qc_ref[j, :, h * dk:(h + 1) * dk] = jnp.sum(qcol[h * dk:(h + 1) * dk, :] * c0, axis=0, keepdims=True)


def _dec_mem(qt, kt, decay, wv, c0):
    nseq, heads, dk, _ = c0.shape
    d = heads * dk
    bb = DEC_BLOCK if nseq % DEC_BLOCK == 0 else 1
    rowblk = pl.BlockSpec((bb, 1, d), lambda i: (i, 0, 0))
    return pl.pallas_call(
        _dec_mem_kernel,
        grid=(nseq // bb,),
        in_specs=[pl.BlockSpec((d, nseq), lambda i: (0, 0)), pl.BlockSpec((d, nseq), lambda i: (0, 0)),
                  rowblk, rowblk,
                  pl.BlockSpec((bb, heads, dk, dk), lambda i: (i, 0, 0, 0))],
        out_specs=[pl.BlockSpec((bb, heads, dk, dk), lambda i: (i, 0, 0, 0)), rowblk],
        out_shape=[jax.ShapeDtypeStruct(c0.shape, F32), jax.ShapeDtypeStruct((nseq, 1, d), F32)],
        compiler_params=_cparams(("arbitrary",)),
        name="decode_mem",
    )(qt, kt, decay.reshape(nseq, 1, d), wv.reshape(nseq, 1, d), c0)


def _dec_post_kernel(x_ref, sh_ref, sc_ref, ga_ref, g_ref, wint_hbm, wbra_hbm, wbrb_hbm, wout_hbm,
                     ha_ref, qc_ref, pw_ref, sv_ref, den_ref, em_ref, o_ref,
                     wm_ref, wbra_ref, wbrb_ref, wout_ref, stage, sem):
    d = x_ref.shape[1]
    _stage_rows(wint_hbm, 4 * d, d, wm_ref, 0, stage, sem)
    _stage_rows(wint_hbm, 5 * d + 2 * M_HEADS, 2 * d, wm_ref, d, stage, sem)
    _stage_rows(wbra_hbm, 0, d, wbra_ref, 0, stage, sem)
    _stage_rows(wbrb_hbm, 0, d, wbrb_ref, 0, stage, sem)
    _stage_rows(wout_hbm, 0, d, wout_ref, 0, stage, sem)
    x = x_ref[...]
    xn = _rms_mod(x, g_ref[...], sh_ref[...], sc_ref[...]).astype(BF16)
    num = pw_ref[...] * qc_ref[...] + sv_ref[...]
    hm = num / jnp.maximum(jnp.abs(den_ref[...]), em_ref[...])
    hb = (jax.nn.sigmoid(_dot_nt(xn, wm_ref[0:d, :])) * hm).astype(BF16)
    g_a = _dot_nt(xn, wm_ref[d:2 * d, :])
    g_b = _dot_nt(xn, wm_ref[2 * d:3 * d, :])
    merged = (jax.nn.sigmoid(g_a) * _dot(ha_ref[...].astype(BF16), wbra_ref[...])
              + jax.nn.sigmoid(g_b) * _dot(hb, wbrb_ref[...]))
    o_ref[...] = x + ga_ref[...] * _dot(merged.astype(BF16), wout_ref[...])


def _dec_post(x2, mod, gain, w_int, wbra, wbrb, wout, ha, qc, pw, sv, den, em):
    rows, d = x2.shape
    full = lambda shape: pl.BlockSpec(shape, lambda i, _n=len(shape): (0,) * _n)
    mspec = lambda j: pl.BlockSpec((rows, d), lambda i: (0, j))
    args = [x2, mod, mod, mod, gain.reshape(1, d), w_int, wbra, wbrb, wout, ha, qc, pw, sv, den, em]
    in_specs = ([full(x2.shape), mspec(3), mspec(4), mspec(5), full((1, d))] + [_HBM] * 4
                + [full(a.shape) for a in args[9:]])
    return pl.pallas_call(
        _dec_post_kernel,
        grid=(1,),
        in_specs=in_specs,
        out_specs=full((rows, d)),
        out_shape=jax.ShapeDtypeStruct((rows, d), F32),
        scratch_shapes=[pltpu.VMEM((3 * d, d), BF16),
                        pltpu.VMEM((d, d), BF16), pltpu.VMEM((d, d), BF16), pltpu.VMEM((d, d), BF16)]
        + _stage_scratch(d),
        compiler_params=_cparams(("arbitrary",)),
        name="decode_post",
    )(*args)


def _pair_blocks(w):
    nblk, c, _ = w.shape
    z = jnp.zeros((nblk // 2, c, c), w.dtype)
    top = jnp.concatenate([w[0::2], z], axis=2)
    bot = jnp.concatenate([z, w[1::2]], axis=2)
    return jnp.concatenate([top, bot], axis=1).astype(BF16)


def kernel(x_prompt, x_sample, state_conv, state_rg_h, state_C, state_n, state_m, c_prompt, c_sample, w_ada, b_ada, g_norm1, w_ff1_in, w_ff1_out, g_norm2, w_in, conv_w, conv_b, w_rg_a, b_rg_a, w_rg_i, b_rg_i, rg_lambda, b_ig, b_fg, w_br_a, w_br_b, w_out, g_norm3, w_ff2_in, w_ff2_out, g_final):
    nb, seq, d = x_prompt.shape
    ns = x_sample.shape[0]
    depth = w_ada.shape[0]
    assert depth == 1 and x_sample.shape[1] == 1 and nb == SUBLANES
    assert seq % RG_STEPS == 0 and seq % ML_ROWS == 0 and seq % FFN_ROWS == 0
    heads, dk = M_HEADS, d // M_HEADS
    assert w_in.shape[2] == 7 * d + 2 * heads and GATE_ROWS >= 2 * heads

    w_int = w_in[0].T
    b_gates = jnp.concatenate([b_ig[0], b_fg[0]])
    bif = jnp.pad(b_gates, (0, LANES - 2 * heads)).reshape(1, LANES)
    bift = jnp.pad(b_gates, (0, GATE_ROWS - 2 * heads)).reshape(GATE_ROWS, 1)
    wa = _pair_blocks(w_rg_a[0])
    wi = _pair_blocks(w_rg_i[0])
    wf1i, wf1o, wf2i, wf2o = w_ff1_in[0], w_ff1_out[0], w_ff2_in[0], w_ff2_out[0]
    wbra, wbrb, wout = w_br_a[0], w_br_b[0], w_out[0]

    mod = _ada(jnp.concatenate([c_prompt, c_sample], axis=0), w_ada[0], b_ada[0])
    mod_p2 = mod[:nb]
    mod_p3 = mod_p2.reshape(nb, 1, N_MOD * d)
    mod_s = mod[nb:]

    xp = x_prompt.reshape(nb * seq, d)
    xp = _ffn(xp, mod_p3, 0, g_norm1[0], wf1i, wf1o, None, rows_per_seq=seq, per_row_mod=False)
    ha, conv_p, h_p = _rglru_prompt(xp.reshape(nb, seq, d), mod_p2, g_norm2[0], w_int, conv_w[0], conv_b[0],
                                    wa, wi, b_rg_a[0], b_rg_i[0], rg_lambda[0])
    xp, c_p, n_p, m_p = _mlstm_prompt(xp, ha.reshape(nb * seq, d), mod_p3, g_norm2[0], w_int, bif, bift,
                                      wbra, wbrb, wout, nb=nb, seq=seq)
    yp = _ffn(xp, mod_p3, 6, g_norm3[0], wf2i, wf2o, g_final, rows_per_seq=seq, per_row_mod=False)

    xs = x_sample.reshape(ns, d)
    xs = _ffn(xs, mod_s, 0, g_norm1[0], wf1i, wf1o, None, rows_per_seq=1, per_row_mod=True)
    conv0 = jnp.swapaxes(state_conv[0], 0, 1)
    (ha_s, conv_s, h_s, n_s, m_s, qt, kt, decay, wv, pw, sv, den, em) = _dec_pre(
        xs, mod_s, g_norm2[0], w_int, bif, conv_w[0], conv_b[0], wa, wi,
        b_rg_a[0], b_rg_i[0], rg_lambda[0], conv0, state_rg_h[0], state_n[0].reshape(ns, d), state_m[0])
    c_s, qc = _dec_mem(qt, kt, decay, wv, state_C[0])
    xs = _dec_post(xs, mod_s, g_norm2[0], w_int, wbra, wbrb, wout, ha_s, qc.reshape(ns, d), pw, sv, den, em)
    ys = _ffn(xs, mod_s, 6, g_norm3[0], wf2i, wf2o, g_final, rows_per_seq=1, per_row_mod=True)

    return (yp.reshape(nb, seq, d), ys.reshape(ns, 1, d),
            jnp.swapaxes(conv_p, 0, 1)[None], h_p[None], c_p[None], n_p[None], m_p.reshape(1, nb, heads),
            jnp.swapaxes(conv_s, 0, 1)[None], h_s[None], c_s[None], n_s.reshape(1, ns, heads, dk), m_s[:, ::dk][None])
```

```python
import functools

import jax
import jax.numpy as jnp
from jax import lax
from jax.experimental import pallas as pl
from jax.experimental.pallas import tpu as pltpu

F32 = jnp.float32
BF16 = jnp.bfloat16

EPS = 1e-6
RG_C = 8.0
CONV_W = 4
N_MOD = 9
M_HEADS = 4
RG_BLOCKS = 8

SUBLANES = 8
LANES = 128
MXU_DIM = 256
VMEM_LIMIT_BYTES = 60 * 1024 * 1024

FFN_ROWS = 1024
FFN_COLS = 256
ML_ROWS = 512
ML_CHUNK = 256
DEC_BLOCK = 2
STAGE_ROWS = 256
STAGE_SLOTS = 4
STAGE_ROWS_WIDE = 64
STAGE_SLOTS_WIDE = 4
GATE_ROWS = 16


def _cparams(sem):
    return pltpu.CompilerParams(dimension_semantics=sem, vmem_limit_bytes=VMEM_LIMIT_BYTES)


def _resident(shape):
    nd = len(shape)
    return pl.BlockSpec(shape, lambda *_: (0,) * nd, pipeline_mode=pl.Buffered(1))


_HBM = pl.BlockSpec(memory_space=pl.ANY)


def _rms_mod(x, gain, shift, scale):
    ms = jnp.mean(x * x, axis=-1, keepdims=True)
    return x * lax.rsqrt(ms + EPS) * gain * (1.0 + scale) + shift


def _dot(a, b):
    return jnp.dot(a, b, preferred_element_type=F32)


def _dot_nt(a, b):
    return lax.dot_general(a, b, (((1,), (1,)), ((), ())), preferred_element_type=F32)


def _dot_tn(a, b):
    return lax.dot_general(a, b, (((0,), (0,)), ((), ())), preferred_element_type=F32)


def _split3(x):
    hi = x.astype(BF16)
    r1 = x - hi.astype(F32)
    mid = r1.astype(BF16)
    lo = (r1 - mid.astype(F32)).astype(BF16)
    return hi, mid, lo


def _stage_rows(src, src_row0, nrows, dst, dst_row0, stage, sem):
    slots, rows_per_copy, _ = stage.shape
    n = pl.cdiv(nrows, rows_per_copy)

    def rows_of(c):
        return min(rows_per_copy, nrows - c * rows_per_copy)

    def copy(c):
        return pltpu.make_async_copy(src.at[pl.ds(src_row0 + c * rows_per_copy, rows_of(c))],
                                     stage.at[c % slots, pl.ds(0, rows_of(c))], sem.at[c % slots])

    for c in range(min(slots, n)):
        copy(c).start(priority=c % 2)
    for c in range(n):
        copy(c).wait()
        r0 = dst_row0 + c * rows_per_copy
        dst[r0:r0 + rows_of(c), :] = stage[c % slots, 0:rows_of(c), :].astype(BF16)
        if c + slots < n:
            copy(c + slots).start(priority=c % 2)


def _stage_scratch(cols, wide=False):
    slots, rows = (STAGE_SLOTS_WIDE, STAGE_ROWS_WIDE) if wide else (STAGE_SLOTS, STAGE_ROWS)
    return [pltpu.VMEM((slots, rows, cols), F32), pltpu.SemaphoreType.DMA((slots,))]


def _ada_kernel(c_ref, w_ref, b_ref, o_ref):
    c = c_ref[...]
    sc = (c * jax.nn.sigmoid(c)).astype(BF16)
    o_ref[...] = _dot(sc, w_ref[...].astype(BF16)) + b_ref[...]


def _ada(c, w_ada, b_ada):
    rows, d = c.shape
    n = w_ada.shape[1]
    tn = d
    return pl.pallas_call(
        _ada_kernel,
        grid=(n // tn,),
        in_specs=[pl.BlockSpec((rows, d), lambda j: (0, 0)),
                  pl.BlockSpec((d, tn), lambda j: (0, j)),
                  pl.BlockSpec((1, tn), lambda j: (0, j))],
        out_specs=pl.BlockSpec((rows, tn), lambda j: (0, j)),
        out_shape=jax.ShapeDtypeStruct((rows, n), F32),
        compiler_params=_cparams(("arbitrary",)),
        name="adaln_mod",
    )(c, w_ada, b_ada.reshape(1, n))


def _ffn_kernel(*refs, d_ff, final):
    if final:
        x_ref, sh_ref, sc_ref, ga_ref, g_ref, wi_hbm, wo_hbm, gf_ref, o_ref = refs[:9]
    else:
        x_ref, sh_ref, sc_ref, ga_ref, g_ref, wi_hbm, wo_hbm, o_ref = refs[:8]
    act_ref, wi_ref, wo_ref, stage_i, sem_i, stage_o, sem_o = refs[-7:]

    @pl.when(pl.program_id(0) == 0)
    def _():
        _stage_rows(wi_hbm, 0, wi_ref.shape[0], wi_ref, 0, stage_i, sem_i)
        _stage_rows(wo_hbm, 0, wo_ref.shape[0], wo_ref, 0, stage_o, sem_o)

    x = x_ref[...]
    xn = _rms_mod(x, g_ref[...], sh_ref[...], sc_ref[...]).astype(BF16)
    for j in range(d_ff // FFN_COLS):
        lo = j * FFN_COLS
        hg = _dot(xn, wi_ref[:, lo:lo + FFN_COLS])
        hu = _dot(xn, wi_ref[:, d_ff + lo:d_ff + lo + FFN_COLS])
        act_ref[:, lo:lo + FFN_COLS] = (hg * jax.nn.sigmoid(hg) * hu).astype(BF16)
    y = _dot(act_ref[...], wo_ref[...])
    out = x + (0.5 * ga_ref[...]) * y
    if final:
        ms = jnp.mean(out * out, axis=-1, keepdims=True)
        out = out * lax.rsqrt(ms + EPS) * gf_ref[...]
    o_ref[...] = out


def _mod_spec_prompt(j, tiles_per_seq, d):
    return pl.BlockSpec((None, 1, d), lambda i: (i // tiles_per_seq, 0, j))


def _ffn(x2, mod, j0, gain, wi, wo, g_final, *, rows_per_seq, per_row_mod):
    rows, d = x2.shape
    d_ff = wo.shape[0]
    if per_row_mod:
        tm = rows
        mspec = lambda j: pl.BlockSpec((tm, d), lambda i: (0, j))
    else:
        tm = min(FFN_ROWS, rows_per_seq)
        mspec = lambda j: _mod_spec_prompt(j, rows_per_seq // tm, d)
    final = g_final is not None
    in_specs = [pl.BlockSpec((tm, d), lambda i: (i, 0)),
                mspec(j0), mspec(j0 + 1), mspec(j0 + 2),
                _resident((1, d)), _HBM, _HBM]
    args = [x2, mod, mod, mod, gain.reshape(1, d), wi, wo]
    if final:
        in_specs.append(_resident((1, d)))
        args.append(g_final.reshape(1, d))
    return pl.pallas_call(
        functools.partial(_ffn_kernel, d_ff=d_ff, final=final),
        grid=(rows // tm,),
        in_specs=in_specs,
        out_specs=pl.BlockSpec((tm, d), lambda i: (i, 0)),
        out_shape=jax.ShapeDtypeStruct((rows, d), F32),
        scratch_shapes=[pltpu.VMEM((tm, d_ff), BF16), pltpu.VMEM(wi.shape, BF16), pltpu.VMEM(wo.shape, BF16)]
        + _stage_scratch(wi.shape[1], wide=True) + _stage_scratch(wo.shape[1]),
        compiler_params=_cparams(("arbitrary",)),
        name="ffn_final" if final else "ffn",
    )(*args)


def _rg_gates(u, wa_ref, wi_ref, ba, bi, lam):
    ub = u.astype(BF16)
    ra, ri = [], []
    for p in range(wa_ref.shape[0]):
        blk = ub[:, p * MXU_DIM:(p + 1) * MXU_DIM]
        ra.append(_dot(blk, wa_ref[p]))
        ri.append(_dot(blk, wi_ref[p]))
    r = jax.nn.sigmoid(jnp.concatenate(ra, axis=-1) + ba)
    i_g = jax.nn.sigmoid(jnp.concatenate(ri, axis=-1) + bi)
    log_a = -RG_C * r * jax.nn.softplus(-lam)
    a = jnp.exp(log_a)
    b = jnp.sqrt(-jnp.tanh(log_a) * (a * a + 1.0)) * (i_g * u)
    return a, b


def _rglru_tile(xn, w_rows_ref, cw_ref, cb_ref, wa_ref, wi_ref, ba, bi, lam,
                lb_ref, tb_ref, a_ref, b_ref, tail_ref, h_ref, conv_ref, ha_ref):
    tl, d = xn.shape
    seg = tl // SUBLANES
    nlb = d // LANES
    pre = (CONV_W - 1) * SUBLANES

    x_rg = _dot_nt(xn, w_rows_ref[0:d, :])
    for cb in range(nlb):
        lb_ref[cb] = x_rg[:, cb * LANES:(cb + 1) * LANES]
    for j in range(seg):
        for cb in range(nlb):
            tb_ref[pre + j * SUBLANES:pre + (j + 1) * SUBLANES, cb * LANES:(cb + 1) * LANES] = (
                lb_ref[cb, pl.ds(j, SUBLANES, stride=seg), :])

    first = lax.broadcasted_iota(jnp.int32, (SUBLANES, d), 0) == 0
    for i in range(CONV_W - 1):
        j = seg - (CONV_W - 1) + i
        slab = tb_ref[pre + j * SUBLANES:pre + (j + 1) * SUBLANES, :]
        tb_ref[i * SUBLANES:(i + 1) * SUBLANES, :] = jnp.where(first, tail_ref[i:i + 1, :],
                                                              pltpu.roll(slab, 1, axis=0))
        last = slab[SUBLANES - 1:SUBLANES, :]
        tail_ref[i:i + 1, :] = last
        conv_ref[i:i + 1, :] = last

    u = cb_ref[...]
    for j in range(CONV_W):
        u = u + tb_ref[j * SUBLANES:j * SUBLANES + tl, :] * cw_ref[j:j + 1, :]
    a, b = _rg_gates(u, wa_ref, wi_ref, ba, bi, lam)
    a_ref[...] = a
    b_ref[...] = b

    acc_a = a_ref[0:SUBLANES, :]
    acc_b = b_ref[0:SUBLANES, :]
    for j in range(1, seg):
        rows = slice(j * SUBLANES, (j + 1) * SUBLANES)
        a_j = a_ref[rows, :]
        acc_b = a_j * acc_b + b_ref[rows, :]
        acc_a = a_j * acc_a
        a_ref[rows, :] = acc_a
        b_ref[rows, :] = acc_b

    h = h_ref[...]
    h_in = []
    for s in range(SUBLANES):
        h_in.append(h)
        h = acc_a[s:s + 1, :] * h + acc_b[s:s + 1, :]
    h_ref[...] = h
    h_in = jnp.concatenate(h_in, axis=0)

    for j in range(seg):
        rows = slice(j * SUBLANES, (j + 1) * SUBLANES)
        h_j = a_ref[rows, :] * h_in + b_ref[rows, :]
        for cb in range(nlb):
            lb_ref[cb, rows, :] = h_j[:, cb * LANES:(cb + 1) * LANES]
    for s in range(SUBLANES):
        for cb in range(nlb):
            ha_ref[s * seg:(s + 1) * seg, cb * LANES:(cb + 1) * LANES] = (
                lb_ref[cb, pl.ds(s, seg, stride=SUBLANES), :].astype(BF16))


def _mixer_kernel(x_ref, sh_ref, sc_ref, ga_ref, g_ref, wint_hbm, bif_ref, bift_ref,
                  wbra_hbm, wbrb_hbm, wout_hbm, cw_ref, cb_ref, wa_ref, wi_ref, ba_ref, bi_ref, lam_ref,
                  o_ref, c_ref, n_ref, m_ref, conv_ref, hl_ref,
                  hm_ref, wm_ref, wif_ref, wbra_ref, wbrb_ref, wout_ref, stage, sem,
                  lb_ref, tb_ref, a_ref, b_ref, tail_ref, h_ref, ha_ref, mg_ref):
    tl, d = x_ref.shape
    dk = d // M_HEADS
    cs = min(ML_CHUNK, tl)
    scale = dk ** -0.5
    gate_row0 = 5 * d

    @pl.when((pl.program_id(0) == 0) & (pl.program_id(1) == 0))
    def _():
        _stage_rows(wint_hbm, 0, 5 * d, wm_ref, 0, stage, sem)
        _stage_rows(wint_hbm, gate_row0 + 2 * M_HEADS, 2 * d, wm_ref, 5 * d, stage, sem)
        _stage_rows(wint_hbm, gate_row0, LANES, wif_ref, 0, stage, sem)
        _stage_rows(wbra_hbm, 0, d, wbra_ref, 0, stage, sem)
        _stage_rows(wbrb_hbm, 0, d, wbrb_ref, 0, stage, sem)
        _stage_rows(wout_hbm, 0, d, wout_ref, 0, stage, sem)

    @pl.when(pl.program_id(1) == 0)
    def _():
        c_ref[...] = jnp.zeros_like(c_ref)
        n_ref[...] = jnp.zeros_like(n_ref)
        m_ref[...] = jnp.zeros_like(m_ref)
        tail_ref[...] = jnp.zeros_like(tail_ref)
        h_ref[...] = jnp.zeros_like(h_ref)

    x = x_ref[...]
    xn = _rms_mod(x, g_ref[...], sh_ref[...], sc_ref[...]).astype(BF16)

    _rglru_tile(xn, wm_ref, cw_ref, cb_ref, wa_ref, wi_ref, ba_ref[...], bi_ref[...], lam_ref[...],
                lb_ref, tb_ref, a_ref, b_ref, tail_ref, h_ref, conv_ref, ha_ref)
    hl_ref[...] = h_ref[...]
    wq0 = d

    pre_c = _dot_nt(xn, wif_ref[...]) + bif_ref[...]
    pre_r = _dot_nt(wif_ref[0:GATE_ROWS, :], xn) + bift_ref[...]
    col_is_f = lax.broadcasted_iota(jnp.int32, pre_c.shape, 1) >= M_HEADS
    row_is_f = lax.broadcasted_iota(jnp.int32, pre_r.shape, 0) >= M_HEADS
    gate_c = jnp.where(col_is_f, jax.nn.log_sigmoid(pre_c), pre_c)
    gate_r = jnp.where(row_is_f, jax.nn.log_sigmoid(pre_r), pre_r)

    ti = lax.broadcasted_iota(jnp.int32, (cs, cs), 0)
    si = lax.broadcasted_iota(jnp.int32, (cs, cs), 1)
    causal = si <= ti
    lower = causal.astype(BF16)
    upper = (ti <= si).astype(BF16)

    chunks = []
    for c in range(tl // cs):
        gc = gate_c[c * cs:(c + 1) * cs, :]
        gr = gate_r[:, c * cs:(c + 1) * cs]
        cum_c = sum(_dot(lower, part) for part in _split3(gc))
        cum_r = sum(_dot(part, upper) for part in _split3(gr))
        chunks.append((gc, gr, cum_c, cum_r))

    for h in range(M_HEADS):
        c0 = h * dk
        q_all = _dot_nt(xn, wm_ref[wq0 + c0:wq0 + c0 + dk, :])
        k_all = _dot_nt(xn, wm_ref[wq0 + d + c0:wq0 + d + c0 + dk, :])
        v_all = _dot_nt(xn, wm_ref[wq0 + 2 * d + c0:wq0 + 2 * d + c0 + dk, :])
        for c, (gc, gr, cum_c, cum_r) in enumerate(chunks):
            r0 = c * cs
            q = q_all[r0:r0 + cs, :]
            k = k_all[r0:r0 + cs, :]
            v = v_all[r0:r0 + cs, :]
            bc = cum_c[:, M_HEADS + h:M_HEADS + h + 1]
            br = cum_r[M_HEADS + h:M_HEADS + h + 1, :]
            ig_c = gc[:, h:h + 1]
            ig_r = gr[h:h + 1, :]
            m_prev = m_ref[h:h + 1, :]
            c_prev = c_ref[h]
            n_prev = n_ref[h:h + 1, :]

            log_d = jnp.where(causal, bc - br + ig_r, -jnp.inf)
            log_past = bc + m_prev
            m_t = jnp.maximum(log_past, jnp.max(log_d, axis=-1, keepdims=True))
            d_mat = jnp.exp(log_d - m_t)
            past_w = jnp.exp(log_past - m_t)
            qs = q * scale
            qsb = qs.astype(BF16)
            kb = k.astype(BF16)
            vb = v.astype(BF16)
            s = _dot_nt(qsb, kb) * d_mat
            num = past_w * _dot(qsb, c_prev.astype(BF16)) + _dot(s.astype(BF16), vb)
            den = past_w * jnp.sum(qs * n_prev, axis=-1, keepdims=True) + jnp.sum(s, axis=-1, keepdims=True)
            hm_ref[r0:r0 + cs, c0:c0 + dk] = num / jnp.maximum(jnp.abs(den), jnp.exp(-m_t))

            m_new = m_t[cs - 1:cs, :]
            b_last = bc[cs - 1:cs, :]
            w_s = jnp.exp(b_last - bc + ig_c - m_new)
            decay = jnp.exp(b_last + m_prev - m_new)
            kw = w_s * k
            c_ref[h] = decay * c_prev + _dot_tn(kw.astype(BF16), vb)
            n_ref[h:h + 1, :] = decay * n_prev + jnp.sum(kw, axis=0, keepdims=True)
            m_ref[h:h + 1, :] = m_new

    hb = (jax.nn.sigmoid(_dot_nt(xn, wm_ref[4 * d:5 * d, :])) * hm_ref[...]).astype(BF16)
    for n0 in range(0, d, MXU_DIM):
        cols = slice(n0, n0 + MXU_DIM)
        g_a = _dot_nt(xn, wm_ref[5 * d + n0:5 * d + n0 + MXU_DIM, :])
        g_b = _dot_nt(xn, wm_ref[6 * d + n0:6 * d + n0 + MXU_DIM, :])
        merged = (jax.nn.sigmoid(g_a) * _dot(ha_ref[...], wbra_ref[:, cols])
                  + jax.nn.sigmoid(g_b) * _dot(hb, wbrb_ref[:, cols]))
        mg_ref[:, cols] = merged.astype(BF16)
    o_ref[...] = x + ga_ref[...] * _dot(mg_ref[...], wout_ref[...])


def _mixer_prompt(x2, mod3, gain, w_int, bif, bift, wbra, wbrb, wout, conv_w, conv_b, wa, wi, ba, bi, lam,
                  *, nb, seq):
    rows, d = x2.shape
    dk = d // M_HEADS
    tl = min(ML_ROWS, seq)
    nt = seq // tl
    row = lambda v: v.reshape(1, d)
    mspec = lambda j: pl.BlockSpec((None, 1, d), lambda b, t: (b, 0, j))
    tile = lambda: pl.BlockSpec((tl, d), lambda b, t: (b * nt + t, 0))
    per_seq = lambda *shape: pl.BlockSpec((None,) + shape, lambda b, t: (b,) + (0,) * len(shape))
    return pl.pallas_call(
        _mixer_kernel,
        grid=(nb, nt),
        in_specs=[tile(), mspec(3), mspec(4), mspec(5), _resident((1, d)),
                  _HBM, _resident(bif.shape), _resident(bift.shape), _HBM, _HBM, _HBM,
                  _resident(conv_w.shape), _resident((1, d)), _resident(wa.shape), _resident(wi.shape),
                  _resident((1, d)), _resident((1, d)), _resident((1, d))],
        out_specs=[tile(), per_seq(M_HEADS, dk, dk), per_seq(M_HEADS, dk), per_seq(M_HEADS, 1),
                   per_seq(CONV_W - 1, d), per_seq(1, d)],
        out_shape=[jax.ShapeDtypeStruct((rows, d), F32),
                   jax.ShapeDtypeStruct((nb, M_HEADS, dk, dk), F32),
                   jax.ShapeDtypeStruct((nb, M_HEADS, dk), F32),
                   jax.ShapeDtypeStruct((nb, M_HEADS, 1), F32),
                   jax.ShapeDtypeStruct((nb, CONV_W - 1, d), F32),
                   jax.ShapeDtypeStruct((nb, 1, d), F32)],
        scratch_shapes=[pltpu.VMEM((tl, d), F32),
                        pltpu.VMEM((7 * d, d), BF16),
                        pltpu.VMEM((LANES, d), BF16),
                        pltpu.VMEM((d, d), BF16), pltpu.VMEM((d, d), BF16), pltpu.VMEM((d, d), BF16)]
        + _stage_scratch(d)
        + [pltpu.VMEM((d // LANES, tl, LANES), F32),
           pltpu.VMEM(((CONV_W - 1) * SUBLANES + tl, d), F32),
           pltpu.VMEM((tl, d), F32),
           pltpu.VMEM((tl, d), F32),
           pltpu.VMEM((CONV_W - 1, d), F32),
           pltpu.VMEM((1, d), F32),
           pltpu.VMEM((tl, d), BF16),
           pltpu.VMEM((tl, d), BF16)],
        compiler_params=_cparams(("arbitrary", "arbitrary")),
        name="mixer_prompt",
    )(x2, mod3, mod3, mod3, row(gain), w_int, bif, bift, wbra, wbrb, wout,
      conv_w, row(conv_b), wa, wi, row(ba), row(bi), row(lam))


def _head_sum(x, dk):
    parts = []
    for h in range(x.shape[1] // dk):
        sl = x[:, h * dk:(h + 1) * dk]
        parts.append(jnp.broadcast_to(jnp.sum(sl, axis=-1, keepdims=True), sl.shape))
    return jnp.concatenate(parts, axis=-1)


def _head_spread(cols, lane0, dk, rows):
    return jnp.concatenate(
        [jnp.broadcast_to(cols[:, lane0 + h:lane0 + h + 1], (rows, dk)) for h in range(M_HEADS)], axis=-1)


def _dec_pre_kernel(x_ref, sh_ref, sc_ref, g_ref, wint_hbm, bif_ref,
                    cw_ref, cb_ref, wa_ref, wi_ref, ba_ref, bi_ref, lam_ref,
                    conv0_ref, h0_ref, n0_ref, m0_ref,
                    ha_ref, conv_ref, hs_ref, n_ref, m_ref, qt_ref, kt_ref, dec_ref, wv_ref, pw_ref, sv_ref,
                    den_ref, em_ref,
                    wm_ref, wif_ref, stage, sem):
    rows, d = x_ref.shape
    dk = d // M_HEADS
    scale = dk ** -0.5
    _stage_rows(wint_hbm, 0, 4 * d, wm_ref, 0, stage, sem)
    _stage_rows(wint_hbm, 5 * d, LANES, wif_ref, 0, stage, sem)
    xn = _rms_mod(x_ref[...], g_ref[...], sh_ref[...], sc_ref[...]).astype(BF16)

    x_rg = _dot_nt(xn, wm_ref[0:d, :])
    u = cb_ref[...]
    for j in range(CONV_W - 1):
        u = u + conv0_ref[j] * cw_ref[j:j + 1, :]
        if j > 0:
            conv_ref[j - 1] = conv0_ref[j]
    u = u + x_rg * cw_ref[CONV_W - 1:CONV_W, :]
    conv_ref[CONV_W - 2] = x_rg
    a, b = _rg_gates(u, wa_ref, wi_ref, ba_ref[...], bi_ref[...], lam_ref[...])
    h = a * h0_ref[...] + b
    hs_ref[...] = h
    ha_ref[...] = h

    q = _dot_nt(xn, wm_ref[d:2 * d, :])
    k = _dot_nt(xn, wm_ref[2 * d:3 * d, :])
    v = _dot_nt(xn, wm_ref[3 * d:4 * d, :])
    qt_ref[...] = _dot_nt(wm_ref[d:2 * d, :], xn).astype(BF16)
    kt_ref[...] = _dot_nt(wm_ref[2 * d:3 * d, :], xn).astype(BF16)
    pre = _dot_nt(xn, wif_ref[...]) + bif_ref[...]
    ig = _head_spread(pre, 0, dk, rows)
    lf = jax.nn.log_sigmoid(_head_spread(pre, M_HEADS, dk, rows))
    m0 = _head_spread(m0_ref[...], 0, dk, rows)
    n0 = n0_ref[...]
    log_past = lf + m0
    m_t = jnp.maximum(log_past, ig)
    d_w = jnp.exp(ig - m_t)
    past_w = jnp.exp(log_past - m_t)
    qs = q * scale
    s = _head_sum(qs * k, dk) * d_w
    den_ref[...] = past_w * _head_sum(qs * n0, dk) + s
    em_ref[...] = jnp.exp(-m_t)
    pw_ref[...] = past_w
    sv_ref[...] = s * v
    w_s = jnp.exp(ig - m_t)
    decay = jnp.exp(lf + m0 - m_t)
    dec_ref[...] = decay
    wv_ref[...] = w_s * v
    n_ref[...] = decay * n0 + w_s * k
    m_ref[...] = m_t


def _dec_pre(x2, mod, gain, w_int, bif, conv_w, conv_b, wa, wi, ba, bi, lam, conv0, h0, n0, m0):
    rows, d = x2.shape
    row = lambda v: v.reshape(1, d)
    m0p = jnp.pad(m0, ((0, 0), (0, LANES - M_HEADS)))
    full = lambda shape: pl.BlockSpec(shape, lambda i, _n=len(shape): (0,) * _n)
    mspec = lambda j: pl.BlockSpec((rows, d), lambda i: (0, j))
    vec = jax.ShapeDtypeStruct((rows, d), F32)
    outs = [vec,
            jax.ShapeDtypeStruct((CONV_W - 1, rows, d), F32),
            vec,
            vec,
            vec,
            jax.ShapeDtypeStruct((d, rows), BF16),
            jax.ShapeDtypeStruct((d, rows), BF16),
            vec, vec, vec, vec, vec, vec]
    args = [x2, mod, mod, row(gain), w_int, bif, conv_w, row(conv_b), wa, wi,
            row(ba), row(bi), row(lam), conv0, h0, n0, m0p]
    in_specs = ([full(x2.shape), mspec(3), mspec(4), full((1, d)), _HBM]
                + [full(a.shape) for a in args[5:]])
    return pl.pallas_call(
        _dec_pre_kernel,
        grid=(1,),
        in_specs=in_specs,
        out_specs=[full(o.shape) for o in outs],
        out_shape=outs,
        scratch_shapes=[pltpu.VMEM((4 * d, d), BF16), pltpu.VMEM((LANES, d), BF16)] + _stage_scratch(d),
        compiler_params=_cparams(("arbitrary",)),
        name="decode_pre",
    )(*args)


def _dec_mem_kernel(qt_ref, kt_ref, dec_ref, wv_ref, c0_ref, c_ref, qc_ref):
    bb = c0_ref.shape[0]
    dk = c0_ref.shape[2]
    nseq = qt_ref.shape[1]
    scale = dk ** -0.5
    base = pl.program_id(0) * bb
    seq_id = lax.broadcasted_iota(jnp.int32, (nseq, dk), 0)
    for j in range(bb):
        pick = (seq_id == base + j).astype(BF16)
        qcol = _dot(qt_ref[...], pick) * scale
        kcol = _dot(kt_ref[...], pick)
        for h in range(M_HEADS):
            c0 = c0_ref[j, h]
            dec = dec_ref[j, :, h * dk:(h + 1) * dk]
            wv = wv_ref[j, :, h * dk:(h + 1) * dk]
            c_ref[j, h] = dec * c0 + kcol[h * dk:(h + 1) * dk, :] * wv
            qc_ref[j, :, h * dk:(h + 1) * dk] = jnp.sum(qcol[h * dk:(h + 1) * dk, :] * c0, axis=0, keepdims=True)


def _dec_mem(qt, kt, decay, wv, c0):
    nseq, heads, dk, _ = c0.shape
    d = heads * dk
    bb = DEC_BLOCK if nseq % DEC_BLOCK == 0 else 1
    rowblk = pl.BlockSpec((bb, 1, d), lambda i: (i, 0, 0))
    return pl.pallas_call(
        _dec_mem_kernel,
        grid=(nseq // bb,),
        in_specs=[pl.BlockSpec((d, nseq), lambda i: (0, 0)), pl.BlockSpec((d, nseq), lambda i: (0, 0)),
                  rowblk, rowblk,
                  pl.BlockSpec((bb, heads, dk, dk), lambda i: (i, 0, 0, 0))],
        out_specs=[pl.BlockSpec((bb, heads, dk, dk), lambda i: (i, 0, 0, 0)), rowblk],
        out_shape=[jax.ShapeDtypeStruct(c0.shape, F32), jax.ShapeDtypeStruct((nseq, 1, d), F32)],
        compiler_params=_cparams(("arbitrary",)),
        name="decode_mem",
    )(qt, kt, decay.reshape(nseq, 1, d), wv.reshape(nseq, 1, d), c0)


def _dec_post_kernel(x_ref, sh_ref, sc_ref, ga_ref, g_ref, wint_hbm, wbra_hbm, wbrb_hbm, wout_hbm,
                     ha_ref, qc_ref, pw_ref, sv_ref, den_ref, em_ref, o_ref,
                     wm_ref, wbra_ref, wbrb_ref, wout_ref, stage, sem):
    d = x_ref.shape[1]
    _stage_rows(wint_hbm, 4 * d, d, wm_ref, 0, stage, sem)
    _stage_rows(wint_hbm, 5 * d + 2 * M_HEADS, 2 * d, wm_ref, d, stage, sem)
    _stage_rows(wbra_hbm, 0, d, wbra_ref, 0, stage, sem)
    _stage_rows(wbrb_hbm, 0, d, wbrb_ref, 0, stage, sem)
    _stage_rows(wout_hbm, 0, d, wout_ref, 0, stage, sem)
    x = x_ref[...]
    xn = _rms_mod(x, g_ref[...], sh_ref[...], sc_ref[...]).astype(BF16)
    num = pw_ref[...] * qc_ref[...] + sv_ref[...]
    hm = num / jnp.maximum(jnp.abs(den_ref[...]), em_ref[...])
    hb = (jax.nn.sigmoid(_dot_nt(xn, wm_ref[0:d, :])) * hm).astype(BF16)
    g_a = _dot_nt(xn, wm_ref[d:2 * d, :])
    g_b = _dot_nt(xn, wm_ref[2 * d:3 * d, :])
    merged = (jax.nn.sigmoid(g_a) * _dot(ha_ref[...].astype(BF16), wbra_ref[...])
              + jax.nn.sigmoid(g_b) * _dot(hb, wbrb_ref[...]))
    o_ref[...] = x + ga_ref[...] * _dot(merged.astype(BF16), wout_ref[...])


def _dec_post(x2, mod, gain, w_int, wbra, wbrb, wout, ha, qc, pw, sv, den, em):
    rows, d = x2.shape
    full = lambda shape: pl.BlockSpec(shape, lambda i, _n=len(shape): (0,) * _n)
    mspec = lambda j: pl.BlockSpec((rows, d), lambda i: (0, j))
    args = [x2, mod, mod, mod, gain.reshape(1, d), w_int, wbra, wbrb, wout, ha, qc, pw, sv, den, em]
    in_specs = ([full(x2.shape), mspec(3), mspec(4), mspec(5), full((1, d))] + [_HBM] * 4
                + [full(a.shape) for a in args[9:]])
    return pl.pallas_call(
        _dec_post_kernel,
        grid=(1,),
        in_specs=in_specs,
        out_specs=full((rows, d)),
        out_shape=jax.ShapeDtypeStruct((rows, d), F32),
        scratch_shapes=[pltpu.VMEM((3 * d, d), BF16),
                        pltpu.VMEM((d, d), BF16), pltpu.VMEM((d, d), BF16), pltpu.VMEM((d, d), BF16)]
        + _stage_scratch(d),
        compiler_params=_cparams(("arbitrary",)),
        name="decode_post",
    )(*args)


def _pair_blocks(w):
    nblk, c, _ = w.shape
    z = jnp.zeros((nblk // 2, c, c), w.dtype)
    top = jnp.concatenate([w[0::2], z], axis=2)
    bot = jnp.concatenate([z, w[1::2]], axis=2)
    return jnp.concatenate([top, bot], axis=1).astype(BF16)


def kernel(x_prompt, x_sample, state_conv, state_rg_h, state_C, state_n, state_m, c_prompt, c_sample, w_ada, b_ada, g_norm1, w_ff1_in, w_ff1_out, g_norm2, w_in, conv_w, conv_b, w_rg_a, b_rg_a, w_rg_i, b_rg_i, rg_lambda, b_ig, b_fg, w_br_a, w_br_b, w_out, g_norm3, w_ff2_in, w_ff2_out, g_final):
    nb, seq, d = x_prompt.shape
    ns = x_sample.shape[0]
    depth = w_ada.shape[0]
    assert depth == 1 and x_sample.shape[1] == 1 and nb == SUBLANES
    assert seq % ML_ROWS == 0 and seq % FFN_ROWS == 0 and ML_ROWS % (SUBLANES * SUBLANES) == 0
    heads, dk = M_HEADS, d // M_HEADS
    assert w_in.shape[2] == 7 * d + 2 * heads and GATE_ROWS >= 2 * heads

    w_int = w_in[0].T
    b_gates = jnp.concatenate([b_ig[0], b_fg[0]])
    bif = jnp.pad(b_gates, (0, LANES - 2 * heads)).reshape(1, LANES)
    bift = jnp.pad(b_gates, (0, GATE_ROWS - 2 * heads)).reshape(GATE_ROWS, 1)
    wa = _pair_blocks(w_rg_a[0])
    wi = _pair_blocks(w_rg_i[0])
    wf1i, wf1o, wf2i, wf2o = w_ff1_in[0], w_ff1_out[0], w_ff2_in[0], w_ff2_out[0]
    wbra, wbrb, wout = w_br_a[0], w_br_b[0], w_out[0]

    mod = _ada(jnp.concatenate([c_prompt, c_sample], axis=0), w_ada[0], b_ada[0])
    mod_p3 = mod[:nb].reshape(nb, 1, N_MOD * d)
    mod_s = mod[nb:]

    xp = x_prompt.reshape(nb * seq, d)
    xp = _ffn(xp, mod_p3, 0, g_norm1[0], wf1i, wf1o, None, rows_per_seq=seq, per_row_mod=False)
    xp, c_p, n_p, m_p, conv_p, h_p = _mixer_prompt(
        xp, mod_p3, g_norm2[0], w_int, bif, bift, wbra, wbrb, wout, conv_w[0], conv_b[0], wa, wi,
        b_rg_a[0], b_rg_i[0], rg_lambda[0], nb=nb, seq=seq)
    yp = _ffn(xp, mod_p3, 6, g_norm3[0], wf2i, wf2o, g_final, rows_per_seq=seq, per_row_mod=False)

    xs = x_sample.reshape(ns, d)
    xs = _ffn(xs, mod_s, 0, g_norm1[0], wf1i, wf1o, None, rows_per_seq=1, per_row_mod=True)
    conv0 = jnp.swapaxes(state_conv[0], 0, 1)
    (ha_s, conv_s, h_s, n_s, m_s, qt, kt, decay, wv, pw, sv, den, em) = _dec_pre(
        xs, mod_s, g_norm2[0], w_int, bif, conv_w[0], conv_b[0], wa, wi,
        b_rg_a[0], b_rg_i[0], rg_lambda[0], conv0, state_rg_h[0], state_n[0].reshape(ns, d), state_m[0])
    c_s, qc = _dec_mem(qt, kt, decay, wv, state_C[0])
    xs = _dec_post(xs, mod_s, g_norm2[0], w_int, wbra, wbrb, wout, ha_s, qc.reshape(ns, d), pw, sv, den, em)
    ys = _ffn(xs, mod_s, 6, g_norm3[0], wf2i, wf2o, g_final, rows_per_seq=1, per_row_mod=True)

    return (yp.reshape(nb, seq, d), ys.reshape(ns, 1, d),
            conv_p[None], h_p.reshape(1, nb, d), c_p[None], n_p[None], m_p.reshape(1, nb, heads),
            jnp.swapaxes(conv_s, 0, 1)[None], h_s[None], c_s[None], n_s.reshape(1, ns, heads, dk), m_s[:, ::dk][None])
```

```python
import functools

import jax
import jax.numpy as jnp
from jax import lax
from jax.experimental import pallas as pl
from jax.experimental.pallas import tpu as pltpu

F32 = jnp.float32
BF16 = jnp.bfloat16

EPS = 1e-6
RG_C = 8.0
CONV_W = 4
N_MOD = 9
M_HEADS = 4
RG_BLOCKS = 8

SUBLANES = 8
LANES = 128
MXU_DIM = 256
VMEM_LIMIT_BYTES = 60 * 1024 * 1024

FFN_ROWS = 1024
FFN_COLS = 256
ML_ROWS = 512
ML_CHUNK = 256
DEC_BLOCK = 2
STAGE_ROWS = 256
STAGE_SLOTS = 4
STAGE_ROWS_WIDE = 64
STAGE_SLOTS_WIDE = 4
GATE_ROWS = 16


def _cparams(sem):
    return pltpu.CompilerParams(dimension_semantics=sem, vmem_limit_bytes=VMEM_LIMIT_BYTES)


def _resident(shape):
    nd = len(shape)
    return pl.BlockSpec(shape, lambda *_: (0,) * nd, pipeline_mode=pl.Buffered(1))


_HBM = pl.BlockSpec(memory_space=pl.ANY)


def _rms_mod(x, gain, shift, scale):
    ms = jnp.mean(x * x, axis=-1, keepdims=True)
    return x * lax.rsqrt(ms + EPS) * gain * (1.0 + scale) + shift


def _dot(a, b):
    return jnp.dot(a, b, preferred_element_type=F32)


def _dot_nt(a, b):
    return lax.dot_general(a, b, (((1,), (1,)), ((), ())), preferred_element_type=F32)


def _dot_tn(a, b):
    return lax.dot_general(a, b, (((0,), (0,)), ((), ())), preferred_element_type=F32)


def _split3(x):
    hi = x.astype(BF16)
    r1 = x - hi.astype(F32)
    mid = r1.astype(BF16)
    lo = (r1 - mid.astype(F32)).astype(BF16)
    return hi, mid, lo


def _stage_rows(src, src_row0, nrows, dst, dst_row0, stage, sem):
    slots, rows_per_copy, _ = stage.shape
    n = pl.cdiv(nrows, rows_per_copy)

    def rows_of(c):
        return min(rows_per_copy, nrows - c * rows_per_copy)

    def copy(c):
        return pltpu.make_async_copy(src.at[pl.ds(src_row0 + c * rows_per_copy, rows_of(c))],
                                     stage.at[c % slots, pl.ds(0, rows_of(c))], sem.at[c % slots])

    for c in range(min(slots, n)):
        copy(c).start(priority=c % 2)
    for c in range(n):
        copy(c).wait()
        r0 = dst_row0 + c * rows_per_copy
        dst[r0:r0 + rows_of(c), :] = stage[c % slots, 0:rows_of(c), :].astype(BF16)
        if c + slots < n:
            copy(c + slots).start(priority=c % 2)


def _stage_scratch(cols, wide=False):
    slots, rows = (STAGE_SLOTS_WIDE, STAGE_ROWS_WIDE) if wide else (STAGE_SLOTS, STAGE_ROWS)
    return [pltpu.VMEM((slots, rows, cols), F32), pltpu.SemaphoreType.DMA((slots,))]


def _ada_kernel(c_ref, w_ref, b_ref, o_ref):
    c = c_ref[...]
    sc = (c * jax.nn.sigmoid(c)).astype(BF16)
    o_ref[...] = _dot(sc, w_ref[...].astype(BF16)) + b_ref[...]


def _ada(c, w_ada, b_ada):
    rows, d = c.shape
    n = w_ada.shape[1]
    tn = d
    return pl.pallas_call(
        _ada_kernel,
        grid=(n // tn,),
        in_specs=[pl.BlockSpec((rows, d), lambda j: (0, 0)),
                  pl.BlockSpec((d, tn), lambda j: (0, j)),
                  pl.BlockSpec((1, tn), lambda j: (0, j))],
        out_specs=pl.BlockSpec((rows, tn), lambda j: (0, j)),
        out_shape=jax.ShapeDtypeStruct((rows, n), F32),
        compiler_params=_cparams(("arbitrary",)),
        name="adaln_mod",
    )(c, w_ada, b_ada.reshape(1, n))


def _ffn_kernel(*refs, d_ff, final):
    if final:
        x_ref, sh_ref, sc_ref, ga_ref, g_ref, wi_hbm, wo_hbm, gf_ref, o_ref = refs[:9]
    else:
        x_ref, sh_ref, sc_ref, ga_ref, g_ref, wi_hbm, wo_hbm, o_ref = refs[:8]
    act_ref, wi_ref, wo_ref, stage_i, sem_i, stage_o, sem_o = refs[-7:]

    @pl.when(pl.program_id(0) == 0)
    def _():
        _stage_rows(wi_hbm, 0, wi_ref.shape[0], wi_ref, 0, stage_i, sem_i)
        _stage_rows(wo_hbm, 0, wo_ref.shape[0], wo_ref, 0, stage_o, sem_o)

    x = x_ref[...]
    xn = _rms_mod(x, g_ref[...], sh_ref[...], sc_ref[...]).astype(BF16)
    for j in range(d_ff // FFN_COLS):
        lo = j * FFN_COLS
        hg = _dot(xn, wi_ref[:, lo:lo + FFN_COLS])
        hu = _dot(xn, wi_ref[:, d_ff + lo:d_ff + lo + FFN_COLS])
        act_ref[:, lo:lo + FFN_COLS] = (hg * jax.nn.sigmoid(hg) * hu).astype(BF16)
    y = _dot(act_ref[...], wo_ref[...])
    out = x + (0.5 * ga_ref[...]) * y
    if final:
        ms = jnp.mean(out * out, axis=-1, keepdims=True)
        out = out * lax.rsqrt(ms + EPS) * gf_ref[...]
    o_ref[...] = out


def _mod_spec_prompt(j, tiles_per_seq, d):
    return pl.BlockSpec((None, 1, d), lambda i: (i // tiles_per_seq, 0, j))


def _ffn(x2, mod, j0, gain, wi, wo, g_final, *, rows_per_seq, per_row_mod):
    rows, d = x2.shape
    d_ff = wo.shape[0]
    if per_row_mod:
        tm = rows
        mspec = lambda j: pl.BlockSpec((tm, d), lambda i: (0, j))
    else:
        tm = min(FFN_ROWS, rows_per_seq)
        mspec = lambda j: _mod_spec_prompt(j, rows_per_seq // tm, d)
    final = g_final is not None
    in_specs = [pl.BlockSpec((tm, d), lambda i: (i, 0)),
                mspec(j0), mspec(j0 + 1), mspec(j0 + 2),
                _resident((1, d)), _HBM, _HBM]
    args = [x2, mod, mod, mod, gain.reshape(1, d), wi, wo]
    if final:
        in_specs.append(_resident((1, d)))
        args.append(g_final.reshape(1, d))
    return pl.pallas_call(
        functools.partial(_ffn_kernel, d_ff=d_ff, final=final),
        grid=(rows // tm,),
        in_specs=in_specs,
        out_specs=pl.BlockSpec((tm, d), lambda i: (i, 0)),
        out_shape=jax.ShapeDtypeStruct((rows, d), F32),
        scratch_shapes=[pltpu.VMEM((tm, d_ff), BF16), pltpu.VMEM(wi.shape, BF16), pltpu.VMEM(wo.shape, BF16)]
        + _stage_scratch(wi.shape[1], wide=True) + _stage_scratch(wo.shape[1]),
        compiler_params=_cparams(("arbitrary",)),
        name="ffn_final" if final else "ffn",
    )(*args)


def _rg_gates_block(u, wa, wi, ba, bi, lam):
    ub = u.astype(BF16)
    r = jax.nn.sigmoid(_dot(ub, wa) + ba)
    i_g = jax.nn.sigmoid(_dot(ub, wi) + bi)
    log_a = -RG_C * r * jax.nn.softplus(-lam)
    a = jnp.exp(log_a)
    b = jnp.sqrt(-jnp.tanh(log_a) * (a * a + 1.0)) * (i_g * u)
    return a, b


def _rg_gates(u, wa_ref, wi_ref, ba, bi, lam):
    parts = []
    for p in range(wa_ref.shape[0]):
        cols = slice(p * MXU_DIM, (p + 1) * MXU_DIM)
        parts.append(_rg_gates_block(u[:, cols], wa_ref[p], wi_ref[p], ba[:, cols], bi[:, cols], lam[:, cols]))
    return (jnp.concatenate([a for a, _ in parts], axis=-1), jnp.concatenate([b for _, b in parts], axis=-1))


def _rglru_front(xn_ref, w_rows_ref, lb_ref, tb_ref, tail_ref, conv_ref):
    tl, d = xn_ref.shape
    seg = tl // SUBLANES
    nlb = d // LANES
    pre = (CONV_W - 1) * SUBLANES

    x_rg = _dot_nt(xn_ref[...], w_rows_ref[0:d, :])
    for cb in range(nlb):
        lb_ref[cb] = x_rg[:, cb * LANES:(cb + 1) * LANES]
    for j in range(seg):
        for cb in range(nlb):
            tb_ref[pre + j * SUBLANES:pre + (j + 1) * SUBLANES, cb * LANES:(cb + 1) * LANES] = (
                lb_ref[cb, pl.ds(j, SUBLANES, stride=seg), :])

    first = lax.broadcasted_iota(jnp.int32, (SUBLANES, d), 0) == 0
    for i in range(CONV_W - 1):
        j = seg - (CONV_W - 1) + i
        slab = tb_ref[pre + j * SUBLANES:pre + (j + 1) * SUBLANES, :]
        tb_ref[i * SUBLANES:(i + 1) * SUBLANES, :] = jnp.where(first, tail_ref[i:i + 1, :],
                                                              pltpu.roll(slab, 1, axis=0))
        last = slab[SUBLANES - 1:SUBLANES, :]
        tail_ref[i:i + 1, :] = last
        conv_ref[i:i + 1, :] = last


def _rglru_block(p, cw_ref, cb_ref, wa_ref, wi_ref, ba_ref, bi_ref, lam_ref,
                 lb_ref, tb_ref, a_ref, b_ref, h_ref, ha_ref):
    tl = a_ref.shape[0]
    seg = tl // SUBLANES
    cols = slice(p * MXU_DIM, (p + 1) * MXU_DIM)
    lane_blocks = range(p * MXU_DIM // LANES, (p + 1) * MXU_DIM // LANES)

    u = cb_ref[:, cols]
    for j in range(CONV_W):
        u = u + tb_ref[j * SUBLANES:j * SUBLANES + tl, cols] * cw_ref[j:j + 1, cols]
    a, b = _rg_gates_block(u, wa_ref[p], wi_ref[p], ba_ref[:, cols], bi_ref[:, cols], lam_ref[:, cols])
    a_ref[:, cols] = a
    b_ref[:, cols] = b

    acc_a = a_ref[0:SUBLANES, cols]
    acc_b = b_ref[0:SUBLANES, cols]
    for j in range(1, seg):
        rows = slice(j * SUBLANES, (j + 1) * SUBLANES)
        a_j = a_ref[rows, cols]
        acc_b = a_j * acc_b + b_ref[rows, cols]
        acc_a = a_j * acc_a
        a_ref[rows, cols] = acc_a
        b_ref[rows, cols] = acc_b

    h = h_ref[:, cols]
    h_in = []
    for s in range(SUBLANES):
        h_in.append(h)
        h = acc_a[s:s + 1, :] * h + acc_b[s:s + 1, :]
    h_ref[:, cols] = h
    h_in = jnp.concatenate(h_in, axis=0)

    for j in range(seg):
        rows = slice(j * SUBLANES, (j + 1) * SUBLANES)
        h_j = a_ref[rows, cols] * h_in + b_ref[rows, cols]
        for i, cb in enumerate(lane_blocks):
            lb_ref[cb, rows, :] = h_j[:, i * LANES:(i + 1) * LANES]
    for s in range(SUBLANES):
        for cb in lane_blocks:
            ha_ref[s * seg:(s + 1) * seg, cb * LANES:(cb + 1) * LANES] = (
                lb_ref[cb, pl.ds(s, seg, stride=SUBLANES), :].astype(BF16))


def _mixer_kernel(x_ref, sh_ref, sc_ref, ga_ref, g_ref, wint_hbm, bif_ref, bift_ref,
                  wbra_hbm, wbrb_hbm, wout_hbm, cw_ref, cb_ref, wa_ref, wi_ref, ba_ref, bi_ref, lam_ref,
                  o_ref, c_ref, n_ref, m_ref, conv_ref, hl_ref,
                  hm_ref, wm_ref, wif_ref, wbra_ref, wbrb_ref, wout_ref, stage, sem,
                  lb_ref, tb_ref, a_ref, b_ref, tail_ref, h_ref, ha_ref, mg_ref, xn_ref, hb_ref):
    tl, d = x_ref.shape
    dk = d // M_HEADS
    cs = min(ML_CHUNK, tl)
    scale = dk ** -0.5
    gate_row0 = 5 * d

    @pl.when((pl.program_id(0) == 0) & (pl.program_id(1) == 0))
    def _():
        _stage_rows(wint_hbm, 0, 5 * d, wm_ref, 0, stage, sem)
        _stage_rows(wint_hbm, gate_row0 + 2 * M_HEADS, 2 * d, wm_ref, 5 * d, stage, sem)
        _stage_rows(wint_hbm, gate_row0, LANES, wif_ref, 0, stage, sem)
        _stage_rows(wbra_hbm, 0, d, wbra_ref, 0, stage, sem)
        _stage_rows(wbrb_hbm, 0, d, wbrb_ref, 0, stage, sem)
        _stage_rows(wout_hbm, 0, d, wout_ref, 0, stage, sem)

    @pl.when(pl.program_id(1) == 0)
    def _():
        c_ref[...] = jnp.zeros_like(c_ref)
        n_ref[...] = jnp.zeros_like(n_ref)
        m_ref[...] = jnp.zeros_like(m_ref)
        tail_ref[...] = jnp.zeros_like(tail_ref)
        h_ref[...] = jnp.zeros_like(h_ref)

    x = x_ref[...]
    xn_ref[...] = _rms_mod(x, g_ref[...], sh_ref[...], sc_ref[...]).astype(BF16)

    _rglru_front(xn_ref, wm_ref, lb_ref, tb_ref, tail_ref, conv_ref)
    rglru_block = functools.partial(
        _rglru_block, cw_ref=cw_ref, cb_ref=cb_ref, wa_ref=wa_ref, wi_ref=wi_ref, ba_ref=ba_ref, bi_ref=bi_ref,
        lam_ref=lam_ref, lb_ref=lb_ref, tb_ref=tb_ref, a_ref=a_ref, b_ref=b_ref, h_ref=h_ref, ha_ref=ha_ref)
    n_rg = wa_ref.shape[0]
    wq0 = d

    pre_c = _dot_nt(xn_ref[...], wif_ref[...]) + bif_ref[...]
    pre_r = _dot_nt(wif_ref[0:GATE_ROWS, :], xn_ref[...]) + bift_ref[...]
    col_is_f = lax.broadcasted_iota(jnp.int32, pre_c.shape, 1) >= M_HEADS
    row_is_f = lax.broadcasted_iota(jnp.int32, pre_r.shape, 0) >= M_HEADS
    gate_c = jnp.where(col_is_f, jax.nn.log_sigmoid(pre_c), pre_c)
    gate_r = jnp.where(row_is_f, jax.nn.log_sigmoid(pre_r), pre_r)

    ti = lax.broadcasted_iota(jnp.int32, (cs, cs), 0)
    si = lax.broadcasted_iota(jnp.int32, (cs, cs), 1)
    causal = si <= ti
    lower = causal.astype(BF16)
    upper = (ti <= si).astype(BF16)

    chunks = []
    for c in range(tl // cs):
        gc = gate_c[c * cs:(c + 1) * cs, :]
        gr = gate_r[:, c * cs:(c + 1) * cs]
        cum_c = sum(_dot(lower, part) for part in _split3(gc))
        cum_r = sum(_dot(part, upper) for part in _split3(gr))
        chunks.append((gc, gr, cum_c, cum_r))

    for h in range(M_HEADS):
        c0 = h * dk
        q_all = _dot_nt(xn_ref[...], wm_ref[wq0 + c0:wq0 + c0 + dk, :])
        k_all = _dot_nt(xn_ref[...], wm_ref[wq0 + d + c0:wq0 + d + c0 + dk, :])
        v_all = _dot_nt(xn_ref[...], wm_ref[wq0 + 2 * d + c0:wq0 + 2 * d + c0 + dk, :])
        for c, (gc, gr, cum_c, cum_r) in enumerate(chunks):
            r0 = c * cs
            q = q_all[r0:r0 + cs, :]
            k = k_all[r0:r0 + cs, :]
            v = v_all[r0:r0 + cs, :]
            bc = cum_c[:, M_HEADS + h:M_HEADS + h + 1]
            br = cum_r[M_HEADS + h:M_HEADS + h + 1, :]
            ig_c = gc[:, h:h + 1]
            ig_r = gr[h:h + 1, :]
            m_prev = m_ref[h:h + 1, :]
            c_prev = c_ref[h]
            n_prev = n_ref[h:h + 1, :]

            log_d = jnp.where(causal, bc - br + ig_r, -jnp.inf)
            log_past = bc + m_prev
            m_t = jnp.maximum(log_past, jnp.max(log_d, axis=-1, keepdims=True))
            d_mat = jnp.exp(log_d - m_t)
            past_w = jnp.exp(log_past - m_t)
            qs = q * scale
            qsb = qs.astype(BF16)
            kb = k.astype(BF16)
            vb = v.astype(BF16)
            s = _dot_nt(qsb, kb) * d_mat
            num = past_w * _dot(qsb, c_prev.astype(BF16)) + _dot(s.astype(BF16), vb)
            den = past_w * jnp.sum(qs * n_prev, axis=-1, keepdims=True) + jnp.sum(s, axis=-1, keepdims=True)
            hm_ref[r0:r0 + cs, c0:c0 + dk] = num / jnp.maximum(jnp.abs(den), jnp.exp(-m_t))

            m_new = m_t[cs - 1:cs, :]
            b_last = bc[cs - 1:cs, :]
            w_s = jnp.exp(b_last - bc + ig_c - m_new)
            decay = jnp.exp(b_last + m_prev - m_new)
            kw = w_s * k
            c_ref[h] = decay * c_prev + _dot_tn(kw.astype(BF16), vb)
            n_ref[h:h + 1, :] = decay * n_prev + jnp.sum(kw, axis=0, keepdims=True)
            m_ref[h:h + 1, :] = m_new
        if h < n_rg:
            rglru_block(h)
    for p in range(M_HEADS, n_rg):
        rglru_block(p)
    hl_ref[...] = h_ref[...]

    for n0 in range(0, d, MXU_DIM):
        cols = slice(n0, n0 + MXU_DIM)
        o_pre = _dot_nt(xn_ref[...], wm_ref[4 * d + n0:4 * d + n0 + MXU_DIM, :])
        hb_ref[:, cols] = (jax.nn.sigmoid(o_pre) * hm_ref[:, cols]).astype(BF16)
    for n0 in range(0, d, MXU_DIM):
        cols = slice(n0, n0 + MXU_DIM)
        g_a = _dot_nt(xn_ref[...], wm_ref[5 * d + n0:5 * d + n0 + MXU_DIM, :])
        g_b = _dot_nt(xn_ref[...], wm_ref[6 * d + n0:6 * d + n0 + MXU_DIM, :])
        merged = (jax.nn.sigmoid(g_a) * _dot(ha_ref[...], wbra_ref[:, cols])
                  + jax.nn.sigmoid(g_b) * _dot(hb_ref[...], wbrb_ref[:, cols]))
        mg_ref[:, cols] = merged.astype(BF16)
    o_ref[...] = x + ga_ref[...] * _dot(mg_ref[...], wout_ref[...])


def _mixer_prompt(x2, mod3, gain, w_int, bif, bift, wbra, wbrb, wout, conv_w, conv_b, wa, wi, ba, bi, lam,
                  *, nb, seq):
    rows, d = x2.shape
    dk = d // M_HEADS
    tl = min(ML_ROWS, seq)
    nt = seq // tl
    row = lambda v: v.reshape(1, d)
    mspec = lambda j: pl.BlockSpec((None, 1, d), lambda b, t: (b, 0, j))
    tile = lambda: pl.BlockSpec((tl, d), lambda b, t: (b * nt + t, 0))
    per_seq = lambda *shape: pl.BlockSpec((None,) + shape, lambda b, t: (b,) + (0,) * len(shape))
    return pl.pallas_call(
        _mixer_kernel,
        grid=(nb, nt),
        in_specs=[tile(), mspec(3), mspec(4), mspec(5), _resident((1, d)),
                  _HBM, _resident(bif.shape), _resident(bift.shape), _HBM, _HBM, _HBM,
                  _resident(conv_w.shape), _resident((1, d)), _resident(wa.shape), _resident(wi.shape),
                  _resident((1, d)), _resident((1, d)), _resident((1, d))],
        out_specs=[tile(), per_seq(M_HEADS, dk, dk), per_seq(M_HEADS, dk), per_seq(M_HEADS, 1),
                   per_seq(CONV_W - 1, d), per_seq(1, d)],
        out_shape=[jax.ShapeDtypeStruct((rows, d), F32),
                   jax.ShapeDtypeStruct((nb, M_HEADS, dk, dk), F32),
                   jax.ShapeDtypeStruct((nb, M_HEADS, dk), F32),
                   jax.ShapeDtypeStruct((nb, M_HEADS, 1), F32),
                   jax.ShapeDtypeStruct((nb, CONV_W - 1, d), F32),
                   jax.ShapeDtypeStruct((nb, 1, d), F32)],
        scratch_shapes=[pltpu.VMEM((tl, d), F32),
                        pltpu.VMEM((7 * d, d), BF16),
                        pltpu.VMEM((LANES, d), BF16),
                        pltpu.VMEM((d, d), BF16), pltpu.VMEM((d, d), BF16), pltpu.VMEM((d, d), BF16)]
        + _stage_scratch(d)
        + [pltpu.VMEM((d // LANES, tl, LANES), F32),
           pltpu.VMEM(((CONV_W - 1) * SUBLANES + tl, d), F32),
           pltpu.VMEM((tl, d), F32),
           pltpu.VMEM((tl, d), F32),
           pltpu.VMEM((CONV_W - 1, d), F32),
           pltpu.VMEM((1, d), F32),
           pltpu.VMEM((tl, d), BF16),
           pltpu.VMEM((tl, d), BF16),
           pltpu.VMEM((tl, d), BF16),
           pltpu.VMEM((tl, d), BF16)],
        compiler_params=_cparams(("arbitrary", "arbitrary")),
        name="mixer_prompt",
    )(x2, mod3, mod3, mod3, row(gain), w_int, bif, bift, wbra, wbrb, wout,
      conv_w, row(conv_b), wa, wi, row(ba), row(bi), row(lam))


def _head_sum(x, dk):
    parts = []
    for h in range(x.shape[1] // dk):
        sl = x[:, h * dk:(h + 1) * dk]
        parts.append(jnp.broadcast_to(jnp.sum(sl, axis=-1, keepdims=True), sl.shape))
    return jnp.concatenate(parts, axis=-1)


def _head_spread(cols, lane0, dk, rows):
    return jnp.concatenate(
        [jnp.broadcast_to(cols[:, lane0 + h:lane0 + h + 1], (rows, dk)) for h in range(M_HEADS)], axis=-1)


def _dec_pre_kernel(x_ref, sh_ref, sc_ref, g_ref, wint_hbm, bif_ref,
                    cw_ref, cb_ref, wa_ref, wi_ref, ba_ref, bi_ref, lam_ref,
                    conv0_ref, h0_ref, n0_ref, m0_ref,
                    ha_ref, conv_ref, hs_ref, n_ref, m_ref, qt_ref, kt_ref, dec_ref, wv_ref, pw_ref, sv_ref,
                    den_ref, em_ref,
                    wm_ref, wif_ref, stage, sem):
    rows, d = x_ref.shape
    dk = d // M_HEADS
    scale = dk ** -0.5
    _stage_rows(wint_hbm, 0, 4 * d, wm_ref, 0, stage, sem)
    _stage_rows(wint_hbm, 5 * d, LANES, wif_ref, 0, stage, sem)
    xn = _rms_mod(x_ref[...], g_ref[...], sh_ref[...], sc_ref[...]).astype(BF16)

    x_rg = _dot_nt(xn, wm_ref[0:d, :])
    u = cb_ref[...]
    for j in range(CONV_W - 1):
        u = u + conv0_ref[j] * cw_ref[j:j + 1, :]
        if j > 0:
            conv_ref[j - 1] = conv0_ref[j]
    u = u + x_rg * cw_ref[CONV_W - 1:CONV_W, :]
    conv_ref[CONV_W - 2] = x_rg
    a, b = _rg_gates(u, wa_ref, wi_ref, ba_ref[...], bi_ref[...], lam_ref[...])
    h = a * h0_ref[...] + b
    hs_ref[...] = h
    ha_ref[...] = h

    q = _dot_nt(xn, wm_ref[d:2 * d, :])
    k = _dot_nt(xn, wm_ref[2 * d:3 * d, :])
    v = _dot_nt(xn, wm_ref[3 * d:4 * d, :])
    qt_ref[...] = _dot_nt(wm_ref[d:2 * d, :], xn).astype(BF16)
    kt_ref[...] = _dot_nt(wm_ref[2 * d:3 * d, :], xn).astype(BF16)
    pre = _dot_nt(xn, wif_ref[...]) + bif_ref[...]
    ig = _head_spread(pre, 0, dk, rows)
    lf = jax.nn.log_sigmoid(_head_spread(pre, M_HEADS, dk, rows))
    m0 = _head_spread(m0_ref[...], 0, dk, rows)
    n0 = n0_ref[...]
    log_past = lf + m0
    m_t = jnp.maximum(log_past, ig)
    d_w = jnp.exp(ig - m_t)
    past_w = jnp.exp(log_past - m_t)
    qs = q * scale
    s = _head_sum(qs * k, dk) * d_w
    den_ref[...] = past_w * _head_sum(qs * n0, dk) + s
    em_ref[...] = jnp.exp(-m_t)
    pw_ref[...] = past_w
    sv_ref[...] = s * v
    w_s = jnp.exp(ig - m_t)
    decay = jnp.exp(lf + m0 - m_t)
    dec_ref[...] = decay
    wv_ref[...] = w_s * v
    n_ref[...] = decay * n0 + w_s * k
    m_ref[...] = m_t


def _dec_pre(x2, mod, gain, w_int, bif, conv_w, conv_b, wa, wi, ba, bi, lam, conv0, h0, n0, m0):
    rows, d = x2.shape
    row = lambda v: v.reshape(1, d)
    m0p = jnp.pad(m0, ((0, 0), (0, LANES - M_HEADS)))
    full = lambda shape: pl.BlockSpec(shape, lambda i, _n=len(shape): (0,) * _n)
    mspec = lambda j: pl.BlockSpec((rows, d), lambda i: (0, j))
    vec = jax.ShapeDtypeStruct((rows, d), F32)
    outs = [vec,
            jax.ShapeDtypeStruct((CONV_W - 1, rows, d), F32),
            vec,
            vec,
            vec,
            jax.ShapeDtypeStruct((d, rows), BF16),
            jax.ShapeDtypeStruct((d, rows), BF16),
            vec, vec, vec, vec, vec, vec]
    args = [x2, mod, mod, row(gain), w_int, bif, conv_w, row(conv_b), wa, wi,
            row(ba), row(bi), row(lam), conv0, h0, n0, m0p]
    in_specs = ([full(x2.shape), mspec(3), mspec(4), full((1, d)), _HBM]
                + [full(a.shape) for a in args[5:]])
    return pl.pallas_call(
        _dec_pre_kernel,
        grid=(1,),
        in_specs=in_specs,
        out_specs=[full(o.shape) for o in outs],
        out_shape=outs,
        scratch_shapes=[pltpu.VMEM((4 * d, d), BF16), pltpu.VMEM((LANES, d), BF16)] + _stage_scratch(d),
        compiler_params=_cparams(("arbitrary",)),
        name="decode_pre",
    )(*args)


def _dec_mem_kernel(qt_ref, kt_ref, dec_ref, wv_ref, c0_ref, c_ref, qc_ref):
    bb = c0_ref.shape[0]
    dk = c0_ref.shape[2]
    nseq = qt_ref.shape[1]
    scale = dk ** -0.5
    base = pl.program_id(0) * bb
    seq_id = lax.broadcasted_iota(jnp.int32, (nseq, dk), 0)
    for j in range(bb):
        pick = (seq_id == base + j).astype(BF16)
        qcol = _dot(qt_ref[...], pick) * scale
        kcol = _dot(kt_ref[...], pick)
        for h in range(M_HEADS):
            c0 = c0_ref[j, h]
            dec = dec_ref[j, :, h * dk:(h + 1) * dk]
            wv = wv_ref[j, :, h * dk:(h + 1) * dk]
            c_ref[j, h] = dec * c0 + kcol[h * dk:(h + 1) * dk, :] * wv
            qc_ref[j, :, h * dk:(h + 1) * dk] = jnp.sum(qcol[h * dk:(h + 1) * dk, :] * c0, axis=0, keepdims=True)


def _dec_mem(qt, kt, decay, wv, c0):
    nseq, heads, dk, _ = c0.shape
    d = heads * dk
    bb = DEC_BLOCK if nseq % DEC_BLOCK == 0 else 1
    rowblk = pl.BlockSpec((bb, 1, d), lambda i: (i, 0, 0))
    return pl.pallas_call(
        _dec_mem_kernel,
        grid=(nseq // bb,),
        in_specs=[pl.BlockSpec((d, nseq), lambda i: (0, 0)), pl.BlockSpec((d, nseq), lambda i: (0, 0)),
                  rowblk, rowblk,
                  pl.BlockSpec((bb, heads, dk, dk), lambda i: (i, 0, 0, 0))],
        out_specs=[pl.BlockSpec((bb, heads, dk, dk), lambda i: (i, 0, 0, 0)), rowblk],
        out_shape=[jax.ShapeDtypeStruct(c0.shape, F32), jax.ShapeDtypeStruct((nseq, 1, d), F32)],
        compiler_params=_cparams(("arbitrary",)),
        name="decode_mem",
    )(qt, kt, decay.reshape(nseq, 1, d), wv.reshape(nseq, 1, d), c0)


def _dec_post_kernel(x_ref, sh_ref, sc_ref, ga_ref, g_ref, wint_hbm, wbra_hbm, wbrb_hbm, wout_hbm,
                     ha_ref, qc_ref, pw_ref, sv_ref, den_ref, em_ref, o_ref,
                     wm_ref, wbra_ref, wbrb_ref, wout_ref, stage, sem):
    d = x_ref.shape[1]
    _stage_rows(wint_hbm, 4 * d, d, wm_ref, 0, stage, sem)
    _stage_rows(wint_hbm, 5 * d + 2 * M_HEADS, 2 * d, wm_ref, d, stage, sem)
    _stage_rows(wbra_hbm, 0, d, wbra_ref, 0, stage, sem)
    _stage_rows(wbrb_hbm, 0, d, wbrb_ref, 0, stage, sem)
    _stage_rows(wout_hbm, 0, d, wout_ref, 0, stage, sem)
    x = x_ref[...]
    xn = _rms_mod(x, g_ref[...], sh_ref[...], sc_ref[...]).astype(BF16)
    num = pw_ref[...] * qc_ref[...] + sv_ref[...]
    hm = num / jnp.maximum(jnp.abs(den_ref[...]), em_ref[...])
    hb = (jax.nn.sigmoid(_dot_nt(xn, wm_ref[0:d, :])) * hm).astype(BF16)
    g_a = _dot_nt(xn, wm_ref[d:2 * d, :])
    g_b = _dot_nt(xn, wm_ref[2 * d:3 * d, :])
    merged = (jax.nn.sigmoid(g_a) * _dot(ha_ref[...].astype(BF16), wbra_ref[...])
              + jax.nn.sigmoid(g_b) * _dot(hb, wbrb_ref[...]))
    o_ref[...] = x + ga_ref[...] * _dot(merged.astype(BF16), wout_ref[...])


def _dec_post(x2, mod, gain, w_int, wbra, wbrb, wout, ha, qc, pw, sv, den, em):
    rows, d = x2.shape
    full = lambda shape: pl.BlockSpec(shape, lambda i, _n=len(shape): (0,) * _n)
    mspec = lambda j: pl.BlockSpec((rows, d), lambda i: (0, j))
    args = [x2, mod, mod, mod, gain.reshape(1, d), w_int, wbra, wbrb, wout, ha, qc, pw, sv, den, em]
    in_specs = ([full(x2.shape), mspec(3), mspec(4), mspec(5), full((1, d))] + [_HBM] * 4
                + [full(a.shape) for a in args[9:]])
    return pl.pallas_call(
        _dec_post_kernel,
        grid=(1,),
        in_specs=in_specs,
        out_specs=full((rows, d)),
        out_shape=jax.ShapeDtypeStruct((rows, d), F32),
        scratch_shapes=[pltpu.VMEM((3 * d, d), BF16),
                        pltpu.VMEM((d, d), BF16), pltpu.VMEM((d, d), BF16), pltpu.VMEM((d, d), BF16)]
        + _stage_scratch(d),
        compiler_params=_cparams(("arbitrary",)),
        name="decode_post",
    )(*args)


def _pair_blocks(w):
    nblk, c, _ = w.shape
    z = jnp.zeros((nblk // 2, c, c), w.dtype)
    top = jnp.concatenate([w[0::2], z], axis=2)
    bot = jnp.concatenate([z, w[1::2]], axis=2)
    return jnp.concatenate([top, bot], axis=1).astype(BF16)


def kernel(x_prompt, x_sample, state_conv, state_rg_h, state_C, state_n, state_m, c_prompt, c_sample, w_ada, b_ada, g_norm1, w_ff1_in, w_ff1_out, g_norm2, w_in, conv_w, conv_b, w_rg_a, b_rg_a, w_rg_i, b_rg_i, rg_lambda, b_ig, b_fg, w_br_a, w_br_b, w_out, g_norm3, w_ff2_in, w_ff2_out, g_final):
    nb, seq, d = x_prompt.shape
    ns = x_sample.shape[0]
    depth = w_ada.shape[0]
    assert depth == 1 and x_sample.shape[1] == 1 and nb == SUBLANES
    assert seq % ML_ROWS == 0 and seq % FFN_ROWS == 0 and ML_ROWS % (SUBLANES * SUBLANES) == 0
    heads, dk = M_HEADS, d // M_HEADS
    assert w_in.shape[2] == 7 * d + 2 * heads and GATE_ROWS >= 2 * heads

    w_int = w_in[0].T
    b_gates = jnp.concatenate([b_ig[0], b_fg[0]])
    bif = jnp.pad(b_gates, (0, LANES - 2 * heads)).reshape(1, LANES)
    bift = jnp.pad(b_gates, (0, GATE_ROWS - 2 * heads)).reshape(GATE_ROWS, 1)
    wa = _pair_blocks(w_rg_a[0])
    wi = _pair_blocks(w_rg_i[0])
    wf1i, wf1o, wf2i, wf2o = w_ff1_in[0], w_ff1_out[0], w_ff2_in[0], w_ff2_out[0]
    wbra, wbrb, wout = w_br_a[0], w_br_b[0], w_out[0]

    mod = _ada(jnp.concatenate([c_prompt, c_sample], axis=0), w_ada[0], b_ada[0])
    mod_p3 = mod[:nb].reshape(nb, 1, N_MOD * d)
    mod_s = mod[nb:]

    xp = x_prompt.reshape(nb * seq, d)
    xp = _ffn(xp, mod_p3, 0, g_norm1[0], wf1i, wf1o, None, rows_per_seq=seq, per_row_mod=False)
    xp, c_p, n_p, m_p, conv_p, h_p = _mixer_prompt(
        xp, mod_p3, g_norm2[0], w_int, bif, bift, wbra, wbrb, wout, conv_w[0], conv_b[0], wa, wi,
        b_rg_a[0], b_rg_i[0], rg_lambda[0], nb=nb, seq=seq)
    yp = _ffn(xp, mod_p3, 6, g_norm3[0], wf2i, wf2o, g_final, rows_per_seq=seq, per_row_mod=False)

    xs = x_sample.reshape(ns, d)
    xs = _ffn(xs, mod_s, 0, g_norm1[0], wf1i, wf1o, None, rows_per_seq=1, per_row_mod=True)
    conv0 = jnp.swapaxes(state_conv[0], 0, 1)
    (ha_s, conv_s, h_s, n_s, m_s, qt, kt, decay, wv, pw, sv, den, em) = _dec_pre(
        xs, mod_s, g_norm2[0], w_int, bif, conv_w[0], conv_b[0], wa, wi,
        b_rg_a[0], b_rg_i[0], rg_lambda[0], conv0, state_rg_h[0], state_n[0].reshape(ns, d), state_m[0])
    c_s, qc = _dec_mem(qt, kt, decay, wv, state_C[0])
    xs = _dec_post(xs, mod_s, g_norm2[0], w_int, wbra, wbrb, wout, ha_s, qc.reshape(ns, d), pw, sv, den, em)
    ys = _ffn(xs, mod_s, 6, g_norm3[0], wf2i, wf2o, g_final, rows_per_seq=1, per_row_mod=True)

    return (yp.reshape(nb, seq, d), ys.reshape(ns, 1, d),
            conv_p[None], h_p.reshape(1, nb, d), c_p[None], n_p[None], m_p.reshape(1, nb, heads),
            jnp.swapaxes(conv_s, 0, 1)[None], h_s[None], c_s[None], n_s.reshape(1, ns, heads, dk), m_s[:, ::dk][None])
```

```python
import functools

import jax
import jax.numpy as jnp
from jax import lax
from jax.experimental import pallas as pl
from jax.experimental.pallas import tpu as pltpu

F32 = jnp.float32
BF16 = jnp.bfloat16

EPS = 1e-6
RG_C = 8.0
CONV_W = 4
N_MOD = 9
M_HEADS = 4
RG_BLOCKS = 8

SUBLANES = 8
LANES = 128
MXU_DIM = 256
VMEM_LIMIT_BYTES = 60 * 1024 * 1024

FFN_ROWS = 1024
FFN_COLS = 256
ML_ROWS = 512
ML_CHUNK = 256
DEC_BLOCK = 4
STAGE_ROWS = 256
STAGE_SLOTS = 4
STAGE_ROWS_WIDE = 64
STAGE_SLOTS_WIDE = 4
GATE_ROWS = 16


def _cparams(sem):
    return pltpu.CompilerParams(dimension_semantics=sem, vmem_limit_bytes=VMEM_LIMIT_BYTES)


def _resident(shape):
    nd = len(shape)
    return pl.BlockSpec(shape, lambda *_: (0,) * nd, pipeline_mode=pl.Buffered(1))


_HBM = pl.BlockSpec(memory_space=pl.ANY)


def _rms_mod(x, gain, shift, scale):
    ms = jnp.mean(x * x, axis=-1, keepdims=True)
    return x * lax.rsqrt(ms + EPS) * gain * (1.0 + scale) + shift


def _dot(a, b):
    return jnp.dot(a, b, preferred_element_type=F32)


def _dot_nt(a, b):
    return lax.dot_general(a, b, (((1,), (1,)), ((), ())), preferred_element_type=F32)


def _dot_tn(a, b):
    return lax.dot_general(a, b, (((0,), (0,)), ((), ())), preferred_element_type=F32)


def _split3(x):
    hi = x.astype(BF16)
    r1 = x - hi.astype(F32)
    mid = r1.astype(BF16)
    lo = (r1 - mid.astype(F32)).astype(BF16)
    return hi, mid, lo


def _stage_rows(src, src_row0, nrows, dst, dst_row0, stage, sem):
    slots, rows_per_copy, _ = stage.shape
    n = pl.cdiv(nrows, rows_per_copy)

    def rows_of(c):
        return min(rows_per_copy, nrows - c * rows_per_copy)

    def copy(c):
        return pltpu.make_async_copy(src.at[pl.ds(src_row0 + c * rows_per_copy, rows_of(c))],
                                     stage.at[c % slots, pl.ds(0, rows_of(c))], sem.at[c % slots])

    for c in range(min(slots, n)):
        copy(c).start(priority=c % 2)
    for c in range(n):
        copy(c).wait()
        r0 = dst_row0 + c * rows_per_copy
        dst[r0:r0 + rows_of(c), :] = stage[c % slots, 0:rows_of(c), :].astype(BF16)
        if c + slots < n:
            copy(c + slots).start(priority=c % 2)


def _stage_scratch(cols, wide=False):
    slots, rows = (STAGE_SLOTS_WIDE, STAGE_ROWS_WIDE) if wide else (STAGE_SLOTS, STAGE_ROWS)
    return [pltpu.VMEM((slots, rows, cols), F32), pltpu.SemaphoreType.DMA((slots,))]


def _ada_kernel(c_ref, w_ref, b_ref, o_ref):
    c = c_ref[...]
    sc = (c * jax.nn.sigmoid(c)).astype(BF16)
    o_ref[...] = _dot(sc, w_ref[...].astype(BF16)) + b_ref[...]


def _ada(c, w_ada, b_ada):
    rows, d = c.shape
    n = w_ada.shape[1]
    tn = d
    return pl.pallas_call(
        _ada_kernel,
        grid=(n // tn,),
        in_specs=[pl.BlockSpec((rows, d), lambda j: (0, 0)),
                  pl.BlockSpec((d, tn), lambda j: (0, j)),
                  pl.BlockSpec((1, tn), lambda j: (0, j))],
        out_specs=pl.BlockSpec((rows, tn), lambda j: (0, j)),
        out_shape=jax.ShapeDtypeStruct((rows, n), F32),
        compiler_params=_cparams(("arbitrary",)),
        name="adaln_mod",
    )(c, w_ada, b_ada.reshape(1, n))


def _ffn_rows(x_ref, sh_ref, sc_ref, ga_ref, g_ref, wi_ref, wo_ref, gf_ref, o_ref, act_ref, d_ff):
    rows = x_ref.shape[0]
    x = x_ref[...]
    xn = _rms_mod(x, g_ref[...], sh_ref[...], sc_ref[...]).astype(BF16)
    for j in range(d_ff // FFN_COLS):
        lo = j * FFN_COLS
        hg = _dot(xn, wi_ref[:, lo:lo + FFN_COLS])
        hu = _dot(xn, wi_ref[:, d_ff + lo:d_ff + lo + FFN_COLS])
        act_ref[0:rows, lo:lo + FFN_COLS] = (hg * jax.nn.sigmoid(hg) * hu).astype(BF16)
    y = _dot(act_ref[0:rows, :], wo_ref[...])
    out = x + (0.5 * ga_ref[...]) * y
    if gf_ref is not None:
        ms = jnp.mean(out * out, axis=-1, keepdims=True)
        out = out * lax.rsqrt(ms + EPS) * gf_ref[...]
    o_ref[...] = out


def _ffn_kernel(*refs, d_ff, final, n_prompt_steps):
    n_in = 13 if final else 12
    xp_ref, shp_ref, scp_ref, gap_ref, xs_ref, shs_ref, scs_ref, gas_ref, g_ref, wi_hbm, wo_hbm = refs[:11]
    gf_ref = refs[11] if final else None
    op_ref, os_ref = refs[n_in - 1:n_in + 1]
    act_ref, wi_ref, wo_ref, stage_i, sem_i, stage_o, sem_o = refs[-7:]
    step = pl.program_id(0)

    @pl.when(step == 0)
    def _():
        _stage_rows(wi_hbm, 0, wi_ref.shape[0], wi_ref, 0, stage_i, sem_i)
        _stage_rows(wo_hbm, 0, wo_ref.shape[0], wo_ref, 0, stage_o, sem_o)

    @pl.when(step < n_prompt_steps)
    def _():
        _ffn_rows(xp_ref, shp_ref, scp_ref, gap_ref, g_ref, wi_ref, wo_ref, gf_ref, op_ref, act_ref, d_ff)

    @pl.when(step == n_prompt_steps)
    def _():
        _ffn_rows(xs_ref, shs_ref, scs_ref, gas_ref, g_ref, wi_ref, wo_ref, gf_ref, os_ref, act_ref, d_ff)


def _ffn(xp, mod_p, xs, mod_s, j0, gain, wi, wo, g_final, *, rows_per_seq):
    rows, d = xp.shape
    ns = xs.shape[0]
    d_ff = wo.shape[0]
    tm = min(FFN_ROWS, rows_per_seq)
    tiles_per_seq = rows_per_seq // tm
    n_steps = rows // tm
    tile_of = lambda i: jnp.minimum(i, n_steps - 1)
    mspec_p = lambda j: pl.BlockSpec((None, 1, d), lambda i: (tile_of(i) // tiles_per_seq, 0, j))
    mspec_s = lambda j: pl.BlockSpec((ns, d), lambda i: (0, j))
    final = g_final is not None
    in_specs = [pl.BlockSpec((tm, d), lambda i: (tile_of(i), 0)), mspec_p(j0), mspec_p(j0 + 1), mspec_p(j0 + 2),
                pl.BlockSpec((ns, d), lambda i: (0, 0)), mspec_s(j0), mspec_s(j0 + 1), mspec_s(j0 + 2),
                _resident((1, d)), _HBM, _HBM]
    args = [xp, mod_p, mod_p, mod_p, xs, mod_s, mod_s, mod_s, gain.reshape(1, d), wi, wo]
    if final:
        in_specs.append(_resident((1, d)))
        args.append(g_final.reshape(1, d))
    return pl.pallas_call(
        functools.partial(_ffn_kernel, d_ff=d_ff, final=final, n_prompt_steps=n_steps),
        grid=(n_steps + 1,),
        in_specs=in_specs,
        out_specs=[pl.BlockSpec((tm, d), lambda i: (tile_of(i), 0)), pl.BlockSpec((ns, d), lambda i: (0, 0))],
        out_shape=[jax.ShapeDtypeStruct((rows, d), F32), jax.ShapeDtypeStruct((ns, d), F32)],
        scratch_shapes=[pltpu.VMEM((tm, d_ff), BF16), pltpu.VMEM(wi.shape, BF16), pltpu.VMEM(wo.shape, BF16)]
        + _stage_scratch(wi.shape[1], wide=True) + _stage_scratch(wo.shape[1]),
        compiler_params=_cparams(("arbitrary",)),
        name="ffn_final" if final else "ffn",
    )(*args)


def _rg_gates_block(u, wa, wi, ba, bi, lam):
    ub = u.astype(BF16)
    r = jax.nn.sigmoid(_dot(ub, wa) + ba)
    i_g = jax.nn.sigmoid(_dot(ub, wi) + bi)
    log_a = -RG_C * r * jax.nn.softplus(-lam)
    a = jnp.exp(log_a)
    b = jnp.sqrt(-jnp.tanh(log_a) * (a * a + 1.0)) * (i_g * u)
    return a, b


def _rg_gates(u, wa_ref, wi_ref, ba, bi, lam):
    parts = []
    for p in range(wa_ref.shape[0]):
        cols = slice(p * MXU_DIM, (p + 1) * MXU_DIM)
        parts.append(_rg_gates_block(u[:, cols], wa_ref[p], wi_ref[p], ba[:, cols], bi[:, cols], lam[:, cols]))
    return (jnp.concatenate([a for a, _ in parts], axis=-1), jnp.concatenate([b for _, b in parts], axis=-1))


def _rglru_front(xn_ref, w_rows_ref, lb_ref, tb_ref, tail_ref, conv_ref):
    tl, d = xn_ref.shape
    seg = tl // SUBLANES
    nlb = d // LANES
    pre = (CONV_W - 1) * SUBLANES

    x_rg = _dot_nt(xn_ref[...], w_rows_ref[0:d, :])
    for cb in range(nlb):
        lb_ref[cb] = x_rg[:, cb * LANES:(cb + 1) * LANES]
    for j in range(seg):
        for cb in range(nlb):
            tb_ref[pre + j * SUBLANES:pre + (j + 1) * SUBLANES, cb * LANES:(cb + 1) * LANES] = (
                lb_ref[cb, pl.ds(j, SUBLANES, stride=seg), :])

    first = lax.broadcasted_iota(jnp.int32, (SUBLANES, d), 0) == 0
    for i in range(CONV_W - 1):
        j = seg - (CONV_W - 1) + i
        slab = tb_ref[pre + j * SUBLANES:pre + (j + 1) * SUBLANES, :]
        tb_ref[i * SUBLANES:(i + 1) * SUBLANES, :] = jnp.where(first, tail_ref[i:i + 1, :],
                                                              pltpu.roll(slab, 1, axis=0))
        last = slab[SUBLANES - 1:SUBLANES, :]
        tail_ref[i:i + 1, :] = last
        conv_ref[i:i + 1, :] = last


def _rglru_block(p, cw_ref, cb_ref, wa_ref, wi_ref, ba_ref, bi_ref, lam_ref,
                 lb_ref, tb_ref, a_ref, b_ref, h_ref, ha_ref):
    tl = a_ref.shape[0]
    seg = tl // SUBLANES
    cols = slice(p * MXU_DIM, (p + 1) * MXU_DIM)
    lane_blocks = range(p * MXU_DIM // LANES, (p + 1) * MXU_DIM // LANES)

    u = cb_ref[:, cols]
    for j in range(CONV_W):
        u = u + tb_ref[j * SUBLANES:j * SUBLANES + tl, cols] * cw_ref[j:j + 1, cols]
    a, b = _rg_gates_block(u, wa_ref[p], wi_ref[p], ba_ref[:, cols], bi_ref[:, cols], lam_ref[:, cols])
    a_ref[:, cols] = a
    b_ref[:, cols] = b

    acc_a = a_ref[0:SUBLANES, cols]
    acc_b = b_ref[0:SUBLANES, cols]
    for j in range(1, seg):
        rows = slice(j * SUBLANES, (j + 1) * SUBLANES)
        a_j = a_ref[rows, cols]
        acc_b = a_j * acc_b + b_ref[rows, cols]
        acc_a = a_j * acc_a
        a_ref[rows, cols] = acc_a
        b_ref[rows, cols] = acc_b

    h = h_ref[:, cols]
    h_in = []
    for s in range(SUBLANES):
        h_in.append(h)
        h = acc_a[s:s + 1, :] * h + acc_b[s:s + 1, :]
    h_ref[:, cols] = h
    h_in = jnp.concatenate(h_in, axis=0)

    for j in range(seg):
        rows = slice(j * SUBLANES, (j + 1) * SUBLANES)
        h_j = a_ref[rows, cols] * h_in + b_ref[rows, cols]
        for i, cb in enumerate(lane_blocks):
            lb_ref[cb, rows, :] = h_j[:, i * LANES:(i + 1) * LANES]
    for s in range(SUBLANES):
        for cb in lane_blocks:
            ha_ref[s * seg:(s + 1) * seg, cb * LANES:(cb + 1) * LANES] = (
                lb_ref[cb, pl.ds(s, seg, stride=SUBLANES), :].astype(BF16))


def _mixer_kernel(x_ref, sh_ref, sc_ref, ga_ref, g_ref, wint_hbm, bif_ref, bift_ref,
                  wbra_hbm, wbrb_hbm, wout_hbm, cw_ref, cb_ref, wa_ref, wi_ref, ba_ref, bi_ref, lam_ref,
                  o_ref, c_ref, n_ref, m_ref, conv_ref, hl_ref,
                  hm_ref, wm_ref, wif_ref, wbra_ref, wbrb_ref, wout_ref, stage, sem,
                  lb_ref, tb_ref, a_ref, b_ref, tail_ref, h_ref, ha_ref, mg_ref, xn_ref, hb_ref):
    tl, d = x_ref.shape
    dk = d // M_HEADS
    cs = min(ML_CHUNK, tl)
    scale = dk ** -0.5
    gate_row0 = 5 * d

    @pl.when((pl.program_id(0) == 0) & (pl.program_id(1) == 0))
    def _():
        _stage_rows(wint_hbm, 0, 5 * d, wm_ref, 0, stage, sem)
        _stage_rows(wint_hbm, gate_row0 + 2 * M_HEADS, 2 * d, wm_ref, 5 * d, stage, sem)
        _stage_rows(wint_hbm, gate_row0, LANES, wif_ref, 0, stage, sem)
        _stage_rows(wbra_hbm, 0, d, wbra_ref, 0, stage, sem)
        _stage_rows(wbrb_hbm, 0, d, wbrb_ref, 0, stage, sem)
        _stage_rows(wout_hbm, 0, d, wout_ref, 0, stage, sem)

    @pl.when(pl.program_id(1) == 0)
    def _():
        c_ref[...] = jnp.zeros_like(c_ref)
        n_ref[...] = jnp.zeros_like(n_ref)
        m_ref[...] = jnp.zeros_like(m_ref)
        tail_ref[...] = jnp.zeros_like(tail_ref)
        h_ref[...] = jnp.zeros_like(h_ref)

    x = x_ref[...]
    xn_ref[...] = _rms_mod(x, g_ref[...], sh_ref[...], sc_ref[...]).astype(BF16)

    _rglru_front(xn_ref, wm_ref, lb_ref, tb_ref, tail_ref, conv_ref)
    rglru_block = functools.partial(
        _rglru_block, cw_ref=cw_ref, cb_ref=cb_ref, wa_ref=wa_ref, wi_ref=wi_ref, ba_ref=ba_ref, bi_ref=bi_ref,
        lam_ref=lam_ref, lb_ref=lb_ref, tb_ref=tb_ref, a_ref=a_ref, b_ref=b_ref, h_ref=h_ref, ha_ref=ha_ref)
    n_rg = wa_ref.shape[0]
    wq0 = d

    pre_c = _dot_nt(xn_ref[...], wif_ref[...]) + bif_ref[...]
    pre_r = _dot_nt(wif_ref[0:GATE_ROWS, :], xn_ref[...]) + bift_ref[...]
    col_is_f = lax.broadcasted_iota(jnp.int32, pre_c.shape, 1) >= M_HEADS
    row_is_f = lax.broadcasted_iota(jnp.int32, pre_r.shape, 0) >= M_HEADS
    gate_c = jnp.where(col_is_f, jax.nn.log_sigmoid(pre_c), pre_c)
    gate_r = jnp.where(row_is_f, jax.nn.log_sigmoid(pre_r), pre_r)

    ti = lax.broadcasted_iota(jnp.int32, (cs, cs), 0)
    si = lax.broadcasted_iota(jnp.int32, (cs, cs), 1)
    causal = si <= ti
    lower = causal.astype(BF16)
    upper = (ti <= si).astype(BF16)

    chunks = []
    for c in range(tl // cs):
        gc = gate_c[c * cs:(c + 1) * cs, :]
        gr = gate_r[:, c * cs:(c + 1) * cs]
        cum_c = sum(_dot(lower, part) for part in _split3(gc))
        cum_r = sum(_dot(part, upper) for part in _split3(gr))
        chunks.append((gc, gr, cum_c, cum_r))

    for h in range(M_HEADS):
        c0 = h * dk
        q_all = _dot_nt(xn_ref[...], wm_ref[wq0 + c0:wq0 + c0 + dk, :])
        k_all = _dot_nt(xn_ref[...], wm_ref[wq0 + d + c0:wq0 + d + c0 + dk, :])
        v_all = _dot_nt(xn_ref[...], wm_ref[wq0 + 2 * d + c0:wq0 + 2 * d + c0 + dk, :])
        for c, (gc, gr, cum_c, cum_r) in enumerate(chunks):
            r0 = c * cs
            q = q_all[r0:r0 + cs, :]
            k = k_all[r0:r0 + cs, :]
            v = v_all[r0:r0 + cs, :]
            bc = cum_c[:, M_HEADS + h:M_HEADS + h + 1]
            br = cum_r[M_HEADS + h:M_HEADS + h + 1, :]
            ig_c = gc[:, h:h + 1]
            ig_r = gr[h:h + 1, :]
            m_prev = m_ref[h:h + 1, :]
            c_prev = c_ref[h]
            n_prev = n_ref[h:h + 1, :]

            log_d = jnp.where(causal, bc - br + ig_r, -jnp.inf)
            log_past = bc + m_prev
            m_t = jnp.maximum(log_past, jnp.max(log_d, axis=-1, keepdims=True))
            d_mat = jnp.exp(log_d - m_t)
            past_w = jnp.exp(log_past - m_t)
            qs = q * scale
            qsb = qs.astype(BF16)
            kb = k.astype(BF16)
            vb = v.astype(BF16)
            s = _dot_nt(qsb, kb) * d_mat
            num = past_w * _dot(qsb, c_prev.astype(BF16)) + _dot(s.astype(BF16), vb)
            den = past_w * jnp.sum(qs * n_prev, axis=-1, keepdims=True) + jnp.sum(s, axis=-1, keepdims=True)
            hm_ref[r0:r0 + cs, c0:c0 + dk] = num / jnp.maximum(jnp.abs(den), jnp.exp(-m_t))

            m_new = m_t[cs - 1:cs, :]
            b_last = bc[cs - 1:cs, :]
            w_s = jnp.exp(b_last - bc + ig_c - m_new)
            decay = jnp.exp(b_last + m_prev - m_new)
            kw = w_s * k
            c_ref[h] = decay * c_prev + _dot_tn(kw.astype(BF16), vb)
            n_ref[h:h + 1, :] = decay * n_prev + jnp.sum(kw, axis=0, keepdims=True)
            m_ref[h:h + 1, :] = m_new
        if h < n_rg:
            rglru_block(h)
    for p in range(M_HEADS, n_rg):
        rglru_block(p)
    hl_ref[...] = h_ref[...]

    for n0 in range(0, d, MXU_DIM):
        cols = slice(n0, n0 + MXU_DIM)
        o_pre = _dot_nt(xn_ref[...], wm_ref[4 * d + n0:4 * d + n0 + MXU_DIM, :])
        hb_ref[:, cols] = (jax.nn.sigmoid(o_pre) * hm_ref[:, cols]).astype(BF16)
    for n0 in range(0, d, MXU_DIM):
        cols = slice(n0, n0 + MXU_DIM)
        g_a = _dot_nt(xn_ref[...], wm_ref[5 * d + n0:5 * d + n0 + MXU_DIM, :])
        g_b = _dot_nt(xn_ref[...], wm_ref[6 * d + n0:6 * d + n0 + MXU_DIM, :])
        merged = (jax.nn.sigmoid(g_a) * _dot(ha_ref[...], wbra_ref[:, cols])
                  + jax.nn.sigmoid(g_b) * _dot(hb_ref[...], wbrb_ref[:, cols]))
        mg_ref[:, cols] = merged.astype(BF16)
    o_ref[...] = x + ga_ref[...] * _dot(mg_ref[...], wout_ref[...])


def _mixer_prompt(x2, mod3, gain, w_int, bif, bift, wbra, wbrb, wout, conv_w, conv_b, wa, wi, ba, bi, lam,
                  *, nb, seq):
    rows, d = x2.shape
    dk = d // M_HEADS
    tl = min(ML_ROWS, seq)
    nt = seq // tl
    row = lambda v: v.reshape(1, d)
    mspec = lambda j: pl.BlockSpec((None, 1, d), lambda b, t: (b, 0, j))
    tile = lambda: pl.BlockSpec((tl, d), lambda b, t: (b * nt + t, 0))
    per_seq = lambda *shape: pl.BlockSpec((None,) + shape, lambda b, t: (b,) + (0,) * len(shape))
    return pl.pallas_call(
        _mixer_kernel,
        grid=(nb, nt),
        in_specs=[tile(), mspec(3), mspec(4), mspec(5), _resident((1, d)),
                  _HBM, _resident(bif.shape), _resident(bift.shape), _HBM, _HBM, _HBM,
                  _resident(conv_w.shape), _resident((1, d)), _resident(wa.shape), _resident(wi.shape),
                  _resident((1, d)), _resident((1, d)), _resident((1, d))],
        out_specs=[tile(), per_seq(M_HEADS, dk, dk), per_seq(M_HEADS, dk), per_seq(M_HEADS, 1),
                   per_seq(CONV_W - 1, d), per_seq(1, d)],
        out_shape=[jax.ShapeDtypeStruct((rows, d), F32),
                   jax.ShapeDtypeStruct((nb, M_HEADS, dk, dk), F32),
                   jax.ShapeDtypeStruct((nb, M_HEADS, dk), F32),
                   jax.ShapeDtypeStruct((nb, M_HEADS, 1), F32),
                   jax.ShapeDtypeStruct((nb, CONV_W - 1, d), F32),
                   jax.ShapeDtypeStruct((nb, 1, d), F32)],
        scratch_shapes=[pltpu.VMEM((tl, d), F32),
                        pltpu.VMEM((7 * d, d), BF16),
                        pltpu.VMEM((LANES, d), BF16),
                        pltpu.VMEM((d, d), BF16), pltpu.VMEM((d, d), BF16), pltpu.VMEM((d, d), BF16)]
        + _stage_scratch(d)
        + [pltpu.VMEM((d // LANES, tl, LANES), F32),
           pltpu.VMEM(((CONV_W - 1) * SUBLANES + tl, d), F32),
           pltpu.VMEM((tl, d), F32),
           pltpu.VMEM((tl, d), F32),
           pltpu.VMEM((CONV_W - 1, d), F32),
           pltpu.VMEM((1, d), F32),
           pltpu.VMEM((tl, d), BF16),
           pltpu.VMEM((tl, d), BF16),
           pltpu.VMEM((tl, d), BF16),
           pltpu.VMEM((tl, d), BF16)],
        compiler_params=_cparams(("arbitrary", "arbitrary")),
        name="mixer_prompt",
    )(x2, mod3, mod3, mod3, row(gain), w_int, bif, bift, wbra, wbrb, wout,
      conv_w, row(conv_b), wa, wi, row(ba), row(bi), row(lam))


def _head_sum(x, dk):
    parts = []
    for h in range(x.shape[1] // dk):
        sl = x[:, h * dk:(h + 1) * dk]
        parts.append(jnp.broadcast_to(jnp.sum(sl, axis=-1, keepdims=True), sl.shape))
    return jnp.concatenate(parts, axis=-1)


def _head_spread(cols, lane0, dk, rows):
    return jnp.concatenate(
        [jnp.broadcast_to(cols[:, lane0 + h:lane0 + h + 1], (rows, dk)) for h in range(M_HEADS)], axis=-1)


def _dec_pre_kernel(x_ref, sh_ref, sc_ref, g_ref, wint_hbm, bif_ref,
                    cw_ref, cb_ref, wa_ref, wi_ref, ba_ref, bi_ref, lam_ref,
                    conv0_ref, h0_ref, n0_ref, m0_ref,
                    ha_ref, conv_ref, hs_ref, n_ref, m_ref, qt_ref, kt_ref, dec_ref, wv_ref, pw_ref, sv_ref,
                    den_ref, em_ref,
                    wm_ref, wif_ref, stage, sem):
    rows, d = x_ref.shape
    dk = d // M_HEADS
    scale = dk ** -0.5
    _stage_rows(wint_hbm, 0, 4 * d, wm_ref, 0, stage, sem)
    _stage_rows(wint_hbm, 5 * d, LANES, wif_ref, 0, stage, sem)
    xn = _rms_mod(x_ref[...], g_ref[...], sh_ref[...], sc_ref[...]).astype(BF16)

    x_rg = _dot_nt(xn, wm_ref[0:d, :])
    u = cb_ref[...]
    for j in range(CONV_W - 1):
        u = u + conv0_ref[j] * cw_ref[j:j + 1, :]
        if j > 0:
            conv_ref[j - 1] = conv0_ref[j]
    u = u + x_rg * cw_ref[CONV_W - 1:CONV_W, :]
    conv_ref[CONV_W - 2] = x_rg
    a, b = _rg_gates(u, wa_ref, wi_ref, ba_ref[...], bi_ref[...], lam_ref[...])
    h = a * h0_ref[...] + b
    hs_ref[...] = h
    ha_ref[...] = h

    q = _dot_nt(xn, wm_ref[d:2 * d, :])
    k = _dot_nt(xn, wm_ref[2 * d:3 * d, :])
    v = _dot_nt(xn, wm_ref[3 * d:4 * d, :])
    qt_ref[...] = _dot_nt(wm_ref[d:2 * d, :], xn).astype(BF16)
    kt_ref[...] = _dot_nt(wm_ref[2 * d:3 * d, :], xn).astype(BF16)
    pre = _dot_nt(xn, wif_ref[...]) + bif_ref[...]
    ig = _head_spread(pre, 0, dk, rows)
    lf = jax.nn.log_sigmoid(_head_spread(pre, M_HEADS, dk, rows))
    m0 = _head_spread(m0_ref[...], 0, dk, rows)
    n0 = n0_ref[...]
    log_past = lf + m0
    m_t = jnp.maximum(log_past, ig)
    d_w = jnp.exp(ig - m_t)
    past_w = jnp.exp(log_past - m_t)
    qs = q * scale
    s = _head_sum(qs * k, dk) * d_w
    den_ref[...] = past_w * _head_sum(qs * n0, dk) + s
    em_ref[...] = jnp.exp(-m_t)
    pw_ref[...] = past_w
    sv_ref[...] = s * v
    w_s = jnp.exp(ig - m_t)
    decay = jnp.exp(lf + m0 - m_t)
    dec_ref[...] = decay
    wv_ref[...] = w_s * v
    n_ref[...] = decay * n0 + w_s * k
    m_ref[...] = m_t


def _dec_pre(x2, mod, gain, w_int, bif, conv_w, conv_b, wa, wi, ba, bi, lam, conv0, h0, n0, m0):
    rows, d = x2.shape
    row = lambda v: v.reshape(1, d)
    m0p = jnp.pad(m0, ((0, 0), (0, LANES - M_HEADS)))
    full = lambda shape: pl.BlockSpec(shape, lambda i, _n=len(shape): (0,) * _n)
    mspec = lambda j: pl.BlockSpec((rows, d), lambda i: (0, j))
    vec = jax.ShapeDtypeStruct((rows, d), F32)
    outs = [vec,
            jax.ShapeDtypeStruct((CONV_W - 1, rows, d), F32),
            vec,
            vec,
            vec,
            jax.ShapeDtypeStruct((d, rows), BF16),
            jax.ShapeDtypeStruct((d, rows), BF16),
            vec, vec, vec, vec, vec, vec]
    args = [x2, mod, mod, row(gain), w_int, bif, conv_w, row(conv_b), wa, wi,
            row(ba), row(bi), row(lam), conv0, h0, n0, m0p]
    in_specs = ([full(x2.shape), mspec(3), mspec(4), full((1, d)), _HBM]
                + [full(a.shape) for a in args[5:]])
    return pl.pallas_call(
        _dec_pre_kernel,
        grid=(1,),
        in_specs=in_specs,
        out_specs=[full(o.shape) for o in outs],
        out_shape=outs,
        scratch_shapes=[pltpu.VMEM((4 * d, d), BF16), pltpu.VMEM((LANES, d), BF16)] + _stage_scratch(d),
        compiler_params=_cparams(("arbitrary",)),
        name="decode_pre",
    )(*args)


def _dec_mem_kernel(qt_ref, kt_ref, dec_ref, wv_ref, c0_ref, c_ref, qc_ref):
    bb = c0_ref.shape[0]
    dk = c0_ref.shape[2]
    nseq = qt_ref.shape[1]
    scale = dk ** -0.5
    base = pl.program_id(0) * bb
    seq_id = lax.broadcasted_iota(jnp.int32, (nseq, dk), 0)
    for j in range(bb):
        pick = (seq_id == base + j).astype(BF16)
        qcol = _dot(qt_ref[...], pick) * scale
        kcol = _dot(kt_ref[...], pick)
        for h in range(M_HEADS):
            c0 = c0_ref[j, h]
            dec = dec_ref[j, :, h * dk:(h + 1) * dk]
            wv = wv_ref[j, :, h * dk:(h + 1) * dk]
            c_ref[j, h] = dec * c0 + kcol[h * dk:(h + 1) * dk, :] * wv
            qc_ref[j, :, h * dk:(h + 1) * dk] = jnp.sum(qcol[h * dk:(h + 1) * dk, :] * c0, axis=0, keepdims=True)


def _dec_mem(qt, kt, decay, wv, c0):
    nseq, heads, dk, _ = c0.shape
    d = heads * dk
    bb = DEC_BLOCK if nseq % DEC_BLOCK == 0 else 1
    rowblk = pl.BlockSpec((bb, 1, d), lambda i: (i, 0, 0))
    return pl.pallas_call(
        _dec_mem_kernel,
        grid=(nseq // bb,),
        in_specs=[pl.BlockSpec((d, nseq), lambda i: (0, 0)), pl.BlockSpec((d, nseq), lambda i: (0, 0)),
                  rowblk, rowblk,
                  pl.BlockSpec((bb, heads, dk, dk), lambda i: (i, 0, 0, 0))],
        out_specs=[pl.BlockSpec((bb, heads, dk, dk), lambda i: (i, 0, 0, 0)), rowblk],
        out_shape=[jax.ShapeDtypeStruct(c0.shape, F32), jax.ShapeDtypeStruct((nseq, 1, d), F32)],
        compiler_params=_cparams(("arbitrary",)),
        name="decode_mem",
    )(qt, kt, decay.reshape(nseq, 1, d), wv.reshape(nseq, 1, d), c0)


def _dec_post_kernel(x_ref, sh_ref, sc_ref, ga_ref, g_ref, wint_hbm, wbra_hbm, wbrb_hbm, wout_hbm,
                     ha_ref, qc_ref, pw_ref, sv_ref, den_ref, em_ref, o_ref,
                     wm_ref, wbra_ref, wbrb_ref, wout_ref, stage, sem):
    d = x_ref.shape[1]
    _stage_rows(wint_hbm, 4 * d, d, wm_ref, 0, stage, sem)
    _stage_rows(wint_hbm, 5 * d + 2 * M_HEADS, 2 * d, wm_ref, d, stage, sem)
    _stage_rows(wbra_hbm, 0, d, wbra_ref, 0, stage, sem)
    _stage_rows(wbrb_hbm, 0, d, wbrb_ref, 0, stage, sem)
    _stage_rows(wout_hbm, 0, d, wout_ref, 0, stage, sem)
    x = x_ref[...]
    xn = _rms_mod(x, g_ref[...], sh_ref[...], sc_ref[...]).astype(BF16)
    num = pw_ref[...] * qc_ref[...] + sv_ref[...]
    hm = num / jnp.maximum(jnp.abs(den_ref[...]), em_ref[...])
    hb = (jax.nn.sigmoid(_dot_nt(xn, wm_ref[0:d, :])) * hm).astype(BF16)
    g_a = _dot_nt(xn, wm_ref[d:2 * d, :])
    g_b = _dot_nt(xn, wm_ref[2 * d:3 * d, :])
    merged = (jax.nn.sigmoid(g_a) * _dot(ha_ref[...].astype(BF16), wbra_ref[...])
              + jax.nn.sigmoid(g_b) * _dot(hb, wbrb_ref[...]))
    o_ref[...] = x + ga_ref[...] * _dot(merged.astype(BF16), wout_ref[...])


def _dec_post(x2, mod, gain, w_int, wbra, wbrb, wout, ha, qc, pw, sv, den, em):
    rows, d = x2.shape
    full = lambda shape: pl.BlockSpec(shape, lambda i, _n=len(shape): (0,) * _n)
    mspec = lambda j: pl.BlockSpec((rows, d), lambda i: (0, j))
    args = [x2, mod, mod, mod, gain.reshape(1, d), w_int, wbra, wbrb, wout, ha, qc, pw, sv, den, em]
    in_specs = ([full(x2.shape), mspec(3), mspec(4), mspec(5), full((1, d))] + [_HBM] * 4
                + [full(a.shape) for a in args[9:]])
    return pl.pallas_call(
        _dec_post_kernel,
        grid=(1,),
        in_specs=in_specs,
        out_specs=full((rows, d)),
        out_shape=jax.ShapeDtypeStruct((rows, d), F32),
        scratch_shapes=[pltpu.VMEM((3 * d, d), BF16),
                        pltpu.VMEM((d, d), BF16), pltpu.VMEM((d, d), BF16), pltpu.VMEM((d, d), BF16)]
        + _stage_scratch(d),
        compiler_params=_cparams(("arbitrary",)),
        name="decode_post",
    )(*args)


def _pair_blocks(w):
    nblk, c, _ = w.shape
    z = jnp.zeros((nblk // 2, c, c), w.dtype)
    top = jnp.concatenate([w[0::2], z], axis=2)
    bot = jnp.concatenate([z, w[1::2]], axis=2)
    return jnp.concatenate([top, bot], axis=1).astype(BF16)


def kernel(x_prompt, x_sample, state_conv, state_rg_h, state_C, state_n, state_m, c_prompt, c_sample, w_ada, b_ada, g_norm1, w_ff1_in, w_ff1_out, g_norm2, w_in, conv_w, conv_b, w_rg_a, b_rg_a, w_rg_i, b_rg_i, rg_lambda, b_ig, b_fg, w_br_a, w_br_b, w_out, g_norm3, w_ff2_in, w_ff2_out, g_final):
    nb, seq, d = x_prompt.shape
    ns = x_sample.shape[0]
    depth = w_ada.shape[0]
    assert depth == 1 and x_sample.shape[1] == 1 and nb == SUBLANES
    assert seq % ML_ROWS == 0 and seq % FFN_ROWS == 0 and ML_ROWS % (SUBLANES * SUBLANES) == 0
    heads, dk = M_HEADS, d // M_HEADS
    assert w_in.shape[2] == 7 * d + 2 * heads and GATE_ROWS >= 2 * heads

    w_int = w_in[0].T
    b_gates = jnp.concatenate([b_ig[0], b_fg[0]])
    bif = jnp.pad(b_gates, (0, LANES - 2 * heads)).reshape(1, LANES)
    bift = jnp.pad(b_gates, (0, GATE_ROWS - 2 * heads)).reshape(GATE_ROWS, 1)
    wa = _pair_blocks(w_rg_a[0])
    wi = _pair_blocks(w_rg_i[0])
    wf1i, wf1o, wf2i, wf2o = w_ff1_in[0], w_ff1_out[0], w_ff2_in[0], w_ff2_out[0]
    wbra, wbrb, wout = w_br_a[0], w_br_b[0], w_out[0]

    mod = _ada(jnp.concatenate([c_prompt, c_sample], axis=0), w_ada[0], b_ada[0])
    mod_p3 = mod[:nb].reshape(nb, 1, N_MOD * d)
    mod_s = mod[nb:]

    xp, xs = _ffn(x_prompt.reshape(nb * seq, d), mod_p3, x_sample.reshape(ns, d), mod_s, 0, g_norm1[0],
                  wf1i, wf1o, None, rows_per_seq=seq)

    xp, c_p, n_p, m_p, conv_p, h_p = _mixer_prompt(
        xp, mod_p3, g_norm2[0], w_int, bif, bift, wbra, wbrb, wout, conv_w[0], conv_b[0], wa, wi,
        b_rg_a[0], b_rg_i[0], rg_lambda[0], nb=nb, seq=seq)

    conv0 = jnp.swapaxes(state_conv[0], 0, 1)
    (ha_s, conv_s, h_s, n_s, m_s, qt, kt, decay, wv, pw, sv, den, em) = _dec_pre(
        xs, mod_s, g_norm2[0], w_int, bif, conv_w[0], conv_b[0], wa, wi,
        b_rg_a[0], b_rg_i[0], rg_lambda[0], conv0, state_rg_h[0], state_n[0].reshape(ns, d), state_m[0])
    c_s, qc = _dec_mem(qt, kt, decay, wv, state_C[0])
    xs = _dec_post(xs, mod_s, g_norm2[0], w_int, wbra, wbrb, wout, ha_s, qc.reshape(ns, d), pw, sv, den, em)

    yp, ys = _ffn(xp, mod_p3, xs, mod_s, 6, g_norm3[0], wf2i, wf2o, g_final, rows_per_seq=seq)

    return (yp.reshape(nb, seq, d), ys.reshape(ns, 1, d),
            conv_p[None], h_p.reshape(1, nb, d), c_p[None], n_p[None], m_p.reshape(1, nb, heads),
            jnp.swapaxes(conv_s, 0, 1)[None], h_s[None], c_s[None], n_s.reshape(1, ns, heads, dk), m_s[:, ::dk][None])
```

```python
import functools
import itertools

import jax
import jax.numpy as jnp
from jax import lax
from jax.experimental import pallas as pl
from jax.experimental.pallas import tpu as pltpu

F32 = jnp.float32
BF16 = jnp.bfloat16

EPS = 1e-6
RG_C = 8.0
CONV_W = 4
N_MOD = 9
M_HEADS = 4
RG_BLOCKS = 8

SUBLANES = 8
LANES = 128
MXU_DIM = 256
VMEM_LIMIT_BYTES = 60 * 1024 * 1024

FFN_ROWS = 1024
FFN_COLS = 256
ML_ROWS = 512
ML_SEQS = 1
ML_CHUNK = 256
RG_PIECES = 4
DEC_BLOCK = 4
STAGE_ROWS = 256
STAGE_SLOTS = 4
STAGE_ROWS_WIDE = 64
STAGE_SLOTS_WIDE = 4
GATE_ROWS = 16


def _cparams(sem):
    return pltpu.CompilerParams(dimension_semantics=sem, vmem_limit_bytes=VMEM_LIMIT_BYTES)


def _resident(shape):
    nd = len(shape)
    return pl.BlockSpec(shape, lambda *_: (0,) * nd, pipeline_mode=pl.Buffered(1))


_HBM = pl.BlockSpec(memory_space=pl.ANY)


def _rms_mod(x, gain, shift, scale):
    ms = jnp.mean(x * x, axis=-1, keepdims=True)
    return x * lax.rsqrt(ms + EPS) * gain * (1.0 + scale) + shift


def _dot(a, b):
    return jnp.dot(a, b, preferred_element_type=F32)


def _dot_nt(a, b):
    return lax.dot_general(a, b, (((1,), (1,)), ((), ())), preferred_element_type=F32)


def _dot_tn(a, b):
    return lax.dot_general(a, b, (((0,), (0,)), ((), ())), preferred_element_type=F32)


def _split3(x):
    hi = x.astype(BF16)
    r1 = x - hi.astype(F32)
    mid = r1.astype(BF16)
    lo = (r1 - mid.astype(F32)).astype(BF16)
    return hi, mid, lo


def _stage_rows(src, src_row0, nrows, dst, dst_row0, stage, sem):
    slots, rows_per_copy, _ = stage.shape
    n = pl.cdiv(nrows, rows_per_copy)

    def rows_of(c):
        return min(rows_per_copy, nrows - c * rows_per_copy)

    def copy(c):
        return pltpu.make_async_copy(src.at[pl.ds(src_row0 + c * rows_per_copy, rows_of(c))],
                                     stage.at[c % slots, pl.ds(0, rows_of(c))], sem.at[c % slots])

    for c in range(min(slots, n)):
        copy(c).start(priority=c % 2)
    for c in range(n):
        copy(c).wait()
        r0 = dst_row0 + c * rows_per_copy
        dst[r0:r0 + rows_of(c), :] = stage[c % slots, 0:rows_of(c), :].astype(BF16)
        if c + slots < n:
            copy(c + slots).start(priority=c % 2)


def _stage_scratch(cols, wide=False):
    slots, rows = (STAGE_SLOTS_WIDE, STAGE_ROWS_WIDE) if wide else (STAGE_SLOTS, STAGE_ROWS)
    return [pltpu.VMEM((slots, rows, cols), F32), pltpu.SemaphoreType.DMA((slots,))]


def _ada_kernel(c_ref, w_ref, b_ref, o_ref):
    c = c_ref[...]
    sc = (c * jax.nn.sigmoid(c)).astype(BF16)
    o_ref[...] = _dot(sc, w_ref[...].astype(BF16)) + b_ref[...]


def _ada(c, w_ada, b_ada):
    rows, d = c.shape
    n = w_ada.shape[1]
    tn = d
    return pl.pallas_call(
        _ada_kernel,
        grid=(n // tn,),
        in_specs=[pl.BlockSpec((rows, d), lambda j: (0, 0)),
                  pl.BlockSpec((d, tn), lambda j: (0, j)),
                  pl.BlockSpec((1, tn), lambda j: (0, j))],
        out_specs=pl.BlockSpec((rows, tn), lambda j: (0, j)),
        out_shape=jax.ShapeDtypeStruct((rows, n), F32),
        compiler_params=_cparams(("arbitrary",)),
        name="adaln_mod",
    )(c, w_ada, b_ada.reshape(1, n))


def _ffn_rows(x_ref, sh_ref, sc_ref, ga_ref, g_ref, wi_ref, wo_ref, gf_ref, o_ref, act_ref, d_ff):
    rows = x_ref.shape[0]
    x = x_ref[...]
    xn = _rms_mod(x, g_ref[...], sh_ref[...], sc_ref[...]).astype(BF16)
    for j in range(d_ff // FFN_COLS):
        lo = j * FFN_COLS
        hg = _dot(xn, wi_ref[:, lo:lo + FFN_COLS])
        hu = _dot(xn, wi_ref[:, d_ff + lo:d_ff + lo + FFN_COLS])
        act_ref[0:rows, lo:lo + FFN_COLS] = (hg * jax.nn.sigmoid(hg) * hu).astype(BF16)
    y = _dot(act_ref[0:rows, :], wo_ref[...])
    out = x + (0.5 * ga_ref[...]) * y
    if gf_ref is not None:
        ms = jnp.mean(out * out, axis=-1, keepdims=True)
        out = out * lax.rsqrt(ms + EPS) * gf_ref[...]
    o_ref[...] = out


def _ffn_kernel(*refs, d_ff, final, n_prompt_steps):
    n_in = 13 if final else 12
    xp_ref, shp_ref, scp_ref, gap_ref, xs_ref, shs_ref, scs_ref, gas_ref, g_ref, wi_hbm, wo_hbm = refs[:11]
    gf_ref = refs[11] if final else None
    op_ref, os_ref = refs[n_in - 1:n_in + 1]
    act_ref, wi_ref, wo_ref, stage_i, sem_i, stage_o, sem_o = refs[-7:]
    step = pl.program_id(0)

    @pl.when(step == 0)
    def _():
        _stage_rows(wi_hbm, 0, wi_ref.shape[0], wi_ref, 0, stage_i, sem_i)
        _stage_rows(wo_hbm, 0, wo_ref.shape[0], wo_ref, 0, stage_o, sem_o)

    @pl.when(step < n_prompt_steps)
    def _():
        _ffn_rows(xp_ref, shp_ref, scp_ref, gap_ref, g_ref, wi_ref, wo_ref, gf_ref, op_ref, act_ref, d_ff)

    @pl.when(step == n_prompt_steps)
    def _():
        _ffn_rows(xs_ref, shs_ref, scs_ref, gas_ref, g_ref, wi_ref, wo_ref, gf_ref, os_ref, act_ref, d_ff)


def _ffn(xp, mod_p, xs, mod_s, j0, gain, wi, wo, g_final, *, rows_per_seq):
    rows, d = xp.shape
    ns = xs.shape[0]
    d_ff = wo.shape[0]
    tm = min(FFN_ROWS, rows_per_seq)
    tiles_per_seq = rows_per_seq // tm
    n_steps = rows // tm
    tile_of = lambda i: jnp.minimum(i, n_steps - 1)
    mspec_p = lambda j: pl.BlockSpec((None, 1, d), lambda i: (tile_of(i) // tiles_per_seq, 0, j))
    mspec_s = lambda j: pl.BlockSpec((ns, d), lambda i: (0, j))
    final = g_final is not None
    in_specs = [pl.BlockSpec((tm, d), lambda i: (tile_of(i), 0)), mspec_p(j0), mspec_p(j0 + 1), mspec_p(j0 + 2),
                pl.BlockSpec((ns, d), lambda i: (0, 0)), mspec_s(j0), mspec_s(j0 + 1), mspec_s(j0 + 2),
                _resident((1, d)), _HBM, _HBM]
    args = [xp, mod_p, mod_p, mod_p, xs, mod_s, mod_s, mod_s, gain.reshape(1, d), wi, wo]
    if final:
        in_specs.append(_resident((1, d)))
        args.append(g_final.reshape(1, d))
    return pl.pallas_call(
        functools.partial(_ffn_kernel, d_ff=d_ff, final=final, n_prompt_steps=n_steps),
        grid=(n_steps + 1,),
        in_specs=in_specs,
        out_specs=[pl.BlockSpec((tm, d), lambda i: (tile_of(i), 0)), pl.BlockSpec((ns, d), lambda i: (0, 0))],
        out_shape=[jax.ShapeDtypeStruct((rows, d), F32), jax.ShapeDtypeStruct((ns, d), F32)],
        scratch_shapes=[pltpu.VMEM((tm, d_ff), BF16), pltpu.VMEM(wi.shape, BF16), pltpu.VMEM(wo.shape, BF16)]
        + _stage_scratch(wi.shape[1], wide=True) + _stage_scratch(wo.shape[1]),
        compiler_params=_cparams(("arbitrary",)),
        name="ffn_final" if final else "ffn",
    )(*args)


def _rg_gates_block(u, wa, wi, ba, bi, lam):
    ub = u.astype(BF16)
    r = jax.nn.sigmoid(_dot(ub, wa) + ba)
    i_g = jax.nn.sigmoid(_dot(ub, wi) + bi)
    log_a = -RG_C * r * jax.nn.softplus(-lam)
    a = jnp.exp(log_a)
    b = jnp.sqrt(-jnp.tanh(log_a) * (a * a + 1.0)) * (i_g * u)
    return a, b


def _rg_gates(u, wa_ref, wi_ref, ba, bi, lam):
    parts = []
    for p in range(wa_ref.shape[0]):
        cols = slice(p * MXU_DIM, (p + 1) * MXU_DIM)
        parts.append(_rg_gates_block(u[:, cols], wa_ref[p], wi_ref[p], ba[:, cols], bi[:, cols], lam[:, cols]))
    return (jnp.concatenate([a for a, _ in parts], axis=-1), jnp.concatenate([b for _, b in parts], axis=-1))


def _rglru_front(xn_ref, w_rows_ref, lb_ref, tb_ref, tail_ref, conv_ref):
    nsq = tb_ref.shape[0]
    d = xn_ref.shape[1]
    tl = xn_ref.shape[0] // nsq
    seg = tl // SUBLANES
    nlb = d // LANES
    pre = (CONV_W - 1) * SUBLANES

    x_rg = _dot_nt(xn_ref[...], w_rows_ref[0:d, :])
    for cb in range(nlb):
        lb_ref[cb] = x_rg[:, cb * LANES:(cb + 1) * LANES]
    first = lax.broadcasted_iota(jnp.int32, (SUBLANES, d), 0) == 0
    for q in range(nsq):
        for j in range(seg):
            for cb in range(nlb):
                tb_ref[q, pre + j * SUBLANES:pre + (j + 1) * SUBLANES, cb * LANES:(cb + 1) * LANES] = (
                    lb_ref[cb, pl.ds(q * tl + j, SUBLANES, stride=seg), :])

        for i in range(CONV_W - 1):
            j = seg - (CONV_W - 1) + i
            slab = tb_ref[q, pre + j * SUBLANES:pre + (j + 1) * SUBLANES, :]
            tb_ref[q, i * SUBLANES:(i + 1) * SUBLANES, :] = jnp.where(first, tail_ref[q, i:i + 1, :],
                                                                     pltpu.roll(slab, 1, axis=0))
            last = slab[SUBLANES - 1:SUBLANES, :]
            tail_ref[q, i:i + 1, :] = last
            conv_ref[q, i:i + 1, :] = last


def _rglru_block(p, q, cw_ref, cb_ref, wa_ref, wi_ref, ba_ref, bi_ref, lam_ref,
                 lb_ref, tb_ref, a_ref, b_ref, h_ref, ha_ref):
    tl = a_ref.shape[1]
    seg = tl // SUBLANES
    cols = slice(p * MXU_DIM, (p + 1) * MXU_DIM)
    lane_blocks = range(p * MXU_DIM // LANES, (p + 1) * MXU_DIM // LANES)
    row0 = q * tl

    piece = tl // RG_PIECES
    for r0 in range(0, tl, piece):
        u = cb_ref[:, cols]
        for j in range(CONV_W):
            u = u + tb_ref[q, j * SUBLANES + r0:j * SUBLANES + r0 + piece, cols] * cw_ref[j:j + 1, cols]
        a, b = _rg_gates_block(u, wa_ref[p], wi_ref[p], ba_ref[:, cols], bi_ref[:, cols], lam_ref[:, cols])
        a_ref[q, r0:r0 + piece, cols] = a
        b_ref[q, r0:r0 + piece, cols] = b
        yield

    acc_a = a_ref[q, 0:SUBLANES, cols]
    acc_b = b_ref[q, 0:SUBLANES, cols]
    for j in range(1, seg):
        rows = slice(j * SUBLANES, (j + 1) * SUBLANES)
        a_j = a_ref[q, rows, cols]
        acc_b = a_j * acc_b + b_ref[q, rows, cols]
        acc_a = a_j * acc_a
        a_ref[q, rows, cols] = acc_a
        b_ref[q, rows, cols] = acc_b

    h = h_ref[q, :, cols]
    h_in = []
    for s in range(SUBLANES):
        h_in.append(h)
        h = acc_a[s:s + 1, :] * h + acc_b[s:s + 1, :]
    h_ref[q, :, cols] = h
    h_in = jnp.concatenate(h_in, axis=0)
    yield

    for j in range(seg):
        rows = slice(j * SUBLANES, (j + 1) * SUBLANES)
        h_j = a_ref[q, rows, cols] * h_in + b_ref[q, rows, cols]
        for i, cb in enumerate(lane_blocks):
            lb_ref[cb, row0 + j * SUBLANES:row0 + (j + 1) * SUBLANES, :] = h_j[:, i * LANES:(i + 1) * LANES]
    for s in range(SUBLANES):
        for cb in lane_blocks:
            ha_ref[row0 + s * seg:row0 + (s + 1) * seg, cb * LANES:(cb + 1) * LANES] = (
                lb_ref[cb, pl.ds(row0 + s, seg, stride=SUBLANES), :].astype(BF16))
    yield


def _mixer_kernel(x_ref, sh_ref, sc_ref, ga_ref, g_ref, wint_hbm, bif_ref, bift_ref,
                  wbra_hbm, wbrb_hbm, wout_hbm, cw_ref, cb_ref, wa_ref, wi_ref, ba_ref, bi_ref, lam_ref,
                  o_ref, c_ref, n_ref, m_ref, conv_ref, hl_ref,
                  hm_ref, wm_ref, wif_ref, wbra_ref, wbrb_ref, wout_ref, stage, sem,
                  lb_ref, tb_ref, a_ref, b_ref, tail_ref, h_ref, ha_ref, mg_ref, xn_ref, hb_ref, sa_ref):
    nsq, tl, d = x_ref.shape
    dk = d // M_HEADS
    cs = min(ML_CHUNK, tl)
    scale = dk ** -0.5
    gate_row0 = 5 * d

    @pl.when((pl.program_id(0) == 0) & (pl.program_id(1) == 0))
    def _():
        _stage_rows(wint_hbm, 0, 5 * d, wm_ref, 0, stage, sem)
        _stage_rows(wint_hbm, gate_row0 + 2 * M_HEADS, 2 * d, wm_ref, 5 * d, stage, sem)
        _stage_rows(wint_hbm, gate_row0, LANES, wif_ref, 0, stage, sem)
        _stage_rows(wbra_hbm, 0, d, wbra_ref, 0, stage, sem)
        _stage_rows(wbrb_hbm, 0, d, wbrb_ref, 0, stage, sem)
        _stage_rows(wout_hbm, 0, d, wout_ref, 0, stage, sem)

    @pl.when(pl.program_id(1) == 0)
    def _():
        c_ref[...] = jnp.zeros_like(c_ref)
        n_ref[...] = jnp.zeros_like(n_ref)
        m_ref[...] = jnp.zeros_like(m_ref)
        tail_ref[...] = jnp.zeros_like(tail_ref)
        h_ref[...] = jnp.zeros_like(h_ref)

    for q in range(nsq):
        xn_ref[q * tl:(q + 1) * tl, :] = _rms_mod(x_ref[q], g_ref[...], sh_ref[q], sc_ref[q]).astype(BF16)

    _rglru_front(xn_ref, wm_ref, lb_ref, tb_ref, tail_ref, conv_ref)
    rglru_block = functools.partial(
        _rglru_block, cw_ref=cw_ref, cb_ref=cb_ref, wa_ref=wa_ref, wi_ref=wi_ref, ba_ref=ba_ref, bi_ref=bi_ref,
        lam_ref=lam_ref, lb_ref=lb_ref, tb_ref=tb_ref, a_ref=a_ref, b_ref=b_ref, h_ref=h_ref, ha_ref=ha_ref)
    n_rg = wa_ref.shape[0]
    wq0 = d

    rg_pieces = itertools.chain.from_iterable(rglru_block(p, sq) for p in range(n_rg) for sq in range(nsq))

    def rg_advance(n=1):
        for _ in range(n):
            next(rg_pieces, None)

    pre_c = _dot_nt(xn_ref[...], wif_ref[...]) + bif_ref[...]
    pre_r = _dot_nt(wif_ref[0:GATE_ROWS, :], xn_ref[...]) + bift_ref[...]
    col_is_f = lax.broadcasted_iota(jnp.int32, pre_c.shape, 1) >= M_HEADS
    row_is_f = lax.broadcasted_iota(jnp.int32, pre_r.shape, 0) >= M_HEADS
    gate_c = jnp.where(col_is_f, jax.nn.log_sigmoid(pre_c), pre_c)
    gate_r = jnp.where(row_is_f, jax.nn.log_sigmoid(pre_r), pre_r)

    ti = lax.broadcasted_iota(jnp.int32, (cs, cs), 0)
    si = lax.broadcasted_iota(jnp.int32, (cs, cs), 1)
    causal = si <= ti
    lower = causal.astype(BF16)
    upper = (ti <= si).astype(BF16)

    chunks = []
    for sq in range(nsq):
        for r0 in range(sq * tl, (sq + 1) * tl, cs):
            gc = gate_c[r0:r0 + cs, :]
            gr = gate_r[:, r0:r0 + cs]
            cum_c = sum(_dot(lower, part) for part in _split3(gc))
            cum_r = sum(_dot(part, upper) for part in _split3(gr))
            chunks.append((sq, r0, gc, gr, cum_c, cum_r))

    for h in range(M_HEADS):
        c0 = h * dk
        q_all = _dot_nt(xn_ref[...], wm_ref[wq0 + c0:wq0 + c0 + dk, :])
        k_all = _dot_nt(xn_ref[...], wm_ref[wq0 + d + c0:wq0 + d + c0 + dk, :])
        v_all = _dot_nt(xn_ref[...], wm_ref[wq0 + 2 * d + c0:wq0 + 2 * d + c0 + dk, :])
        rg_advance()
        for sq, r0, gc, gr, cum_c, cum_r in chunks:
            q = q_all[r0:r0 + cs, :]
            k = k_all[r0:r0 + cs, :]
            v = v_all[r0:r0 + cs, :]
            bc = cum_c[:, M_HEADS + h:M_HEADS + h + 1]
            br = cum_r[M_HEADS + h:M_HEADS + h + 1, :]
            ig_c = gc[:, h:h + 1]
            ig_r = gr[h:h + 1, :]
            m_prev = m_ref[sq, h:h + 1, :]
            c_prev = c_ref[sq, h]
            n_prev = n_ref[sq, h:h + 1, :]

            log_d = jnp.where(causal, bc - br + ig_r, -jnp.inf)
            log_past = bc + m_prev
            m_t = jnp.maximum(log_past, jnp.max(log_d, axis=-1, keepdims=True))
            d_mat = jnp.exp(log_d - m_t)
            past_w = jnp.exp(log_past - m_t)
            qs = q * scale
            qsb = qs.astype(BF16)
            kb = k.astype(BF16)
            vb = v.astype(BF16)
            s = _dot_nt(qsb, kb) * d_mat
            num = past_w * _dot(qsb, c_prev.astype(BF16)) + _dot(s.astype(BF16), vb)
            den = past_w * jnp.sum(qs * n_prev, axis=-1, keepdims=True) + jnp.sum(s, axis=-1, keepdims=True)
            hm_ref[r0:r0 + cs, c0:c0 + dk] = num / jnp.maximum(jnp.abs(den), jnp.exp(-m_t))

            m_new = m_t[cs - 1:cs, :]
            b_last = bc[cs - 1:cs, :]
            w_s = jnp.exp(b_last - bc + ig_c - m_new)
            decay = jnp.exp(b_last + m_prev - m_new)
            kw = w_s * k
            c_ref[sq, h] = decay * c_prev + _dot_tn(kw.astype(BF16), vb)
            n_ref[sq, h:h + 1, :] = decay * n_prev + jnp.sum(kw, axis=0, keepdims=True)
            m_ref[sq, h:h + 1, :] = m_new
            rg_advance()

    col_blocks = [slice(n0, n0 + MXU_DIM) for n0 in range(0, d, MXU_DIM)]
    for cols in col_blocks:
        o_pre = _dot_nt(xn_ref[...], wm_ref[4 * d + cols.start:4 * d + cols.stop, :])
        hb_ref[:, cols] = (jax.nn.sigmoid(o_pre) * hm_ref[:, cols]).astype(BF16)
        rg_advance()
    for cols in col_blocks:
        g_b = _dot_nt(xn_ref[...], wm_ref[6 * d + cols.start:6 * d + cols.stop, :])
        rg_advance()
        hm_ref[:, cols] = jax.nn.sigmoid(g_b) * _dot(hb_ref[...], wbrb_ref[:, cols])
        g_a = _dot_nt(xn_ref[...], wm_ref[5 * d + cols.start:5 * d + cols.stop, :])
        sa_ref[:, cols] = jax.nn.sigmoid(g_a)
        rg_advance()
    for _ in rg_pieces:
        pass
    hl_ref[...] = h_ref[...]
    for cols in col_blocks:
        merged = sa_ref[:, cols] * _dot(ha_ref[...], wbra_ref[:, cols]) + hm_ref[:, cols]
        mg_ref[:, cols] = merged.astype(BF16)
    y = _dot(mg_ref[...], wout_ref[...])
    for q in range(nsq):
        o_ref[q] = x_ref[q] + ga_ref[q] * y[q * tl:(q + 1) * tl, :]


def _mixer_prompt(x3, mod3, gain, w_int, bif, bift, wbra, wbrb, wout, conv_w, conv_b, wa, wi, ba, bi, lam):
    nb, seq, _ = x3.shape
    d = x3.shape[2]
    dk = d // M_HEADS
    nsq = ML_SEQS
    tl = min(ML_ROWS, seq)
    rows = nsq * tl
    row = lambda v: v.reshape(1, d)
    mspec = lambda j: pl.BlockSpec((nsq, 1, d), lambda b, t: (b, 0, j))
    tile = lambda: pl.BlockSpec((nsq, tl, d), lambda b, t: (b, t, 0))
    per_seq = lambda *shape: pl.BlockSpec((nsq,) + shape, lambda b, t: (b,) + (0,) * len(shape))
    return pl.pallas_call(
        _mixer_kernel,
        grid=(nb // nsq, seq // tl),
        in_specs=[tile(), mspec(3), mspec(4), mspec(5), _resident((1, d)),
                  _HBM, _resident(bif.shape), _resident(bift.shape), _HBM, _HBM, _HBM,
                  _resident(conv_w.shape), _resident((1, d)), _resident(wa.shape), _resident(wi.shape),
                  _resident((1, d)), _resident((1, d)), _resident((1, d))],
        out_specs=[tile(), per_seq(M_HEADS, dk, dk), per_seq(M_HEADS, dk), per_seq(M_HEADS, 1),
                   per_seq(CONV_W - 1, d), per_seq(1, d)],
        out_shape=[jax.ShapeDtypeStruct((nb, seq, d), F32),
                   jax.ShapeDtypeStruct((nb, M_HEADS, dk, dk), F32),
                   jax.ShapeDtypeStruct((nb, M_HEADS, dk), F32),
                   jax.ShapeDtypeStruct((nb, M_HEADS, 1), F32),
                   jax.ShapeDtypeStruct((nb, CONV_W - 1, d), F32),
                   jax.ShapeDtypeStruct((nb, 1, d), F32)],
        scratch_shapes=[pltpu.VMEM((rows, d), F32),
                        pltpu.VMEM((7 * d, d), BF16),
                        pltpu.VMEM((LANES, d), BF16),
                        pltpu.VMEM((d, d), BF16), pltpu.VMEM((d, d), BF16), pltpu.VMEM((d, d), BF16)]
        + _stage_scratch(d)
        + [pltpu.VMEM((d // LANES, rows, LANES), F32),
           pltpu.VMEM((nsq, (CONV_W - 1) * SUBLANES + tl, d), F32),
           pltpu.VMEM((nsq, tl, d), F32),
           pltpu.VMEM((nsq, tl, d), F32),
           pltpu.VMEM((nsq, CONV_W - 1, d), F32),
           pltpu.VMEM((nsq, 1, d), F32),
           pltpu.VMEM((rows, d), BF16),
           pltpu.VMEM((rows, d), BF16),
           pltpu.VMEM((rows, d), BF16),
           pltpu.VMEM((rows, d), BF16),
           pltpu.VMEM((rows, d), F32)],
        compiler_params=_cparams(("arbitrary", "arbitrary")),
        name="mixer_prompt",
    )(x3, mod3, mod3, mod3, row(gain), w_int, bif, bift, wbra, wbrb, wout,
      conv_w, row(conv_b), wa, wi, row(ba), row(bi), row(lam))


def _head_sum(x, dk):
    parts = []
    for h in range(x.shape[1] // dk):
        sl = x[:, h * dk:(h + 1) * dk]
        parts.append(jnp.broadcast_to(jnp.sum(sl, axis=-1, keepdims=True), sl.shape))
    return jnp.concatenate(parts, axis=-1)


def _head_spread(cols, lane0, dk, rows):
    return jnp.concatenate(
        [jnp.broadcast_to(cols[:, lane0 + h:lane0 + h + 1], (rows, dk)) for h in range(M_HEADS)], axis=-1)


def _dec_pre_kernel(x_ref, sh_ref, sc_ref, g_ref, wint_hbm, bif_ref,
                    cw_ref, cb_ref, wa_ref, wi_ref, ba_ref, bi_ref, lam_ref,
                    conv0_ref, h0_ref, n0_ref, m0_ref,
                    ha_ref, conv_ref, hs_ref, n_ref, m_ref, qt_ref, kt_ref, dec_ref, wv_ref, pw_ref, sv_ref,
                    den_ref, em_ref,
                    wm_ref, wif_ref, stage, sem):
    rows, d = x_ref.shape
    dk = d // M_HEADS
    scale = dk ** -0.5
    _stage_rows(wint_hbm, 0, 4 * d, wm_ref, 0, stage, sem)
    _stage_rows(wint_hbm, 5 * d, LANES, wif_ref, 0, stage, sem)
    xn = _rms_mod(x_ref[...], g_ref[...], sh_ref[...], sc_ref[...]).astype(BF16)

    x_rg = _dot_nt(xn, wm_ref[0:d, :])
    u = cb_ref[...]
    for j in range(CONV_W - 1):
        u = u + conv0_ref[j] * cw_ref[j:j + 1, :]
        if j > 0:
            conv_ref[j - 1] = conv0_ref[j]
    u = u + x_rg * cw_ref[CONV_W - 1:CONV_W, :]
    conv_ref[CONV_W - 2] = x_rg
    a, b = _rg_gates(u, wa_ref, wi_ref, ba_ref[...], bi_ref[...], lam_ref[...])
    h = a * h0_ref[...] + b
    hs_ref[...] = h
    ha_ref[...] = h

    q = _dot_nt(xn, wm_ref[d:2 * d, :])
    k = _dot_nt(xn, wm_ref[2 * d:3 * d, :])
    v = _dot_nt(xn, wm_ref[3 * d:4 * d, :])
    qt_ref[...] = _dot_nt(wm_ref[d:2 * d, :], xn).astype(BF16)
    kt_ref[...] = _dot_nt(wm_ref[2 * d:3 * d, :], xn).astype(BF16)
    pre = _dot_nt(xn, wif_ref[...]) + bif_ref[...]
    ig = _head_spread(pre, 0, dk, rows)
    lf = jax.nn.log_sigmoid(_head_spread(pre, M_HEADS, dk, rows))
    m0 = _head_spread(m0_ref[...], 0, dk, rows)
    n0 = n0_ref[...]
    log_past = lf + m0
    m_t = jnp.maximum(log_past, ig)
    d_w = jnp.exp(ig - m_t)
    past_w = jnp.exp(log_past - m_t)
    qs = q * scale
    s = _head_sum(qs * k, dk) * d_w
    den_ref[...] = past_w * _head_sum(qs * n0, dk) + s
    em_ref[...] = jnp.exp(-m_t)
    pw_ref[...] = past_w
    sv_ref[...] = s * v
    w_s = jnp.exp(ig - m_t)
    decay = jnp.exp(lf + m0 - m_t)
    dec_ref[...] = decay
    wv_ref[...] = w_s * v
    n_ref[...] = decay * n0 + w_s * k
    m_ref[...] = m_t


def _dec_pre(x2, mod, gain, w_int, bif, conv_w, conv_b, wa, wi, ba, bi, lam, conv0, h0, n0, m0):
    rows, d = x2.shape
    row = lambda v: v.reshape(1, d)
    m0p = jnp.pad(m0, ((0, 0), (0, LANES - M_HEADS)))
    full = lambda shape: pl.BlockSpec(shape, lambda i, _n=len(shape): (0,) * _n)
    mspec = lambda j: pl.BlockSpec((rows, d), lambda i: (0, j))
    vec = jax.ShapeDtypeStruct((rows, d), F32)
    outs = [vec,
            jax.ShapeDtypeStruct((CONV_W - 1, rows, d), F32),
            vec,
            vec,
            vec,
            jax.ShapeDtypeStruct((d, rows), BF16),
            jax.ShapeDtypeStruct((d, rows), BF16),
            vec, vec, vec, vec, vec, vec]
    args = [x2, mod, mod, row(gain), w_int, bif, conv_w, row(conv_b), wa, wi,
            row(ba), row(bi), row(lam), conv0, h0, n0, m0p]
    in_specs = ([full(x2.shape), mspec(3), mspec(4), full((1, d)), _HBM]
                + [full(a.shape) for a in args[5:]])
    return pl.pallas_call(
        _dec_pre_kernel,
        grid=(1,),
        in_specs=in_specs,
        out_specs=[full(o.shape) for o in outs],
        out_shape=outs,
        scratch_shapes=[pltpu.VMEM((4 * d, d), BF16), pltpu.VMEM((LANES, d), BF16)] + _stage_scratch(d),
        compiler_params=_cparams(("arbitrary",)),
        name="decode_pre",
    )(*args)


def _dec_mem_kernel(qt_ref, kt_ref, dec_ref, wv_ref, c0_ref, c_ref, qc_ref):
    bb = c0_ref.shape[0]
    dk = c0_ref.shape[2]
    nseq = qt_ref.shape[1]
    scale = dk ** -0.5
    base = pl.program_id(0) * bb
    seq_id = lax.broadcasted_iota(jnp.int32, (nseq, dk), 0)
    for j in range(bb):
        pick = (seq_id == base + j).astype(BF16)
        qcol = _dot(qt_ref[...], pick) * scale
        kcol = _dot(kt_ref[...], pick)
        for h in range(M_HEADS):
            c0 = c0_ref[j, h]
            dec = dec_ref[j, :, h * dk:(h + 1) * dk]
            wv = wv_ref[j, :, h * dk:(h + 1) * dk]
            c_ref[j, h] = dec * c0 + kcol[h * dk:(h + 1) * dk, :] * wv
            qc_ref[j, :, h * dk:(h + 1) * dk] = jnp.sum(qcol[h * dk:(h + 1) * dk, :] * c0, axis=0, keepdims=True)


def _dec_mem(qt, kt, decay, wv, c0):
    nseq, heads, dk, _ = c0.shape
    d = heads * dk
    bb = DEC_BLOCK if nseq % DEC_BLOCK == 0 else 1
    rowblk = pl.BlockSpec((bb, 1, d), lambda i: (i, 0, 0))
    return pl.pallas_call(
        _dec_mem_kernel,
        grid=(nseq // bb,),
        in_specs=[pl.BlockSpec((d, nseq), lambda i: (0, 0)), pl.BlockSpec((d, nseq), lambda i: (0, 0)),
                  rowblk, rowblk,
                  pl.BlockSpec((bb, heads, dk, dk), lambda i: (i, 0, 0, 0))],
        out_specs=[pl.BlockSpec((bb, heads, dk, dk), lambda i: (i, 0, 0, 0)), rowblk],
        out_shape=[jax.ShapeDtypeStruct(c0.shape, F32), jax.ShapeDtypeStruct((nseq, 1, d), F32)],
        compiler_params=_cparams(("arbitrary",)),
        name="decode_mem",
    )(qt, kt, decay.reshape(nseq, 1, d), wv.reshape(nseq, 1, d), c0)


def _dec_post_kernel(x_ref, sh_ref, sc_ref, ga_ref, g_ref, wint_hbm, wbra_hbm, wbrb_hbm, wout_hbm,
                     ha_ref, qc_ref, pw_ref, sv_ref, den_ref, em_ref, o_ref,
                     wm_ref, wbra_ref, wbrb_ref, wout_ref, stage, sem):
    d = x_ref.shape[1]
    _stage_rows(wint_hbm, 4 * d, d, wm_ref, 0, stage, sem)
    _stage_rows(wint_hbm, 5 * d + 2 * M_HEADS, 2 * d, wm_ref, d, stage, sem)
    _stage_rows(wbra_hbm, 0, d, wbra_ref, 0, stage, sem)
    _stage_rows(wbrb_hbm, 0, d, wbrb_ref, 0, stage, sem)
    _stage_rows(wout_hbm, 0, d, wout_ref, 0, stage, sem)
    x = x_ref[...]
    xn = _rms_mod(x, g_ref[...], sh_ref[...], sc_ref[...]).astype(BF16)
    num = pw_ref[...] * qc_ref[...] + sv_ref[...]
    hm = num / jnp.maximum(jnp.abs(den_ref[...]), em_ref[...])
    hb = (jax.nn.sigmoid(_dot_nt(xn, wm_ref[0:d, :])) * hm).astype(BF16)
    g_a = _dot_nt(xn, wm_ref[d:2 * d, :])
    g_b = _dot_nt(xn, wm_ref[2 * d:3 * d, :])
    merged = (jax.nn.sigmoid(g_a) * _dot(ha_ref[...].astype(BF16), wbra_ref[...])
              + jax.nn.sigmoid(g_b) * _dot(hb, wbrb_ref[...]))
    o_ref[...] = x + ga_ref[...] * _dot(merged.astype(BF16), wout_ref[...])


def _dec_post(x2, mod, gain, w_int, wbra, wbrb, wout, ha, qc, pw, sv, den, em):
    rows, d = x2.shape
    full = lambda shape: pl.BlockSpec(shape, lambda i, _n=len(shape): (0,) * _n)
    mspec = lambda j: pl.BlockSpec((rows, d), lambda i: (0, j))
    args = [x2, mod, mod, mod, gain.reshape(1, d), w_int, wbra, wbrb, wout, ha, qc, pw, sv, den, em]
    in_specs = ([full(x2.shape), mspec(3), mspec(4), mspec(5), full((1, d))] + [_HBM] * 4
                + [full(a.shape) for a in args[9:]])
    return pl.pallas_call(
        _dec_post_kernel,
        grid=(1,),
        in_specs=in_specs,
        out_specs=full((rows, d)),
        out_shape=jax.ShapeDtypeStruct((rows, d), F32),
        scratch_shapes=[pltpu.VMEM((3 * d, d), BF16),
                        pltpu.VMEM((d, d), BF16), pltpu.VMEM((d, d), BF16), pltpu.VMEM((d, d), BF16)]
        + _stage_scratch(d),
        compiler_params=_cparams(("arbitrary",)),
        name="decode_post",
    )(*args)


def _pair_blocks(w):
    nblk, c, _ = w.shape
    z = jnp.zeros((nblk // 2, c, c), w.dtype)
    top = jnp.concatenate([w[0::2], z], axis=2)
    bot = jnp.concatenate([z, w[1::2]], axis=2)
    return jnp.concatenate([top, bot], axis=1).astype(BF16)


def kernel(x_prompt, x_sample, state_conv, state_rg_h, state_C, state_n, state_m, c_prompt, c_sample, w_ada, b_ada, g_norm1, w_ff1_in, w_ff1_out, g_norm2, w_in, conv_w, conv_b, w_rg_a, b_rg_a, w_rg_i, b_rg_i, rg_lambda, b_ig, b_fg, w_br_a, w_br_b, w_out, g_norm3, w_ff2_in, w_ff2_out, g_final):
    nb, seq, d = x_prompt.shape
    ns = x_sample.shape[0]
    depth = w_ada.shape[0]
    assert depth == 1 and x_sample.shape[1] == 1 and nb == SUBLANES
    assert seq % ML_ROWS == 0 and seq % FFN_ROWS == 0 and ML_ROWS % (SUBLANES * SUBLANES) == 0
    assert nb % ML_SEQS == 0
    heads, dk = M_HEADS, d // M_HEADS
    assert w_in.shape[2] == 7 * d + 2 * heads and GATE_ROWS >= 2 * heads

    w_int = w_in[0].T
    b_gates = jnp.concatenate([b_ig[0], b_fg[0]])
    bif = jnp.pad(b_gates, (0, LANES - 2 * heads)).reshape(1, LANES)
    bift = jnp.pad(b_gates, (0, GATE_ROWS - 2 * heads)).reshape(GATE_ROWS, 1)
    wa = _pair_blocks(w_rg_a[0])
    wi = _pair_blocks(w_rg_i[0])
    wf1i, wf1o, wf2i, wf2o = w_ff1_in[0], w_ff1_out[0], w_ff2_in[0], w_ff2_out[0]
    wbra, wbrb, wout = w_br_a[0], w_br_b[0], w_out[0]

    mod = _ada(jnp.concatenate([c_prompt, c_sample], axis=0), w_ada[0], b_ada[0])
    mod_p3 = mod[:nb].reshape(nb, 1, N_MOD * d)
    mod_s = mod[nb:]

    xp, xs = _ffn(x_prompt.reshape(nb * seq, d), mod_p3, x_sample.reshape(ns, d), mod_s, 0, g_norm1[0],
                  wf1i, wf1o, None, rows_per_seq=seq)

    xp, c_p, n_p, m_p, conv_p, h_p = _mixer_prompt(
        xp.reshape(nb, seq, d), mod_p3, g_norm2[0], w_int, bif, bift, wbra, wbrb, wout, conv_w[0], conv_b[0],
        wa, wi, b_rg_a[0], b_rg_i[0], rg_lambda[0])
    xp = xp.reshape(nb * seq, d)

    conv0 = jnp.swapaxes(state_conv[0], 0, 1)
    (ha_s, conv_s, h_s, n_s, m_s, qt, kt, decay, wv, pw, sv, den, em) = _dec_pre(
        xs, mod_s, g_norm2[0], w_int, bif, conv_w[0], conv_b[0], wa, wi,
        b_rg_a[0], b_rg_i[0], rg_lambda[0], conv0, state_rg_h[0], state_n[0].reshape(ns, d), state_m[0])
    c_s, qc = _dec_mem(qt, kt, decay, wv, state_C[0])
    xs = _dec_post(xs, mod_s, g_norm2[0], w_int, wbra, wbrb, wout, ha_s, qc.reshape(ns, d), pw, sv, den, em)

    yp, ys = _ffn(xp, mod_p3, xs, mod_s, 6, g_norm3[0], wf2i, wf2o, g_final, rows_per_seq=seq)

    return (yp.reshape(nb, seq, d), ys.reshape(ns, 1, d),
            conv_p[None], h_p.reshape(1, nb, d), c_p[None], n_p[None], m_p.reshape(1, nb, heads),
            jnp.swapaxes(conv_s, 0, 1)[None], h_s[None], c_s[None], n_s.reshape(1, ns, heads, dk), m_s[:, ::dk][None])
```

```python
import functools
import itertools

import jax
import jax.numpy as jnp
from jax import lax
from jax.experimental import pallas as pl
from jax.experimental.pallas import tpu as pltpu

F32 = jnp.float32
BF16 = jnp.bfloat16

EPS = 1e-6
RG_C = 8.0
CONV_W = 4
N_MOD = 9
M_HEADS = 4
RG_BLOCKS = 8

SUBLANES = 8
LANES = 128
MXU_DIM = 256
VMEM_LIMIT_BYTES = 60 * 1024 * 1024

FFN_ROWS = 1024
FFN_COLS = 256
ML_ROWS = 512
ML_SEQS = 1
ML_CHUNK = 256
RG_PIECES = 4
DEC_BLOCK = 4
STAGE_ROWS = 256
STAGE_SLOTS = 4
STAGE_ROWS_WIDE = 64
STAGE_SLOTS_WIDE = 4
GATE_ROWS = 16


def _cparams(sem):
    return pltpu.CompilerParams(dimension_semantics=sem, vmem_limit_bytes=VMEM_LIMIT_BYTES)


def _resident(shape):
    nd = len(shape)
    return pl.BlockSpec(shape, lambda *_: (0,) * nd, pipeline_mode=pl.Buffered(1))


_HBM = pl.BlockSpec(memory_space=pl.ANY)


def _rms_mod(x, gain, shift, scale):
    ms = jnp.mean(x * x, axis=-1, keepdims=True)
    return x * lax.rsqrt(ms + EPS) * gain * (1.0 + scale) + shift


def _dot(a, b):
    return jnp.dot(a, b, preferred_element_type=F32)


def _dot_nt(a, b):
    return lax.dot_general(a, b, (((1,), (1,)), ((), ())), preferred_element_type=F32)


def _dot_tn(a, b):
    return lax.dot_general(a, b, (((0,), (0,)), ((), ())), preferred_element_type=F32)


def _split3(x):
    hi = x.astype(BF16)
    r1 = x - hi.astype(F32)
    mid = r1.astype(BF16)
    lo = (r1 - mid.astype(F32)).astype(BF16)
    return hi, mid, lo


def _stage_rows(src, src_row0, nrows, dst, dst_row0, stage, sem):
    slots, rows_per_copy, _ = stage.shape
    n = pl.cdiv(nrows, rows_per_copy)

    def rows_of(c):
        return min(rows_per_copy, nrows - c * rows_per_copy)

    def copy(c):
        return pltpu.make_async_copy(src.at[pl.ds(src_row0 + c * rows_per_copy, rows_of(c))],
                                     stage.at[c % slots, pl.ds(0, rows_of(c))], sem.at[c % slots])

    for c in range(min(slots, n)):
        copy(c).start(priority=c % 2)
    for c in range(n):
        copy(c).wait()
        r0 = dst_row0 + c * rows_per_copy
        dst[r0:r0 + rows_of(c), :] = stage[c % slots, 0:rows_of(c), :].astype(BF16)
        if c + slots < n:
            copy(c + slots).start(priority=c % 2)


def _stage_scratch(cols, wide=False):
    slots, rows = (STAGE_SLOTS_WIDE, STAGE_ROWS_WIDE) if wide else (STAGE_SLOTS, STAGE_ROWS)
    return [pltpu.VMEM((slots, rows, cols), F32), pltpu.SemaphoreType.DMA((slots,))]


def _ada_kernel(c_ref, w_ref, b_ref, o_ref):
    c = c_ref[...]
    sc = (c * jax.nn.sigmoid(c)).astype(BF16)
    o_ref[...] = _dot(sc, w_ref[...].astype(BF16)) + b_ref[...]


def _ada(c, w_ada, b_ada):
    rows, d = c.shape
    n = w_ada.shape[1]
    tn = d
    return pl.pallas_call(
        _ada_kernel,
        grid=(n // tn,),
        in_specs=[pl.BlockSpec((rows, d), lambda j: (0, 0)),
                  pl.BlockSpec((d, tn), lambda j: (0, j)),
                  pl.BlockSpec((1, tn), lambda j: (0, j))],
        out_specs=pl.BlockSpec((rows, tn), lambda j: (0, j)),
        out_shape=jax.ShapeDtypeStruct((rows, n), F32),
        compiler_params=_cparams(("arbitrary",)),
        name="adaln_mod",
    )(c, w_ada, b_ada.reshape(1, n))


def _ffn_rows(x_ref, shift, scale, gate, g_ref, wi_ref, wo_ref, gf_ref, o_ref, act_ref, d_ff):
    rows = x_ref.shape[0]
    x = x_ref[...]
    xn = _rms_mod(x, g_ref[...], shift, scale).astype(BF16)
    for j in range(d_ff // FFN_COLS):
        lo = j * FFN_COLS
        hg = _dot(xn, wi_ref[:, lo:lo + FFN_COLS])
        hu = _dot(xn, wi_ref[:, d_ff + lo:d_ff + lo + FFN_COLS])
        act_ref[0:rows, lo:lo + FFN_COLS] = (hg * jax.nn.sigmoid(hg) * hu).astype(BF16)
    y = _dot(act_ref[0:rows, :], wo_ref[...])
    out = x + (0.5 * gate) * y
    if gf_ref is not None:
        ms = jnp.mean(out * out, axis=-1, keepdims=True)
        out = out * lax.rsqrt(ms + EPS) * gf_ref[...]
    o_ref[...] = out


def _ffn_kernel(*refs, d_ff, final, n_prompt_steps, tiles_per_seq):
    n_in = 13 if final else 12
    xp_ref, shp_ref, scp_ref, gap_ref, xs_ref, shs_ref, scs_ref, gas_ref, g_ref, wi_hbm, wo_hbm = refs[:11]
    gf_ref = refs[11] if final else None
    op_ref, os_ref = refs[n_in - 1:n_in + 1]
    act_ref, wi_ref, wo_ref, stage_i, sem_i, stage_o, sem_o = refs[-7:]
    step = pl.program_id(0)

    @pl.when(step == 0)
    def _():
        _stage_rows(wi_hbm, 0, wi_ref.shape[0], wi_ref, 0, stage_i, sem_i)
        _stage_rows(wo_hbm, 0, wo_ref.shape[0], wo_ref, 0, stage_o, sem_o)

    @pl.when(step < n_prompt_steps)
    def _():
        seq = pl.ds(step // tiles_per_seq, 1)
        _ffn_rows(xp_ref, shp_ref[seq, :], scp_ref[seq, :], gap_ref[seq, :], g_ref, wi_ref, wo_ref, gf_ref,
                  op_ref, act_ref, d_ff)

    @pl.when(step == n_prompt_steps)
    def _():
        _ffn_rows(xs_ref, shs_ref[...], scs_ref[...], gas_ref[...], g_ref, wi_ref, wo_ref, gf_ref,
                  os_ref, act_ref, d_ff)


def _mod_specs(ns, nb, d, j0, n):
    assert ns % nb == 0
    prompt = [pl.BlockSpec((nb, d), lambda *_, j=j: (ns // nb, j)) for j in range(j0, j0 + n)]
    sample = [pl.BlockSpec((ns, d), lambda *_, j=j: (0, j)) for j in range(j0, j0 + n)]
    return prompt, sample


def _ffn(xp, xs, mod, j0, gain, wi, wo, g_final, *, rows_per_seq):
    rows, d = xp.shape
    ns = xs.shape[0]
    d_ff = wo.shape[0]
    tm = min(FFN_ROWS, rows_per_seq)
    tiles_per_seq = rows_per_seq // tm
    n_steps = rows // tm
    tile_of = lambda i: jnp.minimum(i, n_steps - 1)
    mspec_p, mspec_s = _mod_specs(ns, rows // rows_per_seq, d, j0, 3)
    final = g_final is not None
    in_specs = ([pl.BlockSpec((tm, d), lambda i: (tile_of(i), 0))] + mspec_p
                + [pl.BlockSpec((ns, d), lambda i: (0, 0))] + mspec_s + [_resident((1, d)), _HBM, _HBM])
    args = [xp, mod, mod, mod, xs, mod, mod, mod, gain.reshape(1, d), wi, wo]
    if final:
        in_specs.append(_resident((1, d)))
        args.append(g_final.reshape(1, d))
    return pl.pallas_call(
        functools.partial(_ffn_kernel, d_ff=d_ff, final=final, n_prompt_steps=n_steps, tiles_per_seq=tiles_per_seq),
        grid=(n_steps + 1,),
        in_specs=in_specs,
        out_specs=[pl.BlockSpec((tm, d), lambda i: (tile_of(i), 0)), pl.BlockSpec((ns, d), lambda i: (0, 0))],
        out_shape=[jax.ShapeDtypeStruct((rows, d), F32), jax.ShapeDtypeStruct((ns, d), F32)],
        scratch_shapes=[pltpu.VMEM((tm, d_ff), BF16), pltpu.VMEM(wi.shape, BF16), pltpu.VMEM(wo.shape, BF16)]
        + _stage_scratch(wi.shape[1], wide=True) + _stage_scratch(wo.shape[1]),
        compiler_params=_cparams(("arbitrary",)),
        name="ffn_final" if final else "ffn",
    )(*args)


def _rg_gates_block(u, wa, wi, ba, bi, lam):
    ub = u.astype(BF16)
    r = jax.nn.sigmoid(_dot(ub, wa) + ba)
    i_g = jax.nn.sigmoid(_dot(ub, wi) + bi)
    log_a = -RG_C * r * jax.nn.softplus(-lam)
    a = jnp.exp(log_a)
    b = jnp.sqrt(-jnp.tanh(log_a) * (a * a + 1.0)) * (i_g * u)
    return a, b


def _rg_gates(u, wa_ref, wi_ref, ba, bi, lam):
    parts = []
    for p in range(wa_ref.shape[0]):
        cols = slice(p * MXU_DIM, (p + 1) * MXU_DIM)
        parts.append(_rg_gates_block(u[:, cols], wa_ref[p], wi_ref[p], ba[:, cols], bi[:, cols], lam[:, cols]))
    return (jnp.concatenate([a for a, _ in parts], axis=-1), jnp.concatenate([b for _, b in parts], axis=-1))


def _rglru_front(xn_ref, w_rows_ref, lb_ref, tb_ref, tail_ref, conv_ref):
    nsq = tb_ref.shape[0]
    d = xn_ref.shape[1]
    tl = xn_ref.shape[0] // nsq
    seg = tl // SUBLANES
    nlb = d // LANES
    pre = (CONV_W - 1) * SUBLANES

    x_rg = _dot_nt(xn_ref[...], w_rows_ref[0:d, :])
    for cb in range(nlb):
        lb_ref[cb] = x_rg[:, cb * LANES:(cb + 1) * LANES]
    first = lax.broadcasted_iota(jnp.int32, (SUBLANES, d), 0) == 0
    for q in range(nsq):
        for j in range(seg):
            for cb in range(nlb):
                tb_ref[q, pre + j * SUBLANES:pre + (j + 1) * SUBLANES, cb * LANES:(cb + 1) * LANES] = (
                    lb_ref[cb, pl.ds(q * tl + j, SUBLANES, stride=seg), :])

        for i in range(CONV_W - 1):
            j = seg - (CONV_W - 1) + i
            slab = tb_ref[q, pre + j * SUBLANES:pre + (j + 1) * SUBLANES, :]
            tb_ref[q, i * SUBLANES:(i + 1) * SUBLANES, :] = jnp.where(first, tail_ref[q, i:i + 1, :],
                                                                     pltpu.roll(slab, 1, axis=0))
            last = slab[SUBLANES - 1:SUBLANES, :]
            tail_ref[q, i:i + 1, :] = last
            conv_ref[q, i:i + 1, :] = last


def _rglru_block(p, q, cw_ref, cb_ref, wa_ref, wi_ref, ba_ref, bi_ref, lam_ref,
                 lb_ref, tb_ref, a_ref, b_ref, h_ref, ha_ref):
    tl = a_ref.shape[1]
    seg = tl // SUBLANES
    cols = slice(p * MXU_DIM, (p + 1) * MXU_DIM)
    lane_blocks = range(p * MXU_DIM // LANES, (p + 1) * MXU_DIM // LANES)
    row0 = q * tl

    piece = tl // RG_PIECES
    for r0 in range(0, tl, piece):
        u = cb_ref[:, cols]
        for j in range(CONV_W):
            u = u + tb_ref[q, j * SUBLANES + r0:j * SUBLANES + r0 + piece, cols] * cw_ref[j:j + 1, cols]
        a, b = _rg_gates_block(u, wa_ref[p], wi_ref[p], ba_ref[:, cols], bi_ref[:, cols], lam_ref[:, cols])
        a_ref[q, r0:r0 + piece, cols] = a
        b_ref[q, r0:r0 + piece, cols] = b
        yield

    acc_a = a_ref[q, 0:SUBLANES, cols]
    acc_b = b_ref[q, 0:SUBLANES, cols]
    for j in range(1, seg):
        rows = slice(j * SUBLANES, (j + 1) * SUBLANES)
        a_j = a_ref[q, rows, cols]
        acc_b = a_j * acc_b + b_ref[q, rows, cols]
        acc_a = a_j * acc_a
        a_ref[q, rows, cols] = acc_a
        b_ref[q, rows, cols] = acc_b

    h = h_ref[q, :, cols]
    h_in = []
    for s in range(SUBLANES):
        h_in.append(h)
        h = acc_a[s:s + 1, :] * h + acc_b[s:s + 1, :]
    h_ref[q, :, cols] = h
    h_in = jnp.concatenate(h_in, axis=0)
    yield

    for j in range(seg):
        rows = slice(j * SUBLANES, (j + 1) * SUBLANES)
        h_j = a_ref[q, rows, cols] * h_in + b_ref[q, rows, cols]
        for i, cb in enumerate(lane_blocks):
            lb_ref[cb, row0 + j * SUBLANES:row0 + (j + 1) * SUBLANES, :] = h_j[:, i * LANES:(i + 1) * LANES]
    for s in range(SUBLANES):
        for cb in lane_blocks:
            ha_ref[row0 + s * seg:row0 + (s + 1) * seg, cb * LANES:(cb + 1) * LANES] = (
                lb_ref[cb, pl.ds(row0 + s, seg, stride=SUBLANES), :].astype(BF16))
    yield


def _mixer_kernel(x_ref, sh_ref, sc_ref, ga_ref, g_ref, wint_hbm, bif_ref, bift_ref,
                  wbra_hbm, wbrb_hbm, wout_hbm, cw_ref, cb_ref, wa_ref, wi_ref, ba_ref, bi_ref, lam_ref,
                  o_ref, c_ref, n_ref, m_ref, conv_ref, hl_ref,
                  hm_ref, wm_ref, wif_ref, wbra_ref, wbrb_ref, wout_ref, stage, sem,
                  lb_ref, tb_ref, a_ref, b_ref, tail_ref, h_ref, ha_ref, mg_ref, xn_ref, hb_ref, sa_ref):
    nsq, tl, d = x_ref.shape
    dk = d // M_HEADS
    cs = min(ML_CHUNK, tl)
    scale = dk ** -0.5
    gate_row0 = 5 * d

    @pl.when((pl.program_id(0) == 0) & (pl.program_id(1) == 0))
    def _():
        _stage_rows(wint_hbm, 0, 5 * d, wm_ref, 0, stage, sem)
        _stage_rows(wint_hbm, gate_row0 + 2 * M_HEADS, 2 * d, wm_ref, 5 * d, stage, sem)
        _stage_rows(wint_hbm, gate_row0, LANES, wif_ref, 0, stage, sem)
        _stage_rows(wbra_hbm, 0, d, wbra_ref, 0, stage, sem)
        _stage_rows(wbrb_hbm, 0, d, wbrb_ref, 0, stage, sem)
        _stage_rows(wout_hbm, 0, d, wout_ref, 0, stage, sem)

    @pl.when(pl.program_id(1) == 0)
    def _():
        c_ref[...] = jnp.zeros_like(c_ref)
        n_ref[...] = jnp.zeros_like(n_ref)
        m_ref[...] = jnp.zeros_like(m_ref)
        tail_ref[...] = jnp.zeros_like(tail_ref)
        h_ref[...] = jnp.zeros_like(h_ref)

    seq_rows = [pl.ds(pl.program_id(0) * nsq + q, 1) for q in range(nsq)]
    for q in range(nsq):
        xn_ref[q * tl:(q + 1) * tl, :] = _rms_mod(x_ref[q], g_ref[...], sh_ref[seq_rows[q], :],
                                                  sc_ref[seq_rows[q], :]).astype(BF16)

    _rglru_front(xn_ref, wm_ref, lb_ref, tb_ref, tail_ref, conv_ref)
    rglru_block = functools.partial(
        _rglru_block, cw_ref=cw_ref, cb_ref=cb_ref, wa_ref=wa_ref, wi_ref=wi_ref, ba_ref=ba_ref, bi_ref=bi_ref,
        lam_ref=lam_ref, lb_ref=lb_ref, tb_ref=tb_ref, a_ref=a_ref, b_ref=b_ref, h_ref=h_ref, ha_ref=ha_ref)
    n_rg = wa_ref.shape[0]
    wq0 = d

    rg_pieces = itertools.chain.from_iterable(rglru_block(p, sq) for p in range(n_rg) for sq in range(nsq))

    def rg_advance(n=1):
        for _ in range(n):
            next(rg_pieces, None)

    pre_c = _dot_nt(xn_ref[...], wif_ref[...]) + bif_ref[...]
    pre_r = _dot_nt(wif_ref[0:GATE_ROWS, :], xn_ref[...]) + bift_ref[...]
    col_is_f = lax.broadcasted_iota(jnp.int32, pre_c.shape, 1) >= M_HEADS
    row_is_f = lax.broadcasted_iota(jnp.int32, pre_r.shape, 0) >= M_HEADS
    gate_c = jnp.where(col_is_f, jax.nn.log_sigmoid(pre_c), pre_c)
    gate_r = jnp.where(row_is_f, jax.nn.log_sigmoid(pre_r), pre_r)

    ti = lax.broadcasted_iota(jnp.int32, (cs, cs), 0)
    si = lax.broadcasted_iota(jnp.int32, (cs, cs), 1)
    causal = si <= ti
    lower = causal.astype(BF16)
    upper = (ti <= si).astype(BF16)

    chunks = []
    for sq in range(nsq):
        for r0 in range(sq * tl, (sq + 1) * tl, cs):
            gc = gate_c[r0:r0 + cs, :]
            gr = gate_r[:, r0:r0 + cs]
            cum_c = sum(_dot(lower, part) for part in _split3(gc))
            cum_r = sum(_dot(part, upper) for part in _split3(gr))
            chunks.append((sq, r0, gc, gr, cum_c, cum_r))

    for h in range(M_HEADS):
        c0 = h * dk
        q_all = _dot_nt(xn_ref[...], wm_ref[wq0 + c0:wq0 + c0 + dk, :])
        k_all = _dot_nt(xn_ref[...], wm_ref[wq0 + d + c0:wq0 + d + c0 + dk, :])
        v_all = _dot_nt(xn_ref[...], wm_ref[wq0 + 2 * d + c0:wq0 + 2 * d + c0 + dk, :])
        rg_advance()
        for sq, r0, gc, gr, cum_c, cum_r in chunks:
            q = q_all[r0:r0 + cs, :]
            k = k_all[r0:r0 + cs, :]
            v = v_all[r0:r0 + cs, :]
            bc = cum_c[:, M_HEADS + h:M_HEADS + h + 1]
            br = cum_r[M_HEADS + h:M_HEADS + h + 1, :]
            ig_c = gc[:, h:h + 1]
            ig_r = gr[h:h + 1, :]
            m_prev = m_ref[sq, h:h + 1, :]
            c_prev = c_ref[sq, h]
            n_prev = n_ref[sq, h:h + 1, :]

            log_d = jnp.where(causal, bc - br + ig_r, -jnp.inf)
            log_past = bc + m_prev
            m_t = jnp.maximum(log_past, jnp.max(log_d, axis=-1, keepdims=True))
            d_mat = jnp.exp(log_d - m_t)
            past_w = jnp.exp(log_past - m_t)
            qs = q * scale
            qsb = qs.astype(BF16)
            kb = k.astype(BF16)
            vb = v.astype(BF16)
            s = _dot_nt(qsb, kb) * d_mat
            num = past_w * _dot(qsb, c_prev.astype(BF16)) + _dot(s.astype(BF16), vb)
            den = past_w * jnp.sum(qs * n_prev, axis=-1, keepdims=True) + jnp.sum(s, axis=-1, keepdims=True)
            hm_ref[r0:r0 + cs, c0:c0 + dk] = num / jnp.maximum(jnp.abs(den), jnp.exp(-m_t))

            m_new = m_t[cs - 1:cs, :]
            b_last = bc[cs - 1:cs, :]
            w_s = jnp.exp(b_last - bc + ig_c - m_new)
            decay = jnp.exp(b_last + m_prev - m_new)
            kw = w_s * k
            c_ref[sq, h] = decay * c_prev + _dot_tn(kw.astype(BF16), vb)
            n_ref[sq, h:h + 1, :] = decay * n_prev + jnp.sum(kw, axis=0, keepdims=True)
            m_ref[sq, h:h + 1, :] = m_new
            rg_advance()

    col_blocks = [slice(n0, n0 + MXU_DIM) for n0 in range(0, d, MXU_DIM)]
    for cols in col_blocks:
        o_pre = _dot_nt(xn_ref[...], wm_ref[4 * d + cols.start:4 * d + cols.stop, :])
        hb_ref[:, cols] = (jax.nn.sigmoid(o_pre) * hm_ref[:, cols]).astype(BF16)
        rg_advance()
    for cols in col_blocks:
        g_b = _dot_nt(xn_ref[...], wm_ref[6 * d + cols.start:6 * d + cols.stop, :])
        rg_advance()
        hm_ref[:, cols] = jax.nn.sigmoid(g_b) * _dot(hb_ref[...], wbrb_ref[:, cols])
        g_a = _dot_nt(xn_ref[...], wm_ref[5 * d + cols.start:5 * d + cols.stop, :])
        sa_ref[:, cols] = jax.nn.sigmoid(g_a)
        rg_advance()
    for _ in rg_pieces:
        pass
    hl_ref[...] = h_ref[...]
    for cols in col_blocks:
        merged = sa_ref[:, cols] * _dot(ha_ref[...], wbra_ref[:, cols]) + hm_ref[:, cols]
        mg_ref[:, cols] = merged.astype(BF16)
    y = _dot(mg_ref[...], wout_ref[...])
    for q in range(nsq):
        o_ref[q] = x_ref[q] + ga_ref[seq_rows[q], :] * y[q * tl:(q + 1) * tl, :]


def _mixer_prompt(x3, mod, gain, w_int, bif, bift, wbra, wbrb, wout, conv_w, conv_b, wa, wi, ba, bi, lam):
    nb, seq, _ = x3.shape
    d = x3.shape[2]
    dk = d // M_HEADS
    nsq = ML_SEQS
    tl = min(ML_ROWS, seq)
    rows = nsq * tl
    row = lambda v: v.reshape(1, d)
    mspecs, _ = _mod_specs(mod.shape[0] - nb, nb, d, 3, 3)
    tile = lambda: pl.BlockSpec((nsq, tl, d), lambda b, t: (b, t, 0))
    per_seq = lambda *shape: pl.BlockSpec((nsq,) + shape, lambda b, t: (b,) + (0,) * len(shape))
    return pl.pallas_call(
        _mixer_kernel,
        grid=(nb // nsq, seq // tl),
        in_specs=[tile()] + mspecs + [_resident((1, d)),
                  _HBM, _resident(bif.shape), _resident(bift.shape), _HBM, _HBM, _HBM,
                  _resident(conv_w.shape), _resident((1, d)), _resident(wa.shape), _resident(wi.shape),
                  _resident((1, d)), _resident((1, d)), _resident((1, d))],
        out_specs=[tile(), per_seq(M_HEADS, dk, dk), per_seq(M_HEADS, dk), per_seq(M_HEADS, 1),
                   per_seq(CONV_W - 1, d), per_seq(1, d)],
        out_shape=[jax.ShapeDtypeStruct((nb, seq, d), F32),
                   jax.ShapeDtypeStruct((nb, M_HEADS, dk, dk), F32),
                   jax.ShapeDtypeStruct((nb, M_HEADS, dk), F32),
                   jax.ShapeDtypeStruct((nb, M_HEADS, 1), F32),
                   jax.ShapeDtypeStruct((nb, CONV_W - 1, d), F32),
                   jax.ShapeDtypeStruct((nb, 1, d), F32)],
        scratch_shapes=[pltpu.VMEM((rows, d), F32),
                        pltpu.VMEM((7 * d, d), BF16),
                        pltpu.VMEM((LANES, d), BF16),
                        pltpu.VMEM((d, d), BF16), pltpu.VMEM((d, d), BF16), pltpu.VMEM((d, d), BF16)]
        + _stage_scratch(d)
        + [pltpu.VMEM((d // LANES, rows, LANES), F32),
           pltpu.VMEM((nsq, (CONV_W - 1) * SUBLANES + tl, d), F32),
           pltpu.VMEM((nsq, tl, d), F32),
           pltpu.VMEM((nsq, tl, d), F32),
           pltpu.VMEM((nsq, CONV_W - 1, d), F32),
           pltpu.VMEM((nsq, 1, d), F32),
           pltpu.VMEM((rows, d), BF16),
           pltpu.VMEM((rows, d), BF16),
           pltpu.VMEM((rows, d), BF16),
           pltpu.VMEM((rows, d), BF16),
           pltpu.VMEM((rows, d), F32)],
        compiler_params=_cparams(("arbitrary", "arbitrary")),
        name="mixer_prompt",
    )(x3, mod, mod, mod, row(gain), w_int, bif, bift, wbra, wbrb, wout,
      conv_w, row(conv_b), wa, wi, row(ba), row(bi), row(lam))


def _head_sum(x, dk):
    parts = []
    for h in range(x.shape[1] // dk):
        sl = x[:, h * dk:(h + 1) * dk]
        parts.append(jnp.broadcast_to(jnp.sum(sl, axis=-1, keepdims=True), sl.shape))
    return jnp.concatenate(parts, axis=-1)


def _head_spread(cols, lane0, dk, rows):
    return jnp.concatenate(
        [jnp.broadcast_to(cols[:, lane0 + h:lane0 + h + 1], (rows, dk)) for h in range(M_HEADS)], axis=-1)


def _dec_pre_kernel(x_ref, sh_ref, sc_ref, g_ref, wint_hbm, bif_ref,
                    cw_ref, cb_ref, wa_ref, wi_ref, ba_ref, bi_ref, lam_ref,
                    conv0_ref, h0_ref, n0_ref, m0_ref,
                    ha_ref, conv_ref, hs_ref, n_ref, m_ref, qt_ref, kt_ref, dec_ref, wv_ref, pw_ref, sv_ref,
                    den_ref, em_ref,
                    wm_ref, wif_ref, stage, sem):
    rows, d = x_ref.shape
    dk = d // M_HEADS
    scale = dk ** -0.5
    _stage_rows(wint_hbm, 0, 4 * d, wm_ref, 0, stage, sem)
    _stage_rows(wint_hbm, 5 * d, LANES, wif_ref, 0, stage, sem)
    xn = _rms_mod(x_ref[...], g_ref[...], sh_ref[...], sc_ref[...]).astype(BF16)

    x_rg = _dot_nt(xn, wm_ref[0:d, :])
    u = cb_ref[...]
    for j in range(CONV_W - 1):
        u = u + conv0_ref[j] * cw_ref[j:j + 1, :]
        if j > 0:
            conv_ref[j - 1] = conv0_ref[j]
    u = u + x_rg * cw_ref[CONV_W - 1:CONV_W, :]
    conv_ref[CONV_W - 2] = x_rg
    a, b = _rg_gates(u, wa_ref, wi_ref, ba_ref[...], bi_ref[...], lam_ref[...])
    h = a * h0_ref[...] + b
    hs_ref[...] = h
    ha_ref[...] = h

    q = _dot_nt(xn, wm_ref[d:2 * d, :])
    k = _dot_nt(xn, wm_ref[2 * d:3 * d, :])
    v = _dot_nt(xn, wm_ref[3 * d:4 * d, :])
    qt_ref[...] = _dot_nt(wm_ref[d:2 * d, :], xn).astype(BF16)
    kt_ref[...] = _dot_nt(wm_ref[2 * d:3 * d, :], xn).astype(BF16)
    pre = _dot_nt(xn, wif_ref[...]) + bif_ref[...]
    ig = _head_spread(pre, 0, dk, rows)
    lf = jax.nn.log_sigmoid(_head_spread(pre, M_HEADS, dk, rows))
    m0 = _head_spread(m0_ref[...], 0, dk, rows)
    n0 = n0_ref[...]
    log_past = lf + m0
    m_t = jnp.maximum(log_past, ig)
    d_w = jnp.exp(ig - m_t)
    past_w = jnp.exp(log_past - m_t)
    qs = q * scale
    s = _head_sum(qs * k, dk) * d_w
    den_ref[...] = past_w * _head_sum(qs * n0, dk) + s
    em_ref[...] = jnp.exp(-m_t)
    pw_ref[...] = past_w
    sv_ref[...] = s * v
    w_s = jnp.exp(ig - m_t)
    decay = jnp.exp(lf + m0 - m_t)
    dec_ref[...] = decay
    wv_ref[...] = w_s * v
    n_ref[...] = decay * n0 + w_s * k
    m_ref[...] = m_t


def _dec_pre(x2, mod, gain, w_int, bif, conv_w, conv_b, wa, wi, ba, bi, lam, conv0, h0, n0, m0):
    rows, d = x2.shape
    row = lambda v: v.reshape(1, d)
    m0p = jnp.pad(m0, ((0, 0), (0, LANES - M_HEADS)))
    full = lambda shape: pl.BlockSpec(shape, lambda i, _n=len(shape): (0,) * _n)
    mspec = lambda j: pl.BlockSpec((rows, d), lambda i: (0, j))
    vec = jax.ShapeDtypeStruct((rows, d), F32)
    outs = [vec,
            jax.ShapeDtypeStruct((CONV_W - 1, rows, d), F32),
            vec,
            vec,
            vec,
            jax.ShapeDtypeStruct((d, rows), BF16),
            jax.ShapeDtypeStruct((d, rows), BF16),
            vec, vec, vec, vec, vec, vec]
    args = [x2, mod, mod, row(gain), w_int, bif, conv_w, row(conv_b), wa, wi,
            row(ba), row(bi), row(lam), conv0, h0, n0, m0p]
    in_specs = ([full(x2.shape), mspec(3), mspec(4), full((1, d)), _HBM]
                + [full(a.shape) for a in args[5:]])
    return pl.pallas_call(
        _dec_pre_kernel,
        grid=(1,),
        in_specs=in_specs,
        out_specs=[full(o.shape) for o in outs],
        out_shape=outs,
        scratch_shapes=[pltpu.VMEM((4 * d, d), BF16), pltpu.VMEM((LANES, d), BF16)] + _stage_scratch(d),
        compiler_params=_cparams(("arbitrary",)),
        name="decode_pre",
    )(*args)


def _dec_mem_kernel(qt_ref, kt_ref, dec_ref, wv_ref, c0_ref, c_ref, qc_ref):
    bb = c0_ref.shape[0]
    dk = c0_ref.shape[2]
    nseq = qt_ref.shape[1]
    scale = dk ** -0.5
    base = pl.program_id(0) * bb
    seq_id = lax.broadcasted_iota(jnp.int32, (nseq, dk), 0)
    for j in range(bb):
        pick = (seq_id == base + j).astype(BF16)
        qcol = _dot(qt_ref[...], pick) * scale
        kcol = _dot(kt_ref[...], pick)
        seq = pl.ds(base + j, 1)
        for h in range(M_HEADS):
            cols = slice(h * dk, (h + 1) * dk)
            c0 = c0_ref[j, h]
            dec = dec_ref[seq, cols]
            wv = wv_ref[seq, cols]
            c_ref[j, h] = dec * c0 + kcol[cols, :] * wv
            qc_ref[seq, cols] = jnp.sum(qcol[cols, :] * c0, axis=0, keepdims=True)


def _dec_mem(qt, kt, decay, wv, c0):
    nseq, heads, dk, _ = c0.shape
    d = heads * dk
    bb = DEC_BLOCK if nseq % DEC_BLOCK == 0 else 1
    whole = lambda shape: pl.BlockSpec(shape, lambda i: (0, 0))
    return pl.pallas_call(
        _dec_mem_kernel,
        grid=(nseq // bb,),
        in_specs=[whole((d, nseq)), whole((d, nseq)), whole((nseq, d)), whole((nseq, d)),
                  pl.BlockSpec((bb, heads, dk, dk), lambda i: (i, 0, 0, 0))],
        out_specs=[pl.BlockSpec((bb, heads, dk, dk), lambda i: (i, 0, 0, 0)), whole((nseq, d))],
        out_shape=[jax.ShapeDtypeStruct(c0.shape, F32), jax.ShapeDtypeStruct((nseq, d), F32)],
        compiler_params=_cparams(("arbitrary",)),
        name="decode_mem",
    )(qt, kt, decay, wv, c0)


def _dec_post_kernel(x_ref, sh_ref, sc_ref, ga_ref, g_ref, wint_hbm, wbra_hbm, wbrb_hbm, wout_hbm,
                     ha_ref, qc_ref, pw_ref, sv_ref, den_ref, em_ref, o_ref,
                     wm_ref, wbra_ref, wbrb_ref, wout_ref, stage, sem):
    d = x_ref.shape[1]
    _stage_rows(wint_hbm, 4 * d, d, wm_ref, 0, stage, sem)
    _stage_rows(wint_hbm, 5 * d + 2 * M_HEADS, 2 * d, wm_ref, d, stage, sem)
    _stage_rows(wbra_hbm, 0, d, wbra_ref, 0, stage, sem)
    _stage_rows(wbrb_hbm, 0, d, wbrb_ref, 0, stage, sem)
    _stage_rows(wout_hbm, 0, d, wout_ref, 0, stage, sem)
    x = x_ref[...]
    xn = _rms_mod(x, g_ref[...], sh_ref[...], sc_ref[...]).astype(BF16)
    num = pw_ref[...] * qc_ref[...] + sv_ref[...]
    hm = num / jnp.maximum(jnp.abs(den_ref[...]), em_ref[...])
    hb = (jax.nn.sigmoid(_dot_nt(xn, wm_ref[0:d, :])) * hm).astype(BF16)
    g_a = _dot_nt(xn, wm_ref[d:2 * d, :])
    g_b = _dot_nt(xn, wm_ref[2 * d:3 * d, :])
    merged = (jax.nn.sigmoid(g_a) * _dot(ha_ref[...].astype(BF16), wbra_ref[...])
              + jax.nn.sigmoid(g_b) * _dot(hb, wbrb_ref[...]))
    o_ref[...] = x + ga_ref[...] * _dot(merged.astype(BF16), wout_ref[...])


def _dec_post(x2, mod, gain, w_int, wbra, wbrb, wout, ha, qc, pw, sv, den, em):
    rows, d = x2.shape
    full = lambda shape: pl.BlockSpec(shape, lambda i, _n=len(shape): (0,) * _n)
    mspec = lambda j: pl.BlockSpec((rows, d), lambda i: (0, j))
    args = [x2, mod, mod, mod, gain.reshape(1, d), w_int, wbra, wbrb, wout, ha, qc, pw, sv, den, em]
    in_specs = ([full(x2.shape), mspec(3), mspec(4), mspec(5), full((1, d))] + [_HBM] * 4
                + [full(a.shape) for a in args[9:]])
    return pl.pallas_call(
        _dec_post_kernel,
        grid=(1,),
        in_specs=in_specs,
        out_specs=full((rows, d)),
        out_shape=jax.ShapeDtypeStruct((rows, d), F32),
        scratch_shapes=[pltpu.VMEM((3 * d, d), BF16),
                        pltpu.VMEM((d, d), BF16), pltpu.VMEM((d, d), BF16), pltpu.VMEM((d, d), BF16)]
        + _stage_scratch(d),
        compiler_params=_cparams(("arbitrary",)),
        name="decode_post",
    )(*args)


def _pair_blocks(w):
    nblk, c, _ = w.shape
    z = jnp.zeros((nblk // 2, c, c), w.dtype)
    top = jnp.concatenate([w[0::2], z], axis=2)
    bot = jnp.concatenate([z, w[1::2]], axis=2)
    return jnp.concatenate([top, bot], axis=1).astype(BF16)


def kernel(x_prompt, x_sample, state_conv, state_rg_h, state_C, state_n, state_m, c_prompt, c_sample, w_ada, b_ada, g_norm1, w_ff1_in, w_ff1_out, g_norm2, w_in, conv_w, conv_b, w_rg_a, b_rg_a, w_rg_i, b_rg_i, rg_lambda, b_ig, b_fg, w_br_a, w_br_b, w_out, g_norm3, w_ff2_in, w_ff2_out, g_final):
    nb, seq, d = x_prompt.shape
    ns = x_sample.shape[0]
    depth = w_ada.shape[0]
    assert depth == 1 and x_sample.shape[1] == 1 and nb == SUBLANES
    assert seq % ML_ROWS == 0 and seq % FFN_ROWS == 0 and ML_ROWS % (SUBLANES * SUBLANES) == 0
    assert nb % ML_SEQS == 0
    heads, dk = M_HEADS, d // M_HEADS
    assert w_in.shape[2] == 7 * d + 2 * heads and GATE_ROWS >= 2 * heads

    w_int = w_in[0].T
    b_gates = jnp.concatenate([b_ig[0], b_fg[0]])
    bif = jnp.pad(b_gates, (0, LANES - 2 * heads)).reshape(1, LANES)
    bift = jnp.pad(b_gates, (0, GATE_ROWS - 2 * heads)).reshape(GATE_ROWS, 1)
    wa = _pair_blocks(w_rg_a[0])
    wi = _pair_blocks(w_rg_i[0])
    wf1i, wf1o, wf2i, wf2o = w_ff1_in[0], w_ff1_out[0], w_ff2_in[0], w_ff2_out[0]
    wbra, wbrb, wout = w_br_a[0], w_br_b[0], w_out[0]

    mod = _ada(jnp.concatenate([c_sample, c_prompt], axis=0), w_ada[0], b_ada[0])

    xp, xs = _ffn(x_prompt.reshape(nb * seq, d), x_sample.reshape(ns, d), mod, 0, g_norm1[0],
                  wf1i, wf1o, None, rows_per_seq=seq)

    xp, c_p, n_p, m_p, conv_p, h_p = _mixer_prompt(
        xp.reshape(nb, seq, d), mod, g_norm2[0], w_int, bif, bift, wbra, wbrb, wout, conv_w[0], conv_b[0],
        wa, wi, b_rg_a[0], b_rg_i[0], rg_lambda[0])
    xp = xp.reshape(nb * seq, d)

    conv0 = jnp.swapaxes(state_conv[0], 0, 1)
    (ha_s, conv_s, h_s, n_s, m_s, qt, kt, decay, wv, pw, sv, den, em) = _dec_pre(
        xs, mod, g_norm2[0], w_int, bif, conv_w[0], conv_b[0], wa, wi,
        b_rg_a[0], b_rg_i[0], rg_lambda[0], conv0, state_rg_h[0], state_n[0].reshape(ns, d), state_m[0])
    c_s, qc = _dec_mem(qt, kt, decay, wv, state_C[0])
    xs = _dec_post(xs, mod, g_norm2[0], w_int, wbra, wbrb, wout, ha_s, qc, pw, sv, den, em)

    yp, ys = _ffn(xp, xs, mod, 6, g_norm3[0], wf2i, wf2o, g_final, rows_per_seq=seq)

    return (yp.reshape(nb, seq, d), ys.reshape(ns, 1, d),
            conv_p[None], h_p.reshape(1, nb, d), c_p[None], n_p[None], m_p.reshape(1, nb, heads),
            jnp.swapaxes(conv_s, 0, 1)[None], h_s[None], c_s[None], n_s.reshape(1, ns, heads, dk), m_s[:, ::dk][None])
```

```python
import functools
import itertools

import jax
import jax.numpy as jnp
from jax import lax
from jax.experimental import pallas as pl
from jax.experimental.pallas import tpu as pltpu

F32 = jnp.float32
BF16 = jnp.bfloat16

EPS = 1e-6
RG_C = 8.0
CONV_W = 4
N_MOD = 9
M_HEADS = 4
RG_BLOCKS = 8

SUBLANES = 8
LANES = 128
MXU_DIM = 256
VMEM_LIMIT_BYTES = 60 * 1024 * 1024

ADA_STEPS = 4
FFN_ROWS = 1024
FFN_COLS = 256
ML_ROWS = 512
ML_SEQS = 1
ML_CHUNK = 256
RG_PIECES = 4
DEC_BLOCK = 4
STAGE_ROWS = 256
STAGE_SLOTS = 4
STAGE_ROWS_WIDE = 64
STAGE_SLOTS_WIDE = 4
GATE_ROWS = 16


def _cparams(sem):
    return pltpu.CompilerParams(dimension_semantics=sem, vmem_limit_bytes=VMEM_LIMIT_BYTES)


def _resident(shape):
    nd = len(shape)
    return pl.BlockSpec(shape, lambda *_: (0,) * nd, pipeline_mode=pl.Buffered(1))


_HBM = pl.BlockSpec(memory_space=pl.ANY)


def _rms_mod(x, gain, shift, scale):
    ms = jnp.mean(x * x, axis=-1, keepdims=True)
    return x * lax.rsqrt(ms + EPS) * gain * (1.0 + scale) + shift


def _dot(a, b):
    return jnp.dot(a, b, preferred_element_type=F32)


def _dot_nt(a, b):
    return lax.dot_general(a, b, (((1,), (1,)), ((), ())), preferred_element_type=F32)


def _dot_tn(a, b):
    return lax.dot_general(a, b, (((0,), (0,)), ((), ())), preferred_element_type=F32)


def _split3(x):
    hi = x.astype(BF16)
    r1 = x - hi.astype(F32)
    mid = r1.astype(BF16)
    lo = (r1 - mid.astype(F32)).astype(BF16)
    return hi, mid, lo


def _stage_rows(src, src_row0, nrows, dst, dst_row0, stage, sem):
    slots, rows_per_copy, _ = stage.shape
    n = pl.cdiv(nrows, rows_per_copy)

    def rows_of(c):
        return min(rows_per_copy, nrows - c * rows_per_copy)

    def copy(c):
        return pltpu.make_async_copy(src.at[pl.ds(src_row0 + c * rows_per_copy, rows_of(c))],
                                     stage.at[c % slots, pl.ds(0, rows_of(c))], sem.at[c % slots])

    for c in range(min(slots, n)):
        copy(c).start(priority=c % 2)
    for c in range(n):
        copy(c).wait()
        r0 = dst_row0 + c * rows_per_copy
        dst[r0:r0 + rows_of(c), :] = stage[c % slots, 0:rows_of(c), :].astype(BF16)
        if c + slots < n:
            copy(c + slots).start(priority=c % 2)


def _stage_scratch(cols, wide=False):
    slots, rows = (STAGE_SLOTS_WIDE, STAGE_ROWS_WIDE) if wide else (STAGE_SLOTS, STAGE_ROWS)
    return [pltpu.VMEM((slots, rows, cols), F32), pltpu.SemaphoreType.DMA((slots,))]


def _ada_kernel(c_ref, w_ref, b_ref, o_ref):
    c = c_ref[...]
    sc = (c * jax.nn.sigmoid(c)).astype(BF16)
    o_ref[...] = _dot(sc, w_ref[...].astype(BF16)) + b_ref[...]


def _ada(c, w_ada, b_ada):
    rows, d = c.shape
    n = w_ada.shape[1]
    tn = n // ADA_STEPS
    assert tn * ADA_STEPS == n and tn % LANES == 0
    return pl.pallas_call(
        _ada_kernel,
        grid=(n // tn,),
        in_specs=[pl.BlockSpec((rows, d), lambda j: (0, 0)),
                  pl.BlockSpec((d, tn), lambda j: (0, j)),
                  pl.BlockSpec((1, tn), lambda j: (0, j))],
        out_specs=pl.BlockSpec((rows, tn), lambda j: (0, j)),
        out_shape=jax.ShapeDtypeStruct((rows, n), F32),
        compiler_params=_cparams(("arbitrary",)),
        name="adaln_mod",
    )(c, w_ada, b_ada.reshape(1, n))


def _ffn_rows(x_ref, shift, scale, gate, g_ref, wi_ref, wo_ref, gf_ref, o_ref, act_ref, d_ff):
    rows = x_ref.shape[0]
    x = x_ref[...]
    xn = _rms_mod(x, g_ref[...], shift, scale).astype(BF16)
    for j in range(d_ff // FFN_COLS):
        lo = j * FFN_COLS
        hg = _dot(xn, wi_ref[:, lo:lo + FFN_COLS])
        hu = _dot(xn, wi_ref[:, d_ff + lo:d_ff + lo + FFN_COLS])
        act_ref[0:rows, lo:lo + FFN_COLS] = (hg * jax.nn.sigmoid(hg) * hu).astype(BF16)
    y = _dot(act_ref[0:rows, :], wo_ref[...])
    out = x + (0.5 * gate) * y
    if gf_ref is not None:
        ms = jnp.mean(out * out, axis=-1, keepdims=True)
        out = out * lax.rsqrt(ms + EPS) * gf_ref[...]
    o_ref[...] = out


def _ffn_kernel(*refs, d_ff, final, n_prompt_steps, tiles_per_seq):
    n_in = 13 if final else 12
    xp_ref, shp_ref, scp_ref, gap_ref, xs_ref, shs_ref, scs_ref, gas_ref, g_ref, wi_hbm, wo_hbm = refs[:11]
    gf_ref = refs[11] if final else None
    op_ref, os_ref = refs[n_in - 1:n_in + 1]
    act_ref, wi_ref, wo_ref, stage_i, sem_i, stage_o, sem_o = refs[-7:]
    step = pl.program_id(0)

    @pl.when(step == 0)
    def _():
        _stage_rows(wi_hbm, 0, wi_ref.shape[0], wi_ref, 0, stage_i, sem_i)
        _stage_rows(wo_hbm, 0, wo_ref.shape[0], wo_ref, 0, stage_o, sem_o)

    @pl.when(step < n_prompt_steps)
    def _():
        seq = pl.ds(step // tiles_per_seq, 1)
        _ffn_rows(xp_ref, shp_ref[seq, :], scp_ref[seq, :], gap_ref[seq, :], g_ref, wi_ref, wo_ref, gf_ref,
                  op_ref, act_ref, d_ff)

    @pl.when(step == n_prompt_steps)
    def _():
        _ffn_rows(xs_ref, shs_ref[...], scs_ref[...], gas_ref[...], g_ref, wi_ref, wo_ref, gf_ref,
                  os_ref, act_ref, d_ff)


def _mod_specs(ns, nb, d, j0, n):
    assert ns % nb == 0
    prompt = [pl.BlockSpec((nb, d), lambda *_, j=j: (ns // nb, j)) for j in range(j0, j0 + n)]
    sample = [pl.BlockSpec((ns, d), lambda *_, j=j: (0, j)) for j in range(j0, j0 + n)]
    return prompt, sample


def _ffn(xp, xs, mod, j0, gain, wi, wo, g_final, *, rows_per_seq):
    rows, d = xp.shape
    ns = xs.shape[0]
    d_ff = wo.shape[0]
    tm = min(FFN_ROWS, rows_per_seq)
    tiles_per_seq = rows_per_seq // tm
    n_steps = rows // tm
    tile_of = lambda i: jnp.minimum(i, n_steps - 1)
    mspec_p, mspec_s = _mod_specs(ns, rows // rows_per_seq, d, j0, 3)
    final = g_final is not None
    in_specs = ([pl.BlockSpec((tm, d), lambda i: (tile_of(i), 0))] + mspec_p
                + [pl.BlockSpec((ns, d), lambda i: (0, 0))] + mspec_s + [_resident((1, d)), _HBM, _HBM])
    args = [xp, mod, mod, mod, xs, mod, mod, mod, gain.reshape(1, d), wi, wo]
    if final:
        in_specs.append(_resident((1, d)))
        args.append(g_final.reshape(1, d))
    return pl.pallas_call(
        functools.partial(_ffn_kernel, d_ff=d_ff, final=final, n_prompt_steps=n_steps, tiles_per_seq=tiles_per_seq),
        grid=(n_steps + 1,),
        in_specs=in_specs,
        out_specs=[pl.BlockSpec((tm, d), lambda i: (tile_of(i), 0)), pl.BlockSpec((ns, d), lambda i: (0, 0))],
        out_shape=[jax.ShapeDtypeStruct((rows, d), F32), jax.ShapeDtypeStruct((ns, d), F32)],
        scratch_shapes=[pltpu.VMEM((tm, d_ff), BF16), pltpu.VMEM(wi.shape, BF16), pltpu.VMEM(wo.shape, BF16)]
        + _stage_scratch(wi.shape[1], wide=True) + _stage_scratch(wo.shape[1]),
        compiler_params=_cparams(("arbitrary",)),
        name="ffn_final" if final else "ffn",
    )(*args)


def _rg_gates_block(u, wa, wi, ba, bi, lam):
    ub = u.astype(BF16)
    r = jax.nn.sigmoid(_dot(ub, wa) + ba)
    i_g = jax.nn.sigmoid(_dot(ub, wi) + bi)
    log_a = -RG_C * r * jax.nn.softplus(-lam)
    a = jnp.exp(log_a)
    b = jnp.sqrt(-jnp.tanh(log_a) * (a * a + 1.0)) * (i_g * u)
    return a, b


def _rg_gates(u, wa_ref, wi_ref, ba, bi, lam):
    parts = []
    for p in range(wa_ref.shape[0]):
        cols = slice(p * MXU_DIM, (p + 1) * MXU_DIM)
        parts.append(_rg_gates_block(u[:, cols], wa_ref[p], wi_ref[p], ba[:, cols], bi[:, cols], lam[:, cols]))
    return (jnp.concatenate([a for a, _ in parts], axis=-1), jnp.concatenate([b for _, b in parts], axis=-1))


def _rglru_front(xn_ref, w_rows_ref, lb_ref, tb_ref, tail_ref, conv_ref):
    nsq = tb_ref.shape[0]
    d = xn_ref.shape[1]
    tl = xn_ref.shape[0] // nsq
    seg = tl // SUBLANES
    nlb = d // LANES
    pre = (CONV_W - 1) * SUBLANES

    pitch = seg + SUBLANES
    x_rg = _dot_nt(xn_ref[...], w_rows_ref[0:d, :])
    for cb in range(nlb):
        for s in range(nsq * SUBLANES):
            lb_ref[cb, s * pitch:s * pitch + seg, :] = x_rg[s * seg:(s + 1) * seg, cb * LANES:(cb + 1) * LANES]
    first = lax.broadcasted_iota(jnp.int32, (SUBLANES, d), 0) == 0
    for q in range(nsq):
        for j in range(seg):
            for cb in range(nlb):
                tb_ref[q, pre + j * SUBLANES:pre + (j + 1) * SUBLANES, cb * LANES:(cb + 1) * LANES] = (
                    lb_ref[cb, pl.ds(q * SUBLANES * pitch + j, SUBLANES, stride=pitch), :])

        for i in range(CONV_W - 1):
            j = seg - (CONV_W - 1) + i
            slab = tb_ref[q, pre + j * SUBLANES:pre + (j + 1) * SUBLANES, :]
            tb_ref[q, i * SUBLANES:(i + 1) * SUBLANES, :] = jnp.where(first, tail_ref[q, i:i + 1, :],
                                                                     pltpu.roll(slab, 1, axis=0))
            last = slab[SUBLANES - 1:SUBLANES, :]
            tail_ref[q, i:i + 1, :] = last
            conv_ref[q, i:i + 1, :] = last


def _rglru_block(p, q, cw_ref, cb_ref, wa_ref, wi_ref, ba_ref, bi_ref, lam_ref,
                 lb_ref, tb_ref, a_ref, b_ref, h_ref, ha_ref):
    tl = a_ref.shape[1]
    seg = tl // SUBLANES
    cols = slice(p * MXU_DIM, (p + 1) * MXU_DIM)
    lane_blocks = range(p * MXU_DIM // LANES, (p + 1) * MXU_DIM // LANES)
    row0 = q * tl

    piece = tl // RG_PIECES
    for r0 in range(0, tl, piece):
        u = cb_ref[:, cols]
        for j in range(CONV_W):
            u = u + tb_ref[q, j * SUBLANES + r0:j * SUBLANES + r0 + piece, cols] * cw_ref[j:j + 1, cols]
        a, b = _rg_gates_block(u, wa_ref[p], wi_ref[p], ba_ref[:, cols], bi_ref[:, cols], lam_ref[:, cols])
        a_ref[q, r0:r0 + piece, cols] = a
        b_ref[q, r0:r0 + piece, cols] = b
        yield

    acc_a = a_ref[q, 0:SUBLANES, cols]
    acc_b = b_ref[q, 0:SUBLANES, cols]
    for j in range(1, seg):
        rows = slice(j * SUBLANES, (j + 1) * SUBLANES)
        a_j = a_ref[q, rows, cols]
        acc_b = a_j * acc_b + b_ref[q, rows, cols]
        acc_a = a_j * acc_a
        a_ref[q, rows, cols] = acc_a
        b_ref[q, rows, cols] = acc_b

    h = h_ref[q, :, cols]
    h_in = []
    for s in range(SUBLANES):
        h_in.append(h)
        h = acc_a[s:s + 1, :] * h + acc_b[s:s + 1, :]
    h_ref[q, :, cols] = h
    h_in = jnp.concatenate(h_in, axis=0)
    yield

    for j in range(seg):
        rows = slice(j * SUBLANES, (j + 1) * SUBLANES)
        h_j = a_ref[q, rows, cols] * h_in + b_ref[q, rows, cols]
        for i, cb in enumerate(lane_blocks):
            lb_ref[cb, row0 + j * SUBLANES:row0 + (j + 1) * SUBLANES, :] = h_j[:, i * LANES:(i + 1) * LANES]
    for s in range(SUBLANES):
        for cb in lane_blocks:
            ha_ref[row0 + s * seg:row0 + (s + 1) * seg, cb * LANES:(cb + 1) * LANES] = (
                lb_ref[cb, pl.ds(row0 + s, seg, stride=SUBLANES), :].astype(BF16))
    yield


def _mixer_kernel(x_ref, sh_ref, sc_ref, ga_ref, g_ref, wint_hbm, bif_ref, bift_ref,
                  wbra_hbm, wbrb_hbm, wout_hbm, cw_ref, cb_ref, wa_ref, wi_ref, ba_ref, bi_ref, lam_ref,
                  o_ref, c_ref, n_ref, m_ref, conv_ref, hl_ref,
                  hm_ref, wm_ref, wif_ref, wbra_ref, wbrb_ref, wout_ref, stage, sem,
                  lb_ref, tb_ref, a_ref, b_ref, tail_ref, h_ref, ha_ref, mg_ref, xn_ref, hb_ref, sa_ref):
    nsq, tl, d = x_ref.shape
    dk = d // M_HEADS
    cs = min(ML_CHUNK, tl)
    scale = dk ** -0.5
    gate_row0 = 5 * d

    @pl.when((pl.program_id(0) == 0) & (pl.program_id(1) == 0))
    def _():
        _stage_rows(wint_hbm, 0, 5 * d, wm_ref, 0, stage, sem)
        _stage_rows(wint_hbm, gate_row0 + 2 * M_HEADS, 2 * d, wm_ref, 5 * d, stage, sem)
        _stage_rows(wint_hbm, gate_row0, LANES, wif_ref, 0, stage, sem)
        _stage_rows(wbra_hbm, 0, d, wbra_ref, 0, stage, sem)
        _stage_rows(wbrb_hbm, 0, d, wbrb_ref, 0, stage, sem)
        _stage_rows(wout_hbm, 0, d, wout_ref, 0, stage, sem)

    @pl.when(pl.program_id(1) == 0)
    def _():
        c_ref[...] = jnp.zeros_like(c_ref)
        n_ref[...] = jnp.zeros_like(n_ref)
        m_ref[...] = jnp.zeros_like(m_ref)
        tail_ref[...] = jnp.zeros_like(tail_ref)
        h_ref[...] = jnp.zeros_like(h_ref)

    seq_rows = [pl.ds(pl.program_id(0) * nsq + q, 1) for q in range(nsq)]
    for q in range(nsq):
        xn_ref[q * tl:(q + 1) * tl, :] = _rms_mod(x_ref[q], g_ref[...], sh_ref[seq_rows[q], :],
                                                  sc_ref[seq_rows[q], :]).astype(BF16)

    _rglru_front(xn_ref, wm_ref, lb_ref, tb_ref, tail_ref, conv_ref)
    rglru_block = functools.partial(
        _rglru_block, cw_ref=cw_ref, cb_ref=cb_ref, wa_ref=wa_ref, wi_ref=wi_ref, ba_ref=ba_ref, bi_ref=bi_ref,
        lam_ref=lam_ref, lb_ref=lb_ref, tb_ref=tb_ref, a_ref=a_ref, b_ref=b_ref, h_ref=h_ref, ha_ref=ha_ref)
    n_rg = wa_ref.shape[0]
    wq0 = d

    rg_pieces = itertools.chain.from_iterable(rglru_block(p, sq) for p in range(n_rg) for sq in range(nsq))

    def rg_advance(n=1):
        for _ in range(n):
            next(rg_pieces, None)

    pre_c = _dot_nt(xn_ref[...], wif_ref[...]) + bif_ref[...]
    pre_r = _dot_nt(wif_ref[0:GATE_ROWS, :], xn_ref[...]) + bift_ref[...]
    col_is_f = lax.broadcasted_iota(jnp.int32, pre_c.shape, 1) >= M_HEADS
    row_is_f = lax.broadcasted_iota(jnp.int32, pre_r.shape, 0) >= M_HEADS
    gate_c = jnp.where(col_is_f, jax.nn.log_sigmoid(pre_c), pre_c)
    gate_r = jnp.where(row_is_f, jax.nn.log_sigmoid(pre_r), pre_r)

    ti = lax.broadcasted_iota(jnp.int32, (cs, cs), 0)
    si = lax.broadcasted_iota(jnp.int32, (cs, cs), 1)
    causal = si <= ti
    lower = causal.astype(BF16)
    upper = (ti <= si).astype(BF16)

    chunks = []
    for sq in range(nsq):
        for r0 in range(sq * tl, (sq + 1) * tl, cs):
            gc = gate_c[r0:r0 + cs, :]
            gr = gate_r[:, r0:r0 + cs]
            cum_c = sum(_dot(lower, part) for part in _split3(gc))
            cum_r = sum(_dot(part, upper) for part in _split3(gr))
            chunks.append((sq, r0, gc, gr, cum_c, cum_r))

    for h in range(M_HEADS):
        c0 = h * dk
        q_all = _dot_nt(xn_ref[...], wm_ref[wq0 + c0:wq0 + c0 + dk, :])
        k_all = _dot_nt(xn_ref[...], wm_ref[wq0 + d + c0:wq0 + d + c0 + dk, :])
        v_all = _dot_nt(xn_ref[...], wm_ref[wq0 + 2 * d + c0:wq0 + 2 * d + c0 + dk, :])
        rg_advance()
        for sq, r0, gc, gr, cum_c, cum_r in chunks:
            q = q_all[r0:r0 + cs, :]
            k = k_all[r0:r0 + cs, :]
            v = v_all[r0:r0 + cs, :]
            bc = cum_c[:, M_HEADS + h:M_HEADS + h + 1]
            br = cum_r[M_HEADS + h:M_HEADS + h + 1, :]
            ig_c = gc[:, h:h + 1]
            ig_r = gr[h:h + 1, :]
            m_prev = m_ref[sq, h:h + 1, :]
            c_prev = c_ref[sq, h]
            n_prev = n_ref[sq, h:h + 1, :]

            log_d = jnp.where(causal, bc - br + ig_r, -jnp.inf)
            log_past = bc + m_prev
            m_t = jnp.maximum(log_past, jnp.max(log_d, axis=-1, keepdims=True))
            d_mat = jnp.exp(log_d - m_t)
            past_w = jnp.exp(log_past - m_t)
            qs = q * scale
            qsb = qs.astype(BF16)
            kb = k.astype(BF16)
            vb = v.astype(BF16)
            s = _dot_nt(qsb, kb) * d_mat
            num = past_w * _dot(qsb, c_prev.astype(BF16)) + _dot(s.astype(BF16), vb)
            den = past_w * jnp.sum(qs * n_prev, axis=-1, keepdims=True) + jnp.sum(s, axis=-1, keepdims=True)
            hm_ref[r0:r0 + cs, c0:c0 + dk] = num / jnp.maximum(jnp.abs(den), jnp.exp(-m_t))

            m_new = m_t[cs - 1:cs, :]
            b_last = bc[cs - 1:cs, :]
            w_s = jnp.exp(b_last - bc + ig_c - m_new)
            decay = jnp.exp(b_last + m_prev - m_new)
            kw = w_s * k
            c_ref[sq, h] = decay * c_prev + _dot_tn(kw.astype(BF16), vb)
            n_ref[sq, h:h + 1, :] = decay * n_prev + jnp.sum(kw, axis=0, keepdims=True)
            m_ref[sq, h:h + 1, :] = m_new
            rg_advance()

    col_blocks = [slice(n0, n0 + MXU_DIM) for n0 in range(0, d, MXU_DIM)]
    for cols in col_blocks:
        o_pre = _dot_nt(xn_ref[...], wm_ref[4 * d + cols.start:4 * d + cols.stop, :])
        hb_ref[:, cols] = (jax.nn.sigmoid(o_pre) * hm_ref[:, cols]).astype(BF16)
        rg_advance()
    for cols in col_blocks:
        g_b = _dot_nt(xn_ref[...], wm_ref[6 * d + cols.start:6 * d + cols.stop, :])
        rg_advance()
        hm_ref[:, cols] = jax.nn.sigmoid(g_b) * _dot(hb_ref[...], wbrb_ref[:, cols])
        g_a = _dot_nt(xn_ref[...], wm_ref[5 * d + cols.start:5 * d + cols.stop, :])
        sa_ref[:, cols] = jax.nn.sigmoid(g_a)
        rg_advance()
    for _ in rg_pieces:
        pass
    hl_ref[...] = h_ref[...]
    for cols in col_blocks:
        merged = sa_ref[:, cols] * _dot(ha_ref[...], wbra_ref[:, cols]) + hm_ref[:, cols]
        mg_ref[:, cols] = merged.astype(BF16)
    y = _dot(mg_ref[...], wout_ref[...])
    for q in range(nsq):
        o_ref[q] = x_ref[q] + ga_ref[seq_rows[q], :] * y[q * tl:(q + 1) * tl, :]


def _mixer_prompt(x3, mod, gain, w_int, bif, bift, wbra, wbrb, wout, conv_w, conv_b, wa, wi, ba, bi, lam):
    nb, seq, _ = x3.shape
    d = x3.shape[2]
    dk = d // M_HEADS
    nsq = ML_SEQS
    tl = min(ML_ROWS, seq)
    rows = nsq * tl
    row = lambda v: v.reshape(1, d)
    mspecs, _ = _mod_specs(mod.shape[0] - nb, nb, d, 3, 3)
    tile = lambda: pl.BlockSpec((nsq, tl, d), lambda b, t: (b, t, 0))
    per_seq = lambda *shape: pl.BlockSpec((nsq,) + shape, lambda b, t: (b,) + (0,) * len(shape))
    return pl.pallas_call(
        _mixer_kernel,
        grid=(nb // nsq, seq // tl),
        in_specs=[tile()] + mspecs + [_resident((1, d)),
                  _HBM, _resident(bif.shape), _resident(bift.shape), _HBM, _HBM, _HBM,
                  _resident(conv_w.shape), _resident((1, d)), _resident(wa.shape), _resident(wi.shape),
                  _resident((1, d)), _resident((1, d)), _resident((1, d))],
        out_specs=[tile(), per_seq(M_HEADS, dk, dk), per_seq(M_HEADS, dk), per_seq(M_HEADS, 1),
                   per_seq(CONV_W - 1, d), per_seq(1, d)],
        out_shape=[jax.ShapeDtypeStruct((nb, seq, d), F32),
                   jax.ShapeDtypeStruct((nb, M_HEADS, dk, dk), F32),
                   jax.ShapeDtypeStruct((nb, M_HEADS, dk), F32),
                   jax.ShapeDtypeStruct((nb, M_HEADS, 1), F32),
                   jax.ShapeDtypeStruct((nb, CONV_W - 1, d), F32),
                   jax.ShapeDtypeStruct((nb, 1, d), F32)],
        scratch_shapes=[pltpu.VMEM((rows, d), F32),
                        pltpu.VMEM((7 * d, d), BF16),
                        pltpu.VMEM((LANES, d), BF16),
                        pltpu.VMEM((d, d), BF16), pltpu.VMEM((d, d), BF16), pltpu.VMEM((d, d), BF16)]
        + _stage_scratch(d)
        + [pltpu.VMEM((d // LANES, rows + nsq * SUBLANES * SUBLANES, LANES), F32),
           pltpu.VMEM((nsq, (CONV_W - 1) * SUBLANES + tl, d), F32),
           pltpu.VMEM((nsq, tl, d), F32),
           pltpu.VMEM((nsq, tl, d), F32),
           pltpu.VMEM((nsq, CONV_W - 1, d), F32),
           pltpu.VMEM((nsq, 1, d), F32),
           pltpu.VMEM((rows, d), BF16),
           pltpu.VMEM((rows, d), BF16),
           pltpu.VMEM((rows, d), BF16),
           pltpu.VMEM((rows, d), BF16),
           pltpu.VMEM((rows, d), F32)],
        compiler_params=_cparams(("arbitrary", "arbitrary")),
        name="mixer_prompt",
    )(x3, mod, mod, mod, row(gain), w_int, bif, bift, wbra, wbrb, wout,
      conv_w, row(conv_b), wa, wi, row(ba), row(bi), row(lam))


def _head_sum(x, dk):
    parts = []
    for h in range(x.shape[1] // dk):
        sl = x[:, h * dk:(h + 1) * dk]
        parts.append(jnp.broadcast_to(jnp.sum(sl, axis=-1, keepdims=True), sl.shape))
    return jnp.concatenate(parts, axis=-1)


def _head_spread(cols, lane0, dk, rows):
    return jnp.concatenate(
        [jnp.broadcast_to(cols[:, lane0 + h:lane0 + h + 1], (rows, dk)) for h in range(M_HEADS)], axis=-1)


def _dec_pre_kernel(x_ref, sh_ref, sc_ref, g_ref, wint_hbm, bif_ref,
                    cw_ref, cb_ref, wa_ref, wi_ref, ba_ref, bi_ref, lam_ref,
                    conv0_ref, h0_ref, n0_ref, m0_ref,
                    ha_ref, conv_ref, hs_ref, n_ref, m_ref, qt_ref, kt_ref, dec_ref, wv_ref, pw_ref, sv_ref,
                    den_ref, em_ref,
                    wm_ref, wif_ref, stage, sem):
    rows, d = x_ref.shape
    dk = d // M_HEADS
    scale = dk ** -0.5
    _stage_rows(wint_hbm, 0, 4 * d, wm_ref, 0, stage, sem)
    _stage_rows(wint_hbm, 5 * d, LANES, wif_ref, 0, stage, sem)
    xn = _rms_mod(x_ref[...], g_ref[...], sh_ref[...], sc_ref[...]).astype(BF16)

    x_rg = _dot_nt(xn, wm_ref[0:d, :])
    u = cb_ref[...]
    for j in range(CONV_W - 1):
        u = u + conv0_ref[j] * cw_ref[j:j + 1, :]
        if j > 0:
            conv_ref[j - 1] = conv0_ref[j]
    u = u + x_rg * cw_ref[CONV_W - 1:CONV_W, :]
    conv_ref[CONV_W - 2] = x_rg
    a, b = _rg_gates(u, wa_ref, wi_ref, ba_ref[...], bi_ref[...], lam_ref[...])
    h = a * h0_ref[...] + b
    hs_ref[...] = h
    ha_ref[...] = h

    q = _dot_nt(xn, wm_ref[d:2 * d, :])
    k = _dot_nt(xn, wm_ref[2 * d:3 * d, :])
    v = _dot_nt(xn, wm_ref[3 * d:4 * d, :])
    qt_ref[...] = _dot_nt(wm_ref[d:2 * d, :], xn).astype(BF16)
    kt_ref[...] = _dot_nt(wm_ref[2 * d:3 * d, :], xn).astype(BF16)
    pre = _dot_nt(xn, wif_ref[...]) + bif_ref[...]
    ig = _head_spread(pre, 0, dk, rows)
    lf = jax.nn.log_sigmoid(_head_spread(pre, M_HEADS, dk, rows))
    m0 = _head_spread(m0_ref[...], 0, dk, rows)
    n0 = n0_ref[...]
    log_past = lf + m0
    m_t = jnp.maximum(log_past, ig)
    d_w = jnp.exp(ig - m_t)
    past_w = jnp.exp(log_past - m_t)
    qs = q * scale
    s = _head_sum(qs * k, dk) * d_w
    den_ref[...] = past_w * _head_sum(qs * n0, dk) + s
    em_ref[...] = jnp.exp(-m_t)
    pw_ref[...] = past_w
    sv_ref[...] = s * v
    w_s = jnp.exp(ig - m_t)
    decay = jnp.exp(lf + m0 - m_t)
    dec_ref[...] = decay
    wv_ref[...] = w_s * v
    n_ref[...] = decay * n0 + w_s * k
    m_ref[...] = m_t


def _dec_pre(x2, mod, gain, w_int, bif, conv_w, conv_b, wa, wi, ba, bi, lam, conv0, h0, n0, m0):
    rows, d = x2.shape
    row = lambda v: v.reshape(1, d)
    m0p = jnp.pad(m0, ((0, 0), (0, LANES - M_HEADS)))
    full = lambda shape: pl.BlockSpec(shape, lambda i, _n=len(shape): (0,) * _n)
    mspec = lambda j: pl.BlockSpec((rows, d), lambda i: (0, j))
    vec = jax.ShapeDtypeStruct((rows, d), F32)
    outs = [vec,
            jax.ShapeDtypeStruct((CONV_W - 1, rows, d), F32),
            vec,
            vec,
            vec,
            jax.ShapeDtypeStruct((d, rows), BF16),
            jax.ShapeDtypeStruct((d, rows), BF16),
            vec, vec, vec, vec, vec, vec]
    args = [x2, mod, mod, row(gain), w_int, bif, conv_w, row(conv_b), wa, wi,
            row(ba), row(bi), row(lam), conv0, h0, n0, m0p]
    in_specs = ([full(x2.shape), mspec(3), mspec(4), full((1, d)), _HBM]
                + [full(a.shape) for a in args[5:]])
    return pl.pallas_call(
        _dec_pre_kernel,
        grid=(1,),
        in_specs=in_specs,
        out_specs=[full(o.shape) for o in outs],
        out_shape=outs,
        scratch_shapes=[pltpu.VMEM((4 * d, d), BF16), pltpu.VMEM((LANES, d), BF16)] + _stage_scratch(d),
        compiler_params=_cparams(("arbitrary",)),
        name="decode_pre",
    )(*args)


def _dec_mem_kernel(qt_ref, kt_ref, dec_ref, wv_ref, c0_ref, c_ref, qc_ref):
    bb = c0_ref.shape[0]
    dk = c0_ref.shape[2]
    nseq = qt_ref.shape[1]
    scale = dk ** -0.5
    base = pl.program_id(0) * bb
    seq_id = lax.broadcasted_iota(jnp.int32, (nseq, dk), 0)
    for j in range(bb):
        pick = (seq_id == base + j).astype(BF16)
        qcol = _dot(qt_ref[...], pick) * scale
        kcol = _dot(kt_ref[...], pick)
        seq = pl.ds(base + j, 1)
        for h in range(M_HEADS):
            cols = slice(h * dk, (h + 1) * dk)
            c0 = c0_ref[j, h]
            dec = dec_ref[seq, cols]
            wv = wv_ref[seq, cols]
            c_ref[j, h] = dec * c0 + kcol[cols, :] * wv
            qc_ref[seq, cols] = jnp.sum(qcol[cols, :] * c0, axis=0, keepdims=True)


def _dec_mem(qt, kt, decay, wv, c0):
    nseq, heads, dk, _ = c0.shape
    d = heads * dk
    bb = DEC_BLOCK if nseq % DEC_BLOCK == 0 else 1
    whole = lambda shape: pl.BlockSpec(shape, lambda i: (0, 0))
    return pl.pallas_call(
        _dec_mem_kernel,
        grid=(nseq // bb,),
        in_specs=[whole((d, nseq)), whole((d, nseq)), whole((nseq, d)), whole((nseq, d)),
                  pl.BlockSpec((bb, heads, dk, dk), lambda i: (i, 0, 0, 0))],
        out_specs=[pl.BlockSpec((bb, heads, dk, dk), lambda i: (i, 0, 0, 0)), whole((nseq, d))],
        out_shape=[jax.ShapeDtypeStruct(c0.shape, F32), jax.ShapeDtypeStruct((nseq, d), F32)],
        compiler_params=_cparams(("arbitrary",)),
        name="decode_mem",
    )(qt, kt, decay, wv, c0)


def _dec_post_kernel(x_ref, sh_ref, sc_ref, ga_ref, g_ref, wint_hbm, wbra_hbm, wbrb_hbm, wout_hbm,
                     ha_ref, qc_ref, pw_ref, sv_ref, den_ref, em_ref, o_ref,
                     wm_ref, wbra_ref, wbrb_ref, wout_ref, stage, sem):
    d = x_ref.shape[1]
    _stage_rows(wint_hbm, 4 * d, d, wm_ref, 0, stage, sem)
    _stage_rows(wint_hbm, 5 * d + 2 * M_HEADS, 2 * d, wm_ref, d, stage, sem)
    _stage_rows(wbra_hbm, 0, d, wbra_ref, 0, stage, sem)
    _stage_rows(wbrb_hbm, 0, d, wbrb_ref, 0, stage, sem)
    _stage_rows(wout_hbm, 0, d, wout_ref, 0, stage, sem)
    x = x_ref[...]
    xn = _rms_mod(x, g_ref[...], sh_ref[...], sc_ref[...]).astype(BF16)
    num = pw_ref[...] * qc_ref[...] + sv_ref[...]
    hm = num / jnp.maximum(jnp.abs(den_ref[...]), em_ref[...])
    hb = (jax.nn.sigmoid(_dot_nt(xn, wm_ref[0:d, :])) * hm).astype(BF16)
    g_a = _dot_nt(xn, wm_ref[d:2 * d, :])
    g_b = _dot_nt(xn, wm_ref[2 * d:3 * d, :])
    merged = (jax.nn.sigmoid(g_a) * _dot(ha_ref[...].astype(BF16), wbra_ref[...])
              + jax.nn.sigmoid(g_b) * _dot(hb, wbrb_ref[...]))
    o_ref[...] = x + ga_ref[...] * _dot(merged.astype(BF16), wout_ref[...])


def _dec_post(x2, mod, gain, w_int, wbra, wbrb, wout, ha, qc, pw, sv, den, em):
    rows, d = x2.shape
    full = lambda shape: pl.BlockSpec(shape, lambda i, _n=len(shape): (0,) * _n)
    mspec = lambda j: pl.BlockSpec((rows, d), lambda i: (0, j))
    args = [x2, mod, mod, mod, gain.reshape(1, d), w_int, wbra, wbrb, wout, ha, qc, pw, sv, den, em]
    in_specs = ([full(x2.shape), mspec(3), mspec(4), mspec(5), full((1, d))] + [_HBM] * 4
                + [full(a.shape) for a in args[9:]])
    return pl.pallas_call(
        _dec_post_kernel,
        grid=(1,),
        in_specs=in_specs,
        out_specs=full((rows, d)),
        out_shape=jax.ShapeDtypeStruct((rows, d), F32),
        scratch_shapes=[pltpu.VMEM((3 * d, d), BF16),
                        pltpu.VMEM((d, d), BF16), pltpu.VMEM((d, d), BF16), pltpu.VMEM((d, d), BF16)]
        + _stage_scratch(d),
        compiler_params=_cparams(("arbitrary",)),
        name="decode_post",
    )(*args)


def _pair_blocks(w):
    nblk, c, _ = w.shape
    z = jnp.zeros((nblk // 2, c, c), w.dtype)
    top = jnp.concatenate([w[0::2], z], axis=2)
    bot = jnp.concatenate([z, w[1::2]], axis=2)
    return jnp.concatenate([top, bot], axis=1).astype(BF16)


def kernel(x_prompt, x_sample, state_conv, state_rg_h, state_C, state_n, state_m, c_prompt, c_sample, w_ada, b_ada, g_norm1, w_ff1_in, w_ff1_out, g_norm2, w_in, conv_w, conv_b, w_rg_a, b_rg_a, w_rg_i, b_rg_i, rg_lambda, b_ig, b_fg, w_br_a, w_br_b, w_out, g_norm3, w_ff2_in, w_ff2_out, g_final):
    nb, seq, d = x_prompt.shape
    ns = x_sample.shape[0]
    depth = w_ada.shape[0]
    assert depth == 1 and x_sample.shape[1] == 1 and nb == SUBLANES
    assert seq % ML_ROWS == 0 and seq % FFN_ROWS == 0 and ML_ROWS % (SUBLANES * SUBLANES) == 0
    assert nb % ML_SEQS == 0
    heads, dk = M_HEADS, d // M_HEADS
    assert w_in.shape[2] == 7 * d + 2 * heads and GATE_ROWS >= 2 * heads

    w_int = w_in[0].T
    b_gates = jnp.concatenate([b_ig[0], b_fg[0]])
    bif = jnp.pad(b_gates, (0, LANES - 2 * heads)).reshape(1, LANES)
    bift = jnp.pad(b_gates, (0, GATE_ROWS - 2 * heads)).reshape(GATE_ROWS, 1)
    wa = _pair_blocks(w_rg_a[0])
    wi = _pair_blocks(w_rg_i[0])
    wf1i, wf1o, wf2i, wf2o = w_ff1_in[0], w_ff1_out[0], w_ff2_in[0], w_ff2_out[0]
    wbra, wbrb, wout = w_br_a[0], w_br_b[0], w_out[0]

    mod = _ada(jnp.concatenate([c_sample, c_prompt], axis=0), w_ada[0], b_ada[0])

    xp, xs = _ffn(x_prompt.reshape(nb * seq, d), x_sample.reshape(ns, d), mod, 0, g_norm1[0],
                  wf1i, wf1o, None, rows_per_seq=seq)

    xp, c_p, n_p, m_p, conv_p, h_p = _mixer_prompt(
        xp.reshape(nb, seq, d), mod, g_norm2[0], w_int, bif, bift, wbra, wbrb, wout, conv_w[0], conv_b[0],
        wa, wi, b_rg_a[0], b_rg_i[0], rg_lambda[0])
    xp = xp.reshape(nb * seq, d)

    conv0 = jnp.swapaxes(state_conv[0], 0, 1)
    (ha_s, conv_s, h_s, n_s, m_s, qt, kt, decay, wv, pw, sv, den, em) = _dec_pre(
        xs, mod, g_norm2[0], w_int, bif, conv_w[0], conv_b[0], wa, wi,
        b_rg_a[0], b_rg_i[0], rg_lambda[0], conv0, state_rg_h[0], state_n[0].reshape(ns, d), state_m[0])
    c_s, qc = _dec_mem(qt, kt, decay, wv, state_C[0])
    xs = _dec_post(xs, mod, g_norm2[0], w_int, wbra, wbrb, wout, ha_s, qc, pw, sv, den, em)

    yp, ys = _ffn(xp, xs, mod, 6, g_norm3[0], wf2i, wf2o, g_final, rows_per_seq=seq)

    return (yp.reshape(nb, seq, d), ys.reshape(ns, 1, d),
            conv_p[None], h_p.reshape(1, nb, d), c_p[None], n_p[None], m_p.reshape(1, nb, heads),
            jnp.swapaxes(conv_s, 0, 1)[None], h_s[None], c_s[None], n_s.reshape(1, ns, heads, dk), m_s[:, ::dk][None])
```

```python
import functools
import itertools

import jax
import jax.numpy as jnp
from jax import lax
from jax.experimental import pallas as pl
from jax.experimental.pallas import tpu as pltpu

F32 = jnp.float32
BF16 = jnp.bfloat16

EPS = 1e-6
RG_C = 8.0
CONV_W = 4
N_MOD = 9
M_HEADS = 4
RG_BLOCKS = 8

SUBLANES = 8
LANES = 128
MXU_DIM = 256
VMEM_LIMIT_BYTES = 60 * 1024 * 1024

ADA_STEPS = 4
FFN_ROWS = 1024
FFN_COLS = 256
ML_ROWS = 512
ML_SEQS = 1
ML_CHUNK = 256
RG_PIECES = 4
DEC_BLOCK = 4
STAGE_ROWS = 256
STAGE_SLOTS = 4
STAGE_SLOTS_DEC = 12
STAGE_ROWS_WIDE = 64
STAGE_SLOTS_WIDE = 6
GATE_ROWS = 16


def _cparams(sem):
    return pltpu.CompilerParams(dimension_semantics=sem, vmem_limit_bytes=VMEM_LIMIT_BYTES)


def _resident(shape):
    nd = len(shape)
    return pl.BlockSpec(shape, lambda *_: (0,) * nd, pipeline_mode=pl.Buffered(1))


_HBM = pl.BlockSpec(memory_space=pl.ANY)


def _rms_mod(x, gain, shift, scale):
    ms = jnp.mean(x * x, axis=-1, keepdims=True)
    return x * lax.rsqrt(ms + EPS) * gain * (1.0 + scale) + shift


def _dot(a, b):
    return jnp.dot(a, b, preferred_element_type=F32)


def _dot_nt(a, b):
    return lax.dot_general(a, b, (((1,), (1,)), ((), ())), preferred_element_type=F32)


def _dot_tn(a, b):
    return lax.dot_general(a, b, (((0,), (0,)), ((), ())), preferred_element_type=F32)


def _split3(x):
    hi = x.astype(BF16)
    r1 = x - hi.astype(F32)
    mid = r1.astype(BF16)
    lo = (r1 - mid.astype(F32)).astype(BF16)
    return hi, mid, lo


def _stage_jobs(jobs, stage, sem, extra_slots=()):
    _, rows_per_copy, _ = stage.shape
    ring = [stage.at[i] for i in range(stage.shape[0])] + list(extra_slots)
    slots = len(ring)
    assert sem.shape[0] >= slots
    chunks = []
    for src, src_row0, nrows, dst, dst_row0 in jobs:
        for r in range(0, nrows, rows_per_copy):
            chunks.append((src, src_row0 + r, dst, dst_row0 + r, min(rows_per_copy, nrows - r)))
    n = len(chunks)

    def copy(c):
        src, src_row, _, _, rows = chunks[c]
        return pltpu.make_async_copy(src.at[pl.ds(src_row, rows)],
                                     ring[c % slots].at[pl.ds(0, rows)], sem.at[c % slots])

    for c in range(min(slots, n)):
        copy(c).start(priority=c % 2)
    for c in range(n):
        _, _, dst, dst_row, rows = chunks[c]
        copy(c).wait()
        dst[dst_row:dst_row + rows, :] = ring[c % slots][0:rows, :].astype(BF16)
        if c + slots < n:
            copy(c + slots).start(priority=(c + slots) % 2)


def _stage_rows(src, src_row0, nrows, dst, dst_row0, stage, sem):
    _stage_jobs([(src, src_row0, nrows, dst, dst_row0)], stage, sem)


def _stage_scratch(cols, slots=None, rows=None, n_extra=0):
    slots = STAGE_SLOTS if slots is None else slots
    rows = STAGE_ROWS if rows is None else rows
    return [pltpu.VMEM((slots, rows, cols), F32), pltpu.SemaphoreType.DMA((slots + n_extra,))]


def _ada_kernel(c_ref, w_ref, b_ref, o_ref):
    c = c_ref[...]
    sc = (c * jax.nn.sigmoid(c)).astype(BF16)
    o_ref[...] = _dot(sc, w_ref[...].astype(BF16)) + b_ref[...]


def _ada(c, w_ada, b_ada):
    rows, d = c.shape
    n = w_ada.shape[1]
    tn = n // ADA_STEPS
    assert tn * ADA_STEPS == n and tn % LANES == 0
    return pl.pallas_call(
        _ada_kernel,
        grid=(n // tn,),
        in_specs=[pl.BlockSpec((rows, d), lambda j: (0, 0)),
                  pl.BlockSpec((d, tn), lambda j: (0, j)),
                  pl.BlockSpec((1, tn), lambda j: (0, j))],
        out_specs=pl.BlockSpec((rows, tn), lambda j: (0, j)),
        out_shape=jax.ShapeDtypeStruct((rows, n), F32),
        compiler_params=_cparams(("arbitrary",)),
        name="adaln_mod",
    )(c, w_ada, b_ada.reshape(1, n))


def _ffn_rows(x_ref, shift, scale, gate, g_ref, wi_ref, wo_ref, gf_ref, o_ref, act_ref, d_ff):
    rows = x_ref.shape[0]
    x = x_ref[...]
    xn = _rms_mod(x, g_ref[...], shift, scale).astype(BF16)
    for j in range(d_ff // FFN_COLS):
        lo = j * FFN_COLS
        hg = _dot(xn, wi_ref[:, lo:lo + FFN_COLS])
        hu = _dot(xn, wi_ref[:, d_ff + lo:d_ff + lo + FFN_COLS])
        act_ref[0:rows, lo:lo + FFN_COLS] = (hg * jax.nn.sigmoid(hg) * hu).astype(BF16)
    y = _dot(act_ref[0:rows, :], wo_ref[...])
    out = x + (0.5 * gate) * y
    if gf_ref is not None:
        ms = jnp.mean(out * out, axis=-1, keepdims=True)
        out = out * lax.rsqrt(ms + EPS) * gf_ref[...]
    o_ref[...] = out


def _ffn_kernel(*refs, d_ff, final, n_prompt_steps, tiles_per_seq):
    n_in = 13 if final else 12
    xp_ref, shp_ref, scp_ref, gap_ref, xs_ref, shs_ref, scs_ref, gas_ref, g_ref, wi_hbm, wo_hbm = refs[:11]
    gf_ref = refs[11] if final else None
    op_ref, os_ref = refs[n_in - 1:n_in + 1]
    act_ref, wi_ref, wo_ref, stage_i, sem_i, stage_o, sem_o = refs[-7:]
    step = pl.program_id(0)

    @pl.when(step == 0)
    def _():
        _stage_rows(wi_hbm, 0, wi_ref.shape[0], wi_ref, 0, stage_i, sem_i)
        _stage_rows(wo_hbm, 0, wo_ref.shape[0], wo_ref, 0, stage_o, sem_o)

    @pl.when(step < n_prompt_steps)
    def _():
        seq = pl.ds(step // tiles_per_seq, 1)
        _ffn_rows(xp_ref, shp_ref[seq, :], scp_ref[seq, :], gap_ref[seq, :], g_ref, wi_ref, wo_ref, gf_ref,
                  op_ref, act_ref, d_ff)

    @pl.when(step == n_prompt_steps)
    def _():
        _ffn_rows(xs_ref, shs_ref[...], scs_ref[...], gas_ref[...], g_ref, wi_ref, wo_ref, gf_ref,
                  os_ref, act_ref, d_ff)


def _mod_specs(ns, nb, d, j0, n):
    assert ns % nb == 0
    prompt = [pl.BlockSpec((nb, d), lambda *_, j=j: (ns // nb, j)) for j in range(j0, j0 + n)]
    sample = [pl.BlockSpec((ns, d), lambda *_, j=j: (0, j)) for j in range(j0, j0 + n)]
    return prompt, sample


def _ffn(xp, xs, mod, j0, gain, wi, wo, g_final, *, rows_per_seq):
    rows, d = xp.shape
    ns = xs.shape[0]
    d_ff = wo.shape[0]
    tm = min(FFN_ROWS, rows_per_seq)
    tiles_per_seq = rows_per_seq // tm
    n_steps = rows // tm
    tile_of = lambda i: jnp.minimum(i, n_steps - 1)
    mspec_p, mspec_s = _mod_specs(ns, rows // rows_per_seq, d, j0, 3)
    final = g_final is not None
    in_specs = ([pl.BlockSpec((tm, d), lambda i: (tile_of(i), 0))] + mspec_p
                + [pl.BlockSpec((ns, d), lambda i: (0, 0))] + mspec_s + [_resident((1, d)), _HBM, _HBM])
    args = [xp, mod, mod, mod, xs, mod, mod, mod, gain.reshape(1, d), wi, wo]
    if final:
        in_specs.append(_resident((1, d)))
        args.append(g_final.reshape(1, d))
    return pl.pallas_call(
        functools.partial(_ffn_kernel, d_ff=d_ff, final=final, n_prompt_steps=n_steps, tiles_per_seq=tiles_per_seq),
        grid=(n_steps + 1,),
        in_specs=in_specs,
        out_specs=[pl.BlockSpec((tm, d), lambda i: (tile_of(i), 0)), pl.BlockSpec((ns, d), lambda i: (0, 0))],
        out_shape=[jax.ShapeDtypeStruct((rows, d), F32), jax.ShapeDtypeStruct((ns, d), F32)],
        scratch_shapes=[pltpu.VMEM((tm, d_ff), BF16), pltpu.VMEM(wi.shape, BF16), pltpu.VMEM(wo.shape, BF16)]
        + _stage_scratch(wi.shape[1], STAGE_SLOTS_WIDE, STAGE_ROWS_WIDE) + _stage_scratch(wo.shape[1]),
        compiler_params=_cparams(("arbitrary",)),
        name="ffn_final" if final else "ffn",
    )(*args)


def _rg_gates_block(u, wa, wi, ba, bi, lam):
    ub = u.astype(BF16)
    r = jax.nn.sigmoid(_dot(ub, wa) + ba)
    i_g = jax.nn.sigmoid(_dot(ub, wi) + bi)
    log_a = -RG_C * r * jax.nn.softplus(-lam)
    a = jnp.exp(log_a)
    b = jnp.sqrt(-jnp.tanh(log_a) * (a * a + 1.0)) * (i_g * u)
    return a, b


def _rg_gates(u, wa_ref, wi_ref, ba, bi, lam):
    parts = []
    for p in range(wa_ref.shape[0]):
        cols = slice(p * MXU_DIM, (p + 1) * MXU_DIM)
        parts.append(_rg_gates_block(u[:, cols], wa_ref[p], wi_ref[p], ba[:, cols], bi[:, cols], lam[:, cols]))
    return (jnp.concatenate([a for a, _ in parts], axis=-1), jnp.concatenate([b for _, b in parts], axis=-1))


def _rglru_front(xn_ref, w_rows_ref, lb_ref, tb_ref, tail_ref, conv_ref):
    nsq = tb_ref.shape[0]
    d = xn_ref.shape[1]
    tl = xn_ref.shape[0] // nsq
    seg = tl // SUBLANES
    nlb = d // LANES
    pre = (CONV_W - 1) * SUBLANES

    pitch = seg + SUBLANES
    x_rg = _dot_nt(xn_ref[...], w_rows_ref[0:d, :])
    for cb in range(nlb):
        for s in range(nsq * SUBLANES):
            lb_ref[cb, s * pitch:s * pitch + seg, :] = x_rg[s * seg:(s + 1) * seg, cb * LANES:(cb + 1) * LANES]
    first = lax.broadcasted_iota(jnp.int32, (SUBLANES, d), 0) == 0
    for q in range(nsq):
        for j in range(seg):
            for cb in range(nlb):
                tb_ref[q, pre + j * SUBLANES:pre + (j + 1) * SUBLANES, cb * LANES:(cb + 1) * LANES] = (
                    lb_ref[cb, pl.ds(q * SUBLANES * pitch + j, SUBLANES, stride=pitch), :])

        for i in range(CONV_W - 1):
            j = seg - (CONV_W - 1) + i
            slab = tb_ref[q, pre + j * SUBLANES:pre + (j + 1) * SUBLANES, :]
            tb_ref[q, i * SUBLANES:(i + 1) * SUBLANES, :] = jnp.where(first, tail_ref[q, i:i + 1, :],
                                                                     pltpu.roll(slab, 1, axis=0))
            last = slab[SUBLANES - 1:SUBLANES, :]
            tail_ref[q, i:i + 1, :] = last
            conv_ref[q, i:i + 1, :] = last


def _rglru_block(p, q, cw_ref, cb_ref, wa_ref, wi_ref, ba_ref, bi_ref, lam_ref,
                 lb_ref, tb_ref, a_ref, b_ref, h_ref, ha_ref):
    tl = a_ref.shape[1]
    seg = tl // SUBLANES
    cols = slice(p * MXU_DIM, (p + 1) * MXU_DIM)
    lane_blocks = range(p * MXU_DIM // LANES, (p + 1) * MXU_DIM // LANES)
    row0 = q * tl

    piece = tl // RG_PIECES
    for r0 in range(0, tl, piece):
        u = cb_ref[:, cols]
        for j in range(CONV_W):
            u = u + tb_ref[q, j * SUBLANES + r0:j * SUBLANES + r0 + piece, cols] * cw_ref[j:j + 1, cols]
        a, b = _rg_gates_block(u, wa_ref[p], wi_ref[p], ba_ref[:, cols], bi_ref[:, cols], lam_ref[:, cols])
        a_ref[q, r0:r0 + piece, cols] = a
        b_ref[q, r0:r0 + piece, cols] = b
        yield

    acc_a = a_ref[q, 0:SUBLANES, cols]
    acc_b = b_ref[q, 0:SUBLANES, cols]
    for j in range(1, seg):
        rows = slice(j * SUBLANES, (j + 1) * SUBLANES)
        a_j = a_ref[q, rows, cols]
        acc_b = a_j * acc_b + b_ref[q, rows, cols]
        acc_a = a_j * acc_a
        a_ref[q, rows, cols] = acc_a
        b_ref[q, rows, cols] = acc_b

    h = h_ref[q, :, cols]
    h_in = []
    for s in range(SUBLANES):
        h_in.append(h)
        h = acc_a[s:s + 1, :] * h + acc_b[s:s + 1, :]
    h_ref[q, :, cols] = h
    h_in = jnp.concatenate(h_in, axis=0)
    yield

    for j in range(seg):
        rows = slice(j * SUBLANES, (j + 1) * SUBLANES)
        h_j = a_ref[q, rows, cols] * h_in + b_ref[q, rows, cols]
        for i, cb in enumerate(lane_blocks):
            lb_ref[cb, row0 + j * SUBLANES:row0 + (j + 1) * SUBLANES, :] = h_j[:, i * LANES:(i + 1) * LANES]
    for s in range(SUBLANES):
        for cb in lane_blocks:
            ha_ref[row0 + s * seg:row0 + (s + 1) * seg, cb * LANES:(cb + 1) * LANES] = (
                lb_ref[cb, pl.ds(row0 + s, seg, stride=SUBLANES), :].astype(BF16))
    yield


def _mixer_kernel(x_ref, sh_ref, sc_ref, ga_ref, g_ref, wint_hbm, bif_ref, bift_ref,
                  wbra_hbm, wbrb_hbm, wout_hbm, cw_ref, cb_ref, wa_ref, wi_ref, ba_ref, bi_ref, lam_ref,
                  o_ref, c_ref, n_ref, m_ref, conv_ref, hl_ref,
                  hm_ref, wm_ref, wif_ref, wbra_ref, wbrb_ref, wout_ref, stage, sem,
                  lb_ref, tb_ref, a_ref, b_ref, tail_ref, h_ref, ha_ref, mg_ref, xn_ref, hb_ref, sa_ref):
    nsq, tl, d = x_ref.shape
    dk = d // M_HEADS
    cs = min(ML_CHUNK, tl)
    scale = dk ** -0.5
    gate_row0 = 5 * d
    step_rows = stage.shape[1]
    idle_slots = ([buf.at[q, pl.ds(r, step_rows)] for buf in (a_ref, b_ref) for q in range(nsq)
                   for r in range(0, tl, step_rows)]
                  + [buf.at[pl.ds(r, step_rows)] for buf in (hm_ref, sa_ref) for r in range(0, nsq * tl, step_rows)])

    @pl.when((pl.program_id(0) == 0) & (pl.program_id(1) == 0))
    def _():
        _stage_jobs([(wint_hbm, 0, 5 * d, wm_ref, 0),
                     (wint_hbm, gate_row0, LANES, wif_ref, 0),
                     (wint_hbm, gate_row0 + 2 * M_HEADS, 2 * d, wm_ref, 5 * d),
                     (wbra_hbm, 0, d, wbra_ref, 0), (wbrb_hbm, 0, d, wbrb_ref, 0), (wout_hbm, 0, d, wout_ref, 0)],
                    stage, sem, extra_slots=idle_slots)

    @pl.when(pl.program_id(1) == 0)
    def _():
        c_ref[...] = jnp.zeros_like(c_ref)
        n_ref[...] = jnp.zeros_like(n_ref)
        m_ref[...] = jnp.zeros_like(m_ref)
        tail_ref[...] = jnp.zeros_like(tail_ref)
        h_ref[...] = jnp.zeros_like(h_ref)

    seq_rows = [pl.ds(pl.program_id(0) * nsq + q, 1) for q in range(nsq)]
    for q in range(nsq):
        xn_ref[q * tl:(q + 1) * tl, :] = _rms_mod(x_ref[q], g_ref[...], sh_ref[seq_rows[q], :],
                                                  sc_ref[seq_rows[q], :]).astype(BF16)

    _rglru_front(xn_ref, wm_ref, lb_ref, tb_ref, tail_ref, conv_ref)
    rglru_block = functools.partial(
        _rglru_block, cw_ref=cw_ref, cb_ref=cb_ref, wa_ref=wa_ref, wi_ref=wi_ref, ba_ref=ba_ref, bi_ref=bi_ref,
        lam_ref=lam_ref, lb_ref=lb_ref, tb_ref=tb_ref, a_ref=a_ref, b_ref=b_ref, h_ref=h_ref, ha_ref=ha_ref)
    n_rg = wa_ref.shape[0]
    wq0 = d

    rg_pieces = itertools.chain.from_iterable(rglru_block(p, sq) for p in range(n_rg) for sq in range(nsq))

    def rg_advance(n=1):
        for _ in range(n):
            next(rg_pieces, None)

    pre_c = _dot_nt(xn_ref[...], wif_ref[...]) + bif_ref[...]
    pre_r = _dot_nt(wif_ref[0:GATE_ROWS, :], xn_ref[...]) + bift_ref[...]
    col_is_f = lax.broadcasted_iota(jnp.int32, pre_c.shape, 1) >= M_HEADS
    row_is_f = lax.broadcasted_iota(jnp.int32, pre_r.shape, 0) >= M_HEADS
    gate_c = jnp.where(col_is_f, jax.nn.log_sigmoid(pre_c), pre_c)
    gate_r = jnp.where(row_is_f, jax.nn.log_sigmoid(pre_r), pre_r)

    ti = lax.broadcasted_iota(jnp.int32, (cs, cs), 0)
    si = lax.broadcasted_iota(jnp.int32, (cs, cs), 1)
    causal = si <= ti
    lower = causal.astype(BF16)
    upper = (ti <= si).astype(BF16)

    chunks = []
    for sq in range(nsq):
        for r0 in range(sq * tl, (sq + 1) * tl, cs):
            gc = gate_c[r0:r0 + cs, :]
            gr = gate_r[:, r0:r0 + cs]
            cum_c = sum(_dot(lower, part) for part in _split3(gc))
            cum_r = sum(_dot(part, upper) for part in _split3(gr))
            chunks.append((sq, r0, gc, gr, cum_c, cum_r))

    for h in range(M_HEADS):
        c0 = h * dk
        q_all = _dot_nt(xn_ref[...], wm_ref[wq0 + c0:wq0 + c0 + dk, :])
        k_all = _dot_nt(xn_ref[...], wm_ref[wq0 + d + c0:wq0 + d + c0 + dk, :])
        v_all = _dot_nt(xn_ref[...], wm_ref[wq0 + 2 * d + c0:wq0 + 2 * d + c0 + dk, :])
        rg_advance()
        for sq, r0, gc, gr, cum_c, cum_r in chunks:
            q = q_all[r0:r0 + cs, :]
            k = k_all[r0:r0 + cs, :]
            v = v_all[r0:r0 + cs, :]
            bc = cum_c[:, M_HEADS + h:M_HEADS + h + 1]
            br = cum_r[M_HEADS + h:M_HEADS + h + 1, :]
            ig_c = gc[:, h:h + 1]
            ig_r = gr[h:h + 1, :]
            m_prev = m_ref[sq, h:h + 1, :]
            c_prev = c_ref[sq, h]
            n_prev = n_ref[sq, h:h + 1, :]

            log_d = jnp.where(causal, bc - br + ig_r, -jnp.inf)
            log_past = bc + m_prev
            m_t = jnp.maximum(log_past, jnp.max(log_d, axis=-1, keepdims=True))
            d_mat = jnp.exp(log_d - m_t)
            past_w = jnp.exp(log_past - m_t)
            qs = q * scale
            qsb = qs.astype(BF16)
            kb = k.astype(BF16)
            vb = v.astype(BF16)
            s = _dot_nt(qsb, kb) * d_mat
            num = past_w * _dot(qsb, c_prev.astype(BF16)) + _dot(s.astype(BF16), vb)
            den = past_w * jnp.sum(qs * n_prev, axis=-1, keepdims=True) + jnp.sum(s, axis=-1, keepdims=True)
            hm_ref[r0:r0 + cs, c0:c0 + dk] = num / jnp.maximum(jnp.abs(den), jnp.exp(-m_t))

            m_new = m_t[cs - 1:cs, :]
            b_last = bc[cs - 1:cs, :]
            w_s = jnp.exp(b_last - bc + ig_c - m_new)
            decay = jnp.exp(b_last + m_prev - m_new)
            kw = w_s * k
            c_ref[sq, h] = decay * c_prev + _dot_tn(kw.astype(BF16), vb)
            n_ref[sq, h:h + 1, :] = decay * n_prev + jnp.sum(kw, axis=0, keepdims=True)
            m_ref[sq, h:h + 1, :] = m_new
            rg_advance()

    col_blocks = [slice(n0, n0 + MXU_DIM) for n0 in range(0, d, MXU_DIM)]
    for cols in col_blocks:
        o_pre = _dot_nt(xn_ref[...], wm_ref[4 * d + cols.start:4 * d + cols.stop, :])
        hb_ref[:, cols] = (jax.nn.sigmoid(o_pre) * hm_ref[:, cols]).astype(BF16)
        rg_advance()
    for cols in col_blocks:
        g_b = _dot_nt(xn_ref[...], wm_ref[6 * d + cols.start:6 * d + cols.stop, :])
        rg_advance()
        hm_ref[:, cols] = jax.nn.sigmoid(g_b) * _dot(hb_ref[...], wbrb_ref[:, cols])
        g_a = _dot_nt(xn_ref[...], wm_ref[5 * d + cols.start:5 * d + cols.stop, :])
        sa_ref[:, cols] = jax.nn.sigmoid(g_a)
        rg_advance()
    for _ in rg_pieces:
        pass
    hl_ref[...] = h_ref[...]
    for cols in col_blocks:
        merged = sa_ref[:, cols] * _dot(ha_ref[...], wbra_ref[:, cols]) + hm_ref[:, cols]
        mg_ref[:, cols] = merged.astype(BF16)
    y = _dot(mg_ref[...], wout_ref[...])
    for q in range(nsq):
        o_ref[q] = x_ref[q] + ga_ref[seq_rows[q], :] * y[q * tl:(q + 1) * tl, :]


def _mixer_prompt(x3, mod, gain, w_int, bif, bift, wbra, wbrb, wout, conv_w, conv_b, wa, wi, ba, bi, lam):
    nb, seq, _ = x3.shape
    d = x3.shape[2]
    dk = d // M_HEADS
    nsq = ML_SEQS
    tl = min(ML_ROWS, seq)
    rows = nsq * tl
    row = lambda v: v.reshape(1, d)
    mspecs, _ = _mod_specs(mod.shape[0] - nb, nb, d, 3, 3)
    tile = lambda: pl.BlockSpec((nsq, tl, d), lambda b, t: (b, t, 0))
    per_seq = lambda *shape: pl.BlockSpec((nsq,) + shape, lambda b, t: (b,) + (0,) * len(shape))
    return pl.pallas_call(
        _mixer_kernel,
        grid=(nb // nsq, seq // tl),
        in_specs=[tile()] + mspecs + [_resident((1, d)),
                  _HBM, _resident(bif.shape), _resident(bift.shape), _HBM, _HBM, _HBM,
                  _resident(conv_w.shape), _resident((1, d)), _resident(wa.shape), _resident(wi.shape),
                  _resident((1, d)), _resident((1, d)), _resident((1, d))],
        out_specs=[tile(), per_seq(M_HEADS, dk, dk), per_seq(M_HEADS, dk), per_seq(M_HEADS, 1),
                   per_seq(CONV_W - 1, d), per_seq(1, d)],
        out_shape=[jax.ShapeDtypeStruct((nb, seq, d), F32),
                   jax.ShapeDtypeStruct((nb, M_HEADS, dk, dk), F32),
                   jax.ShapeDtypeStruct((nb, M_HEADS, dk), F32),
                   jax.ShapeDtypeStruct((nb, M_HEADS, 1), F32),
                   jax.ShapeDtypeStruct((nb, CONV_W - 1, d), F32),
                   jax.ShapeDtypeStruct((nb, 1, d), F32)],
        scratch_shapes=[pltpu.VMEM((rows, d), F32),
                        pltpu.VMEM((7 * d, d), BF16),
                        pltpu.VMEM((LANES, d), BF16),
                        pltpu.VMEM((d, d), BF16), pltpu.VMEM((d, d), BF16), pltpu.VMEM((d, d), BF16)]
        + _stage_scratch(d, n_extra=4 * (rows // STAGE_ROWS))
        + [pltpu.VMEM((d // LANES, rows + nsq * SUBLANES * SUBLANES, LANES), F32),
           pltpu.VMEM((nsq, (CONV_W - 1) * SUBLANES + tl, d), F32),
           pltpu.VMEM((nsq, tl, d), F32),
           pltpu.VMEM((nsq, tl, d), F32),
           pltpu.VMEM((nsq, CONV_W - 1, d), F32),
           pltpu.VMEM((nsq, 1, d), F32),
           pltpu.VMEM((rows, d), BF16),
           pltpu.VMEM((rows, d), BF16),
           pltpu.VMEM((rows, d), BF16),
           pltpu.VMEM((rows, d), BF16),
           pltpu.VMEM((rows, d), F32)],
        compiler_params=_cparams(("arbitrary", "arbitrary")),
        name="mixer_prompt",
    )(x3, mod, mod, mod, row(gain), w_int, bif, bift, wbra, wbrb, wout,
      conv_w, row(conv_b), wa, wi, row(ba), row(bi), row(lam))


def _head_sum(x, dk):
    parts = []
    for h in range(x.shape[1] // dk):
        sl = x[:, h * dk:(h + 1) * dk]
        parts.append(jnp.broadcast_to(jnp.sum(sl, axis=-1, keepdims=True), sl.shape))
    return jnp.concatenate(parts, axis=-1)


def _head_spread(cols, lane0, dk, rows):
    return jnp.concatenate(
        [jnp.broadcast_to(cols[:, lane0 + h:lane0 + h + 1], (rows, dk)) for h in range(M_HEADS)], axis=-1)


def _dec_pre_kernel(x_ref, sh_ref, sc_ref, g_ref, wint_hbm, bif_ref,
                    cw_ref, cb_ref, wa_ref, wi_ref, ba_ref, bi_ref, lam_ref,
                    conv0_ref, h0_ref, n0_ref, m0_ref,
                    ha_ref, conv_ref, hs_ref, n_ref, m_ref, qt_ref, kt_ref, dec_ref, wv_ref, pw_ref, sv_ref,
                    den_ref, em_ref,
                    wm_ref, wif_ref, stage, sem):
    rows, d = x_ref.shape
    dk = d // M_HEADS
    scale = dk ** -0.5
    _stage_jobs([(wint_hbm, 0, 4 * d, wm_ref, 0),
                 (wint_hbm, 5 * d, LANES, wif_ref, 0)], stage, sem)
    xn = _rms_mod(x_ref[...], g_ref[...], sh_ref[...], sc_ref[...]).astype(BF16)

    x_rg = _dot_nt(xn, wm_ref[0:d, :])
    u = cb_ref[...]
    for j in range(CONV_W - 1):
        u = u + conv0_ref[j] * cw_ref[j:j + 1, :]
        if j > 0:
            conv_ref[j - 1] = conv0_ref[j]
    u = u + x_rg * cw_ref[CONV_W - 1:CONV_W, :]
    conv_ref[CONV_W - 2] = x_rg
    a, b = _rg_gates(u, wa_ref, wi_ref, ba_ref[...], bi_ref[...], lam_ref[...])
    h = a * h0_ref[...] + b
    hs_ref[...] = h
    ha_ref[...] = h

    q = _dot_nt(xn, wm_ref[d:2 * d, :])
    k = _dot_nt(xn, wm_ref[2 * d:3 * d, :])
    v = _dot_nt(xn, wm_ref[3 * d:4 * d, :])
    qt_ref[...] = _dot_nt(wm_ref[d:2 * d, :], xn).astype(BF16)
    kt_ref[...] = _dot_nt(wm_ref[2 * d:3 * d, :], xn).astype(BF16)
    pre = _dot_nt(xn, wif_ref[...]) + bif_ref[...]
    ig = _head_spread(pre, 0, dk, rows)
    lf = jax.nn.log_sigmoid(_head_spread(pre, M_HEADS, dk, rows))
    m0 = _head_spread(m0_ref[...], 0, dk, rows)
    n0 = n0_ref[...]
    log_past = lf + m0
    m_t = jnp.maximum(log_past, ig)
    d_w = jnp.exp(ig - m_t)
    past_w = jnp.exp(log_past - m_t)
    qs = q * scale
    s = _head_sum(qs * k, dk) * d_w
    den_ref[...] = past_w * _head_sum(qs * n0, dk) + s
    em_ref[...] = jnp.exp(-m_t)
    pw_ref[...] = past_w
    sv_ref[...] = s * v
    w_s = jnp.exp(ig - m_t)
    decay = jnp.exp(lf + m0 - m_t)
    dec_ref[...] = decay
    wv_ref[...] = w_s * v
    n_ref[...] = decay * n0 + w_s * k
    m_ref[...] = m_t


def _dec_pre(x2, mod, gain, w_int, bif, conv_w, conv_b, wa, wi, ba, bi, lam, conv0, h0, n0, m0):
    rows, d = x2.shape
    row = lambda v: v.reshape(1, d)
    m0p = jnp.pad(m0, ((0, 0), (0, LANES - M_HEADS)))
    full = lambda shape: pl.BlockSpec(shape, lambda i, _n=len(shape): (0,) * _n)
    mspec = lambda j: pl.BlockSpec((rows, d), lambda i: (0, j))
    vec = jax.ShapeDtypeStruct((rows, d), F32)
    outs = [vec,
            jax.ShapeDtypeStruct((CONV_W - 1, rows, d), F32),
            vec,
            vec,
            vec,
            jax.ShapeDtypeStruct((d, rows), BF16),
            jax.ShapeDtypeStruct((d, rows), BF16),
            vec, vec, vec, vec, vec, vec]
    args = [x2, mod, mod, row(gain), w_int, bif, conv_w, row(conv_b), wa, wi,
            row(ba), row(bi), row(lam), conv0, h0, n0, m0p]
    in_specs = ([full(x2.shape), mspec(3), mspec(4), full((1, d)), _HBM]
                + [full(a.shape) for a in args[5:]])
    return pl.pallas_call(
        _dec_pre_kernel,
        grid=(1,),
        in_specs=in_specs,
        out_specs=[full(o.shape) for o in outs],
        out_shape=outs,
        scratch_shapes=[pltpu.VMEM((4 * d, d), BF16), pltpu.VMEM((LANES, d), BF16)]
        + _stage_scratch(d, STAGE_SLOTS_DEC),
        compiler_params=_cparams(("arbitrary",)),
        name="decode_pre",
    )(*args)


def _dec_mem_kernel(qt_ref, kt_ref, dec_ref, wv_ref, c0_ref, c_ref, qc_ref):
    bb = c0_ref.shape[0]
    dk = c0_ref.shape[2]
    nseq = qt_ref.shape[1]
    scale = dk ** -0.5
    base = pl.program_id(0) * bb
    seq_id = lax.broadcasted_iota(jnp.int32, (nseq, dk), 0)
    for j in range(bb):
        pick = (seq_id == base + j).astype(BF16)
        qcol = _dot(qt_ref[...], pick) * scale
        kcol = _dot(kt_ref[...], pick)
        seq = pl.ds(base + j, 1)
        for h in range(M_HEADS):
            cols = slice(h * dk, (h + 1) * dk)
            c0 = c0_ref[j, h]
            dec = dec_ref[seq, cols]
            wv = wv_ref[seq, cols]
            c_ref[j, h] = dec * c0 + kcol[cols, :] * wv
            qc_ref[seq, cols] = jnp.sum(qcol[cols, :] * c0, axis=0, keepdims=True)


def _dec_mem(qt, kt, decay, wv, c0):
    nseq, heads, dk, _ = c0.shape
    d = heads * dk
    bb = DEC_BLOCK if nseq % DEC_BLOCK == 0 else 1
    whole = lambda shape: pl.BlockSpec(shape, lambda i: (0, 0))
    return pl.pallas_call(
        _dec_mem_kernel,
        grid=(nseq // bb,),
        in_specs=[whole((d, nseq)), whole((d, nseq)), whole((nseq, d)), whole((nseq, d)),
                  pl.BlockSpec((bb, heads, dk, dk), lambda i: (i, 0, 0, 0))],
        out_specs=[pl.BlockSpec((bb, heads, dk, dk), lambda i: (i, 0, 0, 0)), whole((nseq, d))],
        out_shape=[jax.ShapeDtypeStruct(c0.shape, F32), jax.ShapeDtypeStruct((nseq, d), F32)],
        compiler_params=_cparams(("arbitrary",)),
        name="decode_mem",
    )(qt, kt, decay, wv, c0)


def _dec_post_kernel(x_ref, sh_ref, sc_ref, ga_ref, g_ref, wint_hbm, wbra_hbm, wbrb_hbm, wout_hbm,
                     ha_ref, qc_ref, pw_ref, sv_ref, den_ref, em_ref, o_ref,
                     wm_ref, wbra_ref, wbrb_ref, wout_ref, stage, sem):
    d = x_ref.shape[1]
    _stage_jobs([(wint_hbm, 4 * d, d, wm_ref, 0),
                 (wint_hbm, 5 * d + 2 * M_HEADS, 2 * d, wm_ref, d),
                 (wbra_hbm, 0, d, wbra_ref, 0), (wbrb_hbm, 0, d, wbrb_ref, 0), (wout_hbm, 0, d, wout_ref, 0)],
                stage, sem)
    x = x_ref[...]
    xn = _rms_mod(x, g_ref[...], sh_ref[...], sc_ref[...]).astype(BF16)
    num = pw_ref[...] * qc_ref[...] + sv_ref[...]
    hm = num / jnp.maximum(jnp.abs(den_ref[...]), em_ref[...])
    hb = (jax.nn.sigmoid(_dot_nt(xn, wm_ref[0:d, :])) * hm).astype(BF16)
    g_a = _dot_nt(xn, wm_ref[d:2 * d, :])
    g_b = _dot_nt(xn, wm_ref[2 * d:3 * d, :])
    merged = (jax.nn.sigmoid(g_a) * _dot(ha_ref[...].astype(BF16), wbra_ref[...])
              + jax.nn.sigmoid(g_b) * _dot(hb, wbrb_ref[...]))
    o_ref[...] = x + ga_ref[...] * _dot(merged.astype(BF16), wout_ref[...])


def _dec_post(x2, mod, gain, w_int, wbra, wbrb, wout, ha, qc, pw, sv, den, em):
    rows, d = x2.shape
    full = lambda shape: pl.BlockSpec(shape, lambda i, _n=len(shape): (0,) * _n)
    mspec = lambda j: pl.BlockSpec((rows, d), lambda i: (0, j))
    args = [x2, mod, mod, mod, gain.reshape(1, d), w_int, wbra, wbrb, wout, ha, qc, pw, sv, den, em]
    in_specs = ([full(x2.shape), mspec(3), mspec(4), mspec(5), full((1, d))] + [_HBM] * 4
                + [full(a.shape) for a in args[9:]])
    return pl.pallas_call(
        _dec_post_kernel,
        grid=(1,),
        in_specs=in_specs,
        out_specs=full((rows, d)),
        out_shape=jax.ShapeDtypeStruct((rows, d), F32),
        scratch_shapes=[pltpu.VMEM((3 * d, d), BF16),
                        pltpu.VMEM((d, d), BF16), pltpu.VMEM((d, d), BF16), pltpu.VMEM((d, d), BF16)]
        + _stage_scratch(d, STAGE_SLOTS_DEC),
        compiler_params=_cparams(("arbitrary",)),
        name="decode_post",
    )(*args)


def _pair_blocks(w):
    nblk, c, _ = w.shape
    z = jnp.zeros((nblk // 2, c, c), w.dtype)
    top = jnp.concatenate([w[0::2], z], axis=2)
    bot = jnp.concatenate([z, w[1::2]], axis=2)
    return jnp.concatenate([top, bot], axis=1).astype(BF16)


def kernel(x_prompt, x_sample, state_conv, state_rg_h, state_C, state_n, state_m, c_prompt, c_sample, w_ada, b_ada, g_norm1, w_ff1_in, w_ff1_out, g_norm2, w_in, conv_w, conv_b, w_rg_a, b_rg_a, w_rg_i, b_rg_i, rg_lambda, b_ig, b_fg, w_br_a, w_br_b, w_out, g_norm3, w_ff2_in, w_ff2_out, g_final):
    nb, seq, d = x_prompt.shape
    ns = x_sample.shape[0]
    depth = w_ada.shape[0]
    assert depth == 1 and x_sample.shape[1] == 1 and nb == SUBLANES
    assert seq % ML_ROWS == 0 and seq % FFN_ROWS == 0 and ML_ROWS % (SUBLANES * SUBLANES) == 0
    assert nb % ML_SEQS == 0
    heads, dk = M_HEADS, d // M_HEADS
    assert w_in.shape[2] == 7 * d + 2 * heads and GATE_ROWS >= 2 * heads

    w_int = w_in[0].T
    b_gates = jnp.concatenate([b_ig[0], b_fg[0]])
    bif = jnp.pad(b_gates, (0, LANES - 2 * heads)).reshape(1, LANES)
    bift = jnp.pad(b_gates, (0, GATE_ROWS - 2 * heads)).reshape(GATE_ROWS, 1)
    wa = _pair_blocks(w_rg_a[0])
    wi = _pair_blocks(w_rg_i[0])
    wf1i, wf1o, wf2i, wf2o = w_ff1_in[0], w_ff1_out[0], w_ff2_in[0], w_ff2_out[0]
    wbra, wbrb, wout = w_br_a[0], w_br_b[0], w_out[0]

    mod = _ada(jnp.concatenate([c_sample, c_prompt], axis=0), w_ada[0], b_ada[0])

    xp, xs = _ffn(x_prompt.reshape(nb * seq, d), x_sample.reshape(ns, d), mod, 0, g_norm1[0],
                  wf1i, wf1o, None, rows_per_seq=seq)

    xp, c_p, n_p, m_p, conv_p, h_p = _mixer_prompt(
        xp.reshape(nb, seq, d), mod, g_norm2[0], w_int, bif, bift, wbra, wbrb, wout, conv_w[0], conv_b[0],
        wa, wi, b_rg_a[0], b_rg_i[0], rg_lambda[0])
    xp = xp.reshape(nb * seq, d)

    conv0 = jnp.swapaxes(state_conv[0], 0, 1)
    (ha_s, conv_s, h_s, n_s, m_s, qt, kt, decay, wv, pw, sv, den, em) = _dec_pre(
        xs, mod, g_norm2[0], w_int, bif, conv_w[0], conv_b[0], wa, wi,
        b_rg_a[0], b_rg_i[0], rg_lambda[0], conv0, state_rg_h[0], state_n[0].reshape(ns, d), state_m[0])
    c_s, qc = _dec_mem(qt, kt, decay, wv, state_C[0])
    xs = _dec_post(xs, mod, g_norm2[0], w_int, wbra, wbrb, wout, ha_s, qc, pw, sv, den, em)

    yp, ys = _ffn(xp, xs, mod, 6, g_norm3[0], wf2i, wf2o, g_final, rows_per_seq=seq)

    return (yp.reshape(nb, seq, d), ys.reshape(ns, 1, d),
            conv_p[None], h_p.reshape(1, nb, d), c_p[None], n_p[None], m_p.reshape(1, nb, heads),
            jnp.swapaxes(conv_s, 0, 1)[None], h_s[None], c_s[None], n_s.reshape(1, ns, heads, dk), m_s[:, ::dk][None])
```

```python
import functools
import itertools

import jax
import jax.numpy as jnp
from jax import lax
from jax.experimental import pallas as pl
from jax.experimental.pallas import tpu as pltpu

F32 = jnp.float32
BF16 = jnp.bfloat16

EPS = 1e-6
RG_C = 8.0
CONV_W = 4
N_MOD = 9
M_HEADS = 4
RG_BLOCKS = 8

SUBLANES = 8
LANES = 128
MXU_DIM = 256
VMEM_LIMIT_BYTES = 60 * 1024 * 1024

ADA_STEPS = 4
FFN_ROWS = 1024
FFN_COLS = 256
ML_ROWS = 512
ML_SEQS = 1
ML_CHUNK = 256
RG_PIECES = 4
DEC_BLOCK = 4
STAGE_ROWS = 256
STAGE_SLOTS = 4
STAGE_SLOTS_DEC = 12
STAGE_ROWS_WIDE = 64
STAGE_SLOTS_WIDE = 6
GATE_ROWS = 16


def _cparams(sem):
    return pltpu.CompilerParams(dimension_semantics=sem, vmem_limit_bytes=VMEM_LIMIT_BYTES)


def _resident(shape):
    nd = len(shape)
    return pl.BlockSpec(shape, lambda *_: (0,) * nd, pipeline_mode=pl.Buffered(1))


_HBM = pl.BlockSpec(memory_space=pl.ANY)


def _rms_mod(x, gain, shift, scale):
    ms = jnp.mean(x * x, axis=-1, keepdims=True)
    return x * lax.rsqrt(ms + EPS) * gain * (1.0 + scale) + shift


def _dot(a, b):
    return jnp.dot(a, b, preferred_element_type=F32)


def _dot_nt(a, b):
    return lax.dot_general(a, b, (((1,), (1,)), ((), ())), preferred_element_type=F32)


def _dot_tn(a, b):
    return lax.dot_general(a, b, (((0,), (0,)), ((), ())), preferred_element_type=F32)


def _split3(x):
    hi = x.astype(BF16)
    r1 = x - hi.astype(F32)
    mid = r1.astype(BF16)
    lo = (r1 - mid.astype(F32)).astype(BF16)
    return hi, mid, lo


def _stage_jobs(jobs, stage, sem, extra_slots=()):
    _, rows_per_copy, _ = stage.shape
    ring = [stage.at[i] for i in range(stage.shape[0])] + list(extra_slots)
    slots = len(ring)
    assert sem.shape[0] >= slots
    chunks = []
    for src, src_row0, nrows, dst, dst_row0 in jobs:
        for r in range(0, nrows, rows_per_copy):
            chunks.append((src, src_row0 + r, dst, dst_row0 + r, min(rows_per_copy, nrows - r)))
    n = len(chunks)

    def copy(c):
        src, src_row, _, _, rows = chunks[c]
        return pltpu.make_async_copy(src.at[pl.ds(src_row, rows)],
                                     ring[c % slots].at[pl.ds(0, rows)], sem.at[c % slots])

    for c in range(min(slots, n)):
        copy(c).start(priority=c % 2)
    for c in range(n):
        _, _, dst, dst_row, rows = chunks[c]
        copy(c).wait()
        dst[dst_row:dst_row + rows, :] = ring[c % slots][0:rows, :].astype(BF16)
        if c + slots < n:
            copy(c + slots).start(priority=(c + slots) % 2)


def _stage_rows(src, src_row0, nrows, dst, dst_row0, stage, sem):
    _stage_jobs([(src, src_row0, nrows, dst, dst_row0)], stage, sem)


def _stage_scratch(cols, slots=None, rows=None, n_extra=0):
    slots = STAGE_SLOTS if slots is None else slots
    rows = STAGE_ROWS if rows is None else rows
    return [pltpu.VMEM((slots, rows, cols), F32), pltpu.SemaphoreType.DMA((slots + n_extra,))]


def _ada_kernel(cs_ref, cp_ref, w_ref, b_ref, o_ref):
    ns = cs_ref.shape[0]
    w = w_ref[...].astype(BF16)
    for c_ref, rows in ((cs_ref, slice(0, ns)), (cp_ref, slice(ns, o_ref.shape[0]))):
        c = c_ref[...]
        o_ref[rows, :] = _dot((c * jax.nn.sigmoid(c)).astype(BF16), w) + b_ref[...]


def _ada(c_sample, c_prompt, w_ada, b_ada):
    ns, d = c_sample.shape
    nb = c_prompt.shape[0]
    n = w_ada.shape[1]
    tn = n // ADA_STEPS
    assert tn * ADA_STEPS == n and tn % LANES == 0 and ns % SUBLANES == 0
    return pl.pallas_call(
        _ada_kernel,
        grid=(n // tn,),
        in_specs=[pl.BlockSpec((ns, d), lambda j: (0, 0)),
                  pl.BlockSpec((nb, d), lambda j: (0, 0)),
                  pl.BlockSpec((d, tn), lambda j: (0, j)),
                  pl.BlockSpec((1, tn), lambda j: (0, j))],
        out_specs=pl.BlockSpec((ns + nb, tn), lambda j: (0, j)),
        out_shape=jax.ShapeDtypeStruct((ns + nb, n), F32),
        compiler_params=_cparams(("arbitrary",)),
        name="adaln_mod",
    )(c_sample, c_prompt, w_ada, b_ada.reshape(1, n))


def _ffn_rows(x_ref, shift, scale, gate, g_ref, wi_ref, wo_ref, gf_ref, o_ref, act_ref, d_ff):
    rows = x_ref.shape[0]
    x = x_ref[...]
    xn = _rms_mod(x, g_ref[...], shift, scale).astype(BF16)
    for j in range(d_ff // FFN_COLS):
        lo = j * FFN_COLS
        hg = _dot(xn, wi_ref[:, lo:lo + FFN_COLS])
        hu = _dot(xn, wi_ref[:, d_ff + lo:d_ff + lo + FFN_COLS])
        act_ref[0:rows, lo:lo + FFN_COLS] = (hg * jax.nn.sigmoid(hg) * hu).astype(BF16)
    y = _dot(act_ref[0:rows, :], wo_ref[...])
    out = x + (0.5 * gate) * y
    if gf_ref is not None:
        ms = jnp.mean(out * out, axis=-1, keepdims=True)
        out = out * lax.rsqrt(ms + EPS) * gf_ref[...]
    o_ref[...] = out


def _ffn_kernel(*refs, d_ff, final, n_prompt_steps, tiles_per_seq):
    n_in = 13 if final else 12
    xp_ref, shp_ref, scp_ref, gap_ref, xs_ref, shs_ref, scs_ref, gas_ref, g_ref, wi_hbm, wo_hbm = refs[:11]
    gf_ref = refs[11] if final else None
    op_ref, os_ref = refs[n_in - 1:n_in + 1]
    act_ref, wi_ref, wo_ref, stage_i, sem_i, stage_o, sem_o = refs[-7:]
    step = pl.program_id(0)

    @pl.when(step == 0)
    def _():
        _stage_rows(wi_hbm, 0, wi_ref.shape[0], wi_ref, 0, stage_i, sem_i)
        _stage_rows(wo_hbm, 0, wo_ref.shape[0], wo_ref, 0, stage_o, sem_o)

    @pl.when(step < n_prompt_steps)
    def _():
        seq = pl.ds(step // tiles_per_seq, 1)
        _ffn_rows(xp_ref, shp_ref[seq, :], scp_ref[seq, :], gap_ref[seq, :], g_ref, wi_ref, wo_ref, gf_ref,
                  op_ref, act_ref, d_ff)

    @pl.when(step == n_prompt_steps)
    def _():
        _ffn_rows(xs_ref, shs_ref[...], scs_ref[...], gas_ref[...], g_ref, wi_ref, wo_ref, gf_ref,
                  os_ref, act_ref, d_ff)


def _mod_specs(ns, nb, d, j0, n):
    assert ns % nb == 0
    prompt = [pl.BlockSpec((nb, d), lambda *_, j=j: (ns // nb, j)) for j in range(j0, j0 + n)]
    sample = [pl.BlockSpec((ns, d), lambda *_, j=j: (0, j)) for j in range(j0, j0 + n)]
    return prompt, sample


def _ffn(xp, xs, mod, j0, gain, wi, wo, g_final, *, rows_per_seq):
    rows, d = xp.shape
    ns = xs.shape[0]
    d_ff = wo.shape[0]
    tm = min(FFN_ROWS, rows_per_seq)
    tiles_per_seq = rows_per_seq // tm
    n_steps = rows // tm
    tile_of = lambda i: jnp.minimum(i, n_steps - 1)
    mspec_p, mspec_s = _mod_specs(ns, rows // rows_per_seq, d, j0, 3)
    final = g_final is not None
    in_specs = ([pl.BlockSpec((tm, d), lambda i: (tile_of(i), 0))] + mspec_p
                + [pl.BlockSpec((ns, d), lambda i: (0, 0))] + mspec_s + [_resident((1, d)), _HBM, _HBM])
    args = [xp, mod, mod, mod, xs, mod, mod, mod, gain.reshape(1, d), wi, wo]
    if final:
        in_specs.append(_resident((1, d)))
        args.append(g_final.reshape(1, d))
    return pl.pallas_call(
        functools.partial(_ffn_kernel, d_ff=d_ff, final=final, n_prompt_steps=n_steps, tiles_per_seq=tiles_per_seq),
        grid=(n_steps + 1,),
        in_specs=in_specs,
        out_specs=[pl.BlockSpec((tm, d), lambda i: (tile_of(i), 0)), pl.BlockSpec((ns, d), lambda i: (0, 0))],
        out_shape=[jax.ShapeDtypeStruct((rows, d), F32), jax.ShapeDtypeStruct((ns, d), F32)],
        scratch_shapes=[pltpu.VMEM((tm, d_ff), BF16), pltpu.VMEM(wi.shape, BF16), pltpu.VMEM(wo.shape, BF16)]
        + _stage_scratch(wi.shape[1], STAGE_SLOTS_WIDE, STAGE_ROWS_WIDE) + _stage_scratch(wo.shape[1]),
        compiler_params=_cparams(("arbitrary",)),
        name="ffn_final" if final else "ffn",
    )(*args)


def _pair_blocks(src_ref, dst_ref):
    nblk, c, _ = src_ref.shape
    dst_ref[...] = jnp.zeros(dst_ref.shape, dst_ref.dtype)
    for p in range(nblk // 2):
        dst_ref[p, 0:c, 0:c] = src_ref[2 * p].astype(BF16)
        dst_ref[p, c:2 * c, c:2 * c] = src_ref[2 * p + 1].astype(BF16)


def _rg_gates_block(u, wa, wi, ba, bi, lam):
    ub = u.astype(BF16)
    r = jax.nn.sigmoid(_dot(ub, wa) + ba)
    i_g = jax.nn.sigmoid(_dot(ub, wi) + bi)
    log_a = -RG_C * r * jax.nn.softplus(-lam)
    a = jnp.exp(log_a)
    b = jnp.sqrt(-jnp.tanh(log_a) * (a * a + 1.0)) * (i_g * u)
    return a, b


def _rg_gates(u, wa_ref, wi_ref, ba, bi, lam):
    parts = []
    for p in range(wa_ref.shape[0]):
        cols = slice(p * MXU_DIM, (p + 1) * MXU_DIM)
        parts.append(_rg_gates_block(u[:, cols], wa_ref[p], wi_ref[p], ba[:, cols], bi[:, cols], lam[:, cols]))
    return (jnp.concatenate([a for a, _ in parts], axis=-1), jnp.concatenate([b for _, b in parts], axis=-1))


def _rglru_front(xn_ref, w_rows_ref, lb_ref, tb_ref, tail_ref, conv_ref):
    nsq = tb_ref.shape[0]
    d = xn_ref.shape[1]
    tl = xn_ref.shape[0] // nsq
    seg = tl // SUBLANES
    nlb = d // LANES
    pre = (CONV_W - 1) * SUBLANES

    pitch = seg + SUBLANES
    x_rg = _dot_nt(xn_ref[...], w_rows_ref[0:d, :])
    for cb in range(nlb):
        for s in range(nsq * SUBLANES):
            lb_ref[cb, s * pitch:s * pitch + seg, :] = x_rg[s * seg:(s + 1) * seg, cb * LANES:(cb + 1) * LANES]
    first = lax.broadcasted_iota(jnp.int32, (SUBLANES, d), 0) == 0
    for q in range(nsq):
        for j in range(seg):
            for cb in range(nlb):
                tb_ref[q, pre + j * SUBLANES:pre + (j + 1) * SUBLANES, cb * LANES:(cb + 1) * LANES] = (
                    lb_ref[cb, pl.ds(q * SUBLANES * pitch + j, SUBLANES, stride=pitch), :])

        for i in range(CONV_W - 1):
            j = seg - (CONV_W - 1) + i
            slab = tb_ref[q, pre + j * SUBLANES:pre + (j + 1) * SUBLANES, :]
            tb_ref[q, i * SUBLANES:(i + 1) * SUBLANES, :] = jnp.where(first, tail_ref[q, i:i + 1, :],
                                                                     pltpu.roll(slab, 1, axis=0))
            last = slab[SUBLANES - 1:SUBLANES, :]
            tail_ref[q, i:i + 1, :] = last
            conv_ref[q, i:i + 1, :] = last


def _rglru_block(p, q, cw_ref, cb_ref, wa_ref, wi_ref, ba_ref, bi_ref, lam_ref,
                 lb_ref, tb_ref, a_ref, b_ref, h_ref, ha_ref):
    tl = a_ref.shape[1]
    seg = tl // SUBLANES
    cols = slice(p * MXU_DIM, (p + 1) * MXU_DIM)
    lane_blocks = range(p * MXU_DIM // LANES, (p + 1) * MXU_DIM // LANES)
    row0 = q * tl

    piece = tl // RG_PIECES
    for r0 in range(0, tl, piece):
        u = cb_ref[:, cols]
        for j in range(CONV_W):
            u = u + tb_ref[q, j * SUBLANES + r0:j * SUBLANES + r0 + piece, cols] * cw_ref[j:j + 1, cols]
        a, b = _rg_gates_block(u, wa_ref[p], wi_ref[p], ba_ref[:, cols], bi_ref[:, cols], lam_ref[:, cols])
        a_ref[q, r0:r0 + piece, cols] = a
        b_ref[q, r0:r0 + piece, cols] = b
        yield

    acc_a = a_ref[q, 0:SUBLANES, cols]
    acc_b = b_ref[q, 0:SUBLANES, cols]
    for j in range(1, seg):
        rows = slice(j * SUBLANES, (j + 1) * SUBLANES)
        a_j = a_ref[q, rows, cols]
        acc_b = a_j * acc_b + b_ref[q, rows, cols]
        acc_a = a_j * acc_a
        a_ref[q, rows, cols] = acc_a
        b_ref[q, rows, cols] = acc_b

    h = h_ref[q, :, cols]
    h_in = []
    for s in range(SUBLANES):
        h_in.append(h)
        h = acc_a[s:s + 1, :] * h + acc_b[s:s + 1, :]
    h_ref[q, :, cols] = h
    h_in = jnp.concatenate(h_in, axis=0)
    yield

    for j in range(seg):
        rows = slice(j * SUBLANES, (j + 1) * SUBLANES)
        h_j = a_ref[q, rows, cols] * h_in + b_ref[q, rows, cols]
        for i, cb in enumerate(lane_blocks):
            lb_ref[cb, row0 + j * SUBLANES:row0 + (j + 1) * SUBLANES, :] = h_j[:, i * LANES:(i + 1) * LANES]
    for s in range(SUBLANES):
        for cb in lane_blocks:
            ha_ref[row0 + s * seg:row0 + (s + 1) * seg, cb * LANES:(cb + 1) * LANES] = (
                lb_ref[cb, pl.ds(row0 + s, seg, stride=SUBLANES), :].astype(BF16))
    yield


def _mixer_kernel(x_ref, sh_ref, sc_ref, ga_ref, g_ref, wint_hbm, bif_ref, bift_ref,
                  wbra_hbm, wbrb_hbm, wout_hbm, cw_ref, cb_ref, wa_in, wi_in, ba_ref, bi_ref, lam_ref,
                  o_ref, c_ref, n_ref, m_ref, conv_ref, hl_ref,
                  hm_ref, wm_ref, wif_ref, wbra_ref, wbrb_ref, wout_ref, stage, sem,
                  lb_ref, tb_ref, a_ref, b_ref, tail_ref, h_ref, ha_ref, mg_ref, xn_ref, hb_ref, sa_ref,
                  wa_ref, wi_ref):
    nsq, tl, d = x_ref.shape
    dk = d // M_HEADS
    cs = min(ML_CHUNK, tl)
    scale = dk ** -0.5
    gate_row0 = 5 * d

    @pl.when((pl.program_id(0) == 0) & (pl.program_id(1) == 0))
    def _():
        _stage_jobs([(wint_hbm, 0, 5 * d, wm_ref, 0),
                     (wint_hbm, gate_row0, LANES, wif_ref, 0),
                     (wint_hbm, gate_row0 + 2 * M_HEADS, 2 * d, wm_ref, 5 * d),
                     (wbra_hbm, 0, d, wbra_ref, 0), (wbrb_hbm, 0, d, wbrb_ref, 0), (wout_hbm, 0, d, wout_ref, 0)],
                    stage, sem)
        _pair_blocks(wa_in, wa_ref)
        _pair_blocks(wi_in, wi_ref)

    @pl.when(pl.program_id(1) == 0)
    def _():
        c_ref[...] = jnp.zeros_like(c_ref)
        n_ref[...] = jnp.zeros_like(n_ref)
        m_ref[...] = jnp.zeros_like(m_ref)
        tail_ref[...] = jnp.zeros_like(tail_ref)
        h_ref[...] = jnp.zeros_like(h_ref)

    seq_rows = [pl.ds(pl.program_id(0) * nsq + q, 1) for q in range(nsq)]
    for q in range(nsq):
        xn_ref[q * tl:(q + 1) * tl, :] = _rms_mod(x_ref[q], g_ref[...], sh_ref[seq_rows[q], :],
                                                  sc_ref[seq_rows[q], :]).astype(BF16)

    _rglru_front(xn_ref, wm_ref, lb_ref, tb_ref, tail_ref, conv_ref)
    rglru_block = functools.partial(
        _rglru_block, cw_ref=cw_ref, cb_ref=cb_ref, wa_ref=wa_ref, wi_ref=wi_ref, ba_ref=ba_ref, bi_ref=bi_ref,
        lam_ref=lam_ref, lb_ref=lb_ref, tb_ref=tb_ref, a_ref=a_ref, b_ref=b_ref, h_ref=h_ref, ha_ref=ha_ref)
    n_rg = wa_ref.shape[0]
    wq0 = d

    rg_pieces = itertools.chain.from_iterable(rglru_block(p, sq) for p in range(n_rg) for sq in range(nsq))

    def rg_advance(n=1):
        for _ in range(n):
            next(rg_pieces, None)

    pre_c = _dot_nt(xn_ref[...], wif_ref[...]) + bif_ref[...]
    pre_r = _dot_nt(wif_ref[0:GATE_ROWS, :], xn_ref[...]) + bift_ref[...]
    col_is_f = lax.broadcasted_iota(jnp.int32, pre_c.shape, 1) >= M_HEADS
    row_is_f = lax.broadcasted_iota(jnp.int32, pre_r.shape, 0) >= M_HEADS
    gate_c = jnp.where(col_is_f, jax.nn.log_sigmoid(pre_c), pre_c)
    gate_r = jnp.where(row_is_f, jax.nn.log_sigmoid(pre_r), pre_r)

    ti = lax.broadcasted_iota(jnp.int32, (cs, cs), 0)
    si = lax.broadcasted_iota(jnp.int32, (cs, cs), 1)
    causal = si <= ti
    lower = causal.astype(BF16)
    upper = (ti <= si).astype(BF16)

    chunks = []
    for sq in range(nsq):
        for r0 in range(sq * tl, (sq + 1) * tl, cs):
            gc = gate_c[r0:r0 + cs, :]
            gr = gate_r[:, r0:r0 + cs]
            cum_c = sum(_dot(lower, part) for part in _split3(gc))
            cum_r = sum(_dot(part, upper) for part in _split3(gr))
            chunks.append((sq, r0, gc, gr, cum_c, cum_r))

    for h in range(M_HEADS):
        c0 = h * dk
        q_all = _dot_nt(xn_ref[...], wm_ref[wq0 + c0:wq0 + c0 + dk, :])
        k_all = _dot_nt(xn_ref[...], wm_ref[wq0 + d + c0:wq0 + d + c0 + dk, :])
        v_all = _dot_nt(xn_ref[...], wm_ref[wq0 + 2 * d + c0:wq0 + 2 * d + c0 + dk, :])
        rg_advance()
        for sq, r0, gc, gr, cum_c, cum_r in chunks:
            q = q_all[r0:r0 + cs, :]
            k = k_all[r0:r0 + cs, :]
            v = v_all[r0:r0 + cs, :]
            bc = cum_c[:, M_HEADS + h:M_HEADS + h + 1]
            br = cum_r[M_HEADS + h:M_HEADS + h + 1, :]
            ig_c = gc[:, h:h + 1]
            ig_r = gr[h:h + 1, :]
            m_prev = m_ref[sq, h:h + 1, :]
            c_prev = c_ref[sq, h]
            n_prev = n_ref[sq, h:h + 1, :]

            log_d = jnp.where(causal, bc - br + ig_r, -jnp.inf)
            log_past = bc + m_prev
            m_t = jnp.maximum(log_past, jnp.max(log_d, axis=-1, keepdims=True))
            d_mat = jnp.exp(log_d - m_t)
            past_w = jnp.exp(log_past - m_t)
            qs = q * scale
            qsb = qs.astype(BF16)
            kb = k.astype(BF16)
            vb = v.astype(BF16)
            s = _dot_nt(qsb, kb) * d_mat
            num = past_w * _dot(qsb, c_prev.astype(BF16)) + _dot(s.astype(BF16), vb)
            den = past_w * jnp.sum(qs * n_prev, axis=-1, keepdims=True) + jnp.sum(s, axis=-1, keepdims=True)
            hm_ref[r0:r0 + cs, c0:c0 + dk] = num / jnp.maximum(jnp.abs(den), jnp.exp(-m_t))

            m_new = m_t[cs - 1:cs, :]
            b_last = bc[cs - 1:cs, :]
            w_s = jnp.exp(b_last - bc + ig_c - m_new)
            decay = jnp.exp(b_last + m_prev - m_new)
            kw = w_s * k
            c_ref[sq, h] = decay * c_prev + _dot_tn(kw.astype(BF16), vb)
            n_ref[sq, h:h + 1, :] = decay * n_prev + jnp.sum(kw, axis=0, keepdims=True)
            m_ref[sq, h:h + 1, :] = m_new
            rg_advance()

    col_blocks = [slice(n0, n0 + MXU_DIM) for n0 in range(0, d, MXU_DIM)]
    for cols in col_blocks:
        o_pre = _dot_nt(xn_ref[...], wm_ref[4 * d + cols.start:4 * d + cols.stop, :])
        hb_ref[:, cols] = (jax.nn.sigmoid(o_pre) * hm_ref[:, cols]).astype(BF16)
        rg_advance()
    for cols in col_blocks:
        g_b = _dot_nt(xn_ref[...], wm_ref[6 * d + cols.start:6 * d + cols.stop, :])
        rg_advance()
        hm_ref[:, cols] = jax.nn.sigmoid(g_b) * _dot(hb_ref[...], wbrb_ref[:, cols])
        g_a = _dot_nt(xn_ref[...], wm_ref[5 * d + cols.start:5 * d + cols.stop, :])
        sa_ref[:, cols] = jax.nn.sigmoid(g_a)
        rg_advance()
    for _ in rg_pieces:
        pass
    hl_ref[...] = h_ref[...]
    for cols in col_blocks:
        merged = sa_ref[:, cols] * _dot(ha_ref[...], wbra_ref[:, cols]) + hm_ref[:, cols]
        mg_ref[:, cols] = merged.astype(BF16)
    y = _dot(mg_ref[...], wout_ref[...])
    for q in range(nsq):
        o_ref[q] = x_ref[q] + ga_ref[seq_rows[q], :] * y[q * tl:(q + 1) * tl, :]


def _mixer_prompt(x3, mod, gain, w_int, bif, bift, wbra, wbrb, wout, conv_w, conv_b, wa, wi, ba, bi, lam):
    nb, seq, _ = x3.shape
    d = x3.shape[2]
    dk = d // M_HEADS
    nsq = ML_SEQS
    tl = min(ML_ROWS, seq)
    rows = nsq * tl
    row = lambda v: v.reshape(1, d)
    mspecs, _ = _mod_specs(mod.shape[0] - nb, nb, d, 3, 3)
    rg_pairs = (wa.shape[0] // 2, 2 * wa.shape[1], 2 * wa.shape[2])
    tile = lambda: pl.BlockSpec((nsq, tl, d), lambda b, t: (b, t, 0))
    per_seq = lambda *shape: pl.BlockSpec((nsq,) + shape, lambda b, t: (b,) + (0,) * len(shape))
    return pl.pallas_call(
        _mixer_kernel,
        grid=(nb // nsq, seq // tl),
        in_specs=[tile()] + mspecs + [_resident((1, d)),
                  _HBM, _resident(bif.shape), _resident(bift.shape), _HBM, _HBM, _HBM,
                  _resident(conv_w.shape), _resident((1, d)), _resident(wa.shape), _resident(wi.shape),
                  _resident((1, d)), _resident((1, d)), _resident((1, d))],
        out_specs=[tile(), per_seq(M_HEADS, dk, dk), per_seq(M_HEADS, dk), per_seq(M_HEADS, 1),
                   per_seq(CONV_W - 1, d), per_seq(1, d)],
        out_shape=[jax.ShapeDtypeStruct((nb, seq, d), F32),
                   jax.ShapeDtypeStruct((nb, M_HEADS, dk, dk), F32),
                   jax.ShapeDtypeStruct((nb, M_HEADS, dk), F32),
                   jax.ShapeDtypeStruct((nb, M_HEADS, 1), F32),
                   jax.ShapeDtypeStruct((nb, CONV_W - 1, d), F32),
                   jax.ShapeDtypeStruct((nb, 1, d), F32)],
        scratch_shapes=[pltpu.VMEM((rows, d), F32),
                        pltpu.VMEM((7 * d, d), BF16),
                        pltpu.VMEM((LANES, d), BF16),
                        pltpu.VMEM((d, d), BF16), pltpu.VMEM((d, d), BF16), pltpu.VMEM((d, d), BF16)]
        + _stage_scratch(d)
        + [pltpu.VMEM((d // LANES, rows + nsq * SUBLANES * SUBLANES, LANES), F32),
           pltpu.VMEM((nsq, (CONV_W - 1) * SUBLANES + tl, d), F32),
           pltpu.VMEM((nsq, tl, d), F32),
           pltpu.VMEM((nsq, tl, d), F32),
           pltpu.VMEM((nsq, CONV_W - 1, d), F32),
           pltpu.VMEM((nsq, 1, d), F32),
           pltpu.VMEM((rows, d), BF16),
           pltpu.VMEM((rows, d), BF16),
           pltpu.VMEM((rows, d), BF16),
           pltpu.VMEM((rows, d), BF16),
           pltpu.VMEM((rows, d), F32),
           pltpu.VMEM(rg_pairs, BF16), pltpu.VMEM(rg_pairs, BF16)],
        compiler_params=_cparams(("arbitrary", "arbitrary")),
        name="mixer_prompt",
    )(x3, mod, mod, mod, row(gain), w_int, bif, bift, wbra, wbrb, wout,
      conv_w, row(conv_b), wa, wi, row(ba), row(bi), row(lam))


def _head_sum(x, dk):
    parts = []
    for h in range(x.shape[1] // dk):
        sl = x[:, h * dk:(h + 1) * dk]
        parts.append(jnp.broadcast_to(jnp.sum(sl, axis=-1, keepdims=True), sl.shape))
    return jnp.concatenate(parts, axis=-1)


def _head_spread(cols, lane0, dk, rows):
    return jnp.concatenate(
        [jnp.broadcast_to(cols[:, lane0 + h:lane0 + h + 1], (rows, dk)) for h in range(M_HEADS)], axis=-1)


def _dec_pre_kernel(x_ref, sh_ref, sc_ref, g_ref, wint_hbm, bif_ref,
                    cw_ref, cb_ref, wa_in, wi_in, ba_ref, bi_ref, lam_ref,
                    conv0_ref, h0_ref, n0_ref, m0_ref,
                    ha_ref, conv_ref, hs_ref, n_ref, m_ref, qt_ref, kt_ref, dec_ref, wv_ref, pw_ref, sv_ref,
                    den_ref, em_ref,
                    wm_ref, wif_ref, stage, sem, wa_ref, wi_ref):
    rows, d = x_ref.shape
    dk = d // M_HEADS
    scale = dk ** -0.5
    _pair_blocks(wa_in, wa_ref)
    _pair_blocks(wi_in, wi_ref)
    _stage_jobs([(wint_hbm, 0, 4 * d, wm_ref, 0),
                 (wint_hbm, 5 * d, LANES, wif_ref, 0)], stage, sem)
    xn = _rms_mod(x_ref[...], g_ref[...], sh_ref[...], sc_ref[...]).astype(BF16)

    x_rg = _dot_nt(xn, wm_ref[0:d, :])
    u = cb_ref[...]
    for j in range(CONV_W - 1):
        u = u + conv0_ref[j] * cw_ref[j:j + 1, :]
        if j > 0:
            conv_ref[j - 1] = conv0_ref[j]
    u = u + x_rg * cw_ref[CONV_W - 1:CONV_W, :]
    conv_ref[CONV_W - 2] = x_rg
    a, b = _rg_gates(u, wa_ref, wi_ref, ba_ref[...], bi_ref[...], lam_ref[...])
    h = a * h0_ref[...] + b
    hs_ref[...] = h
    ha_ref[...] = h

    q = _dot_nt(xn, wm_ref[d:2 * d, :])
    k = _dot_nt(xn, wm_ref[2 * d:3 * d, :])
    v = _dot_nt(xn, wm_ref[3 * d:4 * d, :])
    qt_ref[...] = _dot_nt(wm_ref[d:2 * d, :], xn).astype(BF16)
    kt_ref[...] = _dot_nt(wm_ref[2 * d:3 * d, :], xn).astype(BF16)
    pre = _dot_nt(xn, wif_ref[...]) + bif_ref[...]
    ig = _head_spread(pre, 0, dk, rows)
    lf = jax.nn.log_sigmoid(_head_spread(pre, M_HEADS, dk, rows))
    m0 = _head_spread(m0_ref[...], 0, dk, rows)
    n0 = n0_ref[...]
    log_past = lf + m0
    m_t = jnp.maximum(log_past, ig)
    d_w = jnp.exp(ig - m_t)
    past_w = jnp.exp(log_past - m_t)
    qs = q * scale
    s = _head_sum(qs * k, dk) * d_w
    den_ref[...] = past_w * _head_sum(qs * n0, dk) + s
    em_ref[...] = jnp.exp(-m_t)
    pw_ref[...] = past_w
    sv_ref[...] = s * v
    w_s = jnp.exp(ig - m_t)
    decay = jnp.exp(lf + m0 - m_t)
    dec_ref[...] = decay
    wv_ref[...] = w_s * v
    n_ref[...] = decay * n0 + w_s * k
    m_ref[...] = m_t


def _dec_pre(x2, mod, gain, w_int, bif, conv_w, conv_b, wa, wi, ba, bi, lam, conv0, h0, n0, m0):
    rows, d = x2.shape
    row = lambda v: v.reshape(1, d)
    m0p = jnp.pad(m0, ((0, 0), (0, LANES - M_HEADS)))
    full = lambda shape: pl.BlockSpec(shape, lambda i, _n=len(shape): (0,) * _n)
    mspec = lambda j: pl.BlockSpec((rows, d), lambda i: (0, j))
    vec = jax.ShapeDtypeStruct((rows, d), F32)
    outs = [vec,
            jax.ShapeDtypeStruct((CONV_W - 1, rows, d), F32),
            vec,
            vec,
            vec,
            jax.ShapeDtypeStruct((d, rows), BF16),
            jax.ShapeDtypeStruct((d, rows), BF16),
            vec, vec, vec, vec, vec, vec]
    args = [x2, mod, mod, row(gain), w_int, bif, conv_w, row(conv_b), wa, wi,
            row(ba), row(bi), row(lam), conv0, h0, n0, m0p]
    in_specs = ([full(x2.shape), mspec(3), mspec(4), full((1, d)), _HBM]
                + [full(a.shape) for a in args[5:]])
    return pl.pallas_call(
        _dec_pre_kernel,
        grid=(1,),
        in_specs=in_specs,
        out_specs=[full(o.shape) for o in outs],
        out_shape=outs,
        scratch_shapes=[pltpu.VMEM((4 * d, d), BF16), pltpu.VMEM((LANES, d), BF16)]
        + _stage_scratch(d, STAGE_SLOTS_DEC)
        + [pltpu.VMEM((wa.shape[0] // 2, 2 * wa.shape[1], 2 * wa.shape[2]), BF16)] * 2,
        compiler_params=_cparams(("arbitrary",)),
        name="decode_pre",
    )(*args)


def _dec_mem_kernel(qt_ref, kt_ref, dec_ref, wv_ref, c0_ref, c_ref, qc_ref):
    bb = c0_ref.shape[0]
    dk = c0_ref.shape[2]
    nseq = qt_ref.shape[1]
    scale = dk ** -0.5
    base = pl.program_id(0) * bb
    seq_id = lax.broadcasted_iota(jnp.int32, (nseq, dk), 0)
    for j in range(bb):
        pick = (seq_id == base + j).astype(BF16)
        qcol = _dot(qt_ref[...], pick) * scale
        kcol = _dot(kt_ref[...], pick)
        seq = pl.ds(base + j, 1)
        for h in range(M_HEADS):
            cols = slice(h * dk, (h + 1) * dk)
            c0 = c0_ref[j, h]
            dec = dec_ref[seq, cols]
            wv = wv_ref[seq, cols]
            c_ref[j, h] = dec * c0 + kcol[cols, :] * wv
            qc_ref[seq, cols] = jnp.sum(qcol[cols, :] * c0, axis=0, keepdims=True)


def _dec_mem(qt, kt, decay, wv, c0):
    nseq, heads, dk, _ = c0.shape
    d = heads * dk
    bb = DEC_BLOCK if nseq % DEC_BLOCK == 0 else 1
    whole = lambda shape: pl.BlockSpec(shape, lambda i: (0, 0))
    return pl.pallas_call(
        _dec_mem_kernel,
        grid=(nseq // bb,),
        in_specs=[whole((d, nseq)), whole((d, nseq)), whole((nseq, d)), whole((nseq, d)),
                  pl.BlockSpec((bb, heads, dk, dk), lambda i: (i, 0, 0, 0))],
        out_specs=[pl.BlockSpec((bb, heads, dk, dk), lambda i: (i, 0, 0, 0)), whole((nseq, d))],
        out_shape=[jax.ShapeDtypeStruct(c0.shape, F32), jax.ShapeDtypeStruct((nseq, d), F32)],
        compiler_params=_cparams(("arbitrary",)),
        name="decode_mem",
    )(qt, kt, decay, wv, c0)


def _dec_post_kernel(x_ref, sh_ref, sc_ref, ga_ref, g_ref, wint_hbm, wbra_hbm, wbrb_hbm, wout_hbm,
                     ha_ref, qc_ref, pw_ref, sv_ref, den_ref, em_ref, o_ref,
                     wm_ref, wbra_ref, wbrb_ref, wout_ref, stage, sem):
    d = x_ref.shape[1]
    _stage_jobs([(wint_hbm, 4 * d, d, wm_ref, 0),
                 (wint_hbm, 5 * d + 2 * M_HEADS, 2 * d, wm_ref, d),
                 (wbra_hbm, 0, d, wbra_ref, 0), (wbrb_hbm, 0, d, wbrb_ref, 0), (wout_hbm, 0, d, wout_ref, 0)],
                stage, sem)
    x = x_ref[...]
    xn = _rms_mod(x, g_ref[...], sh_ref[...], sc_ref[...]).astype(BF16)
    num = pw_ref[...] * qc_ref[...] + sv_ref[...]
    hm = num / jnp.maximum(jnp.abs(den_ref[...]), em_ref[...])
    hb = (jax.nn.sigmoid(_dot_nt(xn, wm_ref[0:d, :])) * hm).astype(BF16)
    g_a = _dot_nt(xn, wm_ref[d:2 * d, :])
    g_b = _dot_nt(xn, wm_ref[2 * d:3 * d, :])
    merged = (jax.nn.sigmoid(g_a) * _dot(ha_ref[...].astype(BF16), wbra_ref[...])
              + jax.nn.sigmoid(g_b) * _dot(hb, wbrb_ref[...]))
    o_ref[...] = x + ga_ref[...] * _dot(merged.astype(BF16), wout_ref[...])


def _dec_post(x2, mod, gain, w_int, wbra, wbrb, wout, ha, qc, pw, sv, den, em):
    rows, d = x2.shape
    full = lambda shape: pl.BlockSpec(shape, lambda i, _n=len(shape): (0,) * _n)
    mspec = lambda j: pl.BlockSpec((rows, d), lambda i: (0, j))
    args = [x2, mod, mod, mod, gain.reshape(1, d), w_int, wbra, wbrb, wout, ha, qc, pw, sv, den, em]
    in_specs = ([full(x2.shape), mspec(3), mspec(4), mspec(5), full((1, d))] + [_HBM] * 4
                + [full(a.shape) for a in args[9:]])
    return pl.pallas_call(
        _dec_post_kernel,
        grid=(1,),
        in_specs=in_specs,
        out_specs=full((rows, d)),
        out_shape=jax.ShapeDtypeStruct((rows, d), F32),
        scratch_shapes=[pltpu.VMEM((3 * d, d), BF16),
                        pltpu.VMEM((d, d), BF16), pltpu.VMEM((d, d), BF16), pltpu.VMEM((d, d), BF16)]
        + _stage_scratch(d, STAGE_SLOTS_DEC),
        compiler_params=_cparams(("arbitrary",)),
        name="decode_post",
    )(*args)


def kernel(x_prompt, x_sample, state_conv, state_rg_h, state_C, state_n, state_m, c_prompt, c_sample, w_ada, b_ada, g_norm1, w_ff1_in, w_ff1_out, g_norm2, w_in, conv_w, conv_b, w_rg_a, b_rg_a, w_rg_i, b_rg_i, rg_lambda, b_ig, b_fg, w_br_a, w_br_b, w_out, g_norm3, w_ff2_in, w_ff2_out, g_final):
    nb, seq, d = x_prompt.shape
    ns = x_sample.shape[0]
    depth = w_ada.shape[0]
    assert depth == 1 and x_sample.shape[1] == 1 and nb == SUBLANES
    assert seq % ML_ROWS == 0 and seq % FFN_ROWS == 0 and ML_ROWS % (SUBLANES * SUBLANES) == 0
    assert nb % ML_SEQS == 0
    heads, dk = M_HEADS, d // M_HEADS
    assert w_in.shape[2] == 7 * d + 2 * heads and GATE_ROWS >= 2 * heads

    w_int = w_in[0].T
    b_gates = jnp.concatenate([b_ig[0], b_fg[0]])
    bif = jnp.pad(b_gates, (0, LANES - 2 * heads)).reshape(1, LANES)
    bift = jnp.pad(b_gates, (0, GATE_ROWS - 2 * heads)).reshape(GATE_ROWS, 1)
    wa, wi = w_rg_a[0], w_rg_i[0]
    wf1i, wf1o, wf2i, wf2o = w_ff1_in[0], w_ff1_out[0], w_ff2_in[0], w_ff2_out[0]
    wbra, wbrb, wout = w_br_a[0], w_br_b[0], w_out[0]

    mod = _ada(c_sample, c_prompt, w_ada[0], b_ada[0])

    xp, xs = _ffn(x_prompt.reshape(nb * seq, d), x_sample.reshape(ns, d), mod, 0, g_norm1[0],
                  wf1i, wf1o, None, rows_per_seq=seq)

    xp, c_p, n_p, m_p, conv_p, h_p = _mixer_prompt(
        xp.reshape(nb, seq, d), mod, g_norm2[0], w_int, bif, bift, wbra, wbrb, wout, conv_w[0], conv_b[0],
        wa, wi, b_rg_a[0], b_rg_i[0], rg_lambda[0])
    xp = xp.reshape(nb * seq, d)

    conv0 = jnp.swapaxes(state_conv[0], 0, 1)
    (ha_s, conv_s, h_s, n_s, m_s, qt, kt, decay, wv, pw, sv, den, em) = _dec_pre(
        xs, mod, g_norm2[0], w_int, bif, conv_w[0], conv_b[0], wa, wi,
        b_rg_a[0], b_rg_i[0], rg_lambda[0], conv0, state_rg_h[0], state_n[0].reshape(ns, d), state_m[0])
    c_s, qc = _dec_mem(qt, kt, decay, wv, state_C[0])
    xs = _dec_post(xs, mod, g_norm2[0], w_int, wbra, wbrb, wout, ha_s, qc, pw, sv, den, em)

    yp, ys = _ffn(xp, xs, mod, 6, g_norm3[0], wf2i, wf2o, g_final, rows_per_seq=seq)

    return (yp.reshape(nb, seq, d), ys.reshape(ns, 1, d),
            conv_p[None], h_p.reshape(1, nb, d), c_p[None], n_p[None], m_p.reshape(1, nb, heads),
            jnp.swapaxes(conv_s, 0, 1)[None], h_s[None], c_s[None], n_s.reshape(1, ns, heads, dk), m_s[:, ::dk][None])
```

```python
import functools
import itertools

import jax
import jax.numpy as jnp
from jax import lax
from jax.experimental import pallas as pl
from jax.experimental.pallas import tpu as pltpu

F32 = jnp.float32
BF16 = jnp.bfloat16

EPS = 1e-6
RG_C = 8.0
CONV_W = 4
N_MOD = 9
M_HEADS = 4
RG_BLOCKS = 8

SUBLANES = 8
LANES = 128
MXU_DIM = 256
VMEM_LIMIT_BYTES = 60 * 1024 * 1024

ADA_STEPS = 4
FFN_ROWS = 1024
FFN_COLS = 256
ML_ROWS = 512
ML_SEQS = 1
ML_CHUNK = 256
RG_PIECES = 4
DEC_BLOCK = 4
STAGE_ROWS = 256
STAGE_SLOTS = 4
STAGE_SLOTS_DEC = 12
STAGE_ROWS_WIDE = 64
STAGE_SLOTS_WIDE = 4
GATE_ROWS = 16


def _cparams(sem):
    return pltpu.CompilerParams(dimension_semantics=sem, vmem_limit_bytes=VMEM_LIMIT_BYTES)


def _resident(shape):
    nd = len(shape)
    return pl.BlockSpec(shape, lambda *_: (0,) * nd, pipeline_mode=pl.Buffered(1))


_HBM = pl.BlockSpec(memory_space=pl.ANY)


def _rms_mod(x, gain, shift, scale):
    ms = jnp.mean(x * x, axis=-1, keepdims=True)
    return x * lax.rsqrt(ms + EPS) * gain * (1.0 + scale) + shift


def _dot(a, b):
    return jnp.dot(a, b, preferred_element_type=F32)


def _dot_nt(a, b):
    return lax.dot_general(a, b, (((1,), (1,)), ((), ())), preferred_element_type=F32)


def _dot_tn(a, b):
    return lax.dot_general(a, b, (((0,), (0,)), ((), ())), preferred_element_type=F32)


def _split3(x):
    hi = x.astype(BF16)
    r1 = x - hi.astype(F32)
    mid = r1.astype(BF16)
    lo = (r1 - mid.astype(F32)).astype(BF16)
    return hi, mid, lo


def _stage_jobs(jobs, stage, sem, extra_slots=()):
    _, rows_per_copy, _ = stage.shape
    ring = [stage.at[i] for i in range(stage.shape[0])] + list(extra_slots)
    slots = len(ring)
    assert sem.shape[0] >= slots
    chunks = []
    for src, src_row0, nrows, dst, dst_row0 in jobs:
        for r in range(0, nrows, rows_per_copy):
            chunks.append((src, src_row0 + r, dst, dst_row0 + r, min(rows_per_copy, nrows - r)))
    n = len(chunks)

    def copy(c):
        src, src_row, _, _, rows = chunks[c]
        return pltpu.make_async_copy(src.at[pl.ds(src_row, rows)],
                                     ring[c % slots].at[pl.ds(0, rows)], sem.at[c % slots])

    for c in range(min(slots, n)):
        copy(c).start(priority=c % 2)
    for c in range(n):
        _, _, dst, dst_row, rows = chunks[c]
        copy(c).wait()
        dst[dst_row:dst_row + rows, :] = ring[c % slots][0:rows, :].astype(BF16)
        if c + slots < n:
            copy(c + slots).start(priority=(c + slots) % 2)


def _stage_rows(src, src_row0, nrows, dst, dst_row0, stage, sem):
    _stage_jobs([(src, src_row0, nrows, dst, dst_row0)], stage, sem)


def _stage_scratch(cols, slots=None, rows=None, n_extra=0):
    slots = STAGE_SLOTS if slots is None else slots
    rows = STAGE_ROWS if rows is None else rows
    return [pltpu.VMEM((slots, rows, cols), F32), pltpu.SemaphoreType.DMA((slots + n_extra,))]


def _ada_rows(cs_ref, cp_ref, w_ref, b_ref, o_ref):
    ns = cs_ref.shape[0]
    w = w_ref[...].astype(BF16)
    for c_ref, rows in ((cs_ref, slice(0, ns)), (cp_ref, slice(ns, o_ref.shape[0]))):
        c = c_ref[...]
        o_ref[rows, :] = _dot((c * jax.nn.sigmoid(c)).astype(BF16), w) + b_ref[...]


def _ada(c_sample, c_prompt, w_ada, b_ada, n_vectors):
    ns, d = c_sample.shape
    nb = c_prompt.shape[0]
    n = n_vectors * d
    tn = n // ADA_STEPS
    assert tn * ADA_STEPS == n and tn % LANES == 0 and ns % SUBLANES == 0
    return pl.pallas_call(
        _ada_rows,
        grid=(n // tn,),
        in_specs=[pl.BlockSpec((ns, d), lambda j: (0, 0)),
                  pl.BlockSpec((nb, d), lambda j: (0, 0)),
                  pl.BlockSpec((d, tn), lambda j: (0, j)),
                  pl.BlockSpec((1, tn), lambda j: (0, j))],
        out_specs=pl.BlockSpec((ns + nb, tn), lambda j: (0, j)),
        out_shape=jax.ShapeDtypeStruct((ns + nb, n), F32),
        compiler_params=_cparams(("arbitrary",)),
        name="adaln_mod",
    )(c_sample, c_prompt, w_ada, b_ada.reshape(1, w_ada.shape[1]))


def _ffn_rows(x_ref, shift, scale, gate, g_ref, wi_ref, wo_ref, gf_ref, o_ref, act_ref, d_ff):
    rows = x_ref.shape[0]
    x = x_ref[...]
    xn = _rms_mod(x, g_ref[...], shift, scale).astype(BF16)
    for j in range(d_ff // FFN_COLS):
        lo = j * FFN_COLS
        hg = _dot(xn, wi_ref[:, lo:lo + FFN_COLS])
        hu = _dot(xn, wi_ref[:, d_ff + lo:d_ff + lo + FFN_COLS])
        act_ref[0:rows, lo:lo + FFN_COLS] = (hg * jax.nn.sigmoid(hg) * hu).astype(BF16)
    y = _dot(act_ref[0:rows, :], wo_ref[...])
    out = x + (0.5 * gate) * y
    if gf_ref is not None:
        ms = jnp.mean(out * out, axis=-1, keepdims=True)
        out = out * lax.rsqrt(ms + EPS) * gf_ref[...]
    o_ref[...] = out


def _ffn_kernel(*refs, d_ff, final, later_mod, n_prompt_steps, tiles_per_seq):
    refs = list(refs)
    xp_ref, shp_ref, scp_ref, gap_ref, xs_ref, shs_ref, scs_ref, gas_ref, g_ref, wi_hbm, wo_hbm = refs[:11]
    del refs[:11]
    gf_ref = refs.pop(0) if final else None
    if later_mod:
        cs_ref, cp_ref, wada_ref, bada_ref = refs[:4]
        del refs[:4]
    op_ref, os_ref = refs[:2]
    mb_ref = refs[2] if later_mod else None
    act_ref, wi_ref, wo_ref, stage_i, sem_i, stage_o, sem_o = refs[-7:]
    step = pl.program_id(0)

    @pl.when(step == 0)
    def _():
        _stage_rows(wi_hbm, 0, wi_ref.shape[0], wi_ref, 0, stage_i, sem_i)
        _stage_rows(wo_hbm, 0, wo_ref.shape[0], wo_ref, 0, stage_o, sem_o)

    @pl.when(step < n_prompt_steps)
    def _():
        seq = pl.ds(step // tiles_per_seq, 1)
        _ffn_rows(xp_ref, shp_ref[seq, :], scp_ref[seq, :], gap_ref[seq, :], g_ref, wi_ref, wo_ref, gf_ref,
                  op_ref, act_ref, d_ff)
        if later_mod:
            _ada_rows(cs_ref, cp_ref, wada_ref, bada_ref, mb_ref)

    @pl.when(step == n_prompt_steps)
    def _():
        _ffn_rows(xs_ref, shs_ref[...], scs_ref[...], gas_ref[...], g_ref, wi_ref, wo_ref, gf_ref,
                  os_ref, act_ref, d_ff)


def _mod_specs(ns, nb, d, j0, n):
    assert ns % nb == 0
    prompt = [pl.BlockSpec((nb, d), lambda *_, j=j: (ns // nb, j)) for j in range(j0, j0 + n)]
    sample = [pl.BlockSpec((ns, d), lambda *_, j=j: (0, j)) for j in range(j0, j0 + n)]
    return prompt, sample


def _ffn(xp, xs, mod, j0, gain, wi, wo, g_final, *, rows_per_seq, later_mod=None):
    rows, d = xp.shape
    ns = xs.shape[0]
    d_ff = wo.shape[0]
    tm = min(FFN_ROWS, rows_per_seq)
    tiles_per_seq = rows_per_seq // tm
    n_steps = rows // tm
    tile_of = lambda i: jnp.minimum(i, n_steps - 1)
    mspec_p, mspec_s = _mod_specs(ns, rows // rows_per_seq, d, j0, 3)
    final = g_final is not None
    in_specs = ([pl.BlockSpec((tm, d), lambda i: (tile_of(i), 0))] + mspec_p
                + [pl.BlockSpec((ns, d), lambda i: (0, 0))] + mspec_s + [_resident((1, d)), _HBM, _HBM])
    args = [xp, mod, mod, mod, xs, mod, mod, mod, gain.reshape(1, d), wi, wo]
    if final:
        in_specs.append(_resident((1, d)))
        args.append(g_final.reshape(1, d))
    out_specs = [pl.BlockSpec((tm, d), lambda i: (tile_of(i), 0)), pl.BlockSpec((ns, d), lambda i: (0, 0))]
    out_shape = [jax.ShapeDtypeStruct((rows, d), F32), jax.ShapeDtypeStruct((ns, d), F32)]
    if later_mod is not None:
        c_sample, c_prompt, w_ada, b_ada, first = later_mod
        n_later = w_ada.shape[1] - first * d
        tn = n_later // n_steps
        assert tn * n_steps == n_later and tn % LANES == 0 and (first * d) % tn == 0
        col0 = first * d // tn
        in_specs += [_resident(c_sample.shape), _resident(c_prompt.shape),
                     pl.BlockSpec((d, tn), lambda i: (0, col0 + tile_of(i))),
                     pl.BlockSpec((1, tn), lambda i: (0, col0 + tile_of(i)))]
        args += [c_sample, c_prompt, w_ada, b_ada.reshape(1, w_ada.shape[1])]
        out_specs.append(pl.BlockSpec((mod.shape[0], tn), lambda i: (0, tile_of(i))))
        out_shape.append(jax.ShapeDtypeStruct((mod.shape[0], n_later), F32))
    return pl.pallas_call(
        functools.partial(_ffn_kernel, d_ff=d_ff, final=final, later_mod=later_mod is not None,
                          n_prompt_steps=n_steps, tiles_per_seq=tiles_per_seq),
        grid=(n_steps + 1,),
        in_specs=in_specs,
        out_specs=out_specs,
        out_shape=out_shape,
        scratch_shapes=[pltpu.VMEM((tm, d_ff), BF16), pltpu.VMEM(wi.shape, BF16), pltpu.VMEM(wo.shape, BF16)]
        + _stage_scratch(wi.shape[1], STAGE_SLOTS_WIDE, STAGE_ROWS_WIDE) + _stage_scratch(wo.shape[1]),
        compiler_params=_cparams(("arbitrary",)),
        name="ffn_final" if final else "ffn",
    )(*args)


def _pair_blocks(src_ref, dst_ref):
    nblk, c, _ = src_ref.shape
    dst_ref[...] = jnp.zeros(dst_ref.shape, dst_ref.dtype)
    for p in range(nblk // 2):
        dst_ref[p, 0:c, 0:c] = src_ref[2 * p].astype(BF16)
        dst_ref[p, c:2 * c, c:2 * c] = src_ref[2 * p + 1].astype(BF16)


def _rg_gates_block(u, wa, wi, ba, bi, lam):
    ub = u.astype(BF16)
    r = jax.nn.sigmoid(_dot(ub, wa) + ba)
    i_g = jax.nn.sigmoid(_dot(ub, wi) + bi)
    log_a = -RG_C * r * jax.nn.softplus(-lam)
    a = jnp.exp(log_a)
    b = jnp.sqrt(-jnp.tanh(log_a) * (a * a + 1.0)) * (i_g * u)
    return a, b


def _rg_gates(u, wa_ref, wi_ref, ba, bi, lam):
    parts = []
    for p in range(wa_ref.shape[0]):
        cols = slice(p * MXU_DIM, (p + 1) * MXU_DIM)
        parts.append(_rg_gates_block(u[:, cols], wa_ref[p], wi_ref[p], ba[:, cols], bi[:, cols], lam[:, cols]))
    return (jnp.concatenate([a for a, _ in parts], axis=-1), jnp.concatenate([b for _, b in parts], axis=-1))


def _rglru_front(xn_ref, w_rows_ref, lb_ref, tb_ref, tail_ref, conv_ref):
    nsq = tb_ref.shape[0]
    d = xn_ref.shape[1]
    tl = xn_ref.shape[0] // nsq
    seg = tl // SUBLANES
    nlb = d // LANES
    pre = (CONV_W - 1) * SUBLANES

    pitch = seg + SUBLANES
    x_rg = _dot_nt(xn_ref[...], w_rows_ref[0:d, :])
    for cb in range(nlb):
        for s in range(nsq * SUBLANES):
            lb_ref[cb, s * pitch:s * pitch + seg, :] = x_rg[s * seg:(s + 1) * seg, cb * LANES:(cb + 1) * LANES]
    first = lax.broadcasted_iota(jnp.int32, (SUBLANES, d), 0) == 0
    for q in range(nsq):
        for j in range(seg):
            for cb in range(nlb):
                tb_ref[q, pre + j * SUBLANES:pre + (j + 1) * SUBLANES, cb * LANES:(cb + 1) * LANES] = (
                    lb_ref[cb, pl.ds(q * SUBLANES * pitch + j, SUBLANES, stride=pitch), :])

        for i in range(CONV_W - 1):
            j = seg - (CONV_W - 1) + i
            slab = tb_ref[q, pre + j * SUBLANES:pre + (j + 1) * SUBLANES, :]
            tb_ref[q, i * SUBLANES:(i + 1) * SUBLANES, :] = jnp.where(first, tail_ref[q, i:i + 1, :],
                                                                     pltpu.roll(slab, 1, axis=0))
            last = slab[SUBLANES - 1:SUBLANES, :]
            tail_ref[q, i:i + 1, :] = last
            conv_ref[q, i:i + 1, :] = last


def _rglru_block(p, q, cw_ref, cb_ref, wa_ref, wi_ref, ba_ref, bi_ref, lam_ref,
                 lb_ref, tb_ref, a_ref, b_ref, h_ref, ha_ref):
    tl = a_ref.shape[1]
    seg = tl // SUBLANES
    cols = slice(p * MXU_DIM, (p + 1) * MXU_DIM)
    lane_blocks = range(p * MXU_DIM // LANES, (p + 1) * MXU_DIM // LANES)
    row0 = q * tl

    piece = tl // RG_PIECES
    for r0 in range(0, tl, piece):
        u = cb_ref[:, cols]
        for j in range(CONV_W):
            u = u + tb_ref[q, j * SUBLANES + r0:j * SUBLANES + r0 + piece, cols] * cw_ref[j:j + 1, cols]
        a, b = _rg_gates_block(u, wa_ref[p], wi_ref[p], ba_ref[:, cols], bi_ref[:, cols], lam_ref[:, cols])
        a_ref[q, r0:r0 + piece, cols] = a
        b_ref[q, r0:r0 + piece, cols] = b
        yield

    acc_a = a_ref[q, 0:SUBLANES, cols]
    acc_b = b_ref[q, 0:SUBLANES, cols]
    for j in range(1, seg):
        rows = slice(j * SUBLANES, (j + 1) * SUBLANES)
        a_j = a_ref[q, rows, cols]
        acc_b = a_j * acc_b + b_ref[q, rows, cols]
        acc_a = a_j * acc_a
        a_ref[q, rows, cols] = acc_a
        b_ref[q, rows, cols] = acc_b

    h = h_ref[q, :, cols]
    h_in = []
    for s in range(SUBLANES):
        h_in.append(h)
        h = acc_a[s:s + 1, :] * h + acc_b[s:s + 1, :]
    h_ref[q, :, cols] = h
    h_in = jnp.concatenate(h_in, axis=0)
    yield

    for j in range(seg):
        rows = slice(j * SUBLANES, (j + 1) * SUBLANES)
        h_j = a_ref[q, rows, cols] * h_in + b_ref[q, rows, cols]
        for i, cb in enumerate(lane_blocks):
            lb_ref[cb, row0 + j * SUBLANES:row0 + (j + 1) * SUBLANES, :] = h_j[:, i * LANES:(i + 1) * LANES]
    for s in range(SUBLANES):
        for cb in lane_blocks:
            ha_ref[row0 + s * seg:row0 + (s + 1) * seg, cb * LANES:(cb + 1) * LANES] = (
                lb_ref[cb, pl.ds(row0 + s, seg, stride=SUBLANES), :].astype(BF16))
    yield


def _mixer_kernel(x_ref, sh_ref, sc_ref, ga_ref, g_ref, wint_hbm, bif_ref, bift_ref,
                  wbra_hbm, wbrb_hbm, wout_hbm, cw_ref, cb_ref, wa_in, wi_in, ba_ref, bi_ref, lam_ref,
                  o_ref, c_ref, n_ref, m_ref, conv_ref, hl_ref,
                  hm_ref, wm_ref, wif_ref, wbra_ref, wbrb_ref, wout_ref, stage, sem,
                  lb_ref, tb_ref, a_ref, b_ref, tail_ref, h_ref, ha_ref, mg_ref, xn_ref, hb_ref, sa_ref,
                  wa_ref, wi_ref):
    nsq, tl, d = x_ref.shape
    dk = d // M_HEADS
    cs = min(ML_CHUNK, tl)
    scale = dk ** -0.5
    gate_row0 = 5 * d

    @pl.when((pl.program_id(0) == 0) & (pl.program_id(1) == 0))
    def _():
        _stage_jobs([(wint_hbm, 0, 5 * d, wm_ref, 0),
                     (wint_hbm, gate_row0, LANES, wif_ref, 0),
                     (wint_hbm, gate_row0 + 2 * M_HEADS, 2 * d, wm_ref, 5 * d),
                     (wbra_hbm, 0, d, wbra_ref, 0), (wbrb_hbm, 0, d, wbrb_ref, 0), (wout_hbm, 0, d, wout_ref, 0)],
                    stage, sem)
        _pair_blocks(wa_in, wa_ref)
        _pair_blocks(wi_in, wi_ref)

    @pl.when(pl.program_id(1) == 0)
    def _():
        c_ref[...] = jnp.zeros_like(c_ref)
        n_ref[...] = jnp.zeros_like(n_ref)
        m_ref[...] = jnp.zeros_like(m_ref)
        tail_ref[...] = jnp.zeros_like(tail_ref)
        h_ref[...] = jnp.zeros_like(h_ref)

    seq_rows = [pl.ds(pl.program_id(0) * nsq + q, 1) for q in range(nsq)]
    for q in range(nsq):
        xn_ref[q * tl:(q + 1) * tl, :] = _rms_mod(x_ref[q], g_ref[...], sh_ref[seq_rows[q], :],
                                                  sc_ref[seq_rows[q], :]).astype(BF16)

    _rglru_front(xn_ref, wm_ref, lb_ref, tb_ref, tail_ref, conv_ref)
    rglru_block = functools.partial(
        _rglru_block, cw_ref=cw_ref, cb_ref=cb_ref, wa_ref=wa_ref, wi_ref=wi_ref, ba_ref=ba_ref, bi_ref=bi_ref,
        lam_ref=lam_ref, lb_ref=lb_ref, tb_ref=tb_ref, a_ref=a_ref, b_ref=b_ref, h_ref=h_ref, ha_ref=ha_ref)
    n_rg = wa_ref.shape[0]
    wq0 = d

    rg_pieces = itertools.chain.from_iterable(rglru_block(p, sq) for p in range(n_rg) for sq in range(nsq))

    def rg_advance(n=1):
        for _ in range(n):
            next(rg_pieces, None)

    pre_c = _dot_nt(xn_ref[...], wif_ref[...]) + bif_ref[...]
    pre_r = _dot_nt(wif_ref[0:GATE_ROWS, :], xn_ref[...]) + bift_ref[...]
    col_is_f = lax.broadcasted_iota(jnp.int32, pre_c.shape, 1) >= M_HEADS
    row_is_f = lax.broadcasted_iota(jnp.int32, pre_r.shape, 0) >= M_HEADS
    gate_c = jnp.where(col_is_f, jax.nn.log_sigmoid(pre_c), pre_c)
    gate_r = jnp.where(row_is_f, jax.nn.log_sigmoid(pre_r), pre_r)

    ti = lax.broadcasted_iota(jnp.int32, (cs, cs), 0)
    si = lax.broadcasted_iota(jnp.int32, (cs, cs), 1)
    causal = si <= ti
    lower = causal.astype(BF16)
    upper = (ti <= si).astype(BF16)

    chunks = []
    for sq in range(nsq):
        for r0 in range(sq * tl, (sq + 1) * tl, cs):
            gc = gate_c[r0:r0 + cs, :]
            gr = gate_r[:, r0:r0 + cs]
            cum_c = sum(_dot(lower, part) for part in _split3(gc))
            cum_r = sum(_dot(part, upper) for part in _split3(gr))
            chunks.append((sq, r0, gc, gr, cum_c, cum_r))

    for h in range(M_HEADS):
        c0 = h * dk
        q_all = _dot_nt(xn_ref[...], wm_ref[wq0 + c0:wq0 + c0 + dk, :])
        k_all = _dot_nt(xn_ref[...], wm_ref[wq0 + d + c0:wq0 + d + c0 + dk, :])
        v_all = _dot_nt(xn_ref[...], wm_ref[wq0 + 2 * d + c0:wq0 + 2 * d + c0 + dk, :])
        rg_advance()
        for sq, r0, gc, gr, cum_c, cum_r in chunks:
            q = q_all[r0:r0 + cs, :]
            k = k_all[r0:r0 + cs, :]
            v = v_all[r0:r0 + cs, :]
            bc = cum_c[:, M_HEADS + h:M_HEADS + h + 1]
            br = cum_r[M_HEADS + h:M_HEADS + h + 1, :]
            ig_c = gc[:, h:h + 1]
            ig_r = gr[h:h + 1, :]
            m_prev = m_ref[sq, h:h + 1, :]
            c_prev = c_ref[sq, h]
            n_prev = n_ref[sq, h:h + 1, :]

            log_d = jnp.where(causal, bc - br + ig_r, -jnp.inf)
            log_past = bc + m_prev
            m_t = jnp.maximum(log_past, jnp.max(log_d, axis=-1, keepdims=True))
            d_mat = jnp.exp(log_d - m_t)
            past_w = jnp.exp(log_past - m_t)
            qs = q * scale
            qsb = qs.astype(BF16)
            kb = k.astype(BF16)
            vb = v.astype(BF16)
            s = _dot_nt(qsb, kb) * d_mat
            num = past_w * _dot(qsb, c_prev.astype(BF16)) + _dot(s.astype(BF16), vb)
            den = past_w * jnp.sum(qs * n_prev, axis=-1, keepdims=True) + jnp.sum(s, axis=-1, keepdims=True)
            hm_ref[r0:r0 + cs, c0:c0 + dk] = num / jnp.maximum(jnp.abs(den), jnp.exp(-m_t))

            m_new = m_t[cs - 1:cs, :]
            b_last = bc[cs - 1:cs, :]
            w_s = jnp.exp(b_last - bc + ig_c - m_new)
            decay = jnp.exp(b_last + m_prev - m_new)
            kw = w_s * k
            c_ref[sq, h] = decay * c_prev + _dot_tn(kw.astype(BF16), vb)
            n_ref[sq, h:h + 1, :] = decay * n_prev + jnp.sum(kw, axis=0, keepdims=True)
            m_ref[sq, h:h + 1, :] = m_new
            rg_advance()

    col_blocks = [slice(n0, n0 + MXU_DIM) for n0 in range(0, d, MXU_DIM)]
    for cols in col_blocks:
        o_pre = _dot_nt(xn_ref[...], wm_ref[4 * d + cols.start:4 * d + cols.stop, :])
        hb_ref[:, cols] = (jax.nn.sigmoid(o_pre) * hm_ref[:, cols]).astype(BF16)
        rg_advance()
    for cols in col_blocks:
        g_b = _dot_nt(xn_ref[...], wm_ref[6 * d + cols.start:6 * d + cols.stop, :])
        rg_advance()
        hm_ref[:, cols] = jax.nn.sigmoid(g_b) * _dot(hb_ref[...], wbrb_ref[:, cols])
        g_a = _dot_nt(xn_ref[...], wm_ref[5 * d + cols.start:5 * d + cols.stop, :])
        sa_ref[:, cols] = jax.nn.sigmoid(g_a)
        rg_advance()
    for _ in rg_pieces:
        pass
    hl_ref[...] = h_ref[...]
    for cols in col_blocks:
        merged = sa_ref[:, cols] * _dot(ha_ref[...], wbra_ref[:, cols]) + hm_ref[:, cols]
        mg_ref[:, cols] = merged.astype(BF16)
    y = _dot(mg_ref[...], wout_ref[...])
    for q in range(nsq):
        o_ref[q] = x_ref[q] + ga_ref[seq_rows[q], :] * y[q * tl:(q + 1) * tl, :]


def _mixer_prompt(x3, mod, gain, w_int, bif, bift, wbra, wbrb, wout, conv_w, conv_b, wa, wi, ba, bi, lam):
    nb, seq, _ = x3.shape
    d = x3.shape[2]
    dk = d // M_HEADS
    nsq = ML_SEQS
    tl = min(ML_ROWS, seq)
    rows = nsq * tl
    row = lambda v: v.reshape(1, d)
    mspecs, _ = _mod_specs(mod.shape[0] - nb, nb, d, 0, 3)
    rg_pairs = (wa.shape[0] // 2, 2 * wa.shape[1], 2 * wa.shape[2])
    tile = lambda: pl.BlockSpec((nsq, tl, d), lambda b, t: (b, t, 0))
    per_seq = lambda *shape: pl.BlockSpec((nsq,) + shape, lambda b, t: (b,) + (0,) * len(shape))
    return pl.pallas_call(
        _mixer_kernel,
        grid=(nb // nsq, seq // tl),
        in_specs=[tile()] + mspecs + [_resident((1, d)),
                  _HBM, _resident(bif.shape), _resident(bift.shape), _HBM, _HBM, _HBM,
                  _resident(conv_w.shape), _resident((1, d)), _resident(wa.shape), _resident(wi.shape),
                  _resident((1, d)), _resident((1, d)), _resident((1, d))],
        out_specs=[tile(), per_seq(M_HEADS, dk, dk), per_seq(M_HEADS, dk), per_seq(M_HEADS, 1),
                   per_seq(CONV_W - 1, d), per_seq(1, d)],
        out_shape=[jax.ShapeDtypeStruct((nb, seq, d), F32),
                   jax.ShapeDtypeStruct((nb, M_HEADS, dk, dk), F32),
                   jax.ShapeDtypeStruct((nb, M_HEADS, dk), F32),
                   jax.ShapeDtypeStruct((nb, M_HEADS, 1), F32),
                   jax.ShapeDtypeStruct((nb, CONV_W - 1, d), F32),
                   jax.ShapeDtypeStruct((nb, 1, d), F32)],
        scratch_shapes=[pltpu.VMEM((rows, d), F32),
                        pltpu.VMEM((7 * d, d), BF16),
                        pltpu.VMEM((LANES, d), BF16),
                        pltpu.VMEM((d, d), BF16), pltpu.VMEM((d, d), BF16), pltpu.VMEM((d, d), BF16)]
        + _stage_scratch(d)
        + [pltpu.VMEM((d // LANES, rows + nsq * SUBLANES * SUBLANES, LANES), F32),
           pltpu.VMEM((nsq, (CONV_W - 1) * SUBLANES + tl, d), F32),
           pltpu.VMEM((nsq, tl, d), F32),
           pltpu.VMEM((nsq, tl, d), F32),
           pltpu.VMEM((nsq, CONV_W - 1, d), F32),
           pltpu.VMEM((nsq, 1, d), F32),
           pltpu.VMEM((rows, d), BF16),
           pltpu.VMEM((rows, d), BF16),
           pltpu.VMEM((rows, d), BF16),
           pltpu.VMEM((rows, d), BF16),
           pltpu.VMEM((rows, d), F32),
           pltpu.VMEM(rg_pairs, BF16), pltpu.VMEM(rg_pairs, BF16)],
        compiler_params=_cparams(("arbitrary", "arbitrary")),
        name="mixer_prompt",
    )(x3, mod, mod, mod, row(gain), w_int, bif, bift, wbra, wbrb, wout,
      conv_w, row(conv_b), wa, wi, row(ba), row(bi), row(lam))


def _head_sum(x, dk):
    parts = []
    for h in range(x.shape[1] // dk):
        sl = x[:, h * dk:(h + 1) * dk]
        parts.append(jnp.broadcast_to(jnp.sum(sl, axis=-1, keepdims=True), sl.shape))
    return jnp.concatenate(parts, axis=-1)


def _head_spread(cols, lane0, dk, rows):
    return jnp.concatenate(
        [jnp.broadcast_to(cols[:, lane0 + h:lane0 + h + 1], (rows, dk)) for h in range(M_HEADS)], axis=-1)


def _dec_pre_kernel(x_ref, sh_ref, sc_ref, g_ref, wint_hbm, bif_ref,
                    cw_ref, cb_ref, wa_in, wi_in, ba_ref, bi_ref, lam_ref,
                    conv0_ref, h0_ref, n0_ref, m0_ref,
                    ha_ref, conv_ref, hs_ref, n_ref, m_ref, qt_ref, kt_ref, dec_ref, wv_ref, pw_ref, sv_ref,
                    den_ref, em_ref,
                    wm_ref, wif_ref, stage, sem, wa_ref, wi_ref):
    rows, d = x_ref.shape
    dk = d // M_HEADS
    scale = dk ** -0.5
    _pair_blocks(wa_in, wa_ref)
    _pair_blocks(wi_in, wi_ref)
    _stage_jobs([(wint_hbm, 0, 4 * d, wm_ref, 0),
                 (wint_hbm, 5 * d, LANES, wif_ref, 0)], stage, sem)
    xn = _rms_mod(x_ref[...], g_ref[...], sh_ref[...], sc_ref[...]).astype(BF16)

    x_rg = _dot_nt(xn, wm_ref[0:d, :])
    u = cb_ref[...]
    for j in range(CONV_W - 1):
        u = u + conv0_ref[j] * cw_ref[j:j + 1, :]
        if j > 0:
            conv_ref[j - 1] = conv0_ref[j]
    u = u + x_rg * cw_ref[CONV_W - 1:CONV_W, :]
    conv_ref[CONV_W - 2] = x_rg
    a, b = _rg_gates(u, wa_ref, wi_ref, ba_ref[...], bi_ref[...], lam_ref[...])
    h = a * h0_ref[...] + b
    hs_ref[...] = h
    ha_ref[...] = h

    q = _dot_nt(xn, wm_ref[d:2 * d, :])
    k = _dot_nt(xn, wm_ref[2 * d:3 * d, :])
    v = _dot_nt(xn, wm_ref[3 * d:4 * d, :])
    qt_ref[...] = _dot_nt(wm_ref[d:2 * d, :], xn).astype(BF16)
    kt_ref[...] = _dot_nt(wm_ref[2 * d:3 * d, :], xn).astype(BF16)
    pre = _dot_nt(xn, wif_ref[...]) + bif_ref[...]
    ig = _head_spread(pre, 0, dk, rows)
    lf = jax.nn.log_sigmoid(_head_spread(pre, M_HEADS, dk, rows))
    m0 = _head_spread(m0_ref[...], 0, dk, rows)
    n0 = n0_ref[...]
    log_past = lf + m0
    m_t = jnp.maximum(log_past, ig)
    d_w = jnp.exp(ig - m_t)
    past_w = jnp.exp(log_past - m_t)
    qs = q * scale
    s = _head_sum(qs * k, dk) * d_w
    den_ref[...] = past_w * _head_sum(qs * n0, dk) + s
    em_ref[...] = jnp.exp(-m_t)
    pw_ref[...] = past_w
    sv_ref[...] = s * v
    w_s = jnp.exp(ig - m_t)
    decay = jnp.exp(lf + m0 - m_t)
    dec_ref[...] = decay
    wv_ref[...] = w_s * v
    n_ref[...] = decay * n0 + w_s * k
    m_ref[...] = m_t


def _dec_pre(x2, mod, gain, w_int, bif, conv_w, conv_b, wa, wi, ba, bi, lam, conv0, h0, n0, m0):
    rows, d = x2.shape
    row = lambda v: v.reshape(1, d)
    m0p = jnp.pad(m0, ((0, 0), (0, LANES - M_HEADS)))
    full = lambda shape: pl.BlockSpec(shape, lambda i, _n=len(shape): (0,) * _n)
    mspec = lambda j: pl.BlockSpec((rows, d), lambda i: (0, j))
    vec = jax.ShapeDtypeStruct((rows, d), F32)
    outs = [vec,
            jax.ShapeDtypeStruct((CONV_W - 1, rows, d), F32),
            vec,
            vec,
            vec,
            jax.ShapeDtypeStruct((d, rows), BF16),
            jax.ShapeDtypeStruct((d, rows), BF16),
            vec, vec, vec, vec, vec, vec]
    args = [x2, mod, mod, row(gain), w_int, bif, conv_w, row(conv_b), wa, wi,
            row(ba), row(bi), row(lam), conv0, h0, n0, m0p]
    in_specs = ([full(x2.shape), mspec(0), mspec(1), full((1, d)), _HBM]
                + [full(a.shape) for a in args[5:]])
    return pl.pallas_call(
        _dec_pre_kernel,
        grid=(1,),
        in_specs=in_specs,
        out_specs=[full(o.shape) for o in outs],
        out_shape=outs,
        scratch_shapes=[pltpu.VMEM((4 * d, d), BF16), pltpu.VMEM((LANES, d), BF16)]
        + _stage_scratch(d, STAGE_SLOTS_DEC)
        + [pltpu.VMEM((wa.shape[0] // 2, 2 * wa.shape[1], 2 * wa.shape[2]), BF16)] * 2,
        compiler_params=_cparams(("arbitrary",)),
        name="decode_pre",
    )(*args)


def _dec_mem_kernel(qt_ref, kt_ref, dec_ref, wv_ref, c0_ref, c_ref, qc_ref):
    bb = c0_ref.shape[0]
    dk = c0_ref.shape[2]
    nseq = qt_ref.shape[1]
    scale = dk ** -0.5
    base = pl.program_id(0) * bb
    seq_id = lax.broadcasted_iota(jnp.int32, (nseq, dk), 0)
    for j in range(bb):
        pick = (seq_id == base + j).astype(BF16)
        qcol = _dot(qt_ref[...], pick) * scale
        kcol = _dot(kt_ref[...], pick)
        seq = pl.ds(base + j, 1)
        for h in range(M_HEADS):
            cols = slice(h * dk, (h + 1) * dk)
            c0 = c0_ref[j, h]
            dec = dec_ref[seq, cols]
            wv = wv_ref[seq, cols]
            c_ref[j, h] = dec * c0 + kcol[cols, :] * wv
            qc_ref[seq, cols] = jnp.sum(qcol[cols, :] * c0, axis=0, keepdims=True)


def _dec_mem(qt, kt, decay, wv, c0):
    nseq, heads, dk, _ = c0.shape
    d = heads * dk
    bb = DEC_BLOCK if nseq % DEC_BLOCK == 0 else 1
    whole = lambda shape: pl.BlockSpec(shape, lambda i: (0, 0))
    return pl.pallas_call(
        _dec_mem_kernel,
        grid=(nseq // bb,),
        in_specs=[whole((d, nseq)), whole((d, nseq)), whole((nseq, d)), whole((nseq, d)),
                  pl.BlockSpec((bb, heads, dk, dk), lambda i: (i, 0, 0, 0))],
        out_specs=[pl.BlockSpec((bb, heads, dk, dk), lambda i: (i, 0, 0, 0)), whole((nseq, d))],
        out_shape=[jax.ShapeDtypeStruct(c0.shape, F32), jax.ShapeDtypeStruct((nseq, d), F32)],
        compiler_params=_cparams(("arbitrary",)),
        name="decode_mem",
    )(qt, kt, decay, wv, c0)


def _dec_post_kernel(x_ref, sh_ref, sc_ref, ga_ref, g_ref, wint_hbm, wbra_hbm, wbrb_hbm, wout_hbm,
                     ha_ref, qc_ref, pw_ref, sv_ref, den_ref, em_ref, o_ref,
                     wm_ref, wbra_ref, wbrb_ref, wout_ref, stage, sem):
    d = x_ref.shape[1]
    _stage_jobs([(wint_hbm, 4 * d, d, wm_ref, 0),
                 (wint_hbm, 5 * d + 2 * M_HEADS, 2 * d, wm_ref, d),
                 (wbra_hbm, 0, d, wbra_ref, 0), (wbrb_hbm, 0, d, wbrb_ref, 0), (wout_hbm, 0, d, wout_ref, 0)],
                stage, sem)
    x = x_ref[...]
    xn = _rms_mod(x, g_ref[...], sh_ref[...], sc_ref[...]).astype(BF16)
    num = pw_ref[...] * qc_ref[...] + sv_ref[...]
    hm = num / jnp.maximum(jnp.abs(den_ref[...]), em_ref[...])
    hb = (jax.nn.sigmoid(_dot_nt(xn, wm_ref[0:d, :])) * hm).astype(BF16)
    g_a = _dot_nt(xn, wm_ref[d:2 * d, :])
    g_b = _dot_nt(xn, wm_ref[2 * d:3 * d, :])
    merged = (jax.nn.sigmoid(g_a) * _dot(ha_ref[...].astype(BF16), wbra_ref[...])
              + jax.nn.sigmoid(g_b) * _dot(hb, wbrb_ref[...]))
    o_ref[...] = x + ga_ref[...] * _dot(merged.astype(BF16), wout_ref[...])


def _dec_post(x2, mod, gain, w_int, wbra, wbrb, wout, ha, qc, pw, sv, den, em):
    rows, d = x2.shape
    full = lambda shape: pl.BlockSpec(shape, lambda i, _n=len(shape): (0,) * _n)
    mspec = lambda j: pl.BlockSpec((rows, d), lambda i: (0, j))
    args = [x2, mod, mod, mod, gain.reshape(1, d), w_int, wbra, wbrb, wout, ha, qc, pw, sv, den, em]
    in_specs = ([full(x2.shape), mspec(0), mspec(1), mspec(2), full((1, d))] + [_HBM] * 4
                + [full(a.shape) for a in args[9:]])
    return pl.pallas_call(
        _dec_post_kernel,
        grid=(1,),
        in_specs=in_specs,
        out_specs=full((rows, d)),
        out_shape=jax.ShapeDtypeStruct((rows, d), F32),
        scratch_shapes=[pltpu.VMEM((3 * d, d), BF16),
                        pltpu.VMEM((d, d), BF16), pltpu.VMEM((d, d), BF16), pltpu.VMEM((d, d), BF16)]
        + _stage_scratch(d, STAGE_SLOTS_DEC),
        compiler_params=_cparams(("arbitrary",)),
        name="decode_post",
    )(*args)


def kernel(x_prompt, x_sample, state_conv, state_rg_h, state_C, state_n, state_m, c_prompt, c_sample, w_ada, b_ada, g_norm1, w_ff1_in, w_ff1_out, g_norm2, w_in, conv_w, conv_b, w_rg_a, b_rg_a, w_rg_i, b_rg_i, rg_lambda, b_ig, b_fg, w_br_a, w_br_b, w_out, g_norm3, w_ff2_in, w_ff2_out, g_final):
    nb, seq, d = x_prompt.shape
    ns = x_sample.shape[0]
    depth = w_ada.shape[0]
    assert depth == 1 and x_sample.shape[1] == 1 and nb == SUBLANES
    assert seq % ML_ROWS == 0 and seq % FFN_ROWS == 0 and ML_ROWS % (SUBLANES * SUBLANES) == 0
    assert nb % ML_SEQS == 0
    heads, dk = M_HEADS, d // M_HEADS
    assert w_in.shape[2] == 7 * d + 2 * heads and GATE_ROWS >= 2 * heads

    w_int = w_in[0].T
    b_gates = jnp.concatenate([b_ig[0], b_fg[0]])
    bif = jnp.pad(b_gates, (0, LANES - 2 * heads)).reshape(1, LANES)
    bift = jnp.pad(b_gates, (0, GATE_ROWS - 2 * heads)).reshape(GATE_ROWS, 1)
    wa, wi = w_rg_a[0], w_rg_i[0]
    wf1i, wf1o, wf2i, wf2o = w_ff1_in[0], w_ff1_out[0], w_ff2_in[0], w_ff2_out[0]
    wbra, wbrb, wout = w_br_a[0], w_br_b[0], w_out[0]

    mod1 = _ada(c_sample, c_prompt, w_ada[0], b_ada[0], 3)

    xp, xs, mod = _ffn(x_prompt.reshape(nb * seq, d), x_sample.reshape(ns, d), mod1, 0, g_norm1[0],
                       wf1i, wf1o, None, rows_per_seq=seq, later_mod=(c_sample, c_prompt, w_ada[0], b_ada[0], 3))

    xp, c_p, n_p, m_p, conv_p, h_p = _mixer_prompt(
        xp.reshape(nb, seq, d), mod, g_norm2[0], w_int, bif, bift, wbra, wbrb, wout, conv_w[0], conv_b[0],
        wa, wi, b_rg_a[0], b_rg_i[0], rg_lambda[0])
    xp = xp.reshape(nb * seq, d)

    conv0 = jnp.swapaxes(state_conv[0], 0, 1)
    (ha_s, conv_s, h_s, n_s, m_s, qt, kt, decay, wv, pw, sv, den, em) = _dec_pre(
        xs, mod, g_norm2[0], w_int, bif, conv_w[0], conv_b[0], wa, wi,
        b_rg_a[0], b_rg_i[0], rg_lambda[0], conv0, state_rg_h[0], state_n[0].reshape(ns, d), state_m[0])
    c_s, qc = _dec_mem(qt, kt, decay, wv, state_C[0])
    xs = _dec_post(xs, mod, g_norm2[0], w_int, wbra, wbrb, wout, ha_s, qc, pw, sv, den, em)

    yp, ys = _ffn(xp, xs, mod, 3, g_norm3[0], wf2i, wf2o, g_final, rows_per_seq=seq)

    return (yp.reshape(nb, seq, d), ys.reshape(ns, 1, d),
            conv_p[None], h_p.reshape(1, nb, d), c_p[None], n_p[None], m_p.reshape(1, nb, heads),
            jnp.swapaxes(conv_s, 0, 1)[None], h_s[None], c_s[None], n_s.reshape(1, ns, heads, dk), m_s[:, ::dk][None])
```

```python
import functools
import itertools

import jax
import jax.numpy as jnp
from jax import lax
from jax.experimental import pallas as pl
from jax.experimental.pallas import tpu as pltpu

F32 = jnp.float32
BF16 = jnp.bfloat16

EPS = 1e-6
RG_C = 8.0
CONV_W = 4
M_HEADS = 4

SUBLANES = 8
LANES = 128
MXU_DIM = 256
VMEM_LIMIT_BYTES = 60 * 1024 * 1024

ADA_STEPS = 4
FFN_ROWS = 1024
FFN_COLS = 256
ML_ROWS = 512
ML_SEQS = 1
ML_CHUNK = 256
RG_PIECES = 4
DEC_BLOCK = 4
STAGE_ROWS = 256
STAGE_SLOTS = 4
STAGE_SLOTS_DEC = 12
STAGE_ROWS_WIDE = 64
STAGE_SLOTS_WIDE = 6
GATE_ROWS = 16


def _cparams(sem):
    return pltpu.CompilerParams(dimension_semantics=sem, vmem_limit_bytes=VMEM_LIMIT_BYTES)


def _resident(shape):
    nd = len(shape)
    return pl.BlockSpec(shape, lambda *_: (0,) * nd, pipeline_mode=pl.Buffered(1))


_HBM = pl.BlockSpec(memory_space=pl.ANY)


def _rms_mod(x, gain, shift, scale):
    ms = jnp.mean(x * x, axis=-1, keepdims=True)
    return x * lax.rsqrt(ms + EPS) * gain * (1.0 + scale) + shift


def _dot(a, b):
    return jnp.dot(a, b, preferred_element_type=F32)


def _dot_nt(a, b):
    return lax.dot_general(a, b, (((1,), (1,)), ((), ())), preferred_element_type=F32)


def _dot_tn(a, b):
    return lax.dot_general(a, b, (((0,), (0,)), ((), ())), preferred_element_type=F32)


def _split3(x):
    hi = x.astype(BF16)
    r1 = x - hi.astype(F32)
    mid = r1.astype(BF16)
    lo = (r1 - mid.astype(F32)).astype(BF16)
    return hi, mid, lo


def _stage_jobs(jobs, stage, sem):
    slots, rows_per_copy, _ = stage.shape
    chunks = []
    for src, src_row0, nrows, dst, dst_row0 in jobs:
        for r in range(0, nrows, rows_per_copy):
            chunks.append((src, src_row0 + r, dst, dst_row0 + r, min(rows_per_copy, nrows - r)))
    n = len(chunks)

    def copy(c):
        src, src_row, _, _, rows = chunks[c]
        return pltpu.make_async_copy(src.at[pl.ds(src_row, rows)],
                                     stage.at[c % slots, pl.ds(0, rows)], sem.at[c % slots])

    for c in range(min(slots, n)):
        copy(c).start(priority=c % 2)
    for c in range(n):
        _, _, dst, dst_row, rows = chunks[c]
        copy(c).wait()
        dst[dst_row:dst_row + rows, :] = stage[c % slots, 0:rows, :].astype(BF16)
        if c + slots < n:
            copy(c + slots).start(priority=(c + slots) % 2)


def _stage_rows(src, src_row0, nrows, dst, dst_row0, stage, sem):
    _stage_jobs([(src, src_row0, nrows, dst, dst_row0)], stage, sem)


def _stage_scratch(cols, slots=STAGE_SLOTS, rows=STAGE_ROWS):
    return [pltpu.VMEM((slots, rows, cols), F32), pltpu.SemaphoreType.DMA((slots,))]


def _ada_kernel(cs_ref, cp_ref, w_ref, b_ref, o_ref):
    ns = cs_ref.shape[0]
    w = w_ref[...].astype(BF16)
    for c_ref, rows in ((cs_ref, slice(0, ns)), (cp_ref, slice(ns, o_ref.shape[0]))):
        c = c_ref[...]
        o_ref[rows, :] = _dot((c * jax.nn.sigmoid(c)).astype(BF16), w) + b_ref[...]


def _ada(c_sample, c_prompt, w_ada, b_ada):
    ns, d = c_sample.shape
    nb = c_prompt.shape[0]
    n = w_ada.shape[1]
    tn = n // ADA_STEPS
    assert tn * ADA_STEPS == n and tn % LANES == 0 and ns % SUBLANES == 0
    return pl.pallas_call(
        _ada_kernel,
        grid=(n // tn,),
        in_specs=[pl.BlockSpec((ns, d), lambda j: (0, 0)),
                  pl.BlockSpec((nb, d), lambda j: (0, 0)),
                  pl.BlockSpec((d, tn), lambda j: (0, j)),
                  pl.BlockSpec((1, tn), lambda j: (0, j))],
        out_specs=pl.BlockSpec((ns + nb, tn), lambda j: (0, j)),
        out_shape=jax.ShapeDtypeStruct((ns + nb, n), F32),
        compiler_params=_cparams(("arbitrary",)),
        name="adaln_mod",
    )(c_sample, c_prompt, w_ada, b_ada.reshape(1, n))


def _ffn_rows(x_ref, shift, scale, gate, g_ref, wi_ref, wo_ref, gf_ref, o_ref, act_ref, d_ff):
    rows = x_ref.shape[0]
    x = x_ref[...]
    xn = _rms_mod(x, g_ref[...], shift, scale).astype(BF16)
    for j in range(d_ff // FFN_COLS):
        lo = j * FFN_COLS
        hg = _dot(xn, wi_ref[:, lo:lo + FFN_COLS])
        hu = _dot(xn, wi_ref[:, d_ff + lo:d_ff + lo + FFN_COLS])
        act_ref[0:rows, lo:lo + FFN_COLS] = (hg * jax.nn.sigmoid(hg) * hu).astype(BF16)
    y = _dot(act_ref[0:rows, :], wo_ref[...])
    out = x + (0.5 * gate) * y
    if gf_ref is not None:
        ms = jnp.mean(out * out, axis=-1, keepdims=True)
        out = out * lax.rsqrt(ms + EPS) * gf_ref[...]
    o_ref[...] = out


def _ffn_kernel(*refs, d_ff, final, n_prompt_steps, tiles_per_seq):
    n_in = 13 if final else 12
    xp_ref, shp_ref, scp_ref, gap_ref, xs_ref, shs_ref, scs_ref, gas_ref, g_ref, wi_hbm, wo_hbm = refs[:11]
    gf_ref = refs[11] if final else None
    op_ref, os_ref = refs[n_in - 1:n_in + 1]
    act_ref, wi_ref, wo_ref, stage_i, sem_i, stage_o, sem_o = refs[-7:]
    step = pl.program_id(0)

    @pl.when(step == 0)
    def _():
        _stage_rows(wi_hbm, 0, wi_ref.shape[0], wi_ref, 0, stage_i, sem_i)
        _stage_rows(wo_hbm, 0, wo_ref.shape[0], wo_ref, 0, stage_o, sem_o)

    @pl.when(step < n_prompt_steps)
    def _():
        seq = pl.ds(step // tiles_per_seq, 1)
        _ffn_rows(xp_ref, shp_ref[seq, :], scp_ref[seq, :], gap_ref[seq, :], g_ref, wi_ref, wo_ref, gf_ref,
                  op_ref, act_ref, d_ff)

    @pl.when(step == n_prompt_steps)
    def _():
        _ffn_rows(xs_ref, shs_ref[...], scs_ref[...], gas_ref[...], g_ref, wi_ref, wo_ref, gf_ref,
                  os_ref, act_ref, d_ff)


def _mod_specs(ns, nb, d, j0, n):
    assert ns % nb == 0
    prompt = [pl.BlockSpec((nb, d), lambda *_, j=j: (ns // nb, j)) for j in range(j0, j0 + n)]
    sample = [pl.BlockSpec((ns, d), lambda *_, j=j: (0, j)) for j in range(j0, j0 + n)]
    return prompt, sample


def _ffn(xp, xs, mod, j0, gain, wi, wo, g_final, *, rows_per_seq):
    rows, d = xp.shape
    ns = xs.shape[0]
    d_ff = wo.shape[0]
    tm = min(FFN_ROWS, rows_per_seq)
    tiles_per_seq = rows_per_seq // tm
    n_steps = rows // tm
    tile_of = lambda i: jnp.minimum(i, n_steps - 1)
    mspec_p, mspec_s = _mod_specs(ns, rows // rows_per_seq, d, j0, 3)
    final = g_final is not None
    in_specs = ([pl.BlockSpec((tm, d), lambda i: (tile_of(i), 0))] + mspec_p
                + [pl.BlockSpec((ns, d), lambda i: (0, 0))] + mspec_s + [_resident((1, d)), _HBM, _HBM])
    args = [xp, mod, mod, mod, xs, mod, mod, mod, gain.reshape(1, d), wi, wo]
    if final:
        in_specs.append(_resident((1, d)))
        args.append(g_final.reshape(1, d))
    return pl.pallas_call(
        functools.partial(_ffn_kernel, d_ff=d_ff, final=final, n_prompt_steps=n_steps, tiles_per_seq=tiles_per_seq),
        grid=(n_steps + 1,),
        in_specs=in_specs,
        out_specs=[pl.BlockSpec((tm, d), lambda i: (tile_of(i), 0)), pl.BlockSpec((ns, d), lambda i: (0, 0))],
        out_shape=[jax.ShapeDtypeStruct((rows, d), F32), jax.ShapeDtypeStruct((ns, d), F32)],
        scratch_shapes=[pltpu.VMEM((tm, d_ff), BF16), pltpu.VMEM(wi.shape, BF16), pltpu.VMEM(wo.shape, BF16)]
        + _stage_scratch(wi.shape[1], STAGE_SLOTS_WIDE, STAGE_ROWS_WIDE) + _stage_scratch(wo.shape[1]),
        compiler_params=_cparams(("arbitrary",)),
        name="ffn_final" if final else "ffn",
    )(*args)


def _pair_blocks(src_ref, dst_ref):
    nblk, c, _ = src_ref.shape
    dst_ref[...] = jnp.zeros(dst_ref.shape, dst_ref.dtype)
    for p in range(nblk // 2):
        dst_ref[p, 0:c, 0:c] = src_ref[2 * p].astype(BF16)
        dst_ref[p, c:2 * c, c:2 * c] = src_ref[2 * p + 1].astype(BF16)


def _rg_gates_block(u, wa, wi, ba, bi, lam):
    ub = u.astype(BF16)
    r = jax.nn.sigmoid(_dot(ub, wa) + ba)
    i_g = jax.nn.sigmoid(_dot(ub, wi) + bi)
    log_a = -RG_C * r * jax.nn.softplus(-lam)
    a = jnp.exp(log_a)
    b = jnp.sqrt(-jnp.tanh(log_a) * (a * a + 1.0)) * (i_g * u)
    return a, b


def _rg_gates(u, wa_ref, wi_ref, ba, bi, lam):
    parts = []
    for p in range(wa_ref.shape[0]):
        cols = slice(p * MXU_DIM, (p + 1) * MXU_DIM)
        parts.append(_rg_gates_block(u[:, cols], wa_ref[p], wi_ref[p], ba[:, cols], bi[:, cols], lam[:, cols]))
    return (jnp.concatenate([a for a, _ in parts], axis=-1), jnp.concatenate([b for _, b in parts], axis=-1))


def _rglru_front(xn_ref, w_rows_ref, lb_ref, tb_ref, tail_ref, conv_ref):
    nsq = tb_ref.shape[0]
    d = xn_ref.shape[1]
    tl = xn_ref.shape[0] // nsq
    seg = tl // SUBLANES
    nlb = d // LANES
    pre = (CONV_W - 1) * SUBLANES

    pitch = seg + SUBLANES
    x_rg = _dot_nt(xn_ref[...], w_rows_ref[0:d, :])
    for cb in range(nlb):
        for s in range(nsq * SUBLANES):
            lb_ref[cb, s * pitch:s * pitch + seg, :] = x_rg[s * seg:(s + 1) * seg, cb * LANES:(cb + 1) * LANES]
    first = lax.broadcasted_iota(jnp.int32, (SUBLANES, d), 0) == 0
    for q in range(nsq):
        for j in range(seg):
            for cb in range(nlb):
                tb_ref[q, pre + j * SUBLANES:pre + (j + 1) * SUBLANES, cb * LANES:(cb + 1) * LANES] = (
                    lb_ref[cb, pl.ds(q * SUBLANES * pitch + j, SUBLANES, stride=pitch), :])

        for i in range(CONV_W - 1):
            j = seg - (CONV_W - 1) + i
            slab = tb_ref[q, pre + j * SUBLANES:pre + (j + 1) * SUBLANES, :]
            tb_ref[q, i * SUBLANES:(i + 1) * SUBLANES, :] = jnp.where(first, tail_ref[q, i:i + 1, :],
                                                                     pltpu.roll(slab, 1, axis=0))
            last = slab[SUBLANES - 1:SUBLANES, :]
            tail_ref[q, i:i + 1, :] = last
            conv_ref[q, i:i + 1, :] = last


def _rglru_block(p, q, cw_ref, cb_ref, wa_ref, wi_ref, ba_ref, bi_ref, lam_ref,
                 lb_ref, tb_ref, a_ref, b_ref, h_ref, ha_ref):
    tl = a_ref.shape[1]
    seg = tl // SUBLANES
    cols = slice(p * MXU_DIM, (p + 1) * MXU_DIM)
    lane_blocks = range(p * MXU_DIM // LANES, (p + 1) * MXU_DIM // LANES)
    row0 = q * tl

    piece = tl // RG_PIECES
    for r0 in range(0, tl, piece):
        u = cb_ref[:, cols]
        for j in range(CONV_W):
            u = u + tb_ref[q, j * SUBLANES + r0:j * SUBLANES + r0 + piece, cols] * cw_ref[j:j + 1, cols]
        a, b = _rg_gates_block(u, wa_ref[p], wi_ref[p], ba_ref[:, cols], bi_ref[:, cols], lam_ref[:, cols])
        a_ref[q, r0:r0 + piece, cols] = a
        b_ref[q, r0:r0 + piece, cols] = b
        yield

    acc_a = a_ref[q, 0:SUBLANES, cols]
    acc_b = b_ref[q, 0:SUBLANES, cols]
    for j in range(1, seg):
        rows = slice(j * SUBLANES, (j + 1) * SUBLANES)
        a_j = a_ref[q, rows, cols]
        acc_b = a_j * acc_b + b_ref[q, rows, cols]
        acc_a = a_j * acc_a
        a_ref[q, rows, cols] = acc_a
        b_ref[q, rows, cols] = acc_b

    h = h_ref[q, :, cols]
    h_in = []
    for s in range(SUBLANES):
        h_in.append(h)
        h = acc_a[s:s + 1, :] * h + acc_b[s:s + 1, :]
    h_ref[q, :, cols] = h
    h_in = jnp.concatenate(h_in, axis=0)
    yield

    for j in range(seg):
        rows = slice(j * SUBLANES, (j + 1) * SUBLANES)
        h_j = a_ref[q, rows, cols] * h_in + b_ref[q, rows, cols]
        for i, cb in enumerate(lane_blocks):
            lb_ref[cb, row0 + j * SUBLANES:row0 + (j + 1) * SUBLANES, :] = h_j[:, i * LANES:(i + 1) * LANES]
    for s in range(SUBLANES):
        for cb in lane_blocks:
            ha_ref[row0 + s * seg:row0 + (s + 1) * seg, cb * LANES:(cb + 1) * LANES] = (
                lb_ref[cb, pl.ds(row0 + s, seg, stride=SUBLANES), :].astype(BF16))
    yield


def _mixer_kernel(x_ref, sh_ref, sc_ref, ga_ref, g_ref, wint_hbm, bif_ref, bift_ref,
                  wbra_hbm, wbrb_hbm, wout_hbm, cw_ref, cb_ref, wa_in, wi_in, ba_ref, bi_ref, lam_ref,
                  o_ref, c_ref, n_ref, m_ref, conv_ref, hl_ref,
                  hm_ref, wm_ref, wif_ref, wbra_ref, wbrb_ref, wout_ref, stage, sem,
                  lb_ref, tb_ref, a_ref, b_ref, tail_ref, h_ref, ha_ref, mg_ref, xn_ref, hb_ref, sa_ref,
                  wa_ref, wi_ref):
    nsq, tl, d = x_ref.shape
    dk = d // M_HEADS
    cs = min(ML_CHUNK, tl)
    scale = dk ** -0.5
    gate_row0 = 5 * d

    @pl.when((pl.program_id(0) == 0) & (pl.program_id(1) == 0))
    def _():
        _stage_jobs([(wint_hbm, 0, 5 * d, wm_ref, 0),
                     (wint_hbm, gate_row0, LANES, wif_ref, 0),
                     (wint_hbm, gate_row0 + 2 * M_HEADS, 2 * d, wm_ref, 5 * d),
                     (wbra_hbm, 0, d, wbra_ref, 0), (wbrb_hbm, 0, d, wbrb_ref, 0), (wout_hbm, 0, d, wout_ref, 0)],
                    stage, sem)
        _pair_blocks(wa_in, wa_ref)
        _pair_blocks(wi_in, wi_ref)

    @pl.when(pl.program_id(1) == 0)
    def _():
        c_ref[...] = jnp.zeros_like(c_ref)
        n_ref[...] = jnp.zeros_like(n_ref)
        m_ref[...] = jnp.zeros_like(m_ref)
        tail_ref[...] = jnp.zeros_like(tail_ref)
        h_ref[...] = jnp.zeros_like(h_ref)

    seq_rows = [pl.ds(pl.program_id(0) * nsq + q, 1) for q in range(nsq)]
    for q in range(nsq):
        xn_ref[q * tl:(q + 1) * tl, :] = _rms_mod(x_ref[q], g_ref[...], sh_ref[seq_rows[q], :],
                                                  sc_ref[seq_rows[q], :]).astype(BF16)

    _rglru_front(xn_ref, wm_ref, lb_ref, tb_ref, tail_ref, conv_ref)
    rglru_block = functools.partial(
        _rglru_block, cw_ref=cw_ref, cb_ref=cb_ref, wa_ref=wa_ref, wi_ref=wi_ref, ba_ref=ba_ref, bi_ref=bi_ref,
        lam_ref=lam_ref, lb_ref=lb_ref, tb_ref=tb_ref, a_ref=a_ref, b_ref=b_ref, h_ref=h_ref, ha_ref=ha_ref)
    n_rg = wa_ref.shape[0]
    wq0 = d

    rg_pieces = itertools.chain.from_iterable(rglru_block(p, sq) for p in range(n_rg) for sq in range(nsq))

    def rg_advance(n=1):
        for _ in range(n):
            next(rg_pieces, None)

    pre_c = _dot_nt(xn_ref[...], wif_ref[...]) + bif_ref[...]
    pre_r = _dot_nt(wif_ref[0:GATE_ROWS, :], xn_ref[...]) + bift_ref[...]
    col_is_f = lax.broadcasted_iota(jnp.int32, pre_c.shape, 1) >= M_HEADS
    row_is_f = lax.broadcasted_iota(jnp.int32, pre_r.shape, 0) >= M_HEADS
    gate_c = jnp.where(col_is_f, jax.nn.log_sigmoid(pre_c), pre_c)
    gate_r = jnp.where(row_is_f, jax.nn.log_sigmoid(pre_r), pre_r)

    ti = lax.broadcasted_iota(jnp.int32, (cs, cs), 0)
    si = lax.broadcasted_iota(jnp.int32, (cs, cs), 1)
    causal = si <= ti
    lower = causal.astype(BF16)
    upper = (ti <= si).astype(BF16)

    chunks = []
    for sq in range(nsq):
        for r0 in range(sq * tl, (sq + 1) * tl, cs):
            gc = gate_c[r0:r0 + cs, :]
            gr = gate_r[:, r0:r0 + cs]
            cum_c = sum(_dot(lower, part) for part in _split3(gc))
            cum_r = sum(_dot(part, upper) for part in _split3(gr))
            chunks.append((sq, r0, gc, gr, cum_c, cum_r))

    for h in range(M_HEADS):
        c0 = h * dk
        q_all = _dot_nt(xn_ref[...], wm_ref[wq0 + c0:wq0 + c0 + dk, :])
        k_all = _dot_nt(xn_ref[...], wm_ref[wq0 + d + c0:wq0 + d + c0 + dk, :])
        v_all = _dot_nt(xn_ref[...], wm_ref[wq0 + 2 * d + c0:wq0 + 2 * d + c0 + dk, :])
        rg_advance()
        for sq, r0, gc, gr, cum_c, cum_r in chunks:
            q = q_all[r0:r0 + cs, :]
            k = k_all[r0:r0 + cs, :]
            v = v_all[r0:r0 + cs, :]
            bc = cum_c[:, M_HEADS + h:M_HEADS + h + 1]
            br = cum_r[M_HEADS + h:M_HEADS + h + 1, :]
            ig_c = gc[:, h:h + 1]
            ig_r = gr[h:h + 1, :]
            m_prev = m_ref[sq, h:h + 1, :]
            c_prev = c_ref[sq, h]
            n_prev = n_ref[sq, h:h + 1, :]

            log_d = jnp.where(causal, bc - br + ig_r, -jnp.inf)
            log_past = bc + m_prev
            m_t = jnp.maximum(log_past, jnp.max(log_d, axis=-1, keepdims=True))
            d_mat = jnp.exp(log_d - m_t)
            past_w = jnp.exp(log_past - m_t)
            qs = q * scale
            qsb = qs.astype(BF16)
            kb = k.astype(BF16)
            vb = v.astype(BF16)
            s = _dot_nt(qsb, kb) * d_mat
            num = past_w * _dot(qsb, c_prev.astype(BF16)) + _dot(s.astype(BF16), vb)
            den = past_w * jnp.sum(qs * n_prev, axis=-1, keepdims=True) + jnp.sum(s, axis=-1, keepdims=True)
            hm_ref[r0:r0 + cs, c0:c0 + dk] = num / jnp.maximum(jnp.abs(den), jnp.exp(-m_t))

            m_new = m_t[cs - 1:cs, :]
            b_last = bc[cs - 1:cs, :]
            w_s = jnp.exp(b_last - bc + ig_c - m_new)
            decay = jnp.exp(b_last + m_prev - m_new)
            kw = w_s * k
            c_ref[sq, h] = decay * c_prev + _dot_tn(kw.astype(BF16), vb)
            n_ref[sq, h:h + 1, :] = decay * n_prev + jnp.sum(kw, axis=0, keepdims=True)
            m_ref[sq, h:h + 1, :] = m_new
            rg_advance()

    col_blocks = [slice(n0, n0 + MXU_DIM) for n0 in range(0, d, MXU_DIM)]
    for cols in col_blocks:
        o_pre = _dot_nt(xn_ref[...], wm_ref[4 * d + cols.start:4 * d + cols.stop, :])
        hb_ref[:, cols] = (jax.nn.sigmoid(o_pre) * hm_ref[:, cols]).astype(BF16)
        rg_advance()
    for cols in col_blocks:
        g_b = _dot_nt(xn_ref[...], wm_ref[6 * d + cols.start:6 * d + cols.stop, :])
        rg_advance()
        hm_ref[:, cols] = jax.nn.sigmoid(g_b) * _dot(hb_ref[...], wbrb_ref[:, cols])
        g_a = _dot_nt(xn_ref[...], wm_ref[5 * d + cols.start:5 * d + cols.stop, :])
        sa_ref[:, cols] = jax.nn.sigmoid(g_a)
        rg_advance()
    for _ in rg_pieces:
        pass
    hl_ref[...] = h_ref[...]
    for cols in col_blocks:
        merged = sa_ref[:, cols] * _dot(ha_ref[...], wbra_ref[:, cols]) + hm_ref[:, cols]
        mg_ref[:, cols] = merged.astype(BF16)
    y = _dot(mg_ref[...], wout_ref[...])
    for q in range(nsq):
        o_ref[q] = x_ref[q] + ga_ref[seq_rows[q], :] * y[q * tl:(q + 1) * tl, :]


def _mixer_prompt(x3, mod, gain, w_int, bif, bift, wbra, wbrb, wout, conv_w, conv_b, wa, wi, ba, bi, lam):
    nb, seq, _ = x3.shape
    d = x3.shape[2]
    dk = d // M_HEADS
    nsq = ML_SEQS
    tl = min(ML_ROWS, seq)
    rows = nsq * tl
    row = lambda v: v.reshape(1, d)
    mspecs, _ = _mod_specs(mod.shape[0] - nb, nb, d, 3, 3)
    rg_pairs = (wa.shape[0] // 2, 2 * wa.shape[1], 2 * wa.shape[2])
    tile = lambda: pl.BlockSpec((nsq, tl, d), lambda b, t: (b, t, 0))
    per_seq = lambda *shape: pl.BlockSpec((nsq,) + shape, lambda b, t: (b,) + (0,) * len(shape))
    return pl.pallas_call(
        _mixer_kernel,
        grid=(nb // nsq, seq // tl),
        in_specs=[tile()] + mspecs + [_resident((1, d)),
                  _HBM, _resident(bif.shape), _resident(bift.shape), _HBM, _HBM, _HBM,
                  _resident(conv_w.shape), _resident((1, d)), _resident(wa.shape), _resident(wi.shape),
                  _resident((1, d)), _resident((1, d)), _resident((1, d))],
        out_specs=[tile(), per_seq(M_HEADS, dk, dk), per_seq(M_HEADS, dk), per_seq(M_HEADS, 1),
                   per_seq(CONV_W - 1, d), per_seq(1, d)],
        out_shape=[jax.ShapeDtypeStruct((nb, seq, d), F32),
                   jax.ShapeDtypeStruct((nb, M_HEADS, dk, dk), F32),
                   jax.ShapeDtypeStruct((nb, M_HEADS, dk), F32),
                   jax.ShapeDtypeStruct((nb, M_HEADS, 1), F32),
                   jax.ShapeDtypeStruct((nb, CONV_W - 1, d), F32),
                   jax.ShapeDtypeStruct((nb, 1, d), F32)],
        scratch_shapes=[pltpu.VMEM((rows, d), F32),
                        pltpu.VMEM((7 * d, d), BF16),
                        pltpu.VMEM((LANES, d), BF16),
                        pltpu.VMEM((d, d), BF16), pltpu.VMEM((d, d), BF16), pltpu.VMEM((d, d), BF16)]
        + _stage_scratch(d)
        + [pltpu.VMEM((d // LANES, rows + nsq * SUBLANES * SUBLANES, LANES), F32),
           pltpu.VMEM((nsq, (CONV_W - 1) * SUBLANES + tl, d), F32),
           pltpu.VMEM((nsq, tl, d), F32),
           pltpu.VMEM((nsq, tl, d), F32),
           pltpu.VMEM((nsq, CONV_W - 1, d), F32),
           pltpu.VMEM((nsq, 1, d), F32),
           pltpu.VMEM((rows, d), BF16),
           pltpu.VMEM((rows, d), BF16),
           pltpu.VMEM((rows, d), BF16),
           pltpu.VMEM((rows, d), BF16),
           pltpu.VMEM((rows, d), F32),
           pltpu.VMEM(rg_pairs, BF16), pltpu.VMEM(rg_pairs, BF16)],
        compiler_params=_cparams(("arbitrary", "arbitrary")),
        name="mixer_prompt",
    )(x3, mod, mod, mod, row(gain), w_int, bif, bift, wbra, wbrb, wout,
      conv_w, row(conv_b), wa, wi, row(ba), row(bi), row(lam))


def _head_sum(x, dk):
    parts = []
    for h in range(x.shape[1] // dk):
        sl = x[:, h * dk:(h + 1) * dk]
        parts.append(jnp.broadcast_to(jnp.sum(sl, axis=-1, keepdims=True), sl.shape))
    return jnp.concatenate(parts, axis=-1)


def _head_spread(cols, lane0, dk, rows):
    return jnp.concatenate(
        [jnp.broadcast_to(cols[:, lane0 + h:lane0 + h + 1], (rows, dk)) for h in range(M_HEADS)], axis=-1)


def _dec_pre_kernel(x_ref, sh_ref, sc_ref, g_ref, wint_hbm, bif_ref,
                    cw_ref, cb_ref, wa_in, wi_in, ba_ref, bi_ref, lam_ref,
                    conv0_ref, h0_ref, n0_ref, m0_ref,
                    ha_ref, conv_ref, hs_ref, n_ref, m_ref, qt_ref, kt_ref, dec_ref, wv_ref, pw_ref, sv_ref,
                    den_ref, em_ref,
                    wm_ref, wif_ref, stage, sem, wa_ref, wi_ref):
    rows, d = x_ref.shape
    dk = d // M_HEADS
    scale = dk ** -0.5
    _pair_blocks(wa_in, wa_ref)
    _pair_blocks(wi_in, wi_ref)
    _stage_jobs([(wint_hbm, 0, 4 * d, wm_ref, 0),
                 (wint_hbm, 5 * d, LANES, wif_ref, 0)], stage, sem)
    xn = _rms_mod(x_ref[...], g_ref[...], sh_ref[...], sc_ref[...]).astype(BF16)

    x_rg = _dot_nt(xn, wm_ref[0:d, :])
    u = cb_ref[...]
    for j in range(CONV_W - 1):
        u = u + conv0_ref[j] * cw_ref[j:j + 1, :]
        if j > 0:
            conv_ref[j - 1] = conv0_ref[j]
    u = u + x_rg * cw_ref[CONV_W - 1:CONV_W, :]
    conv_ref[CONV_W - 2] = x_rg
    a, b = _rg_gates(u, wa_ref, wi_ref, ba_ref[...], bi_ref[...], lam_ref[...])
    h = a * h0_ref[...] + b
    hs_ref[...] = h
    ha_ref[...] = h

    q = _dot_nt(xn, wm_ref[d:2 * d, :])
    k = _dot_nt(xn, wm_ref[2 * d:3 * d, :])
    v = _dot_nt(xn, wm_ref[3 * d:4 * d, :])
    qt_ref[...] = _dot_nt(wm_ref[d:2 * d, :], xn).astype(BF16)
    kt_ref[...] = _dot_nt(wm_ref[2 * d:3 * d, :], xn).astype(BF16)
    pre = _dot_nt(xn, wif_ref[...]) + bif_ref[...]
    ig = _head_spread(pre, 0, dk, rows)
    lf = jax.nn.log_sigmoid(_head_spread(pre, M_HEADS, dk, rows))
    m0 = _head_spread(m0_ref[...], 0, dk, rows)
    n0 = n0_ref[...]
    log_past = lf + m0
    m_t = jnp.maximum(log_past, ig)
    d_w = jnp.exp(ig - m_t)
    past_w = jnp.exp(log_past - m_t)
    qs = q * scale
    s = _head_sum(qs * k, dk) * d_w
    den_ref[...] = past_w * _head_sum(qs * n0, dk) + s
    em_ref[...] = jnp.exp(-m_t)
    pw_ref[...] = past_w
    sv_ref[...] = s * v
    w_s = jnp.exp(ig - m_t)
    decay = jnp.exp(lf + m0 - m_t)
    dec_ref[...] = decay
    wv_ref[...] = w_s * v
    n_ref[...] = decay * n0 + w_s * k
    m_ref[...] = m_t


def _dec_pre(x2, mod, gain, w_int, bif, conv_w, conv_b, wa, wi, ba, bi, lam, conv0, h0, n0, m0):
    rows, d = x2.shape
    row = lambda v: v.reshape(1, d)
    m0p = jnp.pad(m0, ((0, 0), (0, LANES - M_HEADS)))
    full = lambda shape: pl.BlockSpec(shape, lambda i, _n=len(shape): (0,) * _n)
    mspec = lambda j: pl.BlockSpec((rows, d), lambda i: (0, j))
    vec = jax.ShapeDtypeStruct((rows, d), F32)
    outs = [vec,
            jax.ShapeDtypeStruct((CONV_W - 1, rows, d), F32),
            vec,
            vec,
            vec,
            jax.ShapeDtypeStruct((d, rows), BF16),
            jax.ShapeDtypeStruct((d, rows), BF16),
            vec, vec, vec, vec, vec, vec]
    args = [x2, mod, mod, row(gain), w_int, bif, conv_w, row(conv_b), wa, wi,
            row(ba), row(bi), row(lam), conv0, h0, n0, m0p]
    in_specs = ([full(x2.shape), mspec(3), mspec(4), full((1, d)), _HBM]
                + [full(a.shape) for a in args[5:]])
    return pl.pallas_call(
        _dec_pre_kernel,
        grid=(1,),
        in_specs=in_specs,
        out_specs=[full(o.shape) for o in outs],
        out_shape=outs,
        scratch_shapes=[pltpu.VMEM((4 * d, d), BF16), pltpu.VMEM((LANES, d), BF16)]
        + _stage_scratch(d, STAGE_SLOTS_DEC)
        + [pltpu.VMEM((wa.shape[0] // 2, 2 * wa.shape[1], 2 * wa.shape[2]), BF16)] * 2,
        compiler_params=_cparams(("arbitrary",)),
        name="decode_pre",
    )(*args)


def _dec_mem_kernel(qt_ref, kt_ref, dec_ref, wv_ref, c0_ref, c_ref, qc_ref):
    bb = c0_ref.shape[0]
    dk = c0_ref.shape[2]
    nseq = qt_ref.shape[1]
    scale = dk ** -0.5
    base = pl.program_id(0) * bb
    seq_id = lax.broadcasted_iota(jnp.int32, (nseq, dk), 0)
    for j in range(bb):
        pick = (seq_id == base + j).astype(BF16)
        qcol = _dot(qt_ref[...], pick) * scale
        kcol = _dot(kt_ref[...], pick)
        seq = pl.ds(base + j, 1)
        for h in range(M_HEADS):
            cols = slice(h * dk, (h + 1) * dk)
            c0 = c0_ref[j, h]
            dec = dec_ref[seq, cols]
            wv = wv_ref[seq, cols]
            c_ref[j, h] = dec * c0 + kcol[cols, :] * wv
            qc_ref[seq, cols] = jnp.sum(qcol[cols, :] * c0, axis=0, keepdims=True)


def _dec_mem(qt, kt, decay, wv, c0):
    nseq, heads, dk, _ = c0.shape
    d = heads * dk
    bb = DEC_BLOCK if nseq % DEC_BLOCK == 0 else 1
    whole = lambda shape: pl.BlockSpec(shape, lambda i: (0, 0))
    return pl.pallas_call(
        _dec_mem_kernel,
        grid=(nseq // bb,),
        in_specs=[whole((d, nseq)), whole((d, nseq)), whole((nseq, d)), whole((nseq, d)),
                  pl.BlockSpec((bb, heads, dk, dk), lambda i: (i, 0, 0, 0))],
        out_specs=[pl.BlockSpec((bb, heads, dk, dk), lambda i: (i, 0, 0, 0)), whole((nseq, d))],
        out_shape=[jax.ShapeDtypeStruct(c0.shape, F32), jax.ShapeDtypeStruct((nseq, d), F32)],
        compiler_params=_cparams(("arbitrary",)),
        name="decode_mem",
    )(qt, kt, decay, wv, c0)


def _dec_post_kernel(x_ref, sh_ref, sc_ref, ga_ref, g_ref, wint_hbm, wbra_hbm, wbrb_hbm, wout_hbm,
                     ha_ref, qc_ref, pw_ref, sv_ref, den_ref, em_ref, o_ref,
                     wm_ref, wbra_ref, wbrb_ref, wout_ref, stage, sem):
    d = x_ref.shape[1]
    _stage_jobs([(wint_hbm, 4 * d, d, wm_ref, 0),
                 (wint_hbm, 5 * d + 2 * M_HEADS, 2 * d, wm_ref, d),
                 (wbra_hbm, 0, d, wbra_ref, 0), (wbrb_hbm, 0, d, wbrb_ref, 0), (wout_hbm, 0, d, wout_ref, 0)],
                stage, sem)
    x = x_ref[...]
    xn = _rms_mod(x, g_ref[...], sh_ref[...], sc_ref[...]).astype(BF16)
    num = pw_ref[...] * qc_ref[...] + sv_ref[...]
    hm = num / jnp.maximum(jnp.abs(den_ref[...]), em_ref[...])
    hb = (jax.nn.sigmoid(_dot_nt(xn, wm_ref[0:d, :])) * hm).astype(BF16)
    g_a = _dot_nt(xn, wm_ref[d:2 * d, :])
    g_b = _dot_nt(xn, wm_ref[2 * d:3 * d, :])
    merged = (jax.nn.sigmoid(g_a) * _dot(ha_ref[...].astype(BF16), wbra_ref[...])
              + jax.nn.sigmoid(g_b) * _dot(hb, wbrb_ref[...]))
    o_ref[...] = x + ga_ref[...] * _dot(merged.astype(BF16), wout_ref[...])


def _dec_post(x2, mod, gain, w_int, wbra, wbrb, wout, ha, qc, pw, sv, den, em):
    rows, d = x2.shape
    full = lambda shape: pl.BlockSpec(shape, lambda i, _n=len(shape): (0,) * _n)
    mspec = lambda j: pl.BlockSpec((rows, d), lambda i: (0, j))
    args = [x2, mod, mod, mod, gain.reshape(1, d), w_int, wbra, wbrb, wout, ha, qc, pw, sv, den, em]
    in_specs = ([full(x2.shape), mspec(3), mspec(4), mspec(5), full((1, d))] + [_HBM] * 4
                + [full(a.shape) for a in args[9:]])
    return pl.pallas_call(
        _dec_post_kernel,
        grid=(1,),
        in_specs=in_specs,
        out_specs=full((rows, d)),
        out_shape=jax.ShapeDtypeStruct((rows, d), F32),
        scratch_shapes=[pltpu.VMEM((3 * d, d), BF16),
                        pltpu.VMEM((d, d), BF16), pltpu.VMEM((d, d), BF16), pltpu.VMEM((d, d), BF16)]
        + _stage_scratch(d, STAGE_SLOTS_DEC),
        compiler_params=_cparams(("arbitrary",)),
        name="decode_post",
    )(*args)


def kernel(x_prompt, x_sample, state_conv, state_rg_h, state_C, state_n, state_m, c_prompt, c_sample, w_ada, b_ada, g_norm1, w_ff1_in, w_ff1_out, g_norm2, w_in, conv_w, conv_b, w_rg_a, b_rg_a, w_rg_i, b_rg_i, rg_lambda, b_ig, b_fg, w_br_a, w_br_b, w_out, g_norm3, w_ff2_in, w_ff2_out, g_final):
    nb, seq, d = x_prompt.shape
    ns = x_sample.shape[0]
    depth = w_ada.shape[0]
    assert depth == 1 and x_sample.shape[1] == 1 and nb == SUBLANES
    assert seq % ML_ROWS == 0 and seq % FFN_ROWS == 0 and ML_ROWS % (SUBLANES * SUBLANES) == 0
    assert nb % ML_SEQS == 0
    heads, dk = M_HEADS, d // M_HEADS
    assert w_in.shape[2] == 7 * d + 2 * heads and GATE_ROWS >= 2 * heads

    w_int = w_in[0].T
    b_gates = jnp.concatenate([b_ig[0], b_fg[0]])
    bif = jnp.pad(b_gates, (0, LANES - 2 * heads)).reshape(1, LANES)
    bift = jnp.pad(b_gates, (0, GATE_ROWS - 2 * heads)).reshape(GATE_ROWS, 1)
    wa, wi = w_rg_a[0], w_rg_i[0]
    wf1i, wf1o, wf2i, wf2o = w_ff1_in[0], w_ff1_out[0], w_ff2_in[0], w_ff2_out[0]
    wbra, wbrb, wout = w_br_a[0], w_br_b[0], w_out[0]

    mod = _ada(c_sample, c_prompt, w_ada[0], b_ada[0])

    xp, xs = _ffn(x_prompt.reshape(nb * seq, d), x_sample.reshape(ns, d), mod, 0, g_norm1[0],
                  wf1i, wf1o, None, rows_per_seq=seq)

    xp, c_p, n_p, m_p, conv_p, h_p = _mixer_prompt(
        xp.reshape(nb, seq, d), mod, g_norm2[0], w_int, bif, bift, wbra, wbrb, wout, conv_w[0], conv_b[0],
        wa, wi, b_rg_a[0], b_rg_i[0], rg_lambda[0])
    xp = xp.reshape(nb * seq, d)

    conv0 = jnp.swapaxes(state_conv[0], 0, 1)
    (ha_s, conv_s, h_s, n_s, m_s, qt, kt, decay, wv, pw, sv, den, em) = _dec_pre(
        xs, mod, g_norm2[0], w_int, bif, conv_w[0], conv_b[0], wa, wi,
        b_rg_a[0], b_rg_i[0], rg_lambda[0], conv0, state_rg_h[0], state_n[0].reshape(ns, d), state_m[0])
    c_s, qc = _dec_mem(qt, kt, decay, wv, state_C[0])
    xs = _dec_post(xs, mod, g_norm2[0], w_int, wbra, wbrb, wout, ha_s, qc, pw, sv, den, em)

    yp, ys = _ffn(xp, xs, mod, 6, g_norm3[0], wf2i, wf2o, g_final, rows_per_seq=seq)

    return (yp.reshape(nb, seq, d), ys.reshape(ns, 1, d),
            conv_p[None], h_p.reshape(1, nb, d), c_p[None], n_p[None], m_p.reshape(1, nb, heads),
            jnp.swapaxes(conv_s, 0, 1)[None], h_s[None], c_s[None], n_s.reshape(1, ns, heads, dk), m_s[:, ::dk][None])
```

```python
import functools
import itertools

import jax
import jax.numpy as jnp
from jax import lax
from jax.experimental import pallas as pl
from jax.experimental.pallas import tpu as pltpu

F32 = jnp.float32
BF16 = jnp.bfloat16

EPS = 1e-6
RG_C = 8.0
CONV_W = 4
M_HEADS = 4

SUBLANES = 8
LANES = 128
MXU_DIM = 256
VMEM_LIMIT_BYTES = 60 * 1024 * 1024

ADA_STEPS = 3
FFN_ROWS = 1024
FFN_COLS = 256
ML_ROWS = 512
ML_SEQS = 1
ML_CHUNK = 256
RG_PIECES = 4
DEC_BLOCK = 8
STAGE_ROWS = 256
STAGE_SLOTS = 4
STAGE_SLOTS_DEC = 12
STAGE_ROWS_WIDE = 64
STAGE_SLOTS_WIDE = 6
GATE_ROWS = 16


def _cparams(sem):
    return pltpu.CompilerParams(dimension_semantics=sem, vmem_limit_bytes=VMEM_LIMIT_BYTES)


def _resident(shape):
    nd = len(shape)
    return pl.BlockSpec(shape, lambda *_: (0,) * nd, pipeline_mode=pl.Buffered(1))


_HBM = pl.BlockSpec(memory_space=pl.ANY)


def _rms_mod(x, gain, shift, scale):
    ms = jnp.mean(x * x, axis=-1, keepdims=True)
    return x * lax.rsqrt(ms + EPS) * gain * (1.0 + scale) + shift


def _dot(a, b):
    return jnp.dot(a, b, preferred_element_type=F32)


def _dot_nt(a, b):
    return lax.dot_general(a, b, (((1,), (1,)), ((), ())), preferred_element_type=F32)


def _dot_tn(a, b):
    return lax.dot_general(a, b, (((0,), (0,)), ((), ())), preferred_element_type=F32)


def _split3(x):
    hi = x.astype(BF16)
    r1 = x - hi.astype(F32)
    mid = r1.astype(BF16)
    lo = (r1 - mid.astype(F32)).astype(BF16)
    return hi, mid, lo


def _stage_groups(groups):
    plans = []
    for jobs, stage, sem in groups:
        slots, rows_per_copy, _ = stage.shape
        chunks = []
        for src, src_row0, nrows, dst, dst_row0 in jobs:
            for r in range(0, nrows, rows_per_copy):
                chunks.append((src, src_row0 + r, dst, dst_row0 + r, min(rows_per_copy, nrows - r)))

        def copy(c, chunks=chunks, stage=stage, sem=sem, slots=slots):
            src, src_row, _, _, rows = chunks[c]
            return pltpu.make_async_copy(src.at[pl.ds(src_row, rows)],
                                         stage.at[c % slots, pl.ds(0, rows)], sem.at[c % slots])
        plans.append((copy, chunks, stage, slots))

    for copy, chunks, _, slots in plans:
        for c in range(min(slots, len(chunks))):
            copy(c).start(priority=c % 2)
    for copy, chunks, stage, slots in plans:
        for c, (_, _, dst, dst_row, rows) in enumerate(chunks):
            copy(c).wait()
            dst[dst_row:dst_row + rows, :] = stage[c % slots, 0:rows, :].astype(BF16)
            if c + slots < len(chunks):
                copy(c + slots).start(priority=(c + slots) % 2)


def _stage_jobs(jobs, stage, sem):
    _stage_groups([(jobs, stage, sem)])


def _stage_scratch(cols, slots=STAGE_SLOTS, rows=STAGE_ROWS):
    return [pltpu.VMEM((slots, rows, cols), F32), pltpu.SemaphoreType.DMA((slots,))]


def _ada_kernel(cs_ref, cp_ref, w_ref, b_ref, o_ref):
    ns = cs_ref.shape[0]
    w = w_ref[...].astype(BF16)
    for c_ref, rows in ((cs_ref, slice(0, ns)), (cp_ref, slice(ns, o_ref.shape[0]))):
        c = c_ref[...]
        o_ref[rows, :] = _dot((c * jax.nn.sigmoid(c)).astype(BF16), w) + b_ref[...]


def _ada(c_sample, c_prompt, w_ada, b_ada):
    ns, d = c_sample.shape
    nb = c_prompt.shape[0]
    n = w_ada.shape[1]
    tn = n // ADA_STEPS
    assert tn * ADA_STEPS == n and tn % LANES == 0 and ns % SUBLANES == 0
    return pl.pallas_call(
        _ada_kernel,
        grid=(n // tn,),
        in_specs=[pl.BlockSpec((ns, d), lambda j: (0, 0)),
                  pl.BlockSpec((nb, d), lambda j: (0, 0)),
                  pl.BlockSpec((d, tn), lambda j: (0, j)),
                  pl.BlockSpec((1, tn), lambda j: (0, j))],
        out_specs=pl.BlockSpec((ns + nb, tn), lambda j: (0, j)),
        out_shape=jax.ShapeDtypeStruct((ns + nb, n), F32),
        compiler_params=_cparams(("arbitrary",)),
        name="adaln_mod",
    )(c_sample, c_prompt, w_ada, b_ada.reshape(1, n))


def _ffn_rows(x_ref, shift, scale, gate, g_ref, wi_ref, wo_ref, gf_ref, o_ref, act_ref, d_ff):
    rows = x_ref.shape[0]
    x = x_ref[...]
    xn = _rms_mod(x, g_ref[...], shift, scale).astype(BF16)
    for j in range(d_ff // FFN_COLS):
        lo = j * FFN_COLS
        hg = _dot(xn, wi_ref[:, lo:lo + FFN_COLS])
        hu = _dot(xn, wi_ref[:, d_ff + lo:d_ff + lo + FFN_COLS])
        act_ref[0:rows, lo:lo + FFN_COLS] = (hg * jax.nn.sigmoid(hg) * hu).astype(BF16)
    y = _dot(act_ref[0:rows, :], wo_ref[...])
    out = x + (0.5 * gate) * y
    if gf_ref is not None:
        ms = jnp.mean(out * out, axis=-1, keepdims=True)
        out = out * lax.rsqrt(ms + EPS) * gf_ref[...]
    o_ref[...] = out


def _ffn_kernel(*refs, d_ff, final, n_prompt_steps, tiles_per_seq):
    n_in = 13 if final else 12
    xp_ref, shp_ref, scp_ref, gap_ref, xs_ref, shs_ref, scs_ref, gas_ref, g_ref, wi_hbm, wo_hbm = refs[:11]
    gf_ref = refs[11] if final else None
    op_ref, os_ref = refs[n_in - 1:n_in + 1]
    act_ref, wi_ref, wo_ref, stage_i, sem_i, stage_o, sem_o = refs[-7:]
    step = pl.program_id(0)

    @pl.when(step == 0)
    def _():
        _stage_groups([([(wi_hbm, 0, wi_ref.shape[0], wi_ref, 0)], stage_i, sem_i),
                       ([(wo_hbm, 0, wo_ref.shape[0], wo_ref, 0)], stage_o, sem_o)])

    @pl.when(step < n_prompt_steps)
    def _():
        seq = pl.ds(step // tiles_per_seq, 1)
        _ffn_rows(xp_ref, shp_ref[seq, :], scp_ref[seq, :], gap_ref[seq, :], g_ref, wi_ref, wo_ref, gf_ref,
                  op_ref, act_ref, d_ff)

    @pl.when(step == n_prompt_steps)
    def _():
        _ffn_rows(xs_ref, shs_ref[...], scs_ref[...], gas_ref[...], g_ref, wi_ref, wo_ref, gf_ref,
                  os_ref, act_ref, d_ff)


def _mod_specs(ns, nb, d, j0, n):
    assert ns % nb == 0
    prompt = [pl.BlockSpec((nb, d), lambda *_, j=j: (ns // nb, j)) for j in range(j0, j0 + n)]
    sample = [pl.BlockSpec((ns, d), lambda *_, j=j: (0, j)) for j in range(j0, j0 + n)]
    return prompt, sample


def _ffn(xp, xs, mod, j0, gain, wi, wo, g_final, *, rows_per_seq):
    rows, d = xp.shape
    ns = xs.shape[0]
    d_ff = wo.shape[0]
    tm = min(FFN_ROWS, rows_per_seq)
    tiles_per_seq = rows_per_seq // tm
    n_steps = rows // tm
    tile_of = lambda i: jnp.minimum(i, n_steps - 1)
    mspec_p, mspec_s = _mod_specs(ns, rows // rows_per_seq, d, j0, 3)
    final = g_final is not None
    in_specs = ([pl.BlockSpec((tm, d), lambda i: (tile_of(i), 0))] + mspec_p
                + [pl.BlockSpec((ns, d), lambda i: (0, 0))] + mspec_s + [_resident((1, d)), _HBM, _HBM])
    args = [xp, mod, mod, mod, xs, mod, mod, mod, gain.reshape(1, d), wi, wo]
    if final:
        in_specs.append(_resident((1, d)))
        args.append(g_final.reshape(1, d))
    return pl.pallas_call(
        functools.partial(_ffn_kernel, d_ff=d_ff, final=final, n_prompt_steps=n_steps, tiles_per_seq=tiles_per_seq),
        grid=(n_steps + 1,),
        in_specs=in_specs,
        out_specs=[pl.BlockSpec((tm, d), lambda i: (tile_of(i), 0)), pl.BlockSpec((ns, d), lambda i: (0, 0))],
        out_shape=[jax.ShapeDtypeStruct((rows, d), F32), jax.ShapeDtypeStruct((ns, d), F32)],
        scratch_shapes=[pltpu.VMEM((tm, d_ff), BF16), pltpu.VMEM(wi.shape, BF16), pltpu.VMEM(wo.shape, BF16)]
        + _stage_scratch(wi.shape[1], STAGE_SLOTS_WIDE, STAGE_ROWS_WIDE) + _stage_scratch(wo.shape[1]),
        compiler_params=_cparams(("arbitrary",)),
        name="ffn_final" if final else "ffn",
    )(*args)


def _pair_blocks(src_ref, dst_ref):
    nblk, c, _ = src_ref.shape
    dst_ref[...] = jnp.zeros(dst_ref.shape, dst_ref.dtype)
    for p in range(nblk // 2):
        dst_ref[p, 0:c, 0:c] = src_ref[2 * p].astype(BF16)
        dst_ref[p, c:2 * c, c:2 * c] = src_ref[2 * p + 1].astype(BF16)


def _rg_gates_block(u, wa, wi, ba, bi, lam):
    ub = u.astype(BF16)
    r = jax.nn.sigmoid(_dot(ub, wa) + ba)
    i_g = jax.nn.sigmoid(_dot(ub, wi) + bi)
    log_a = -RG_C * r * jax.nn.softplus(-lam)
    a = jnp.exp(log_a)
    b = jnp.sqrt(-jnp.tanh(log_a) * (a * a + 1.0)) * (i_g * u)
    return a, b


def _rg_gates(u, wa_ref, wi_ref, ba, bi, lam):
    parts = []
    for p in range(wa_ref.shape[0]):
        cols = slice(p * MXU_DIM, (p + 1) * MXU_DIM)
        parts.append(_rg_gates_block(u[:, cols], wa_ref[p], wi_ref[p], ba[:, cols], bi[:, cols], lam[:, cols]))
    return (jnp.concatenate([a for a, _ in parts], axis=-1), jnp.concatenate([b for _, b in parts], axis=-1))


def _rglru_front(xn_ref, w_rows_ref, lb_ref, tb_ref, tail_ref, conv_ref):
    nsq = tb_ref.shape[0]
    d = xn_ref.shape[1]
    tl = xn_ref.shape[0] // nsq
    seg = tl // SUBLANES
    nlb = d // LANES
    pre = (CONV_W - 1) * SUBLANES

    pitch = seg + SUBLANES
    x_rg = _dot_nt(xn_ref[...], w_rows_ref[0:d, :])
    for cb in range(nlb):
        for s in range(nsq * SUBLANES):
            lb_ref[cb, s * pitch:s * pitch + seg, :] = x_rg[s * seg:(s + 1) * seg, cb * LANES:(cb + 1) * LANES]
    first = lax.broadcasted_iota(jnp.int32, (SUBLANES, d), 0) == 0
    for q in range(nsq):
        for j in range(seg):
            for cb in range(nlb):
                tb_ref[q, pre + j * SUBLANES:pre + (j + 1) * SUBLANES, cb * LANES:(cb + 1) * LANES] = (
                    lb_ref[cb, pl.ds(q * SUBLANES * pitch + j, SUBLANES, stride=pitch), :])

        for i in range(CONV_W - 1):
            j = seg - (CONV_W - 1) + i
            slab = tb_ref[q, pre + j * SUBLANES:pre + (j + 1) * SUBLANES, :]
            tb_ref[q, i * SUBLANES:(i + 1) * SUBLANES, :] = jnp.where(first, tail_ref[q, i:i + 1, :],
                                                                     pltpu.roll(slab, 1, axis=0))
            last = slab[SUBLANES - 1:SUBLANES, :]
            tail_ref[q, i:i + 1, :] = last
            conv_ref[q, i:i + 1, :] = last


def _rglru_block(p, q, cw_ref, cb_ref, wa_ref, wi_ref, ba_ref, bi_ref, lam_ref,
                 lb_ref, tb_ref, a_ref, b_ref, h_ref, ha_ref):
    tl = a_ref.shape[1]
    seg = tl // SUBLANES
    cols = slice(p * MXU_DIM, (p + 1) * MXU_DIM)
    lane_blocks = range(p * MXU_DIM // LANES, (p + 1) * MXU_DIM // LANES)
    row0 = q * tl

    piece = tl // RG_PIECES
    for r0 in range(0, tl, piece):
        u = cb_ref[:, cols]
        for j in range(CONV_W):
            u = u + tb_ref[q, j * SUBLANES + r0:j * SUBLANES + r0 + piece, cols] * cw_ref[j:j + 1, cols]
        a, b = _rg_gates_block(u, wa_ref[p], wi_ref[p], ba_ref[:, cols], bi_ref[:, cols], lam_ref[:, cols])
        a_ref[q, r0:r0 + piece, cols] = a
        b_ref[q, r0:r0 + piece, cols] = b
        yield

    acc_a = a_ref[q, 0:SUBLANES, cols]
    acc_b = b_ref[q, 0:SUBLANES, cols]
    for j in range(1, seg):
        rows = slice(j * SUBLANES, (j + 1) * SUBLANES)
        a_j = a_ref[q, rows, cols]
        acc_b = a_j * acc_b + b_ref[q, rows, cols]
        acc_a = a_j * acc_a
        a_ref[q, rows, cols] = acc_a
        b_ref[q, rows, cols] = acc_b

    h = h_ref[q, :, cols]
    h_in = []
    for s in range(SUBLANES):
        h_in.append(h)
        h = acc_a[s:s + 1, :] * h + acc_b[s:s + 1, :]
    h_ref[q, :, cols] = h
    h_in = jnp.concatenate(h_in, axis=0)
    yield

    for j in range(seg):
        rows = slice(j * SUBLANES, (j + 1) * SUBLANES)
        h_j = a_ref[q, rows, cols] * h_in + b_ref[q, rows, cols]
        for i, cb in enumerate(lane_blocks):
            lb_ref[cb, row0 + j * SUBLANES:row0 + (j + 1) * SUBLANES, :] = h_j[:, i * LANES:(i + 1) * LANES]
    for s in range(SUBLANES):
        for cb in lane_blocks:
            ha_ref[row0 + s * seg:row0 + (s + 1) * seg, cb * LANES:(cb + 1) * LANES] = (
                lb_ref[cb, pl.ds(row0 + s, seg, stride=SUBLANES), :].astype(BF16))
    yield


def _mixer_kernel(x_ref, sh_ref, sc_ref, ga_ref, g_ref, wint_hbm, bif_ref, bift_ref,
                  wbra_hbm, wbrb_hbm, wout_hbm, cw_ref, cb_ref, wa_in, wi_in, ba_ref, bi_ref, lam_ref,
                  o_ref, c_ref, n_ref, m_ref, conv_ref, hl_ref,
                  hm_ref, wm_ref, wif_ref, wbra_ref, wbrb_ref, wout_ref, stage, sem,
                  lb_ref, tb_ref, a_ref, b_ref, tail_ref, h_ref, ha_ref, mg_ref, xn_ref, hb_ref, sa_ref,
                  wa_ref, wi_ref):
    nsq, tl, d = x_ref.shape
    dk = d // M_HEADS
    cs = min(ML_CHUNK, tl)
    scale = dk ** -0.5
    gate_row0 = 5 * d

    @pl.when((pl.program_id(0) == 0) & (pl.program_id(1) == 0))
    def _():
        _stage_jobs([(wint_hbm, 0, 5 * d, wm_ref, 0),
                     (wint_hbm, gate_row0, LANES, wif_ref, 0),
                     (wint_hbm, gate_row0 + 2 * M_HEADS, 2 * d, wm_ref, 5 * d),
                     (wbra_hbm, 0, d, wbra_ref, 0), (wbrb_hbm, 0, d, wbrb_ref, 0), (wout_hbm, 0, d, wout_ref, 0)],
                    stage, sem)
        _pair_blocks(wa_in, wa_ref)
        _pair_blocks(wi_in, wi_ref)

    @pl.when(pl.program_id(1) == 0)
    def _():
        c_ref[...] = jnp.zeros_like(c_ref)
        n_ref[...] = jnp.zeros_like(n_ref)
        m_ref[...] = jnp.zeros_like(m_ref)
        tail_ref[...] = jnp.zeros_like(tail_ref)
        h_ref[...] = jnp.zeros_like(h_ref)

    seq_rows = [pl.ds(pl.program_id(0) * nsq + q, 1) for q in range(nsq)]
    for q in range(nsq):
        xn_ref[q * tl:(q + 1) * tl, :] = _rms_mod(x_ref[q], g_ref[...], sh_ref[seq_rows[q], :],
                                                  sc_ref[seq_rows[q], :]).astype(BF16)

    _rglru_front(xn_ref, wm_ref, lb_ref, tb_ref, tail_ref, conv_ref)
    rglru_block = functools.partial(
        _rglru_block, cw_ref=cw_ref, cb_ref=cb_ref, wa_ref=wa_ref, wi_ref=wi_ref, ba_ref=ba_ref, bi_ref=bi_ref,
        lam_ref=lam_ref, lb_ref=lb_ref, tb_ref=tb_ref, a_ref=a_ref, b_ref=b_ref, h_ref=h_ref, ha_ref=ha_ref)
    n_rg = wa_ref.shape[0]
    wq0 = d

    rg_pieces = itertools.chain.from_iterable(rglru_block(p, sq) for p in range(n_rg) for sq in range(nsq))

    def rg_advance(n=1):
        for _ in range(n):
            next(rg_pieces, None)

    pre_c = _dot_nt(xn_ref[...], wif_ref[...]) + bif_ref[...]
    pre_r = _dot_nt(wif_ref[0:GATE_ROWS, :], xn_ref[...]) + bift_ref[...]
    col_is_f = lax.broadcasted_iota(jnp.int32, pre_c.shape, 1) >= M_HEADS
    row_is_f = lax.broadcasted_iota(jnp.int32, pre_r.shape, 0) >= M_HEADS
    gate_c = jnp.where(col_is_f, jax.nn.log_sigmoid(pre_c), pre_c)
    gate_r = jnp.where(row_is_f, jax.nn.log_sigmoid(pre_r), pre_r)

    ti = lax.broadcasted_iota(jnp.int32, (cs, cs), 0)
    si = lax.broadcasted_iota(jnp.int32, (cs, cs), 1)
    causal = si <= ti
    lower = causal.astype(BF16)
    upper = (ti <= si).astype(BF16)

    chunks = []
    for sq in range(nsq):
        for r0 in range(sq * tl, (sq + 1) * tl, cs):
            gc = gate_c[r0:r0 + cs, :]
            gr = gate_r[:, r0:r0 + cs]
            cum_c = sum(_dot(lower, part) for part in _split3(gc))
            cum_r = sum(_dot(part, upper) for part in _split3(gr))
            chunks.append((sq, r0, gc, gr, cum_c, cum_r))

    for h in range(M_HEADS):
        c0 = h * dk
        q_all = _dot_nt(xn_ref[...], wm_ref[wq0 + c0:wq0 + c0 + dk, :])
        k_all = _dot_nt(xn_ref[...], wm_ref[wq0 + d + c0:wq0 + d + c0 + dk, :])
        v_all = _dot_nt(xn_ref[...], wm_ref[wq0 + 2 * d + c0:wq0 + 2 * d + c0 + dk, :])
        rg_advance()
        for sq, r0, gc, gr, cum_c, cum_r in chunks:
            q = q_all[r0:r0 + cs, :]
            k = k_all[r0:r0 + cs, :]
            v = v_all[r0:r0 + cs, :]
            bc = cum_c[:, M_HEADS + h:M_HEADS + h + 1]
            br = cum_r[M_HEADS + h:M_HEADS + h + 1, :]
            ig_c = gc[:, h:h + 1]
            ig_r = gr[h:h + 1, :]
            m_prev = m_ref[sq, h:h + 1, :]
            c_prev = c_ref[sq, h]
            n_prev = n_ref[sq, h:h + 1, :]

            log_d = jnp.where(causal, bc - br + ig_r, -jnp.inf)
            log_past = bc + m_prev
            m_t = jnp.maximum(log_past, jnp.max(log_d, axis=-1, keepdims=True))
            d_mat = jnp.exp(log_d - m_t)
            past_w = jnp.exp(log_past - m_t)
            qs = q * scale
            qsb = qs.astype(BF16)
            kb = k.astype(BF16)
            vb = v.astype(BF16)
            s = _dot_nt(qsb, kb) * d_mat
            num = past_w * _dot(qsb, c_prev.astype(BF16)) + _dot(s.astype(BF16), vb)
            den = past_w * jnp.sum(qs * n_prev, axis=-1, keepdims=True) + jnp.sum(s, axis=-1, keepdims=True)
            hm_ref[r0:r0 + cs, c0:c0 + dk] = num / jnp.maximum(jnp.abs(den), jnp.exp(-m_t))

            m_new = m_t[cs - 1:cs, :]
            b_last = bc[cs - 1:cs, :]
            w_s = jnp.exp(b_last - bc + ig_c - m_new)
            decay = jnp.exp(b_last + m_prev - m_new)
            kw = w_s * k
            c_ref[sq, h] = decay * c_prev + _dot_tn(kw.astype(BF16), vb)
            n_ref[sq, h:h + 1, :] = decay * n_prev + jnp.sum(kw, axis=0, keepdims=True)
            m_ref[sq, h:h + 1, :] = m_new
            rg_advance()

    col_blocks = [slice(n0, n0 + MXU_DIM) for n0 in range(0, d, MXU_DIM)]
    for cols in col_blocks:
        o_pre = _dot_nt(xn_ref[...], wm_ref[4 * d + cols.start:4 * d + cols.stop, :])
        hb_ref[:, cols] = (jax.nn.sigmoid(o_pre) * hm_ref[:, cols]).astype(BF16)
        rg_advance()
    for cols in col_blocks:
        g_b = _dot_nt(xn_ref[...], wm_ref[6 * d + cols.start:6 * d + cols.stop, :])
        rg_advance()
        hm_ref[:, cols] = jax.nn.sigmoid(g_b) * _dot(hb_ref[...], wbrb_ref[:, cols])
        g_a = _dot_nt(xn_ref[...], wm_ref[5 * d + cols.start:5 * d + cols.stop, :])
        sa_ref[:, cols] = jax.nn.sigmoid(g_a)
        rg_advance()
    for _ in rg_pieces:
        pass
    hl_ref[...] = h_ref[...]
    for cols in col_blocks:
        merged = sa_ref[:, cols] * _dot(ha_ref[...], wbra_ref[:, cols]) + hm_ref[:, cols]
        mg_ref[:, cols] = merged.astype(BF16)
    y = _dot(mg_ref[...], wout_ref[...])
    for q in range(nsq):
        o_ref[q] = x_ref[q] + ga_ref[seq_rows[q], :] * y[q * tl:(q + 1) * tl, :]


def _mixer_prompt(x3, mod, gain, w_int, bif, bift, wbra, wbrb, wout, conv_w, conv_b, wa, wi, ba, bi, lam):
    nb, seq, _ = x3.shape
    d = x3.shape[2]
    dk = d // M_HEADS
    nsq = ML_SEQS
    tl = min(ML_ROWS, seq)
    rows = nsq * tl
    row = lambda v: v.reshape(1, d)
    mspecs, _ = _mod_specs(mod.shape[0] - nb, nb, d, 3, 3)
    rg_pairs = (wa.shape[0] // 2, 2 * wa.shape[1], 2 * wa.shape[2])
    tile = lambda: pl.BlockSpec((nsq, tl, d), lambda b, t: (b, t, 0))
    per_seq = lambda *shape: pl.BlockSpec((nsq,) + shape, lambda b, t: (b,) + (0,) * len(shape))
    return pl.pallas_call(
        _mixer_kernel,
        grid=(nb // nsq, seq // tl),
        in_specs=[tile()] + mspecs + [_resident((1, d)),
                  _HBM, _resident(bif.shape), _resident(bift.shape), _HBM, _HBM, _HBM,
                  _resident(conv_w.shape), _resident((1, d)), _resident(wa.shape), _resident(wi.shape),
                  _resident((1, d)), _resident((1, d)), _resident((1, d))],
        out_specs=[tile(), per_seq(M_HEADS, dk, dk), per_seq(M_HEADS, dk), per_seq(M_HEADS, 1),
                   per_seq(CONV_W - 1, d), per_seq(1, d)],
        out_shape=[jax.ShapeDtypeStruct((nb, seq, d), F32),
                   jax.ShapeDtypeStruct((nb, M_HEADS, dk, dk), F32),
                   jax.ShapeDtypeStruct((nb, M_HEADS, dk), F32),
                   jax.ShapeDtypeStruct((nb, M_HEADS, 1), F32),
                   jax.ShapeDtypeStruct((nb, CONV_W - 1, d), F32),
                   jax.ShapeDtypeStruct((nb, 1, d), F32)],
        scratch_shapes=[pltpu.VMEM((rows, d), F32),
                        pltpu.VMEM((7 * d, d), BF16),
                        pltpu.VMEM((LANES, d), BF16),
                        pltpu.VMEM((d, d), BF16), pltpu.VMEM((d, d), BF16), pltpu.VMEM((d, d), BF16)]
        + _stage_scratch(d)
        + [pltpu.VMEM((d // LANES, rows + nsq * SUBLANES * SUBLANES, LANES), F32),
           pltpu.VMEM((nsq, (CONV_W - 1) * SUBLANES + tl, d), F32),
           pltpu.VMEM((nsq, tl, d), F32),
           pltpu.VMEM((nsq, tl, d), F32),
           pltpu.VMEM((nsq, CONV_W - 1, d), F32),
           pltpu.VMEM((nsq, 1, d), F32),
           pltpu.VMEM((rows, d), BF16),
           pltpu.VMEM((rows, d), BF16),
           pltpu.VMEM((rows, d), BF16),
           pltpu.VMEM((rows, d), BF16),
           pltpu.VMEM((rows, d), F32),
           pltpu.VMEM(rg_pairs, BF16), pltpu.VMEM(rg_pairs, BF16)],
        compiler_params=_cparams(("arbitrary", "arbitrary")),
        name="mixer_prompt",
    )(x3, mod, mod, mod, row(gain), w_int, bif, bift, wbra, wbrb, wout,
      conv_w, row(conv_b), wa, wi, row(ba), row(bi), row(lam))


def _head_sum(x, dk):
    parts = []
    for h in range(x.shape[1] // dk):
        sl = x[:, h * dk:(h + 1) * dk]
        parts.append(jnp.broadcast_to(jnp.sum(sl, axis=-1, keepdims=True), sl.shape))
    return jnp.concatenate(parts, axis=-1)


def _head_spread(cols, lane0, dk, rows):
    return jnp.concatenate(
        [jnp.broadcast_to(cols[:, lane0 + h:lane0 + h + 1], (rows, dk)) for h in range(M_HEADS)], axis=-1)


def _dec_pre_kernel(x_ref, sh_ref, sc_ref, g_ref, wint_hbm, bif_ref,
                    cw_ref, cb_ref, wa_in, wi_in, ba_ref, bi_ref, lam_ref,
                    conv0_ref, h0_ref, n0_ref, m0_ref,
                    ha_ref, conv_ref, hs_ref, n_ref, m_ref, qt_ref, kt_ref, dec_ref, wv_ref, pw_ref, sv_ref,
                    den_ref, em_ref,
                    wm_ref, wif_ref, stage, sem, wa_ref, wi_ref):
    rows, d = x_ref.shape
    dk = d // M_HEADS
    scale = dk ** -0.5
    _pair_blocks(wa_in, wa_ref)
    _pair_blocks(wi_in, wi_ref)
    _stage_jobs([(wint_hbm, 0, 4 * d, wm_ref, 0),
                 (wint_hbm, 5 * d, LANES, wif_ref, 0)], stage, sem)
    xn = _rms_mod(x_ref[...], g_ref[...], sh_ref[...], sc_ref[...]).astype(BF16)

    x_rg = _dot_nt(xn, wm_ref[0:d, :])
    u = cb_ref[...]
    for j in range(CONV_W - 1):
        u = u + conv0_ref[j] * cw_ref[j:j + 1, :]
        if j > 0:
            conv_ref[j - 1] = conv0_ref[j]
    u = u + x_rg * cw_ref[CONV_W - 1:CONV_W, :]
    conv_ref[CONV_W - 2] = x_rg
    a, b = _rg_gates(u, wa_ref, wi_ref, ba_ref[...], bi_ref[...], lam_ref[...])
    h = a * h0_ref[...] + b
    hs_ref[...] = h
    ha_ref[...] = h

    q = _dot_nt(xn, wm_ref[d:2 * d, :])
    k = _dot_nt(xn, wm_ref[2 * d:3 * d, :])
    v = _dot_nt(xn, wm_ref[3 * d:4 * d, :])
    qt_ref[...] = _dot_nt(wm_ref[d:2 * d, :], xn).astype(BF16)
    kt_ref[...] = _dot_nt(wm_ref[2 * d:3 * d, :], xn).astype(BF16)
    pre = _dot_nt(xn, wif_ref[...]) + bif_ref[...]
    ig = _head_spread(pre, 0, dk, rows)
    lf = jax.nn.log_sigmoid(_head_spread(pre, M_HEADS, dk, rows))
    m0 = _head_spread(m0_ref[...], 0, dk, rows)
    n0 = n0_ref[...]
    log_past = lf + m0
    m_t = jnp.maximum(log_past, ig)
    d_w = jnp.exp(ig - m_t)
    past_w = jnp.exp(log_past - m_t)
    qs = q * scale
    s = _head_sum(qs * k, dk) * d_w
    den_ref[...] = past_w * _head_sum(qs * n0, dk) + s
    em_ref[...] = jnp.exp(-m_t)
    pw_ref[...] = past_w
    sv_ref[...] = s * v
    w_s = jnp.exp(ig - m_t)
    decay = jnp.exp(lf + m0 - m_t)
    dec_ref[...] = decay
    wv_ref[...] = w_s * v
    n_ref[...] = decay * n0 + w_s * k
    m_ref[...] = m_t


def _dec_pre(x2, mod, gain, w_int, bif, conv_w, conv_b, wa, wi, ba, bi, lam, conv0, h0, n0, m0):
    rows, d = x2.shape
    row = lambda v: v.reshape(1, d)
    m0p = jnp.pad(m0, ((0, 0), (0, LANES - M_HEADS)))
    full = lambda shape: pl.BlockSpec(shape, lambda i, _n=len(shape): (0,) * _n)
    mspec = lambda j: pl.BlockSpec((rows, d), lambda i: (0, j))
    vec = jax.ShapeDtypeStruct((rows, d), F32)
    outs = [vec,
            jax.ShapeDtypeStruct((CONV_W - 1, rows, d), F32),
            vec,
            vec,
            vec,
            jax.ShapeDtypeStruct((d, rows), BF16),
            jax.ShapeDtypeStruct((d, rows), BF16),
            vec, vec, vec, vec, vec, vec]
    args = [x2, mod, mod, row(gain), w_int, bif, conv_w, row(conv_b), wa, wi,
            row(ba), row(bi), row(lam), conv0, h0, n0, m0p]
    in_specs = ([full(x2.shape), mspec(3), mspec(4), full((1, d)), _HBM]
                + [full(a.shape) for a in args[5:]])
    return pl.pallas_call(
        _dec_pre_kernel,
        grid=(1,),
        in_specs=in_specs,
        out_specs=[full(o.shape) for o in outs],
        out_shape=outs,
        scratch_shapes=[pltpu.VMEM((4 * d, d), BF16), pltpu.VMEM((LANES, d), BF16)]
        + _stage_scratch(d, STAGE_SLOTS_DEC)
        + [pltpu.VMEM((wa.shape[0] // 2, 2 * wa.shape[1], 2 * wa.shape[2]), BF16)] * 2,
        compiler_params=_cparams(("arbitrary",)),
        name="decode_pre",
    )(*args)


def _dec_mem_kernel(qt_ref, kt_ref, dec_ref, wv_ref, c0_ref, c_ref, qc_ref):
    bb = c0_ref.shape[0]
    dk = c0_ref.shape[2]
    nseq = qt_ref.shape[1]
    scale = dk ** -0.5
    base = pl.program_id(0) * bb
    seq_id = lax.broadcasted_iota(jnp.int32, (nseq, dk), 0)
    for j in range(bb):
        pick = (seq_id == base + j).astype(BF16)
        qcol = _dot(qt_ref[...], pick) * scale
        kcol = _dot(kt_ref[...], pick)
        seq = pl.ds(base + j, 1)
        for h in range(M_HEADS):
            cols = slice(h * dk, (h + 1) * dk)
            c0 = c0_ref[j, h]
            dec = dec_ref[seq, cols]
            wv = wv_ref[seq, cols]
            c_ref[j, h] = dec * c0 + kcol[cols, :] * wv
            qc_ref[seq, cols] = jnp.sum(qcol[cols, :] * c0, axis=0, keepdims=True)


def _dec_mem(qt, kt, decay, wv, c0):
    nseq, heads, dk, _ = c0.shape
    d = heads * dk
    bb = DEC_BLOCK if nseq % DEC_BLOCK == 0 else 1
    whole = lambda shape: pl.BlockSpec(shape, lambda i: (0, 0))
    return pl.pallas_call(
        _dec_mem_kernel,
        grid=(nseq // bb,),
        in_specs=[whole((d, nseq)), whole((d, nseq)), whole((nseq, d)), whole((nseq, d)),
                  pl.BlockSpec((bb, heads, dk, dk), lambda i: (i, 0, 0, 0))],
        out_specs=[pl.BlockSpec((bb, heads, dk, dk), lambda i: (i, 0, 0, 0)), whole((nseq, d))],
        out_shape=[jax.ShapeDtypeStruct(c0.shape, F32), jax.ShapeDtypeStruct((nseq, d), F32)],
        compiler_params=_cparams(("arbitrary",)),
        name="decode_mem",
    )(qt, kt, decay, wv, c0)


def _dec_post_kernel(x_ref, sh_ref, sc_ref, ga_ref, g_ref, wint_hbm, wbra_hbm, wbrb_hbm, wout_hbm,
                     ha_ref, qc_ref, pw_ref, sv_ref, den_ref, em_ref, o_ref,
                     wm_ref, wbra_ref, wbrb_ref, wout_ref, stage, sem):
    d = x_ref.shape[1]
    _stage_jobs([(wint_hbm, 4 * d, d, wm_ref, 0),
                 (wint_hbm, 5 * d + 2 * M_HEADS, 2 * d, wm_ref, d),
                 (wbra_hbm, 0, d, wbra_ref, 0), (wbrb_hbm, 0, d, wbrb_ref, 0), (wout_hbm, 0, d, wout_ref, 0)],
                stage, sem)
    x = x_ref[...]
    xn = _rms_mod(x, g_ref[...], sh_ref[...], sc_ref[...]).astype(BF16)
    num = pw_ref[...] * qc_ref[...] + sv_ref[...]
    hm = num / jnp.maximum(jnp.abs(den_ref[...]), em_ref[...])
    hb = (jax.nn.sigmoid(_dot_nt(xn, wm_ref[0:d, :])) * hm).astype(BF16)
    g_a = _dot_nt(xn, wm_ref[d:2 * d, :])
    g_b = _dot_nt(xn, wm_ref[2 * d:3 * d, :])
    merged = (jax.nn.sigmoid(g_a) * _dot(ha_ref[...].astype(BF16), wbra_ref[...])
              + jax.nn.sigmoid(g_b) * _dot(hb, wbrb_ref[...]))
    o_ref[...] = x + ga_ref[...] * _dot(merged.astype(BF16), wout_ref[...])


def _dec_post(x2, mod, gain, w_int, wbra, wbrb, wout, ha, qc, pw, sv, den, em):
    rows, d = x2.shape
    full = lambda shape: pl.BlockSpec(shape, lambda i, _n=len(shape): (0,) * _n)
    mspec = lambda j: pl.BlockSpec((rows, d), lambda i: (0, j))
    args = [x2, mod, mod, mod, gain.reshape(1, d), w_int, wbra, wbrb, wout, ha, qc, pw, sv, den, em]
    in_specs = ([full(x2.shape), mspec(3), mspec(4), mspec(5), full((1, d))] + [_HBM] * 4
                + [full(a.shape) for a in args[9:]])
    return pl.pallas_call(
        _dec_post_kernel,
        grid=(1,),
        in_specs=in_specs,
        out_specs=full((rows, d)),
        out_shape=jax.ShapeDtypeStruct((rows, d), F32),
        scratch_shapes=[pltpu.VMEM((3 * d, d), BF16),
                        pltpu.VMEM((d, d), BF16), pltpu.VMEM((d, d), BF16), pltpu.VMEM((d, d), BF16)]
        + _stage_scratch(d, STAGE_SLOTS_DEC),
        compiler_params=_cparams(("arbitrary",)),
        name="decode_post",
    )(*args)


def kernel(x_prompt, x_sample, state_conv, state_rg_h, state_C, state_n, state_m, c_prompt, c_sample, w_ada, b_ada, g_norm1, w_ff1_in, w_ff1_out, g_norm2, w_in, conv_w, conv_b, w_rg_a, b_rg_a, w_rg_i, b_rg_i, rg_lambda, b_ig, b_fg, w_br_a, w_br_b, w_out, g_norm3, w_ff2_in, w_ff2_out, g_final):
    nb, seq, d = x_prompt.shape
    ns = x_sample.shape[0]
    depth = w_ada.shape[0]
    assert depth == 1 and x_sample.shape[1] == 1 and nb == SUBLANES
    assert seq % ML_ROWS == 0 and seq % FFN_ROWS == 0 and ML_ROWS % (SUBLANES * SUBLANES) == 0
    assert nb % ML_SEQS == 0
    heads, dk = M_HEADS, d // M_HEADS
    assert w_in.shape[2] == 7 * d + 2 * heads and GATE_ROWS >= 2 * heads

    w_int = w_in[0].T
    b_gates = jnp.concatenate([b_ig[0], b_fg[0]])
    bif = jnp.pad(b_gates, (0, LANES - 2 * heads)).reshape(1, LANES)
    bift = jnp.pad(b_gates, (0, GATE_ROWS - 2 * heads)).reshape(GATE_ROWS, 1)
    wa, wi = w_rg_a[0], w_rg_i[0]
    wf1i, wf1o, wf2i, wf2o = w_ff1_in[0], w_ff1_out[0], w_ff2_in[0], w_ff2_out[0]
    wbra, wbrb, wout = w_br_a[0], w_br_b[0], w_out[0]

    mod = _ada(c_sample, c_prompt, w_ada[0], b_ada[0])

    xp, xs = _ffn(x_prompt.reshape(nb * seq, d), x_sample.reshape(ns, d), mod, 0, g_norm1[0],
                  wf1i, wf1o, None, rows_per_seq=seq)

    xp, c_p, n_p, m_p, conv_p, h_p = _mixer_prompt(
        xp.reshape(nb, seq, d), mod, g_norm2[0], w_int, bif, bift, wbra, wbrb, wout, conv_w[0], conv_b[0],
        wa, wi, b_rg_a[0], b_rg_i[0], rg_lambda[0])
    xp = xp.reshape(nb * seq, d)

    conv0 = jnp.swapaxes(state_conv[0], 0, 1)
    (ha_s, conv_s, h_s, n_s, m_s, qt, kt, decay, wv, pw, sv, den, em) = _dec_pre(
        xs, mod, g_norm2[0], w_int, bif, conv_w[0], conv_b[0], wa, wi,
        b_rg_a[0], b_rg_i[0], rg_lambda[0], conv0, state_rg_h[0], state_n[0].reshape(ns, d), state_m[0])
    c_s, qc = _dec_mem(qt, kt, decay, wv, state_C[0])
    xs = _dec_post(xs, mod, g_norm2[0], w_int, wbra, wbrb, wout, ha_s, qc, pw, sv, den, em)

    yp, ys = _ffn(xp, xs, mod, 6, g_norm3[0], wf2i, wf2o, g_final, rows_per_seq=seq)

    return (yp.reshape(nb, seq, d), ys.reshape(ns, 1, d),
            conv_p[None], h_p.reshape(1, nb, d), c_p[None], n_p[None], m_p.reshape(1, nb, heads),
            jnp.swapaxes(conv_s, 0, 1)[None], h_s[None], c_s[None], n_s.reshape(1, ns, heads, dk), m_s[:, ::dk][None])
```

```python
import functools
import itertools

import jax
import jax.numpy as jnp
from jax import lax
from jax.experimental import pallas as pl
from jax.experimental.pallas import tpu as pltpu

F32 = jnp.float32
BF16 = jnp.bfloat16

EPS = 1e-6
RG_C = 8.0
CONV_W = 4
M_HEADS = 4

SUBLANES = 8
LANES = 128
MXU_DIM = 256
VMEM_LIMIT_BYTES = 60 * 1024 * 1024

ADA_STEPS = 4
FFN_ROWS = 1024
FFN_COLS = 256
ML_ROWS = 512
ML_SEQS = 1
ML_CHUNK = 256
RG_PIECES = 4
DEC_BLOCK = 8
STAGE_ROWS = 256
STAGE_SLOTS = 4
STAGE_SLOTS_MIX = 6
STAGE_SLOTS_DEC = 12
STAGE_ROWS_WIDE = 64
STAGE_SLOTS_WIDE = 6
GATE_ROWS = 16


def _cparams(sem):
    return pltpu.CompilerParams(dimension_semantics=sem, vmem_limit_bytes=VMEM_LIMIT_BYTES)


def _resident(shape):
    nd = len(shape)
    return pl.BlockSpec(shape, lambda *_: (0,) * nd, pipeline_mode=pl.Buffered(1))


_HBM = pl.BlockSpec(memory_space=pl.ANY)


def _rms_mod(x, gain, shift, scale):
    ms = jnp.mean(x * x, axis=-1, keepdims=True)
    return x * lax.rsqrt(ms + EPS) * gain * (1.0 + scale) + shift


def _dot(a, b):
    return jnp.dot(a, b, preferred_element_type=F32)


def _dot_nt(a, b):
    return lax.dot_general(a, b, (((1,), (1,)), ((), ())), preferred_element_type=F32)


def _dot_tn(a, b):
    return lax.dot_general(a, b, (((0,), (0,)), ((), ())), preferred_element_type=F32)


def _split3(x):
    hi = x.astype(BF16)
    r1 = x - hi.astype(F32)
    mid = r1.astype(BF16)
    lo = (r1 - mid.astype(F32)).astype(BF16)
    return hi, mid, lo


def _stage_groups(groups):
    plans = []
    for jobs, stage, sem in groups:
        slots, rows_per_copy, _ = stage.shape
        chunks = []
        for src, src_row0, nrows, dst, dst_row0 in jobs:
            for r in range(0, nrows, rows_per_copy):
                chunks.append((src, src_row0 + r, dst, dst_row0 + r, min(rows_per_copy, nrows - r)))

        def copy(c, chunks=chunks, stage=stage, sem=sem, slots=slots):
            src, src_row, _, _, rows = chunks[c]
            return pltpu.make_async_copy(src.at[pl.ds(src_row, rows)],
                                         stage.at[c % slots, pl.ds(0, rows)], sem.at[c % slots])
        plans.append((copy, chunks, stage, slots))

    for copy, chunks, _, slots in plans:
        for c in range(min(slots, len(chunks))):
            copy(c).start(priority=c % 2)
    for copy, chunks, stage, slots in plans:
        for c, (_, _, dst, dst_row, rows) in enumerate(chunks):
            copy(c).wait()
            dst[dst_row:dst_row + rows, :] = stage[c % slots, 0:rows, :].astype(BF16)
            if c + slots < len(chunks):
                copy(c + slots).start(priority=(c + slots) % 2)


def _stage_jobs(jobs, stage, sem):
    _stage_groups([(jobs, stage, sem)])


def _stage_scratch(cols, slots=STAGE_SLOTS, rows=STAGE_ROWS):
    return [pltpu.VMEM((slots, rows, cols), F32), pltpu.SemaphoreType.DMA((slots,))]


def _ada_kernel(cs_ref, cp_ref, w_ref, b_ref, o_ref):
    ns = cs_ref.shape[0]
    w = w_ref[...].astype(BF16)
    for c_ref, rows in ((cs_ref, slice(0, ns)), (cp_ref, slice(ns, o_ref.shape[0]))):
        c = c_ref[...]
        o_ref[rows, :] = _dot((c * jax.nn.sigmoid(c)).astype(BF16), w) + b_ref[...]


def _ada(c_sample, c_prompt, w_ada, b_ada):
    ns, d = c_sample.shape
    nb = c_prompt.shape[0]
    n = w_ada.shape[1]
    tn = n // ADA_STEPS
    assert tn * ADA_STEPS == n and tn % LANES == 0 and ns % SUBLANES == 0
    return pl.pallas_call(
        _ada_kernel,
        grid=(n // tn,),
        in_specs=[pl.BlockSpec((ns, d), lambda j: (0, 0)),
                  pl.BlockSpec((nb, d), lambda j: (0, 0)),
                  pl.BlockSpec((d, tn), lambda j: (0, j)),
                  pl.BlockSpec((1, tn), lambda j: (0, j))],
        out_specs=pl.BlockSpec((ns + nb, tn), lambda j: (0, j)),
        out_shape=jax.ShapeDtypeStruct((ns + nb, n), F32),
        compiler_params=_cparams(("arbitrary",)),
        name="adaln_mod",
    )(c_sample, c_prompt, w_ada, b_ada.reshape(1, n))


def _ffn_rows(x_ref, shift, scale, gate, g_ref, wi_ref, wo_ref, gf_ref, o_ref, act_ref, d_ff):
    rows = x_ref.shape[0]
    x = x_ref[...]
    xn = _rms_mod(x, g_ref[...], shift, scale).astype(BF16)
    for j in range(d_ff // FFN_COLS):
        lo = j * FFN_COLS
        hg = _dot(xn, wi_ref[:, lo:lo + FFN_COLS])
        hu = _dot(xn, wi_ref[:, d_ff + lo:d_ff + lo + FFN_COLS])
        act_ref[0:rows, lo:lo + FFN_COLS] = (hg * jax.nn.sigmoid(hg) * hu).astype(BF16)
    y = _dot(act_ref[0:rows, :], wo_ref[...])
    out = x + (0.5 * gate) * y
    if gf_ref is not None:
        ms = jnp.mean(out * out, axis=-1, keepdims=True)
        out = out * lax.rsqrt(ms + EPS) * gf_ref[...]
    o_ref[...] = out


def _ffn_kernel(*refs, d_ff, final, n_prompt_steps, tiles_per_seq):
    n_in = 13 if final else 12
    xp_ref, shp_ref, scp_ref, gap_ref, xs_ref, shs_ref, scs_ref, gas_ref, g_ref, wi_hbm, wo_hbm = refs[:11]
    gf_ref = refs[11] if final else None
    op_ref, os_ref = refs[n_in - 1:n_in + 1]
    act_ref, wi_ref, wo_ref, stage_i, sem_i, stage_o, sem_o = refs[-7:]
    step = pl.program_id(0)

    @pl.when(step == 0)
    def _():
        _stage_groups([([(wi_hbm, 0, wi_ref.shape[0], wi_ref, 0)], stage_i, sem_i),
                       ([(wo_hbm, 0, wo_ref.shape[0], wo_ref, 0)], stage_o, sem_o)])

    @pl.when(step < n_prompt_steps)
    def _():
        seq = pl.ds(step // tiles_per_seq, 1)
        _ffn_rows(xp_ref, shp_ref[seq, :], scp_ref[seq, :], gap_ref[seq, :], g_ref, wi_ref, wo_ref, gf_ref,
                  op_ref, act_ref, d_ff)

    @pl.when(step == n_prompt_steps)
    def _():
        _ffn_rows(xs_ref, shs_ref[...], scs_ref[...], gas_ref[...], g_ref, wi_ref, wo_ref, gf_ref,
                  os_ref, act_ref, d_ff)


def _mod_specs(ns, nb, d, j0, n):
    assert ns % nb == 0
    prompt = [pl.BlockSpec((nb, d), lambda *_, j=j: (ns // nb, j)) for j in range(j0, j0 + n)]
    sample = [pl.BlockSpec((ns, d), lambda *_, j=j: (0, j)) for j in range(j0, j0 + n)]
    return prompt, sample


def _ffn(xp, xs, mod, j0, gain, wi, wo, g_final, *, rows_per_seq):
    rows, d = xp.shape
    ns = xs.shape[0]
    d_ff = wo.shape[0]
    tm = min(FFN_ROWS, rows_per_seq)
    tiles_per_seq = rows_per_seq // tm
    n_steps = rows // tm
    tile_of = lambda i: jnp.minimum(i, n_steps - 1)
    mspec_p, mspec_s = _mod_specs(ns, rows // rows_per_seq, d, j0, 3)
    final = g_final is not None
    in_specs = ([pl.BlockSpec((tm, d), lambda i: (tile_of(i), 0))] + mspec_p
                + [pl.BlockSpec((ns, d), lambda i: (0, 0))] + mspec_s + [_resident((1, d)), _HBM, _HBM])
    args = [xp, mod, mod, mod, xs, mod, mod, mod, gain.reshape(1, d), wi, wo]
    if final:
        in_specs.append(_resident((1, d)))
        args.append(g_final.reshape(1, d))
    return pl.pallas_call(
        functools.partial(_ffn_kernel, d_ff=d_ff, final=final, n_prompt_steps=n_steps, tiles_per_seq=tiles_per_seq),
        grid=(n_steps + 1,),
        in_specs=in_specs,
        out_specs=[pl.BlockSpec((tm, d), lambda i: (tile_of(i), 0)), pl.BlockSpec((ns, d), lambda i: (0, 0))],
        out_shape=[jax.ShapeDtypeStruct((rows, d), F32), jax.ShapeDtypeStruct((ns, d), F32)],
        scratch_shapes=[pltpu.VMEM((tm, d_ff), BF16), pltpu.VMEM(wi.shape, BF16), pltpu.VMEM(wo.shape, BF16)]
        + _stage_scratch(wi.shape[1], STAGE_SLOTS_WIDE, STAGE_ROWS_WIDE) + _stage_scratch(wo.shape[1]),
        compiler_params=_cparams(("arbitrary",)),
        name="ffn_final" if final else "ffn",
    )(*args)


def _pair_blocks(src_ref, dst_ref):
    nblk, c, _ = src_ref.shape
    dst_ref[...] = jnp.zeros(dst_ref.shape, dst_ref.dtype)
    for p in range(nblk // 2):
        dst_ref[p, 0:c, 0:c] = src_ref[2 * p].astype(BF16)
        dst_ref[p, c:2 * c, c:2 * c] = src_ref[2 * p + 1].astype(BF16)


def _rg_gates_block(u, wa, wi, ba, bi, lam):
    ub = u.astype(BF16)
    r = jax.nn.sigmoid(_dot(ub, wa) + ba)
    i_g = jax.nn.sigmoid(_dot(ub, wi) + bi)
    log_a = -RG_C * r * jax.nn.softplus(-lam)
    a = jnp.exp(log_a)
    b = jnp.sqrt(-jnp.tanh(log_a) * (a * a + 1.0)) * (i_g * u)
    return a, b


def _rg_gates(u, wa_ref, wi_ref, ba, bi, lam):
    parts = []
    for p in range(wa_ref.shape[0]):
        cols = slice(p * MXU_DIM, (p + 1) * MXU_DIM)
        parts.append(_rg_gates_block(u[:, cols], wa_ref[p], wi_ref[p], ba[:, cols], bi[:, cols], lam[:, cols]))
    return (jnp.concatenate([a for a, _ in parts], axis=-1), jnp.concatenate([b for _, b in parts], axis=-1))


def _rglru_front(xn_ref, w_rows_ref, lb_ref, tb_ref, tail_ref, conv_ref):
    nsq = tb_ref.shape[0]
    d = xn_ref.shape[1]
    tl = xn_ref.shape[0] // nsq
    seg = tl // SUBLANES
    nlb = d // LANES
    pre = (CONV_W - 1) * SUBLANES

    pitch = seg + SUBLANES
    x_rg = _dot_nt(xn_ref[...], w_rows_ref[0:d, :])
    for cb in range(nlb):
        for s in range(nsq * SUBLANES):
            lb_ref[cb, s * pitch:s * pitch + seg, :] = x_rg[s * seg:(s + 1) * seg, cb * LANES:(cb + 1) * LANES]
    first = lax.broadcasted_iota(jnp.int32, (SUBLANES, d), 0) == 0
    for q in range(nsq):
        for j in range(seg):
            for cb in range(nlb):
                tb_ref[q, pre + j * SUBLANES:pre + (j + 1) * SUBLANES, cb * LANES:(cb + 1) * LANES] = (
                    lb_ref[cb, pl.ds(q * SUBLANES * pitch + j, SUBLANES, stride=pitch), :])

        for i in range(CONV_W - 1):
            j = seg - (CONV_W - 1) + i
            slab = tb_ref[q, pre + j * SUBLANES:pre + (j + 1) * SUBLANES, :]
            tb_ref[q, i * SUBLANES:(i + 1) * SUBLANES, :] = jnp.where(first, tail_ref[q, i:i + 1, :],
                                                                     pltpu.roll(slab, 1, axis=0))
            last = slab[SUBLANES - 1:SUBLANES, :]
            tail_ref[q, i:i + 1, :] = last
            conv_ref[q, i:i + 1, :] = last


def _rglru_block(p, q, cw_ref, cb_ref, wa_ref, wi_ref, ba_ref, bi_ref, lam_ref,
                 lb_ref, tb_ref, a_ref, b_ref, h_ref, ha_ref):
    tl = a_ref.shape[1]
    seg = tl // SUBLANES
    cols = slice(p * MXU_DIM, (p + 1) * MXU_DIM)
    lane_blocks = range(p * MXU_DIM // LANES, (p + 1) * MXU_DIM // LANES)
    row0 = q * tl

    piece = tl // RG_PIECES
    for r0 in range(0, tl, piece):
        u = cb_ref[:, cols]
        for j in range(CONV_W):
            u = u + tb_ref[q, j * SUBLANES + r0:j * SUBLANES + r0 + piece, cols] * cw_ref[j:j + 1, cols]
        a, b = _rg_gates_block(u, wa_ref[p], wi_ref[p], ba_ref[:, cols], bi_ref[:, cols], lam_ref[:, cols])
        a_ref[q, r0:r0 + piece, cols] = a
        b_ref[q, r0:r0 + piece, cols] = b
        yield

    acc_a = a_ref[q, 0:SUBLANES, cols]
    acc_b = b_ref[q, 0:SUBLANES, cols]
    for j in range(1, seg):
        rows = slice(j * SUBLANES, (j + 1) * SUBLANES)
        a_j = a_ref[q, rows, cols]
        acc_b = a_j * acc_b + b_ref[q, rows, cols]
        acc_a = a_j * acc_a
        a_ref[q, rows, cols] = acc_a
        b_ref[q, rows, cols] = acc_b

    h = h_ref[q, :, cols]
    h_in = []
    for s in range(SUBLANES):
        h_in.append(h)
        h = acc_a[s:s + 1, :] * h + acc_b[s:s + 1, :]
    h_ref[q, :, cols] = h
    h_in = jnp.concatenate(h_in, axis=0)
    yield

    for j in range(seg):
        rows = slice(j * SUBLANES, (j + 1) * SUBLANES)
        h_j = a_ref[q, rows, cols] * h_in + b_ref[q, rows, cols]
        for i, cb in enumerate(lane_blocks):
            lb_ref[cb, row0 + j * SUBLANES:row0 + (j + 1) * SUBLANES, :] = h_j[:, i * LANES:(i + 1) * LANES]
    for s in range(SUBLANES):
        for cb in lane_blocks:
            ha_ref[row0 + s * seg:row0 + (s + 1) * seg, cb * LANES:(cb + 1) * LANES] = (
                lb_ref[cb, pl.ds(row0 + s, seg, stride=SUBLANES), :].astype(BF16))
    yield


def _mixer_kernel(x_ref, sh_ref, sc_ref, ga_ref, g_ref, wint_hbm, bif_ref, bift_ref,
                  wbra_hbm, wbrb_hbm, wout_hbm, cw_ref, cb_ref, wa_in, wi_in, ba_ref, bi_ref, lam_ref,
                  o_ref, c_ref, n_ref, m_ref, conv_ref, hl_ref,
                  hm_ref, wm_ref, wif_ref, wbra_ref, wbrb_ref, wout_ref, stage, sem,
                  lb_ref, tb_ref, a_ref, b_ref, tail_ref, h_ref, ha_ref, mg_ref, xn_ref, hb_ref, sa_ref,
                  wa_ref, wi_ref):
    nsq, tl, d = x_ref.shape
    dk = d // M_HEADS
    cs = min(ML_CHUNK, tl)
    scale = dk ** -0.5
    gate_row0 = 5 * d

    @pl.when((pl.program_id(0) == 0) & (pl.program_id(1) == 0))
    def _():
        _stage_jobs([(wint_hbm, 0, 5 * d, wm_ref, 0),
                     (wint_hbm, gate_row0, LANES, wif_ref, 0),
                     (wint_hbm, gate_row0 + 2 * M_HEADS, 2 * d, wm_ref, 5 * d),
                     (wbra_hbm, 0, d, wbra_ref, 0), (wbrb_hbm, 0, d, wbrb_ref, 0), (wout_hbm, 0, d, wout_ref, 0)],
                    stage, sem)
        _pair_blocks(wa_in, wa_ref)
        _pair_blocks(wi_in, wi_ref)

    @pl.when(pl.program_id(1) == 0)
    def _():
        c_ref[...] = jnp.zeros_like(c_ref)
        n_ref[...] = jnp.zeros_like(n_ref)
        m_ref[...] = jnp.zeros_like(m_ref)
        tail_ref[...] = jnp.zeros_like(tail_ref)
        h_ref[...] = jnp.zeros_like(h_ref)

    seq_rows = [pl.ds(pl.program_id(0) * nsq + q, 1) for q in range(nsq)]
    for q in range(nsq):
        xn_ref[q * tl:(q + 1) * tl, :] = _rms_mod(x_ref[q], g_ref[...], sh_ref[seq_rows[q], :],
                                                  sc_ref[seq_rows[q], :]).astype(BF16)

    _rglru_front(xn_ref, wm_ref, lb_ref, tb_ref, tail_ref, conv_ref)
    rglru_block = functools.partial(
        _rglru_block, cw_ref=cw_ref, cb_ref=cb_ref, wa_ref=wa_ref, wi_ref=wi_ref, ba_ref=ba_ref, bi_ref=bi_ref,
        lam_ref=lam_ref, lb_ref=lb_ref, tb_ref=tb_ref, a_ref=a_ref, b_ref=b_ref, h_ref=h_ref, ha_ref=ha_ref)
    n_rg = wa_ref.shape[0]
    wq0 = d

    rg_pieces = itertools.chain.from_iterable(rglru_block(p, sq) for p in range(n_rg) for sq in range(nsq))

    def rg_advance(n=1):
        for _ in range(n):
            next(rg_pieces, None)

    pre_c = _dot_nt(xn_ref[...], wif_ref[...]) + bif_ref[...]
    pre_r = _dot_nt(wif_ref[0:GATE_ROWS, :], xn_ref[...]) + bift_ref[...]
    col_is_f = lax.broadcasted_iota(jnp.int32, pre_c.shape, 1) >= M_HEADS
    row_is_f = lax.broadcasted_iota(jnp.int32, pre_r.shape, 0) >= M_HEADS
    gate_c = jnp.where(col_is_f, jax.nn.log_sigmoid(pre_c), pre_c)
    gate_r = jnp.where(row_is_f, jax.nn.log_sigmoid(pre_r), pre_r)

    ti = lax.broadcasted_iota(jnp.int32, (cs, cs), 0)
    si = lax.broadcasted_iota(jnp.int32, (cs, cs), 1)
    causal = si <= ti
    lower = causal.astype(BF16)
    upper = (ti <= si).astype(BF16)

    chunks = []
    for sq in range(nsq):
        for r0 in range(sq * tl, (sq + 1) * tl, cs):
            gc = gate_c[r0:r0 + cs, :]
            gr = gate_r[:, r0:r0 + cs]
            cum_c = sum(_dot(lower, part) for part in _split3(gc))
            cum_r = sum(_dot(part, upper) for part in _split3(gr))
            chunks.append((sq, r0, gc, gr, cum_c, cum_r))

    for h in range(M_HEADS):
        c0 = h * dk
        q_all = _dot_nt(xn_ref[...], wm_ref[wq0 + c0:wq0 + c0 + dk, :])
        k_all = _dot_nt(xn_ref[...], wm_ref[wq0 + d + c0:wq0 + d + c0 + dk, :])
        v_all = _dot_nt(xn_ref[...], wm_ref[wq0 + 2 * d + c0:wq0 + 2 * d + c0 + dk, :])
        rg_advance()
        for sq, r0, gc, gr, cum_c, cum_r in chunks:
            q = q_all[r0:r0 + cs, :]
            k = k_all[r0:r0 + cs, :]
            v = v_all[r0:r0 + cs, :]
            bc = cum_c[:, M_HEADS + h:M_HEADS + h + 1]
            br = cum_r[M_HEADS + h:M_HEADS + h + 1, :]
            ig_c = gc[:, h:h + 1]
            ig_r = gr[h:h + 1, :]
            m_prev = m_ref[sq, h:h + 1, :]
            c_prev = c_ref[sq, h]
            n_prev = n_ref[sq, h:h + 1, :]

            log_d = jnp.where(causal, bc - br + ig_r, -jnp.inf)
            log_past = bc + m_prev
            m_t = jnp.maximum(log_past, jnp.max(log_d, axis=-1, keepdims=True))
            d_mat = jnp.exp(log_d - m_t)
            past_w = jnp.exp(log_past - m_t)
            qs = q * scale
            qsb = qs.astype(BF16)
            kb = k.astype(BF16)
            vb = v.astype(BF16)
            s = _dot_nt(qsb, kb) * d_mat
            num = past_w * _dot(qsb, c_prev.astype(BF16)) + _dot(s.astype(BF16), vb)
            den = past_w * jnp.sum(qs * n_prev, axis=-1, keepdims=True) + jnp.sum(s, axis=-1, keepdims=True)
            hm_ref[r0:r0 + cs, c0:c0 + dk] = num / jnp.maximum(jnp.abs(den), jnp.exp(-m_t))

            m_new = m_t[cs - 1:cs, :]
            b_last = bc[cs - 1:cs, :]
            w_s = jnp.exp(b_last - bc + ig_c - m_new)
            decay = jnp.exp(b_last + m_prev - m_new)
            kw = w_s * k
            c_ref[sq, h] = decay * c_prev + _dot_tn(kw.astype(BF16), vb)
            n_ref[sq, h:h + 1, :] = decay * n_prev + jnp.sum(kw, axis=0, keepdims=True)
            m_ref[sq, h:h + 1, :] = m_new
            rg_advance()

    col_blocks = [slice(n0, n0 + MXU_DIM) for n0 in range(0, d, MXU_DIM)]
    for cols in col_blocks:
        o_pre = _dot_nt(xn_ref[...], wm_ref[4 * d + cols.start:4 * d + cols.stop, :])
        hb_ref[:, cols] = (jax.nn.sigmoid(o_pre) * hm_ref[:, cols]).astype(BF16)
        rg_advance()
    for cols in col_blocks:
        g_b = _dot_nt(xn_ref[...], wm_ref[6 * d + cols.start:6 * d + cols.stop, :])
        rg_advance()
        hm_ref[:, cols] = jax.nn.sigmoid(g_b) * _dot(hb_ref[...], wbrb_ref[:, cols])
        g_a = _dot_nt(xn_ref[...], wm_ref[5 * d + cols.start:5 * d + cols.stop, :])
        sa_ref[:, cols] = jax.nn.sigmoid(g_a)
        rg_advance()
    for _ in rg_pieces:
        pass
    hl_ref[...] = h_ref[...]
    for cols in col_blocks:
        merged = sa_ref[:, cols] * _dot(ha_ref[...], wbra_ref[:, cols]) + hm_ref[:, cols]
        mg_ref[:, cols] = merged.astype(BF16)
    y = _dot(mg_ref[...], wout_ref[...])
    for q in range(nsq):
        o_ref[q] = x_ref[q] + ga_ref[seq_rows[q], :] * y[q * tl:(q + 1) * tl, :]


def _mixer_prompt(x3, mod, gain, w_int, bif, bift, wbra, wbrb, wout, conv_w, conv_b, wa, wi, ba, bi, lam):
    nb, seq, _ = x3.shape
    d = x3.shape[2]
    dk = d // M_HEADS
    nsq = ML_SEQS
    tl = min(ML_ROWS, seq)
    rows = nsq * tl
    row = lambda v: v.reshape(1, d)
    mspecs, _ = _mod_specs(mod.shape[0] - nb, nb, d, 3, 3)
    rg_pairs = (wa.shape[0] // 2, 2 * wa.shape[1], 2 * wa.shape[2])
    tile = lambda: pl.BlockSpec((nsq, tl, d), lambda b, t: (b, t, 0))
    per_seq = lambda *shape: pl.BlockSpec((nsq,) + shape, lambda b, t: (b,) + (0,) * len(shape))
    return pl.pallas_call(
        _mixer_kernel,
        grid=(nb // nsq, seq // tl),
        in_specs=[tile()] + mspecs + [_resident((1, d)),
                  _HBM, _resident(bif.shape), _resident(bift.shape), _HBM, _HBM, _HBM,
                  _resident(conv_w.shape), _resident((1, d)), _resident(wa.shape), _resident(wi.shape),
                  _resident((1, d)), _resident((1, d)), _resident((1, d))],
        out_specs=[tile(), per_seq(M_HEADS, dk, dk), per_seq(M_HEADS, dk), per_seq(M_HEADS, 1),
                   per_seq(CONV_W - 1, d), per_seq(1, d)],
        out_shape=[jax.ShapeDtypeStruct((nb, seq, d), F32),
                   jax.ShapeDtypeStruct((nb, M_HEADS, dk, dk), F32),
                   jax.ShapeDtypeStruct((nb, M_HEADS, dk), F32),
                   jax.ShapeDtypeStruct((nb, M_HEADS, 1), F32),
                   jax.ShapeDtypeStruct((nb, CONV_W - 1, d), F32),
                   jax.ShapeDtypeStruct((nb, 1, d), F32)],
        scratch_shapes=[pltpu.VMEM((rows, d), F32),
                        pltpu.VMEM((7 * d, d), BF16),
                        pltpu.VMEM((LANES, d), BF16),
                        pltpu.VMEM((d, d), BF16), pltpu.VMEM((d, d), BF16), pltpu.VMEM((d, d), BF16)]
        + _stage_scratch(d, STAGE_SLOTS_MIX)
        + [pltpu.VMEM((d // LANES, rows + nsq * SUBLANES * SUBLANES, LANES), F32),
           pltpu.VMEM((nsq, (CONV_W - 1) * SUBLANES + tl, d), F32),
           pltpu.VMEM((nsq, tl, d), F32),
           pltpu.VMEM((nsq, tl, d), F32),
           pltpu.VMEM((nsq, CONV_W - 1, d), F32),
           pltpu.VMEM((nsq, 1, d), F32),
           pltpu.VMEM((rows, d), BF16),
           pltpu.VMEM((rows, d), BF16),
           pltpu.VMEM((rows, d), BF16),
           pltpu.VMEM((rows, d), BF16),
           pltpu.VMEM((rows, d), F32),
           pltpu.VMEM(rg_pairs, BF16), pltpu.VMEM(rg_pairs, BF16)],
        compiler_params=_cparams(("arbitrary", "arbitrary")),
        name="mixer_prompt",
    )(x3, mod, mod, mod, row(gain), w_int, bif, bift, wbra, wbrb, wout,
      conv_w, row(conv_b), wa, wi, row(ba), row(bi), row(lam))


def _head_sum(x, dk):
    parts = []
    for h in range(x.shape[1] // dk):
        sl = x[:, h * dk:(h + 1) * dk]
        parts.append(jnp.broadcast_to(jnp.sum(sl, axis=-1, keepdims=True), sl.shape))
    return jnp.concatenate(parts, axis=-1)


def _head_spread(cols, lane0, dk, rows):
    return jnp.concatenate(
        [jnp.broadcast_to(cols[:, lane0 + h:lane0 + h + 1], (rows, dk)) for h in range(M_HEADS)], axis=-1)


def _dec_pre_kernel(x_ref, sh_ref, sc_ref, g_ref, wint_hbm, bif_ref,
                    cw_ref, cb_ref, wa_in, wi_in, ba_ref, bi_ref, lam_ref,
                    conv0_ref, h0_ref, n0_ref, m0_ref,
                    ha_ref, conv_ref, hs_ref, n_ref, m_ref, qt_ref, kt_ref, dec_ref, wv_ref, pw_ref, sv_ref,
                    den_ref, em_ref,
                    wm_ref, wif_ref, stage, sem, wa_ref, wi_ref):
    rows, d = x_ref.shape
    dk = d // M_HEADS
    scale = dk ** -0.5
    _pair_blocks(wa_in, wa_ref)
    _pair_blocks(wi_in, wi_ref)
    _stage_jobs([(wint_hbm, 0, 4 * d, wm_ref, 0),
                 (wint_hbm, 5 * d, LANES, wif_ref, 0)], stage, sem)
    xn = _rms_mod(x_ref[...], g_ref[...], sh_ref[...], sc_ref[...]).astype(BF16)

    x_rg = _dot_nt(xn, wm_ref[0:d, :])
    u = cb_ref[...]
    for j in range(CONV_W - 1):
        u = u + conv0_ref[j] * cw_ref[j:j + 1, :]
        if j > 0:
            conv_ref[j - 1] = conv0_ref[j]
    u = u + x_rg * cw_ref[CONV_W - 1:CONV_W, :]
    conv_ref[CONV_W - 2] = x_rg
    a, b = _rg_gates(u, wa_ref, wi_ref, ba_ref[...], bi_ref[...], lam_ref[...])
    h = a * h0_ref[...] + b
    hs_ref[...] = h
    ha_ref[...] = h

    q = _dot_nt(xn, wm_ref[d:2 * d, :])
    k = _dot_nt(xn, wm_ref[2 * d:3 * d, :])
    v = _dot_nt(xn, wm_ref[3 * d:4 * d, :])
    qt_ref[...] = _dot_nt(wm_ref[d:2 * d, :], xn).astype(BF16)
    kt_ref[...] = _dot_nt(wm_ref[2 * d:3 * d, :], xn).astype(BF16)
    pre = _dot_nt(xn, wif_ref[...]) + bif_ref[...]
    ig = _head_spread(pre, 0, dk, rows)
    lf = jax.nn.log_sigmoid(_head_spread(pre, M_HEADS, dk, rows))
    m0 = _head_spread(m0_ref[...], 0, dk, rows)
    n0 = n0_ref[...]
    log_past = lf + m0
    m_t = jnp.maximum(log_past, ig)
    d_w = jnp.exp(ig - m_t)
    past_w = jnp.exp(log_past - m_t)
    qs = q * scale
    s = _head_sum(qs * k, dk) * d_w
    den_ref[...] = past_w * _head_sum(qs * n0, dk) + s
    em_ref[...] = jnp.exp(-m_t)
    pw_ref[...] = past_w
    sv_ref[...] = s * v
    w_s = jnp.exp(ig - m_t)
    decay = jnp.exp(lf + m0 - m_t)
    dec_ref[...] = decay
    wv_ref[...] = w_s * v
    n_ref[...] = decay * n0 + w_s * k
    m_ref[...] = m_t


def _dec_pre(x2, mod, gain, w_int, bif, conv_w, conv_b, wa, wi, ba, bi, lam, conv0, h0, n0, m0):
    rows, d = x2.shape
    row = lambda v: v.reshape(1, d)
    m0p = jnp.pad(m0, ((0, 0), (0, LANES - M_HEADS)))
    full = lambda shape: pl.BlockSpec(shape, lambda i, _n=len(shape): (0,) * _n)
    mspec = lambda j: pl.BlockSpec((rows, d), lambda i: (0, j))
    vec = jax.ShapeDtypeStruct((rows, d), F32)
    outs = [vec,
            jax.ShapeDtypeStruct((CONV_W - 1, rows, d), F32),
            vec,
            vec,
            vec,
            jax.ShapeDtypeStruct((d, rows), BF16),
            jax.ShapeDtypeStruct((d, rows), BF16),
            vec, vec, vec, vec, vec, vec]
    args = [x2, mod, mod, row(gain), w_int, bif, conv_w, row(conv_b), wa, wi,
            row(ba), row(bi), row(lam), conv0, h0, n0, m0p]
    in_specs = ([full(x2.shape), mspec(3), mspec(4), full((1, d)), _HBM]
                + [full(a.shape) for a in args[5:]])
    return pl.pallas_call(
        _dec_pre_kernel,
        grid=(1,),
        in_specs=in_specs,
        out_specs=[full(o.shape) for o in outs],
        out_shape=outs,
        scratch_shapes=[pltpu.VMEM((4 * d, d), BF16), pltpu.VMEM((LANES, d), BF16)]
        + _stage_scratch(d, STAGE_SLOTS_DEC)
        + [pltpu.VMEM((wa.shape[0] // 2, 2 * wa.shape[1], 2 * wa.shape[2]), BF16)] * 2,
        compiler_params=_cparams(("arbitrary",)),
        name="decode_pre",
    )(*args)


def _dec_mem_kernel(qt_ref, kt_ref, dec_ref, wv_ref, c0_ref, c_ref, qc_ref):
    bb = c0_ref.shape[0]
    dk = c0_ref.shape[2]
    nseq = qt_ref.shape[1]
    scale = dk ** -0.5
    base = pl.program_id(0) * bb
    seq_id = lax.broadcasted_iota(jnp.int32, (nseq, dk), 0)
    for j in range(bb):
        pick = (seq_id == base + j).astype(BF16)
        qcol = _dot(qt_ref[...], pick) * scale
        kcol = _dot(kt_ref[...], pick)
        seq = pl.ds(base + j, 1)
        for h in range(M_HEADS):
            cols = slice(h * dk, (h + 1) * dk)
            c0 = c0_ref[j, h]
            dec = dec_ref[seq, cols]
            wv = wv_ref[seq, cols]
            c_ref[j, h] = dec * c0 + kcol[cols, :] * wv
            qc_ref[seq, cols] = jnp.sum(qcol[cols, :] * c0, axis=0, keepdims=True)


def _dec_mem(qt, kt, decay, wv, c0):
    nseq, heads, dk, _ = c0.shape
    d = heads * dk
    bb = DEC_BLOCK if nseq % DEC_BLOCK == 0 else 1
    whole = lambda shape: pl.BlockSpec(shape, lambda i: (0, 0))
    return pl.pallas_call(
        _dec_mem_kernel,
        grid=(nseq // bb,),
        in_specs=[whole((d, nseq)), whole((d, nseq)), whole((nseq, d)), whole((nseq, d)),
                  pl.BlockSpec((bb, heads, dk, dk), lambda i: (i, 0, 0, 0))],
        out_specs=[pl.BlockSpec((bb, heads, dk, dk), lambda i: (i, 0, 0, 0)), whole((nseq, d))],
        out_shape=[jax.ShapeDtypeStruct(c0.shape, F32), jax.ShapeDtypeStruct((nseq, d), F32)],
        compiler_params=_cparams(("arbitrary",)),
        name="decode_mem",
    )(qt, kt, decay, wv, c0)


def _dec_post_kernel(x_ref, sh_ref, sc_ref, ga_ref, g_ref, wint_hbm, wbra_hbm, wbrb_hbm, wout_hbm,
                     ha_ref, qc_ref, pw_ref, sv_ref, den_ref, em_ref, o_ref,
                     wm_ref, wbra_ref, wbrb_ref, wout_ref, stage, sem):
    d = x_ref.shape[1]
    _stage_jobs([(wint_hbm, 4 * d, d, wm_ref, 0),
                 (wint_hbm, 5 * d + 2 * M_HEADS, 2 * d, wm_ref, d),
                 (wbra_hbm, 0, d, wbra_ref, 0), (wbrb_hbm, 0, d, wbrb_ref, 0), (wout_hbm, 0, d, wout_ref, 0)],
                stage, sem)
    x = x_ref[...]
    xn = _rms_mod(x, g_ref[...], sh_ref[...], sc_ref[...]).astype(BF16)
    num = pw_ref[...] * qc_ref[...] + sv_ref[...]
    hm = num / jnp.maximum(jnp.abs(den_ref[...]), em_ref[...])
    hb = (jax.nn.sigmoid(_dot_nt(xn, wm_ref[0:d, :])) * hm).astype(BF16)
    g_a = _dot_nt(xn, wm_ref[d:2 * d, :])
    g_b = _dot_nt(xn, wm_ref[2 * d:3 * d, :])
    merged = (jax.nn.sigmoid(g_a) * _dot(ha_ref[...].astype(BF16), wbra_ref[...])
              + jax.nn.sigmoid(g_b) * _dot(hb, wbrb_ref[...]))
    o_ref[...] = x + ga_ref[...] * _dot(merged.astype(BF16), wout_ref[...])


def _dec_post(x2, mod, gain, w_int, wbra, wbrb, wout, ha, qc, pw, sv, den, em):
    rows, d = x2.shape
    full = lambda shape: pl.BlockSpec(shape, lambda i, _n=len(shape): (0,) * _n)
    mspec = lambda j: pl.BlockSpec((rows, d), lambda i: (0, j))
    args = [x2, mod, mod, mod, gain.reshape(1, d), w_int, wbra, wbrb, wout, ha, qc, pw, sv, den, em]
    in_specs = ([full(x2.shape), mspec(3), mspec(4), mspec(5), full((1, d))] + [_HBM] * 4
                + [full(a.shape) for a in args[9:]])
    return pl.pallas_call(
        _dec_post_kernel,
        grid=(1,),
        in_specs=in_specs,
        out_specs=full((rows, d)),
        out_shape=jax.ShapeDtypeStruct((rows, d), F32),
        scratch_shapes=[pltpu.VMEM((3 * d, d), BF16),
                        pltpu.VMEM((d, d), BF16), pltpu.VMEM((d, d), BF16), pltpu.VMEM((d, d), BF16)]
        + _stage_scratch(d, STAGE_SLOTS_DEC),
        compiler_params=_cparams(("arbitrary",)),
        name="decode_post",
    )(*args)


def kernel(x_prompt, x_sample, state_conv, state_rg_h, state_C, state_n, state_m, c_prompt, c_sample, w_ada, b_ada, g_norm1, w_ff1_in, w_ff1_out, g_norm2, w_in, conv_w, conv_b, w_rg_a, b_rg_a, w_rg_i, b_rg_i, rg_lambda, b_ig, b_fg, w_br_a, w_br_b, w_out, g_norm3, w_ff2_in, w_ff2_out, g_final):
    nb, seq, d = x_prompt.shape
    ns = x_sample.shape[0]
    depth = w_ada.shape[0]
    assert depth == 1 and x_sample.shape[1] == 1 and nb == SUBLANES
    assert seq % ML_ROWS == 0 and seq % FFN_ROWS == 0 and ML_ROWS % (SUBLANES * SUBLANES) == 0
    assert nb % ML_SEQS == 0
    heads, dk = M_HEADS, d // M_HEADS
    assert w_in.shape[2] == 7 * d + 2 * heads and GATE_ROWS >= 2 * heads

    w_int = w_in[0].T
    b_gates = jnp.concatenate([b_ig[0], b_fg[0]])
    bif = jnp.pad(b_gates, (0, LANES - 2 * heads)).reshape(1, LANES)
    bift = jnp.pad(b_gates, (0, GATE_ROWS - 2 * heads)).reshape(GATE_ROWS, 1)
    wa, wi = w_rg_a[0], w_rg_i[0]
    wf1i, wf1o, wf2i, wf2o = w_ff1_in[0], w_ff1_out[0], w_ff2_in[0], w_ff2_out[0]
    wbra, wbrb, wout = w_br_a[0], w_br_b[0], w_out[0]

    mod = _ada(c_sample, c_prompt, w_ada[0], b_ada[0])

    xp, xs = _ffn(x_prompt.reshape(nb * seq, d), x_sample.reshape(ns, d), mod, 0, g_norm1[0],
                  wf1i, wf1o, None, rows_per_seq=seq)

    xp, c_p, n_p, m_p, conv_p, h_p = _mixer_prompt(
        xp.reshape(nb, seq, d), mod, g_norm2[0], w_int, bif, bift, wbra, wbrb, wout, conv_w[0], conv_b[0],
        wa, wi, b_rg_a[0], b_rg_i[0], rg_lambda[0])
    xp = xp.reshape(nb * seq, d)

    conv0 = jnp.swapaxes(state_conv[0], 0, 1)
    (ha_s, conv_s, h_s, n_s, m_s, qt, kt, decay, wv, pw, sv, den, em) = _dec_pre(
        xs, mod, g_norm2[0], w_int, bif, conv_w[0], conv_b[0], wa, wi,
        b_rg_a[0], b_rg_i[0], rg_lambda[0], conv0, state_rg_h[0], state_n[0].reshape(ns, d), state_m[0])
    c_s, qc = _dec_mem(qt, kt, decay, wv, state_C[0])
    xs = _dec_post(xs, mod, g_norm2[0], w_int, wbra, wbrb, wout, ha_s, qc, pw, sv, den, em)

    yp, ys = _ffn(xp, xs, mod, 6, g_norm3[0], wf2i, wf2o, g_final, rows_per_seq=seq)

    return (yp.reshape(nb, seq, d), ys.reshape(ns, 1, d),
            conv_p[None], h_p.reshape(1, nb, d), c_p[None], n_p[None], m_p.reshape(1, nb, heads),
            jnp.swapaxes(conv_s, 0, 1)[None], h_s[None], c_s[None], n_s.reshape(1, ns, heads, dk), m_s[:, ::dk][None])
```

```python
import functools
import itertools

import jax
import jax.numpy as jnp
from jax import lax
from jax.experimental import pallas as pl
from jax.experimental.pallas import tpu as pltpu

F32 = jnp.float32
BF16 = jnp.bfloat16

EPS = 1e-6
RG_C = 8.0
CONV_W = 4
M_HEADS = 4

SUBLANES = 8
LANES = 128
MXU_DIM = 256
VMEM_LIMIT_BYTES = 60 * 1024 * 1024

ADA_STEPS = 4
FFN_ROWS = 1024
FFN_COLS = 256
ML_ROWS = 512
ML_SEQS = 1
ML_CHUNK = 256
RG_PIECES = 4
DEC_BLOCK = 8
STAGE_ROWS = 256
STAGE_SLOTS = 4
STAGE_SLOTS_MIX = 8
STAGE_SLOTS_DEC = 16
STAGE_ROWS_WIDE = 64
STAGE_SLOTS_WIDE = 6
GATE_ROWS = 16


def _cparams(sem):
    return pltpu.CompilerParams(dimension_semantics=sem, vmem_limit_bytes=VMEM_LIMIT_BYTES)


def _resident(shape):
    nd = len(shape)
    return pl.BlockSpec(shape, lambda *_: (0,) * nd, pipeline_mode=pl.Buffered(1))


_HBM = pl.BlockSpec(memory_space=pl.ANY)


def _rms_mod(x, gain, shift, scale):
    ms = jnp.mean(x * x, axis=-1, keepdims=True)
    return x * lax.rsqrt(ms + EPS) * gain * (1.0 + scale) + shift


def _dot(a, b):
    return jnp.dot(a, b, preferred_element_type=F32)


def _dot_nt(a, b):
    return lax.dot_general(a, b, (((1,), (1,)), ((), ())), preferred_element_type=F32)


def _dot_tn(a, b):
    return lax.dot_general(a, b, (((0,), (0,)), ((), ())), preferred_element_type=F32)


def _split3(x):
    hi = x.astype(BF16)
    r1 = x - hi.astype(F32)
    mid = r1.astype(BF16)
    lo = (r1 - mid.astype(F32)).astype(BF16)
    return hi, mid, lo


def _stage_groups(groups):
    plans = []
    for jobs, stage, sem in groups:
        slots, rows_per_copy, _ = stage.shape
        chunks = []
        for src, src_row0, nrows, dst, dst_row0 in jobs:
            for r in range(0, nrows, rows_per_copy):
                chunks.append((src, src_row0 + r, dst, dst_row0 + r, min(rows_per_copy, nrows - r)))

        def copy(c, chunks=chunks, stage=stage, sem=sem, slots=slots):
            src, src_row, _, _, rows = chunks[c]
            return pltpu.make_async_copy(src.at[pl.ds(src_row, rows)],
                                         stage.at[c % slots, pl.ds(0, rows)], sem.at[c % slots])
        plans.append((copy, chunks, stage, slots))

    for copy, chunks, _, slots in plans:
        for c in range(min(slots, len(chunks))):
            copy(c).start(priority=c % 2)
    for copy, chunks, stage, slots in plans:
        for c, (_, _, dst, dst_row, rows) in enumerate(chunks):
            copy(c).wait()
            dst[dst_row:dst_row + rows, :] = stage[c % slots, 0:rows, :].astype(BF16)
            if c + slots < len(chunks):
                copy(c + slots).start(priority=(c + slots) % 2)


def _stage_jobs(jobs, stage, sem):
    _stage_groups([(jobs, stage, sem)])


def _stage_scratch(cols, slots=STAGE_SLOTS, rows=STAGE_ROWS):
    return [pltpu.VMEM((slots, rows, cols), F32), pltpu.SemaphoreType.DMA((slots,))]


def _ada_kernel(cs_ref, cp_ref, w_ref, b_ref, o_ref):
    ns = cs_ref.shape[0]
    w = w_ref[...].astype(BF16)
    for c_ref, rows in ((cs_ref, slice(0, ns)), (cp_ref, slice(ns, o_ref.shape[0]))):
        c = c_ref[...]
        o_ref[rows, :] = _dot((c * jax.nn.sigmoid(c)).astype(BF16), w) + b_ref[...]


def _ada(c_sample, c_prompt, w_ada, b_ada):
    ns, d = c_sample.shape
    nb = c_prompt.shape[0]
    n = w_ada.shape[1]
    tn = n // ADA_STEPS
    assert tn * ADA_STEPS == n and tn % LANES == 0 and ns % SUBLANES == 0
    return pl.pallas_call(
        _ada_kernel,
        grid=(n // tn,),
        in_specs=[pl.BlockSpec((ns, d), lambda j: (0, 0)),
                  pl.BlockSpec((nb, d), lambda j: (0, 0)),
                  pl.BlockSpec((d, tn), lambda j: (0, j)),
                  pl.BlockSpec((1, tn), lambda j: (0, j))],
        out_specs=pl.BlockSpec((ns + nb, tn), lambda j: (0, j)),
        out_shape=jax.ShapeDtypeStruct((ns + nb, n), F32),
        compiler_params=_cparams(("arbitrary",)),
        name="adaln_mod",
    )(c_sample, c_prompt, w_ada, b_ada.reshape(1, n))


def _ffn_rows(x_ref, shift, scale, gate, g_ref, wi_ref, wo_ref, gf_ref, o_ref, act_ref, d_ff):
    rows = x_ref.shape[0]
    x = x_ref[...]
    xn = _rms_mod(x, g_ref[...], shift, scale).astype(BF16)
    for j in range(d_ff // FFN_COLS):
        lo = j * FFN_COLS
        hg = _dot(xn, wi_ref[:, lo:lo + FFN_COLS])
        hu = _dot(xn, wi_ref[:, d_ff + lo:d_ff + lo + FFN_COLS])
        act_ref[0:rows, lo:lo + FFN_COLS] = (hg * jax.nn.sigmoid(hg) * hu).astype(BF16)
    y = _dot(act_ref[0:rows, :], wo_ref[...])
    out = x + (0.5 * gate) * y
    if gf_ref is not None:
        ms = jnp.mean(out * out, axis=-1, keepdims=True)
        out = out * lax.rsqrt(ms + EPS) * gf_ref[...]
    o_ref[...] = out


def _ffn_kernel(*refs, d_ff, final, n_prompt_steps, tiles_per_seq):
    n_in = 13 if final else 12
    xp_ref, shp_ref, scp_ref, gap_ref, xs_ref, shs_ref, scs_ref, gas_ref, g_ref, wi_hbm, wo_hbm = refs[:11]
    gf_ref = refs[11] if final else None
    op_ref, os_ref = refs[n_in - 1:n_in + 1]
    act_ref, wi_ref, wo_ref, stage_i, sem_i, stage_o, sem_o = refs[-7:]
    step = pl.program_id(0)

    @pl.when(step == 0)
    def _():
        _stage_groups([([(wi_hbm, 0, wi_ref.shape[0], wi_ref, 0)], stage_i, sem_i),
                       ([(wo_hbm, 0, wo_ref.shape[0], wo_ref, 0)], stage_o, sem_o)])

    @pl.when(step < n_prompt_steps)
    def _():
        seq = pl.ds(step // tiles_per_seq, 1)
        _ffn_rows(xp_ref, shp_ref[seq, :], scp_ref[seq, :], gap_ref[seq, :], g_ref, wi_ref, wo_ref, gf_ref,
                  op_ref, act_ref, d_ff)

    @pl.when(step == n_prompt_steps)
    def _():
        _ffn_rows(xs_ref, shs_ref[...], scs_ref[...], gas_ref[...], g_ref, wi_ref, wo_ref, gf_ref,
                  os_ref, act_ref, d_ff)


def _mod_specs(ns, nb, d, j0, n):
    assert ns % nb == 0
    prompt = [pl.BlockSpec((nb, d), lambda *_, j=j: (ns // nb, j)) for j in range(j0, j0 + n)]
    sample = [pl.BlockSpec((ns, d), lambda *_, j=j: (0, j)) for j in range(j0, j0 + n)]
    return prompt, sample


def _ffn(xp, xs, mod, j0, gain, wi, wo, g_final, *, rows_per_seq):
    rows, d = xp.shape
    ns = xs.shape[0]
    d_ff = wo.shape[0]
    tm = min(FFN_ROWS, rows_per_seq)
    tiles_per_seq = rows_per_seq // tm
    n_steps = rows // tm
    tile_of = lambda i: jnp.minimum(i, n_steps - 1)
    mspec_p, mspec_s = _mod_specs(ns, rows // rows_per_seq, d, j0, 3)
    final = g_final is not None
    in_specs = ([pl.BlockSpec((tm, d), lambda i: (tile_of(i), 0))] + mspec_p
                + [pl.BlockSpec((ns, d), lambda i: (0, 0))] + mspec_s + [_resident((1, d)), _HBM, _HBM])
    args = [xp, mod, mod, mod, xs, mod, mod, mod, gain.reshape(1, d), wi, wo]
    if final:
        in_specs.append(_resident((1, d)))
        args.append(g_final.reshape(1, d))
    return pl.pallas_call(
        functools.partial(_ffn_kernel, d_ff=d_ff, final=final, n_prompt_steps=n_steps, tiles_per_seq=tiles_per_seq),
        grid=(n_steps + 1,),
        in_specs=in_specs,
        out_specs=[pl.BlockSpec((tm, d), lambda i: (tile_of(i), 0)), pl.BlockSpec((ns, d), lambda i: (0, 0))],
        out_shape=[jax.ShapeDtypeStruct((rows, d), F32), jax.ShapeDtypeStruct((ns, d), F32)],
        scratch_shapes=[pltpu.VMEM((tm, d_ff), BF16), pltpu.VMEM(wi.shape, BF16), pltpu.VMEM(wo.shape, BF16)]
        + _stage_scratch(wi.shape[1], STAGE_SLOTS_WIDE, STAGE_ROWS_WIDE) + _stage_scratch(wo.shape[1]),
        compiler_params=_cparams(("arbitrary",)),
        name="ffn_final" if final else "ffn",
    )(*args)


def _pair_blocks(src_ref, dst_ref):
    nblk, c, _ = src_ref.shape
    dst_ref[...] = jnp.zeros(dst_ref.shape, dst_ref.dtype)
    for p in range(nblk // 2):
        dst_ref[p, 0:c, 0:c] = src_ref[2 * p].astype(BF16)
        dst_ref[p, c:2 * c, c:2 * c] = src_ref[2 * p + 1].astype(BF16)


def _rg_gates_block(u, wa, wi, ba, bi, lam):
    ub = u.astype(BF16)
    r = jax.nn.sigmoid(_dot(ub, wa) + ba)
    i_g = jax.nn.sigmoid(_dot(ub, wi) + bi)
    log_a = -RG_C * r * jax.nn.softplus(-lam)
    a = jnp.exp(log_a)
    b = jnp.sqrt(-jnp.tanh(log_a) * (a * a + 1.0)) * (i_g * u)
    return a, b


def _rg_gates(u, wa_ref, wi_ref, ba, bi, lam):
    parts = []
    for p in range(wa_ref.shape[0]):
        cols = slice(p * MXU_DIM, (p + 1) * MXU_DIM)
        parts.append(_rg_gates_block(u[:, cols], wa_ref[p], wi_ref[p], ba[:, cols], bi[:, cols], lam[:, cols]))
    return (jnp.concatenate([a for a, _ in parts], axis=-1), jnp.concatenate([b for _, b in parts], axis=-1))


def _rglru_front(xn_ref, w_rows_ref, lb_ref, tb_ref, tail_ref, conv_ref):
    nsq = tb_ref.shape[0]
    d = xn_ref.shape[1]
    tl = xn_ref.shape[0] // nsq
    seg = tl // SUBLANES
    nlb = d // LANES
    pre = (CONV_W - 1) * SUBLANES

    pitch = seg + SUBLANES
    x_rg = _dot_nt(xn_ref[...], w_rows_ref[0:d, :])
    for cb in range(nlb):
        for s in range(nsq * SUBLANES):
            lb_ref[cb, s * pitch:s * pitch + seg, :] = x_rg[s * seg:(s + 1) * seg, cb * LANES:(cb + 1) * LANES]
    first = lax.broadcasted_iota(jnp.int32, (SUBLANES, d), 0) == 0
    for q in range(nsq):
        for j in range(seg):
            for cb in range(nlb):
                tb_ref[q, pre + j * SUBLANES:pre + (j + 1) * SUBLANES, cb * LANES:(cb + 1) * LANES] = (
                    lb_ref[cb, pl.ds(q * SUBLANES * pitch + j, SUBLANES, stride=pitch), :])

        for i in range(CONV_W - 1):
            j = seg - (CONV_W - 1) + i
            slab = tb_ref[q, pre + j * SUBLANES:pre + (j + 1) * SUBLANES, :]
            tb_ref[q, i * SUBLANES:(i + 1) * SUBLANES, :] = jnp.where(first, tail_ref[q, i:i + 1, :],
                                                                     pltpu.roll(slab, 1, axis=0))
            last = slab[SUBLANES - 1:SUBLANES, :]
            tail_ref[q, i:i + 1, :] = last
            conv_ref[q, i:i + 1, :] = last


def _rglru_block(p, q, cw_ref, cb_ref, wa_ref, wi_ref, ba_ref, bi_ref, lam_ref,
                 lb_ref, tb_ref, a_ref, b_ref, h_ref, ha_ref):
    tl = a_ref.shape[1]
    seg = tl // SUBLANES
    cols = slice(p * MXU_DIM, (p + 1) * MXU_DIM)
    lane_blocks = range(p * MXU_DIM // LANES, (p + 1) * MXU_DIM // LANES)
    row0 = q * tl

    piece = tl // RG_PIECES
    for r0 in range(0, tl, piece):
        u = cb_ref[:, cols]
        for j in range(CONV_W):
            u = u + tb_ref[q, j * SUBLANES + r0:j * SUBLANES + r0 + piece, cols] * cw_ref[j:j + 1, cols]
        a, b = _rg_gates_block(u, wa_ref[p], wi_ref[p], ba_ref[:, cols], bi_ref[:, cols], lam_ref[:, cols])
        a_ref[q, r0:r0 + piece, cols] = a
        b_ref[q, r0:r0 + piece, cols] = b
        yield

    acc_a = a_ref[q, 0:SUBLANES, cols]
    acc_b = b_ref[q, 0:SUBLANES, cols]
    for j in range(1, seg):
        rows = slice(j * SUBLANES, (j + 1) * SUBLANES)
        a_j = a_ref[q, rows, cols]
        acc_b = a_j * acc_b + b_ref[q, rows, cols]
        acc_a = a_j * acc_a
        a_ref[q, rows, cols] = acc_a
        b_ref[q, rows, cols] = acc_b

    h = h_ref[q, :, cols]
    h_in = []
    for s in range(SUBLANES):
        h_in.append(h)
        h = acc_a[s:s + 1, :] * h + acc_b[s:s + 1, :]
    h_ref[q, :, cols] = h
    h_in = jnp.concatenate(h_in, axis=0)
    yield

    for j in range(seg):
        rows = slice(j * SUBLANES, (j + 1) * SUBLANES)
        h_j = a_ref[q, rows, cols] * h_in + b_ref[q, rows, cols]
        for i, cb in enumerate(lane_blocks):
            lb_ref[cb, row0 + j * SUBLANES:row0 + (j + 1) * SUBLANES, :] = h_j[:, i * LANES:(i + 1) * LANES]
    for s in range(SUBLANES):
        for cb in lane_blocks:
            ha_ref[row0 + s * seg:row0 + (s + 1) * seg, cb * LANES:(cb + 1) * LANES] = (
                lb_ref[cb, pl.ds(row0 + s, seg, stride=SUBLANES), :].astype(BF16))
    yield


def _mixer_kernel(x_ref, sh_ref, sc_ref, ga_ref, g_ref, wint_hbm, bif_ref, bift_ref,
                  wbra_hbm, wbrb_hbm, wout_hbm, cw_ref, cb_ref, wa_in, wi_in, ba_ref, bi_ref, lam_ref,
                  o_ref, c_ref, n_ref, m_ref, conv_ref, hl_ref,
                  hm_ref, wm_ref, wif_ref, wbra_ref, wbrb_ref, wout_ref, stage, sem,
                  lb_ref, tb_ref, a_ref, b_ref, tail_ref, h_ref, ha_ref, mg_ref, xn_ref, hb_ref, sa_ref,
                  wa_ref, wi_ref):
    nsq, tl, d = x_ref.shape
    dk = d // M_HEADS
    cs = min(ML_CHUNK, tl)
    scale = dk ** -0.5
    gate_row0 = 5 * d

    @pl.when((pl.program_id(0) == 0) & (pl.program_id(1) == 0))
    def _():
        _stage_jobs([(wint_hbm, 0, 5 * d, wm_ref, 0),
                     (wint_hbm, gate_row0, LANES, wif_ref, 0),
                     (wint_hbm, gate_row0 + 2 * M_HEADS, 2 * d, wm_ref, 5 * d),
                     (wbra_hbm, 0, d, wbra_ref, 0), (wbrb_hbm, 0, d, wbrb_ref, 0), (wout_hbm, 0, d, wout_ref, 0)],
                    stage, sem)
        _pair_blocks(wa_in, wa_ref)
        _pair_blocks(wi_in, wi_ref)

    @pl.when(pl.program_id(1) == 0)
    def _():
        c_ref[...] = jnp.zeros_like(c_ref)
        n_ref[...] = jnp.zeros_like(n_ref)
        m_ref[...] = jnp.zeros_like(m_ref)
        tail_ref[...] = jnp.zeros_like(tail_ref)
        h_ref[...] = jnp.zeros_like(h_ref)

    seq_rows = [pl.ds(pl.program_id(0) * nsq + q, 1) for q in range(nsq)]
    for q in range(nsq):
        xn_ref[q * tl:(q + 1) * tl, :] = _rms_mod(x_ref[q], g_ref[...], sh_ref[seq_rows[q], :],
                                                  sc_ref[seq_rows[q], :]).astype(BF16)

    _rglru_front(xn_ref, wm_ref, lb_ref, tb_ref, tail_ref, conv_ref)
    rglru_block = functools.partial(
        _rglru_block, cw_ref=cw_ref, cb_ref=cb_ref, wa_ref=wa_ref, wi_ref=wi_ref, ba_ref=ba_ref, bi_ref=bi_ref,
        lam_ref=lam_ref, lb_ref=lb_ref, tb_ref=tb_ref, a_ref=a_ref, b_ref=b_ref, h_ref=h_ref, ha_ref=ha_ref)
    n_rg = wa_ref.shape[0]
    wq0 = d

    rg_pieces = itertools.chain.from_iterable(rglru_block(p, sq) for p in range(n_rg) for sq in range(nsq))

    def rg_advance(n=1):
        for _ in range(n):
            next(rg_pieces, None)

    pre_c = _dot_nt(xn_ref[...], wif_ref[...]) + bif_ref[...]
    pre_r = _dot_nt(wif_ref[0:GATE_ROWS, :], xn_ref[...]) + bift_ref[...]
    col_is_f = lax.broadcasted_iota(jnp.int32, pre_c.shape, 1) >= M_HEADS
    row_is_f = lax.broadcasted_iota(jnp.int32, pre_r.shape, 0) >= M_HEADS
    gate_c = jnp.where(col_is_f, jax.nn.log_sigmoid(pre_c), pre_c)
    gate_r = jnp.where(row_is_f, jax.nn.log_sigmoid(pre_r), pre_r)

    ti = lax.broadcasted_iota(jnp.int32, (cs, cs), 0)
    si = lax.broadcasted_iota(jnp.int32, (cs, cs), 1)
    causal = si <= ti
    lower = causal.astype(BF16)
    upper = (ti <= si).astype(BF16)

    chunks = []
    for sq in range(nsq):
        for r0 in range(sq * tl, (sq + 1) * tl, cs):
            gc = gate_c[r0:r0 + cs, :]
            gr = gate_r[:, r0:r0 + cs]
            cum_c = sum(_dot(lower, part) for part in _split3(gc))
            cum_r = sum(_dot(part, upper) for part in _split3(gr))
            chunks.append((sq, r0, gc, gr, cum_c, cum_r))

    for h in range(M_HEADS):
        c0 = h * dk
        q_all = _dot_nt(xn_ref[...], wm_ref[wq0 + c0:wq0 + c0 + dk, :])
        k_all = _dot_nt(xn_ref[...], wm_ref[wq0 + d + c0:wq0 + d + c0 + dk, :])
        v_all = _dot_nt(xn_ref[...], wm_ref[wq0 + 2 * d + c0:wq0 + 2 * d + c0 + dk, :])
        rg_advance()
        for sq, r0, gc, gr, cum_c, cum_r in chunks:
            q = q_all[r0:r0 + cs, :]
            k = k_all[r0:r0 + cs, :]
            v = v_all[r0:r0 + cs, :]
            bc = cum_c[:, M_HEADS + h:M_HEADS + h + 1]
            br = cum_r[M_HEADS + h:M_HEADS + h + 1, :]
            ig_c = gc[:, h:h + 1]
            ig_r = gr[h:h + 1, :]
            m_prev = m_ref[sq, h:h + 1, :]
            c_prev = c_ref[sq, h]
            n_prev = n_ref[sq, h:h + 1, :]

            log_d = jnp.where(causal, bc - br + ig_r, -jnp.inf)
            log_past = bc + m_prev
            m_t = jnp.maximum(log_past, jnp.max(log_d, axis=-1, keepdims=True))
            d_mat = jnp.exp(log_d - m_t)
            past_w = jnp.exp(log_past - m_t)
            qs = q * scale
            qsb = qs.astype(BF16)
            kb = k.astype(BF16)
            vb = v.astype(BF16)
            s = _dot_nt(qsb, kb) * d_mat
            num = past_w * _dot(qsb, c_prev.astype(BF16)) + _dot(s.astype(BF16), vb)
            den = past_w * jnp.sum(qs * n_prev, axis=-1, keepdims=True) + jnp.sum(s, axis=-1, keepdims=True)
            hm_ref[r0:r0 + cs, c0:c0 + dk] = num / jnp.maximum(jnp.abs(den), jnp.exp(-m_t))

            m_new = m_t[cs - 1:cs, :]
            b_last = bc[cs - 1:cs, :]
            w_s = jnp.exp(b_last - bc + ig_c - m_new)
            decay = jnp.exp(b_last + m_prev - m_new)
            kw = w_s * k
            c_ref[sq, h] = decay * c_prev + _dot_tn(kw.astype(BF16), vb)
            n_ref[sq, h:h + 1, :] = decay * n_prev + jnp.sum(kw, axis=0, keepdims=True)
            m_ref[sq, h:h + 1, :] = m_new
            rg_advance()

    col_blocks = [slice(n0, n0 + MXU_DIM) for n0 in range(0, d, MXU_DIM)]
    for cols in col_blocks:
        o_pre = _dot_nt(xn_ref[...], wm_ref[4 * d + cols.start:4 * d + cols.stop, :])
        hb_ref[:, cols] = (jax.nn.sigmoid(o_pre) * hm_ref[:, cols]).astype(BF16)
        rg_advance()
    for cols in col_blocks:
        g_b = _dot_nt(xn_ref[...], wm_ref[6 * d + cols.start:6 * d + cols.stop, :])
        rg_advance()
        hm_ref[:, cols] = jax.nn.sigmoid(g_b) * _dot(hb_ref[...], wbrb_ref[:, cols])
        g_a = _dot_nt(xn_ref[...], wm_ref[5 * d + cols.start:5 * d + cols.stop, :])
        sa_ref[:, cols] = jax.nn.sigmoid(g_a)
        rg_advance()
    for _ in rg_pieces:
        pass
    hl_ref[...] = h_ref[...]
    for cols in col_blocks:
        merged = sa_ref[:, cols] * _dot(ha_ref[...], wbra_ref[:, cols]) + hm_ref[:, cols]
        mg_ref[:, cols] = merged.astype(BF16)
    y = _dot(mg_ref[...], wout_ref[...])
    for q in range(nsq):
        o_ref[q] = x_ref[q] + ga_ref[seq_rows[q], :] * y[q * tl:(q + 1) * tl, :]


def _mixer_prompt(x3, mod, gain, w_int, bif, bift, wbra, wbrb, wout, conv_w, conv_b, wa, wi, ba, bi, lam):
    nb, seq, _ = x3.shape
    d = x3.shape[2]
    dk = d // M_HEADS
    nsq = ML_SEQS
    tl = min(ML_ROWS, seq)
    rows = nsq * tl
    row = lambda v: v.reshape(1, d)
    mspecs, _ = _mod_specs(mod.shape[0] - nb, nb, d, 3, 3)
    rg_pairs = (wa.shape[0] // 2, 2 * wa.shape[1], 2 * wa.shape[2])
    tile = lambda: pl.BlockSpec((nsq, tl, d), lambda b, t: (b, t, 0))
    per_seq = lambda *shape: pl.BlockSpec((nsq,) + shape, lambda b, t: (b,) + (0,) * len(shape))
    return pl.pallas_call(
        _mixer_kernel,
        grid=(nb // nsq, seq // tl),
        in_specs=[tile()] + mspecs + [_resident((1, d)),
                  _HBM, _resident(bif.shape), _resident(bift.shape), _HBM, _HBM, _HBM,
                  _resident(conv_w.shape), _resident((1, d)), _resident(wa.shape), _resident(wi.shape),
                  _resident((1, d)), _resident((1, d)), _resident((1, d))],
        out_specs=[tile(), per_seq(M_HEADS, dk, dk), per_seq(M_HEADS, dk), per_seq(M_HEADS, 1),
                   per_seq(CONV_W - 1, d), per_seq(1, d)],
        out_shape=[jax.ShapeDtypeStruct((nb, seq, d), F32),
                   jax.ShapeDtypeStruct((nb, M_HEADS, dk, dk), F32),
                   jax.ShapeDtypeStruct((nb, M_HEADS, dk), F32),
                   jax.ShapeDtypeStruct((nb, M_HEADS, 1), F32),
                   jax.ShapeDtypeStruct((nb, CONV_W - 1, d), F32),
                   jax.ShapeDtypeStruct((nb, 1, d), F32)],
        scratch_shapes=[pltpu.VMEM((rows, d), F32),
                        pltpu.VMEM((7 * d, d), BF16),
                        pltpu.VMEM((LANES, d), BF16),
                        pltpu.VMEM((d, d), BF16), pltpu.VMEM((d, d), BF16), pltpu.VMEM((d, d), BF16)]
        + _stage_scratch(d, STAGE_SLOTS_MIX)
        + [pltpu.VMEM((d // LANES, rows + nsq * SUBLANES * SUBLANES, LANES), F32),
           pltpu.VMEM((nsq, (CONV_W - 1) * SUBLANES + tl, d), F32),
           pltpu.VMEM((nsq, tl, d), F32),
           pltpu.VMEM((nsq, tl, d), F32),
           pltpu.VMEM((nsq, CONV_W - 1, d), F32),
           pltpu.VMEM((nsq, 1, d), F32),
           pltpu.VMEM((rows, d), BF16),
           pltpu.VMEM((rows, d), BF16),
           pltpu.VMEM((rows, d), BF16),
           pltpu.VMEM((rows, d), BF16),
           pltpu.VMEM((rows, d), F32),
           pltpu.VMEM(rg_pairs, BF16), pltpu.VMEM(rg_pairs, BF16)],
        compiler_params=_cparams(("arbitrary", "arbitrary")),
        name="mixer_prompt",
    )(x3, mod, mod, mod, row(gain), w_int, bif, bift, wbra, wbrb, wout,
      conv_w, row(conv_b), wa, wi, row(ba), row(bi), row(lam))


def _head_sum(x, dk):
    parts = []
    for h in range(x.shape[1] // dk):
        sl = x[:, h * dk:(h + 1) * dk]
        parts.append(jnp.broadcast_to(jnp.sum(sl, axis=-1, keepdims=True), sl.shape))
    return jnp.concatenate(parts, axis=-1)


def _head_spread(cols, lane0, dk, rows):
    return jnp.concatenate(
        [jnp.broadcast_to(cols[:, lane0 + h:lane0 + h + 1], (rows, dk)) for h in range(M_HEADS)], axis=-1)


def _dec_pre_kernel(x_ref, sh_ref, sc_ref, g_ref, wint_hbm, bif_ref,
                    cw_ref, cb_ref, wa_in, wi_in, ba_ref, bi_ref, lam_ref,
                    conv0_ref, h0_ref, n0_ref, m0_ref,
                    ha_ref, conv_ref, hs_ref, n_ref, m_ref, qt_ref, kt_ref, dec_ref, wv_ref, pw_ref, sv_ref,
                    den_ref, em_ref,
                    wm_ref, wif_ref, stage, sem, wa_ref, wi_ref):
    rows, d = x_ref.shape
    dk = d // M_HEADS
    scale = dk ** -0.5
    _pair_blocks(wa_in, wa_ref)
    _pair_blocks(wi_in, wi_ref)
    _stage_jobs([(wint_hbm, 0, 4 * d, wm_ref, 0),
                 (wint_hbm, 5 * d, LANES, wif_ref, 0)], stage, sem)
    xn = _rms_mod(x_ref[...], g_ref[...], sh_ref[...], sc_ref[...]).astype(BF16)

    x_rg = _dot_nt(xn, wm_ref[0:d, :])
    u = cb_ref[...]
    for j in range(CONV_W - 1):
        u = u + conv0_ref[j] * cw_ref[j:j + 1, :]
        if j > 0:
            conv_ref[j - 1] = conv0_ref[j]
    u = u + x_rg * cw_ref[CONV_W - 1:CONV_W, :]
    conv_ref[CONV_W - 2] = x_rg
    a, b = _rg_gates(u, wa_ref, wi_ref, ba_ref[...], bi_ref[...], lam_ref[...])
    h = a * h0_ref[...] + b
    hs_ref[...] = h
    ha_ref[...] = h

    q = _dot_nt(xn, wm_ref[d:2 * d, :])
    k = _dot_nt(xn, wm_ref[2 * d:3 * d, :])
    v = _dot_nt(xn, wm_ref[3 * d:4 * d, :])
    qt_ref[...] = _dot_nt(wm_ref[d:2 * d, :], xn).astype(BF16)
    kt_ref[...] = _dot_nt(wm_ref[2 * d:3 * d, :], xn).astype(BF16)
    pre = _dot_nt(xn, wif_ref[...]) + bif_ref[...]
    ig = _head_spread(pre, 0, dk, rows)
    lf = jax.nn.log_sigmoid(_head_spread(pre, M_HEADS, dk, rows))
    m0 = _head_spread(m0_ref[...], 0, dk, rows)
    n0 = n0_ref[...]
    log_past = lf + m0
    m_t = jnp.maximum(log_past, ig)
    d_w = jnp.exp(ig - m_t)
    past_w = jnp.exp(log_past - m_t)
    qs = q * scale
    s = _head_sum(qs * k, dk) * d_w
    den_ref[...] = past_w * _head_sum(qs * n0, dk) + s
    em_ref[...] = jnp.exp(-m_t)
    pw_ref[...] = past_w
    sv_ref[...] = s * v
    w_s = jnp.exp(ig - m_t)
    decay = jnp.exp(lf + m0 - m_t)
    dec_ref[...] = decay
    wv_ref[...] = w_s * v
    n_ref[...] = decay * n0 + w_s * k
    m_ref[...] = m_t


def _dec_pre(x2, mod, gain, w_int, bif, conv_w, conv_b, wa, wi, ba, bi, lam, conv0, h0, n0, m0):
    rows, d = x2.shape
    row = lambda v: v.reshape(1, d)
    m0p = jnp.pad(m0, ((0, 0), (0, LANES - M_HEADS)))
    full = lambda shape: pl.BlockSpec(shape, lambda i, _n=len(shape): (0,) * _n)
    mspec = lambda j: pl.BlockSpec((rows, d), lambda i: (0, j))
    vec = jax.ShapeDtypeStruct((rows, d), F32)
    outs = [vec,
            jax.ShapeDtypeStruct((CONV_W - 1, rows, d), F32),
            vec,
            vec,
            vec,
            jax.ShapeDtypeStruct((d, rows), BF16),
            jax.ShapeDtypeStruct((d, rows), BF16),
            vec, vec, vec, vec, vec, vec]
    args = [x2, mod, mod, row(gain), w_int, bif, conv_w, row(conv_b), wa, wi,
            row(ba), row(bi), row(lam), conv0, h0, n0, m0p]
    in_specs = ([full(x2.shape), mspec(3), mspec(4), full((1, d)), _HBM]
                + [full(a.shape) for a in args[5:]])
    return pl.pallas_call(
        _dec_pre_kernel,
        grid=(1,),
        in_specs=in_specs,
        out_specs=[full(o.shape) for o in outs],
        out_shape=outs,
        scratch_shapes=[pltpu.VMEM((4 * d, d), BF16), pltpu.VMEM((LANES, d), BF16)]
        + _stage_scratch(d, STAGE_SLOTS_DEC)
        + [pltpu.VMEM((wa.shape[0] // 2, 2 * wa.shape[1], 2 * wa.shape[2]), BF16)] * 2,
        compiler_params=_cparams(("arbitrary",)),
        name="decode_pre",
    )(*args)


def _dec_mem_kernel(qt_ref, kt_ref, dec_ref, wv_ref, c0_ref, c_ref, qc_ref):
    bb = c0_ref.shape[0]
    dk = c0_ref.shape[2]
    nseq = qt_ref.shape[1]
    scale = dk ** -0.5
    base = pl.program_id(0) * bb
    seq_id = lax.broadcasted_iota(jnp.int32, (nseq, dk), 0)
    for j in range(bb):
        pick = (seq_id == base + j).astype(BF16)
        qcol = _dot(qt_ref[...], pick) * scale
        kcol = _dot(kt_ref[...], pick)
        seq = pl.ds(base + j, 1)
        for h in range(M_HEADS):
            cols = slice(h * dk, (h + 1) * dk)
            c0 = c0_ref[j, h]
            dec = dec_ref[seq, cols]
            wv = wv_ref[seq, cols]
            c_ref[j, h] = dec * c0 + kcol[cols, :] * wv
            qc_ref[seq, cols] = jnp.sum(qcol[cols, :] * c0, axis=0, keepdims=True)


def _dec_mem(qt, kt, decay, wv, c0):
    nseq, heads, dk, _ = c0.shape
    d = heads * dk
    bb = DEC_BLOCK if nseq % DEC_BLOCK == 0 else 1
    whole = lambda shape: pl.BlockSpec(shape, lambda i: (0, 0))
    return pl.pallas_call(
        _dec_mem_kernel,
        grid=(nseq // bb,),
        in_specs=[whole((d, nseq)), whole((d, nseq)), whole((nseq, d)), whole((nseq, d)),
                  pl.BlockSpec((bb, heads, dk, dk), lambda i: (i, 0, 0, 0))],
        out_specs=[pl.BlockSpec((bb, heads, dk, dk), lambda i: (i, 0, 0, 0)), whole((nseq, d))],
        out_shape=[jax.ShapeDtypeStruct(c0.shape, F32), jax.ShapeDtypeStruct((nseq, d), F32)],
        compiler_params=_cparams(("arbitrary",)),
        name="decode_mem",
    )(qt, kt, decay, wv, c0)


def _dec_post_kernel(x_ref, sh_ref, sc_ref, ga_ref, g_ref, wint_hbm, wbra_hbm, wbrb_hbm, wout_hbm,
                     ha_ref, qc_ref, pw_ref, sv_ref, den_ref, em_ref, o_ref,
                     wm_ref, wbra_ref, wbrb_ref, wout_ref, stage, sem):
    d = x_ref.shape[1]
    _stage_jobs([(wint_hbm, 4 * d, d, wm_ref, 0),
                 (wint_hbm, 5 * d + 2 * M_HEADS, 2 * d, wm_ref, d),
                 (wbra_hbm, 0, d, wbra_ref, 0), (wbrb_hbm, 0, d, wbrb_ref, 0), (wout_hbm, 0, d, wout_ref, 0)],
                stage, sem)
    x = x_ref[...]
    xn = _rms_mod(x, g_ref[...], sh_ref[...], sc_ref[...]).astype(BF16)
    num = pw_ref[...] * qc_ref[...] + sv_ref[...]
    hm = num / jnp.maximum(jnp.abs(den_ref[...]), em_ref[...])
    hb = (jax.nn.sigmoid(_dot_nt(xn, wm_ref[0:d, :])) * hm).astype(BF16)
    g_a = _dot_nt(xn, wm_ref[d:2 * d, :])
    g_b = _dot_nt(xn, wm_ref[2 * d:3 * d, :])
    merged = (jax.nn.sigmoid(g_a) * _dot(ha_ref[...].astype(BF16), wbra_ref[...])
              + jax.nn.sigmoid(g_b) * _dot(hb, wbrb_ref[...]))
    o_ref[...] = x + ga_ref[...] * _dot(merged.astype(BF16), wout_ref[...])


def _dec_post(x2, mod, gain, w_int, wbra, wbrb, wout, ha, qc, pw, sv, den, em):
    rows, d = x2.shape
    full = lambda shape: pl.BlockSpec(shape, lambda i, _n=len(shape): (0,) * _n)
    mspec = lambda j: pl.BlockSpec((rows, d), lambda i: (0, j))
    args = [x2, mod, mod, mod, gain.reshape(1, d), w_int, wbra, wbrb, wout, ha, qc, pw, sv, den, em]
    in_specs = ([full(x2.shape), mspec(3), mspec(4), mspec(5), full((1, d))] + [_HBM] * 4
                + [full(a.shape) for a in args[9:]])
    return pl.pallas_call(
        _dec_post_kernel,
        grid=(1,),
        in_specs=in_specs,
        out_specs=full((rows, d)),
        out_shape=jax.ShapeDtypeStruct((rows, d), F32),
        scratch_shapes=[pltpu.VMEM((3 * d, d), BF16),
                        pltpu.VMEM((d, d), BF16), pltpu.VMEM((d, d), BF16), pltpu.VMEM((d, d), BF16)]
        + _stage_scratch(d, STAGE_SLOTS_DEC),
        compiler_params=_cparams(("arbitrary",)),
        name="decode_post",
    )(*args)


def kernel(x_prompt, x_sample, state_conv, state_rg_h, state_C, state_n, state_m, c_prompt, c_sample, w_ada, b_ada, g_norm1, w_ff1_in, w_ff1_out, g_norm2, w_in, conv_w, conv_b, w_rg_a, b_rg_a, w_rg_i, b_rg_i, rg_lambda, b_ig, b_fg, w_br_a, w_br_b, w_out, g_norm3, w_ff2_in, w_ff2_out, g_final):
    nb, seq, d = x_prompt.shape
    ns = x_sample.shape[0]
    depth = w_ada.shape[0]
    assert depth == 1 and x_sample.shape[1] == 1 and nb == SUBLANES
    assert seq % ML_ROWS == 0 and seq % FFN_ROWS == 0 and ML_ROWS % (SUBLANES * SUBLANES) == 0
    assert nb % ML_SEQS == 0
    heads, dk = M_HEADS, d // M_HEADS
    assert w_in.shape[2] == 7 * d + 2 * heads and GATE_ROWS >= 2 * heads

    w_int = w_in[0].T
    b_gates = jnp.concatenate([b_ig[0], b_fg[0]])
    bif = jnp.pad(b_gates, (0, LANES - 2 * heads)).reshape(1, LANES)
    bift = jnp.pad(b_gates, (0, GATE_ROWS - 2 * heads)).reshape(GATE_ROWS, 1)
    wa, wi = w_rg_a[0], w_rg_i[0]
    wf1i, wf1o, wf2i, wf2o = w_ff1_in[0], w_ff1_out[0], w_ff2_in[0], w_ff2_out[0]
    wbra, wbrb, wout = w_br_a[0], w_br_b[0], w_out[0]

    mod = _ada(c_sample, c_prompt, w_ada[0], b_ada[0])

    xp, xs = _ffn(x_prompt.reshape(nb * seq, d), x_sample.reshape(ns, d), mod, 0, g_norm1[0],
                  wf1i, wf1o, None, rows_per_seq=seq)

    xp, c_p, n_p, m_p, conv_p, h_p = _mixer_prompt(
        xp.reshape(nb, seq, d), mod, g_norm2[0], w_int, bif, bift, wbra, wbrb, wout, conv_w[0], conv_b[0],
        wa, wi, b_rg_a[0], b_rg_i[0], rg_lambda[0])
    xp = xp.reshape(nb * seq, d)

    conv0 = jnp.swapaxes(state_conv[0], 0, 1)
    (ha_s, conv_s, h_s, n_s, m_s, qt, kt, decay, wv, pw, sv, den, em) = _dec_pre(
        xs, mod, g_norm2[0], w_int, bif, conv_w[0], conv_b[0], wa, wi,
        b_rg_a[0], b_rg_i[0], rg_lambda[0], conv0, state_rg_h[0], state_n[0].reshape(ns, d), state_m[0])
    c_s, qc = _dec_mem(qt, kt, decay, wv, state_C[0])
    xs = _dec_post(xs, mod, g_norm2[0], w_int, wbra, wbrb, wout, ha_s, qc, pw, sv, den, em)

    yp, ys = _ffn(xp, xs, mod, 6, g_norm3[0], wf2i, wf2o, g_final, rows_per_seq=seq)

    return (yp.reshape(nb, seq, d), ys.reshape(ns, 1, d),
            conv_p[None], h_p.reshape(1, nb, d), c_p[None], n_p[None], m_p.reshape(1, nb, heads),
            jnp.swapaxes(conv_s, 0, 1)[None], h_s[None], c_s[None], n_s.reshape(1, ns, heads, dk), m_s[:, ::dk][None])
```
